```python
import math
import jax, jax.numpy as jnp
from jax import lax
import numpy as np

D_MODEL = 2048
BATCH = 2
SEQ = 4096
DEPTH = 1
DEC_BATCH = 32
DEC_SEQ = 4
PAST_LEN = 16384
PAGE_SIZE = 128

HEAD_DIM = 64
MIX_WIDTH = D_MODEL
RWKV_WIDTH = MIX_WIDTH // 2
RWKV_HEADS = RWKV_WIDTH // HEAD_DIM
NSA_WIDTH = MIX_WIDTH - RWKV_WIDTH
NSA_HEADS = NSA_WIDTH // HEAD_DIM
NSA_KV_HEADS = 4
NSA_GROUP = NSA_HEADS // NSA_KV_HEADS
NSA_KV_WIDTH = NSA_KV_HEADS * HEAD_DIM
DECAY_LORA = 64
AAA_LORA = 64
GATE_LORA = 160
RWKV_PROJ = 3 * RWKV_WIDTH + DECAY_LORA + AAA_LORA + GATE_LORA
NSA_PROJ = NSA_WIDTH + 6 * NSA_KV_WIDTH + 3 * NSA_HEADS
IN_COLS = RWKV_PROJ + NSA_PROJ
CMP_LEN = 32
CMP_STRIDE = 16
CMP_HIDDEN = 64
SEL_BLOCK = 64
SEL_TOPK = 16
WINDOW = 512
Q_BLOCK = 128
REL_BUCKETS = 32
REL_MAX_DIST = 2048
PEER_HEADS = 8
PEER_KEYS = 128
PEER_TOPK = 16
PEER_QDIM = 256
N_EXPERTS = PEER_KEYS ** 2
PEER_CHUNK = 128
N_MODS = 6
RMS_EPS = 1e-6
LNX_EPS = 64e-5
NEG = -1e30

kernel_name = 'hymba_rwkv7_nsa_peer_step'


def rms_norm(x, g):
    xf = x.astype(jnp.float32)
    y = xf * lax.rsqrt(jnp.mean(xf * xf, axis=-1, keepdims=True) + RMS_EPS)
    return (y * g.astype(jnp.float32)).astype(x.dtype)


def masked_softmax(logits, mask):
    z = jnp.where(mask, logits.astype(jnp.float32), NEG)
    return jax.nn.softmax(z, axis=-1) * jnp.any(mask, axis=-1, keepdims=True)


def rel_bucket(dist):
    d = jnp.maximum(dist, 0)
    exact = REL_BUCKETS // 2
    ratio = jnp.maximum(d, exact).astype(jnp.float32) / exact
    large = exact + (jnp.log(ratio) / math.log(REL_MAX_DIST / exact) * (REL_BUCKETS - exact)).astype(jnp.int32)
    return jnp.where(d < exact, d, jnp.minimum(large, REL_BUCKETS - 1))


def adaln_mods(c, w_ada, b_ada):
    m = jax.nn.silu(c) @ w_ada + b_ada
    return jnp.split(m[:, None, :], N_MODS, axis=-1)


def wkv_scan(r, w, k, v, a, b, s0):
    def step(s, inp):
        r_t, w_t, k_t, v_t, a_t, b_t = inp
        sa = jnp.einsum('bhij,bhj->bhi', s, a_t)
        s = s * w_t[:, :, None, :] + sa[..., None] * b_t[:, :, None, :] + v_t[..., None] * k_t[:, :, None, :]
        return s, jnp.einsum('bhij,bhj->bhi', s, r_t)
    xs = tuple(jnp.swapaxes(z, 0, 1) for z in (r, w, k, v, a, b))
    s_fin, ys = lax.scan(step, s0, xs)
    return jnp.swapaxes(ys, 0, 1), s_fin


def rwkv_mix(p_r, shift_prev, wkv0, W):
    b, t, _ = p_r.shape
    C = RWKV_WIDTH
    prev = jnp.concatenate([shift_prev[:, None].astype(p_r.dtype), p_r[:, :-1]], axis=1)
    xs = p_r + (prev - p_r) * W['rwkv_mu']
    r, k, v = xs[..., :C], xs[..., C:2 * C], xs[..., 2 * C:3 * C]
    o = 3 * C
    xw = xs[..., o:o + DECAY_LORA]
    xa = xs[..., o + DECAY_LORA:o + DECAY_LORA + AAA_LORA]
    xg = xs[..., o + DECAY_LORA + AAA_LORA:]
    w = -jax.nn.softplus(-(W['rwkv_w0'] + jnp.tanh(xw) @ W['rwkv_w_up'])) - 0.5
    a = jax.nn.sigmoid(W['rwkv_a0'] + xa @ W['rwkv_a_up'])
    gate = jax.nn.sigmoid(xg) @ W['rwkv_g_up']

    def heads(z):
        return z.reshape(b, t, RWKV_HEADS, HEAD_DIM).astype(jnp.float32)
    kk = heads(k * W['rwkv_k_k'])
    kk = kk / jnp.maximum(jnp.sqrt(jnp.sum(kk * kk, axis=-1, keepdims=True)), 1e-12)
    k = k * (1.0 + (a - 1.0) * W['rwkv_k_a'])
    decay = jnp.exp(-jnp.exp(w.astype(jnp.float32)))
    rh, kh, vh, ah = heads(r), heads(k), heads(v), heads(a)
    y, s_fin = wkv_scan(rh, heads(decay), kh, vh, -kk, kk * ah, wkv0.astype(jnp.float32))
    mu = jnp.mean(y, axis=-1, keepdims=True)
    var = jnp.mean((y - mu) ** 2, axis=-1, keepdims=True)
    y = ((y - mu) * lax.rsqrt(var + LNX_EPS)).reshape(b, t, C) * W['lnx_w'] + W['lnx_b']
    bonus = jnp.sum(rh * kh * W['rwkv_r_k'], axis=-1, keepdims=True) * vh
    y = (y + bonus.reshape(b, t, C)) * gate
    return y.astype(p_r.dtype), s_fin.astype(wkv0.dtype), p_r[:, -1]


def compress(k, w1, b1, w2, b2):
    b, l, g, dk = k.shape
    ns = l // CMP_STRIDE
    r2 = CMP_LEN // CMP_STRIDE
    nc = ns - r2 + 1
    kb = k[:, :ns * CMP_STRIDE].reshape(b, ns, CMP_STRIDE, g, dk)
    w1r = w1.reshape(r2, CMP_STRIDE, dk, CMP_HIDDEN)
    part = jnp.einsum('bnsgd,hsde->hbnge', kb, w1r)
    hid = part[0, :, :nc] + b1
    for h in range(1, r2):
        hid = hid + part[h, :, h:h + nc]
    return jnp.einsum('bnge,ed->bngd', jax.nn.gelu(hid), w2) + b2


def to_blocks(k):
    b, l, g, dk = k.shape
    nb = l // SEL_BLOCK
    return k.reshape(b, nb, SEL_BLOCK, g, dk).transpose(0, 3, 1, 2, 4).reshape(b, g, nb, SEL_BLOCK * dk)


def slc_offsets():
    r1 = SEL_BLOCK // CMP_STRIDE
    r2 = CMP_LEN // CMP_STRIDE
    offs = np.arange(-(r2 - 1), r1)
    wts = np.array([sum(1 for m in range(r1) for n in range(r2) if m - n == o) for o in offs], np.float32)
    return offs, wts


def nsa_project(p_n, qk_norm_g):
    b, t, _ = p_n.shape
    q = rms_norm(p_n[..., :NSA_WIDTH].reshape(b, t, NSA_KV_HEADS, NSA_GROUP, HEAD_DIM), qk_norm_g[0])
    kv = p_n[..., NSA_WIDTH:NSA_WIDTH + 6 * NSA_KV_WIDTH].reshape(b, t, 6, NSA_KV_HEADS, HEAD_DIM)
    gates = jax.nn.sigmoid(p_n[..., NSA_WIDTH + 6 * NSA_KV_WIDTH:]).reshape(b, t, NSA_KV_HEADS, NSA_GROUP, 3)
    rows = jnp.stack([kv[:, :, 0], kv[:, :, 1], rms_norm(kv[:, :, 2], qk_norm_g[2]), kv[:, :, 3]], axis=2)
    win = jnp.stack([rms_norm(kv[:, :, 4], qk_norm_g[3]), kv[:, :, 5]], axis=2)
    return q, gates, rows, win


def nsa_context(rows, W):
    kc = rms_norm(compress(rows[:, :, 0], W['cmp_w1'][0], W['cmp_b1'][0], W['cmp_w2'][0], W['cmp_b2'][0]), W['qk_norm_g'][1])
    vc = compress(rows[:, :, 1], W['cmp_w1'][1], W['cmp_b1'][1], W['cmp_w2'][1], W['cmp_b2'][1])
    c_end = jnp.arange(kc.shape[1]) * CMP_STRIDE + CMP_LEN - 1
    return kc, vc, c_end, to_blocks(rows[:, :, 2]), to_blocks(rows[:, :, 3])


def nsa_core(q, gates, q_pos, kc, vc, c_end, ks_blk, vs_blk, kw, vw, kw_pos, rel_bias):
    b, tq, g, r, dk = q.shape
    nb = ks_blk.shape[2]
    nc = kc.shape[1]
    scale = HEAD_DIM ** -0.5
    table = rel_bias.astype(jnp.float32).reshape(REL_BUCKETS, g, r)

    dist_c = q_pos[:, None] - c_end[None, :]
    bias_c = table[rel_bucket(dist_c)].transpose(2, 3, 0, 1)
    lg_c = jnp.einsum('bqgrd,bcgd->bgrqc', q, kc).astype(jnp.float32) * scale + bias_c
    p_c = masked_softmax(lg_c, dist_c >= 0)
    o_c = jnp.einsum('bgrqc,bcgd->bqgrd', p_c.astype(vc.dtype), vc)

    offs, wts = slc_offsets()
    jb = jnp.arange(nb)
    cidx = (SEL_BLOCK // CMP_STRIDE) * jb[:, None] + offs[None, :]
    cval = (cidx >= 0) & (cidx < nc)
    p_grp = jnp.sum(p_c, axis=2)
    p_slc = jnp.sum(p_grp[..., jnp.clip(cidx, 0, nc - 1)] * (wts * cval), axis=-1)
    cur = q_pos // SEL_BLOCK
    forced = (jb[None] == 0) | (jb[None] == cur[:, None]) | (jb[None] == cur[:, None] - 1)
    future = jb[None] > cur[:, None]
    score = jnp.where(future, -1.0, jnp.where(forced, 1e6, p_slc))
    n_sel = min(SEL_TOPK, nb)
    _, sel = lax.top_k(score, n_sel)
    sel_ok = sel <= cur[:, None]
    bi = jnp.arange(b)[:, None, None]
    gi = jnp.arange(g)[None, :, None]
    flat = sel.reshape(b, g, tq * n_sel)
    ksg = ks_blk[bi, gi, flat].reshape(b, g, tq, n_sel * SEL_BLOCK, dk)
    vsg = vs_blk[bi, gi, flat].reshape(b, g, tq, n_sel * SEL_BLOCK, dk)
    pos4 = sel[..., None] * SEL_BLOCK + jnp.arange(SEL_BLOCK)
    ok_s = (sel_ok[..., None] & (pos4 <= q_pos[:, None, None])).reshape(b, g, tq, n_sel * SEL_BLOCK)
    pos_s = pos4.reshape(b, g, tq, n_sel * SEL_BLOCK)
    tg = table.transpose(1, 0, 2)
    bias_s = tg[jnp.arange(g)[None, :, None, None], rel_bucket(q_pos[:, None] - pos_s)].transpose(0, 1, 4, 2, 3)
    lg_s = jnp.einsum('bqgrd,bgqkd->bgrqk', q, ksg).astype(jnp.float32) * scale + bias_s
    p_s = masked_softmax(lg_s, ok_s[:, :, None])
    o_s = jnp.einsum('bgrqk,bgqkd->bqgrd', p_s.astype(vsg.dtype), vsg)

    dist_w = q_pos[:, None] - kw_pos[None, :]
    ok_w = (dist_w >= 0) & (dist_w < WINDOW) & (kw_pos[None, :] >= 0)
    bias_w = table[rel_bucket(dist_w)].transpose(2, 3, 0, 1)
    lg_w = jnp.einsum('bqgrd,bkgd->bgrqk', q, kw).astype(jnp.float32) * scale + bias_w
    p_w = masked_softmax(lg_w, ok_w)
    o_w = jnp.einsum('bgrqk,bkgd->bqgrd', p_w.astype(vw.dtype), vw)

    o = gates[..., 0:1] * o_c + gates[..., 1:2] * o_s + gates[..., 2:3] * o_w
    return o.reshape(b, tq, g * r * dk)


def nsa_prompt(p_n, W):
    q, gates, rows, win = nsa_project(p_n, W['qk_norm_g'])
    b, s = q.shape[:2]
    kc, vc, c_end, ks_blk, vs_blk = nsa_context(rows, W)
    win_pad = jnp.pad(win, ((0, 0), (WINDOW, 0), (0, 0), (0, 0), (0, 0)))

    def block(i):
        start = i * Q_BLOCK
        qb = lax.dynamic_slice_in_dim(q, start, Q_BLOCK, axis=1)
        gb = lax.dynamic_slice_in_dim(gates, start, Q_BLOCK, axis=1)
        wb = lax.dynamic_slice_in_dim(win_pad, start, WINDOW + Q_BLOCK, axis=1)
        q_pos = start + jnp.arange(Q_BLOCK)
        kw_pos = start - WINDOW + jnp.arange(WINDOW + Q_BLOCK)
        return nsa_core(qb, gb, q_pos, kc, vc, c_end, ks_blk, vs_blk, wb[:, :, 0], wb[:, :, 1], kw_pos, W['rel_bias'])

    o = lax.map(block, jnp.arange(s // Q_BLOCK))
    o = o.transpose(1, 0, 2, 3).reshape(b, s, NSA_WIDTH)
    return o, (rows, win[:, s - min(WINDOW, s):])


def nsa_sample(p_n, cache_kv, cache_win, page_table, W):
    q, gates, rows_new, win_new = nsa_project(p_n, W['qk_norm_g'])
    db, ds = q.shape[:2]
    past_len = page_table.shape[1] * PAGE_SIZE
    past_rows = cache_kv[page_table].reshape(db, past_len, 4, NSA_KV_HEADS, HEAD_DIM)
    pad = (-ds) % SEL_BLOCK
    rows = jnp.concatenate([past_rows, jnp.pad(rows_new, ((0, 0), (0, pad), (0, 0), (0, 0), (0, 0))).astype(past_rows.dtype)], axis=1)
    kc, vc, c_end, ks_blk, vs_blk = nsa_context(rows, W)
    keep = cache_win.shape[1]
    win_all = jnp.concatenate([cache_win, win_new.astype(cache_win.dtype)], axis=1)
    q_pos = past_len + jnp.arange(ds)
    kw_pos = past_len - keep + jnp.arange(keep + ds)
    o = nsa_core(q, gates, q_pos, kc, vc, c_end, ks_blk, vs_blk, win_all[:, :, 0], win_all[:, :, 1], kw_pos, W['rel_bias'])
    n_keep = min(WINDOW, past_len + ds)
    return o, (rows_new, win_all[:, win_all.shape[1] - n_keep:])


def peer(h, W):
    n, d = h.shape
    pad = (-n) % PEER_CHUNK
    hc = jnp.pad(h, ((0, pad), (0, 0))).reshape(-1, PEER_CHUNK, d)
    half = PEER_QDIM // 2
    sub_keys, expert_u, expert_v = W['peer_sub_keys'], W['peer_u'], W['peer_v']

    def chunk(x):
        qh = (x @ W['peer_w_query']).reshape(PEER_CHUNK, PEER_HEADS, 2, half)
        s1 = jnp.einsum('chd,kd->chk', qh[:, :, 0], sub_keys[0]).astype(jnp.float32)
        s2 = jnp.einsum('chd,kd->chk', qh[:, :, 1], sub_keys[1]).astype(jnp.float32)
        v1, i1 = lax.top_k(s1, PEER_TOPK)
        v2, i2 = lax.top_k(s2, PEER_TOPK)
        cand = (v1[..., :, None] + v2[..., None, :]).reshape(PEER_CHUNK, PEER_HEADS, PEER_TOPK * PEER_TOPK)
        cidx = (i1[..., :, None] * PEER_KEYS + i2[..., None, :]).reshape(PEER_CHUNK, PEER_HEADS, PEER_TOPK * PEER_TOPK)
        top, pos = lax.top_k(cand, PEER_TOPK)
        eidx = jnp.take_along_axis(cidx, pos, axis=-1)
        gw = jax.nn.softmax(top, axis=-1)
        act = jax.nn.gelu(jnp.einsum('chkd,cd->chk', expert_u[eidx], x).astype(jnp.float32))
        return jnp.einsum('chk,chkd->cd', (gw * act).astype(x.dtype), expert_v[eidx])

    return lax.map(chunk, hc).reshape(-1, d)[:n]


def decoder_layer(x, c, nsa_fn, shift_prev, wkv0, W):
    sh1, sc1, ga1, sh2, sc2, ga2 = adaln_mods(c, W['w_ada'], W['b_ada'])
    h = rms_norm(x, W['norm1_g']) * (1.0 + sc1) + sh1
    p = h @ W['w_in']
    y_r, wkv_new, shift_new = rwkv_mix(p[..., :RWKV_PROJ], shift_prev, wkv0, W)
    y_n, nsa_state = nsa_fn(p[..., RWKV_PROJ:])
    x = x + ga1 * (jnp.concatenate([y_r, y_n.astype(y_r.dtype)], axis=-1) @ W['w_out'])
    h2 = rms_norm(x, W['norm2_g']) * (1.0 + sc2) + sh2
    b, t, d = h2.shape
    x = x + ga2 * peer(h2.reshape(b * t, d), W).reshape(b, t, d)
    return x, nsa_state, wkv_new, shift_new


def setup_inputs(seed: int = 0) -> dict:
    key = jax.random.key(seed)
    keys = iter(list(jax.random.split(key, 48)))

    def nrm(shape, s):
        return jax.random.normal(next(keys), shape, jnp.float32) * s

    n_pages = PAST_LEN // PAGE_SIZE
    n_pool = (DEC_BATCH * n_pages * 5) // 4
    win_keep = min(WINDOW, PAST_LEN)
    return {
        'x_prompt': nrm((BATCH, SEQ, D_MODEL), 1.0),
        'x_sample': nrm((DEC_BATCH, DEC_SEQ, D_MODEL), 1.0),
        'c_prompt': nrm((BATCH, D_MODEL), 1.0),
        'c_sample': nrm((DEC_BATCH, D_MODEL), 1.0),
        'cache_kv': nrm((n_pool, PAGE_SIZE, 4, NSA_KV_HEADS, HEAD_DIM), 1.0),
        'cache_win': nrm((DEC_BATCH, win_keep, 2, NSA_KV_HEADS, HEAD_DIM), 1.0),
        'state_wkv': nrm((DEC_BATCH, RWKV_HEADS, HEAD_DIM, HEAD_DIM), 0.1),
        'state_shift': nrm((DEC_BATCH, RWKV_PROJ), 1.0),
        'page_table': jax.random.permutation(next(keys), n_pool)[:DEC_BATCH * n_pages].reshape(DEC_BATCH, n_pages).astype(jnp.int32),
        'norm1_g': 1.0 + nrm((D_MODEL,), 0.02),
        'norm2_g': 1.0 + nrm((D_MODEL,), 0.02),
        'w_ada': nrm((D_MODEL, N_MODS * D_MODEL), 0.5 * D_MODEL ** -0.5),
        'b_ada': nrm((N_MODS * D_MODEL,), 0.02),
        'w_in': nrm((D_MODEL, IN_COLS), D_MODEL ** -0.5),
        'w_out': nrm((MIX_WIDTH, D_MODEL), MIX_WIDTH ** -0.5),
        'rwkv_mu': jax.random.uniform(next(keys), (RWKV_PROJ,), jnp.float32),
        'rwkv_w0': jnp.linspace(-6.0, -1.0, RWKV_WIDTH, dtype=jnp.float32) + nrm((RWKV_WIDTH,), 0.1),
        'rwkv_w_up': nrm((DECAY_LORA, RWKV_WIDTH), 0.5 * DECAY_LORA ** -0.5),
        'rwkv_a0': nrm((RWKV_WIDTH,), 0.1),
        'rwkv_a_up': nrm((AAA_LORA, RWKV_WIDTH), 0.5 * AAA_LORA ** -0.5),
        'rwkv_g_up': nrm((GATE_LORA, RWKV_WIDTH), GATE_LORA ** -0.5),
        'rwkv_k_k': 0.85 + nrm((RWKV_WIDTH,), 0.02),
        'rwkv_k_a': 1.0 + nrm((RWKV_WIDTH,), 0.02),
        'rwkv_r_k': nrm((RWKV_HEADS, HEAD_DIM), 0.1),
        'lnx_w': 1.0 + nrm((RWKV_WIDTH,), 0.02),
        'lnx_b': nrm((RWKV_WIDTH,), 0.02),
        'qk_norm_g': 1.0 + nrm((4, HEAD_DIM), 0.02),
        'cmp_w1': nrm((2, CMP_LEN * HEAD_DIM, CMP_HIDDEN), (CMP_LEN * HEAD_DIM) ** -0.5),
        'cmp_b1': nrm((2, CMP_HIDDEN), 0.02),
        'cmp_w2': nrm((2, CMP_HIDDEN, HEAD_DIM), CMP_HIDDEN ** -0.5),
        'cmp_b2': nrm((2, HEAD_DIM), 0.02),
        'rel_bias': nrm((REL_BUCKETS, NSA_HEADS), 0.5),
        'peer_w_query': nrm((D_MODEL, PEER_HEADS * PEER_QDIM), D_MODEL ** -0.5),
        'peer_sub_keys': nrm((2, PEER_KEYS, PEER_QDIM // 2), (PEER_QDIM // 2) ** -0.5),
        'peer_u': nrm((N_EXPERTS, D_MODEL), D_MODEL ** -0.5),
        'peer_v': nrm((N_EXPERTS, D_MODEL), 0.5),
    }


def reference(x_prompt, x_sample, c_prompt, c_sample, cache_kv, cache_win, state_wkv, state_shift, page_table,
              norm1_g, norm2_g, w_ada, b_ada, w_in, w_out,
              rwkv_mu, rwkv_w0, rwkv_w_up, rwkv_a0, rwkv_a_up, rwkv_g_up, rwkv_k_k, rwkv_k_a, rwkv_r_k, lnx_w, lnx_b,
              qk_norm_g, cmp_w1, cmp_b1, cmp_w2, cmp_b2, rel_bias,
              peer_w_query, peer_sub_keys, peer_u, peer_v):
    W = dict(norm1_g=norm1_g, norm2_g=norm2_g, w_ada=w_ada, b_ada=b_ada, w_in=w_in, w_out=w_out,
             rwkv_mu=rwkv_mu, rwkv_w0=rwkv_w0, rwkv_w_up=rwkv_w_up, rwkv_a0=rwkv_a0, rwkv_a_up=rwkv_a_up,
             rwkv_g_up=rwkv_g_up, rwkv_k_k=rwkv_k_k, rwkv_k_a=rwkv_k_a, rwkv_r_k=rwkv_r_k, lnx_w=lnx_w, lnx_b=lnx_b,
             qk_norm_g=qk_norm_g, cmp_w1=cmp_w1, cmp_b1=cmp_b1, cmp_w2=cmp_w2, cmp_b2=cmp_b2, rel_bias=rel_bias,
             peer_w_query=peer_w_query, peer_sub_keys=peer_sub_keys, peer_u=peer_u, peer_v=peer_v)
    bp = x_prompt.shape[0]
    shift0 = jnp.zeros((bp, RWKV_PROJ), x_prompt.dtype)
    wkv0 = jnp.zeros((bp, RWKV_HEADS, HEAD_DIM, HEAD_DIM), jnp.float32)

    y_prompt = x_prompt
    for _ in range(DEPTH):
        y_prompt, (rows_p, win_p), wkv_p, shift_p = decoder_layer(
            y_prompt, c_prompt, lambda pn: nsa_prompt(pn, W), shift0, wkv0, W)

    y_sample = x_sample
    for _ in range(DEPTH):
        y_sample, (rows_s, win_s), wkv_s, shift_s = decoder_layer(
            y_sample, c_sample, lambda pn: nsa_sample(pn, cache_kv, cache_win, page_table, W), state_shift, state_wkv, W)

    return (y_prompt, y_sample, rows_p, win_p, wkv_p, shift_p, rows_s, win_s, wkv_s, shift_s)
```

```python
import functools
import math

import numpy as np
import jax
import jax.numpy as jnp
from jax import lax
from jax.experimental import pallas as pl
from jax.experimental.pallas import tpu as pltpu

F32 = jnp.float32
BF16 = jnp.bfloat16
HI = lax.Precision.HIGHEST

HEAD_DIM = 64
PAGE_SIZE = 128
CMP_LEN = 32
CMP_STRIDE = 16
SEL_BLOCK = 64
SEL_TOPK = 16
WINDOW = 512
REL_BUCKETS = 32
REL_MAX_DIST = 2048
PEER_TOPK = 16
N_MODS = 6
RMS_EPS = 1e-6
LNX_EPS = 64e-5
NEG = -1e30

LANES = 128
MXU = 256
HEADS_PER_GROUP = MXU // HEAD_DIM
VMEM_LIMIT = 56 * 1024 * 1024


def _cparams(sem):
    return pltpu.CompilerParams(dimension_semantics=sem, vmem_limit_bytes=VMEM_LIMIT)


def _dot(a, b, precision=None):
    return jnp.dot(a, b, preferred_element_type=F32, precision=precision)


def _dot_nt(a, b, precision=None):
    return lax.dot_general(a, b, (((1,), (1,)), ((), ())), preferred_element_type=F32, precision=precision)


def _ada_kernel(c_ref, w_ref, b_ref, o_ref):
    c = c_ref[...]
    s = c * jax.nn.sigmoid(c)
    o_ref[...] = _dot(s.astype(BF16), w_ref[...].astype(BF16)) + b_ref[...]


def _ada_mods(c, w_ada, b_ada):
    n, d = c.shape
    cols = w_ada.shape[1]
    tn = 1024
    return pl.pallas_call(
        _ada_kernel,
        grid=(cols // tn,),
        in_specs=[pl.BlockSpec((n, d), lambda j: (0, 0)),
                  pl.BlockSpec((d, tn), lambda j: (0, j)),
                  pl.BlockSpec((1, tn), lambda j: (0, j))],
        out_specs=pl.BlockSpec((n, tn), lambda j: (0, j)),
        out_shape=jax.ShapeDtypeStruct((n, cols), F32),
        compiler_params=_cparams(("arbitrary",)),
        name="ada_mods",
    )(c, w_ada, b_ada.reshape(1, cols))


def _nmm_kernel(x_ref, g_ref, sc_ref, sh_ref, w_ref, o_ref, h_ref):
    @pl.when(pl.program_id(1) == 0)
    def _():
        x = x_ref[...]
        ms = jnp.mean(x * x, axis=-1, keepdims=True)
        y = x * lax.rsqrt(ms + RMS_EPS) * g_ref[...]
        h_ref[...] = (y * (1.0 + sc_ref[0]) + sh_ref[0]).astype(h_ref.dtype)

    o_ref[...] = _dot(h_ref[...], w_ref[...])


def _norm_mod_matmul(x, g, sc, sh, w, tm, tn):
    t, d = x.shape
    n = w.shape[1]
    nmod, rows, _ = sc.shape
    tiles_per_mod = (t // tm) // nmod
    mod_spec = pl.BlockSpec((1, rows, d), lambda i, j: (i // tiles_per_mod, 0, 0))
    return pl.pallas_call(
        _nmm_kernel,
        grid=(t // tm, n // tn),
        in_specs=[pl.BlockSpec((tm, d), lambda i, j: (i, 0)),
                  pl.BlockSpec((1, d), lambda i, j: (0, 0)),
                  mod_spec, mod_spec,
                  pl.BlockSpec((d, tn), lambda i, j: (0, j))],
        out_specs=[pl.BlockSpec((tm, tn), lambda i, j: (i, j)),
                   pl.BlockSpec((tm, d), lambda i, j: (i, 0))],
        out_shape=[jax.ShapeDtypeStruct((t, n), F32), jax.ShapeDtypeStruct((t, d), BF16)],
        compiler_params=_cparams(("arbitrary", "arbitrary")),
        name="norm_mod_matmul",
    )(x, g.reshape(1, d), sc, sh, w)


def _outproj_kernel(x_ref, yr_ref, yn_ref, ga_ref, w1_ref, w2_ref, o_ref):
    acc = _dot(yr_ref[...].astype(BF16), w1_ref[...]) + _dot(yn_ref[...].astype(BF16), w2_ref[...])
    o_ref[...] = x_ref[...] + ga_ref[0] * acc


def _out_proj(x, y_r, y_n, ga, w1, w2, tm, tn):
    t, d = x.shape
    nmod, rows, _ = ga.shape
    tiles_per_mod = (t // tm) // nmod
    cr, cn = y_r.shape[1], y_n.shape[1]
    return pl.pallas_call(
        _outproj_kernel,
        grid=(t // tm, d // tn),
        in_specs=[pl.BlockSpec((tm, tn), lambda i, j: (i, j)),
                  pl.BlockSpec((tm, cr), lambda i, j: (i, 0)),
                  pl.BlockSpec((tm, cn), lambda i, j: (i, 0)),
                  pl.BlockSpec((1, rows, tn), lambda i, j: (i // tiles_per_mod, 0, j)),
                  pl.BlockSpec((cr, tn), lambda i, j: (0, j)),
                  pl.BlockSpec((cn, tn), lambda i, j: (0, j))],
        out_specs=pl.BlockSpec((tm, tn), lambda i, j: (i, j)),
        out_shape=jax.ShapeDtypeStruct((t, d), F32),
        compiler_params=_cparams(("arbitrary", "arbitrary")),
        name="out_proj",
    )(x, y_r, y_n, ga, w1, w2)


def _softplus(z):
    return jnp.maximum(z, 0.0) + jnp.log(1.0 + jnp.exp(-jnp.abs(z)))


def _rwkv_kernel(p_ref, shift_ref, s0_ref, mu_ref, vec_ref, wup_ref, aup_ref, gup_ref,
                 y_ref, sfin_ref, carry, state, *, t_valid, n_chunks):
    c = pl.program_id(1)
    chunk = p_ref.shape[1]
    cw = vec_ref.shape[1]
    n_groups = cw // MXU
    hg = HEADS_PER_GROUP
    rows_g = hg * chunk

    @pl.when(c == 0)
    def _():
        carry[...] = shift_ref[0]
        state[...] = s0_ref[0]

    p = p_ref[0]
    row = lax.broadcasted_iota(jnp.int32, (chunk, 1), 0)
    prev = jnp.where(row == 0, carry[...], pltpu.roll(p, 1, axis=0))
    carry[...] = p[chunk - 1:chunk, :]
    xs = p + (prev - p) * mu_ref[...]

    w0, a0, k_k, k_a, r_k, lnx_w, lnx_b = (vec_ref[i:i + 1, :] for i in range(7))
    r = xs[:, 0:cw]
    k = xs[:, cw:2 * cw]
    v = xs[:, 2 * cw:3 * cw]
    o = 3 * cw
    nw, na, ng = wup_ref.shape[0], aup_ref.shape[0], gup_ref.shape[0]
    xw = xs[:, o:o + nw]
    xa = xs[:, o + nw:o + nw + na]
    xg = xs[:, o + nw + na:o + nw + na + ng]
    w_log = -_softplus(-(w0 + _dot(jnp.tanh(xw), wup_ref[...], HI))) - 0.5
    a = jax.nn.sigmoid(a0 + _dot(xa, aup_ref[...], HI))
    gate = _dot(jax.nn.sigmoid(xg), gup_ref[...], HI)

    gi = lax.broadcasted_iota(jnp.int32, (MXU, MXU), 0) // HEAD_DIM
    gj = lax.broadcasted_iota(jnp.int32, (MXU, MXU), 1) // HEAD_DIM
    ones_bd = (gi == gj).astype(F32)

    def head_sum(x):
        return jnp.concatenate([_dot(x[:, g * MXU:(g + 1) * MXU], ones_bd, HI) for g in range(n_groups)], axis=1)

    kk = k * k_k
    kk = kk / jnp.maximum(jnp.sqrt(head_sum(kk * kk)), 1e-12)
    k2 = k * (1.0 + (a - 1.0) * k_a)
    logdec = -jnp.exp(w_log)
    if t_valid < chunk * n_chunks:
        valid = (row + c * chunk) < t_valid
        logdec = jnp.where(valid, logdec, 0.0)
        kk = jnp.where(valid, kk, 0.0)
        k2 = jnp.where(valid, k2, 0.0)
        v = jnp.where(valid, v, 0.0)

    ti = lax.broadcasted_iota(jnp.int32, (chunk, chunk), 0)
    tj = lax.broadcasted_iota(jnp.int32, (chunk, chunk), 1)
    cum = _dot((tj <= ti).astype(F32), logdec, HI)
    cum_end = cum[chunk - 1:chunk, :]
    e_neg = jnp.exp(-cum)
    e_rem = jnp.exp(cum_end - cum)
    r_t = r * jnp.exp(cum)
    a_t = -kk * jnp.exp(cum - logdec)
    b_vec = kk * a
    b_t = b_vec * e_neg
    k_t = k2 * e_neg
    b_rem = b_vec * e_rem
    k_rem = k2 * e_rem
    w_end = jnp.exp(cum_end)

    lane_head = lax.broadcasted_iota(jnp.int32, (chunk, MXU), 1) // HEAD_DIM
    ri = lax.broadcasted_iota(jnp.int32, (rows_g, rows_g), 0)
    rj = lax.broadcasted_iota(jnp.int32, (rows_g, rows_g), 1)
    strict = rj < ri
    incl = rj <= ri
    eye_r = (ri == rj).astype(F32)
    di = lax.broadcasted_iota(jnp.int32, (MXU, MXU), 0)
    dj = lax.broadcasted_iota(jnp.int32, (MXU, MXU), 1)
    n_double = max(int(math.ceil(math.log2(chunk))) - 1, 0)

    def bd(x):
        return jnp.concatenate([jnp.where(lane_head == h, x, 0.0) for h in range(hg)], axis=0)

    def stack(x):
        return jnp.concatenate([x[:, h * HEAD_DIM:(h + 1) * HEAD_DIM] for h in range(hg)], axis=0)

    def unstack(x):
        return jnp.concatenate([x[h * chunk:(h + 1) * chunk, :] for h in range(hg)], axis=1)

    ys = []
    for g in range(n_groups):
        sl = slice(g * MXU, (g + 1) * MXU)
        a_bd, r_bd = bd(a_t[:, sl]), bd(r_t[:, sl])
        b_bd, k_bd = bd(b_t[:, sl]), bd(k_t[:, sl])
        v_st = stack(v[:, sl])
        a_ab = jnp.where(strict, _dot_nt(a_bd, b_bd, HI), 0.0)
        a_ak = jnp.where(strict, _dot_nt(a_bd, k_bd, HI), 0.0)
        a_rb = jnp.where(incl, _dot_nt(r_bd, b_bd, HI), 0.0)
        a_rk = jnp.where(incl, _dot_nt(r_bd, k_bd, HI), 0.0)
        tinv = eye_r + a_ab
        pw = a_ab
        for _ in range(n_double):
            pw = _dot(pw, pw, HI)
            tinv = tinv + _dot(pw, tinv, HI)
        s0 = state[g]
        z = _dot(a_bd, s0, HI) + _dot(a_ak, v_st, HI)
        u = _dot(tinv, z, HI)
        y_st = _dot(r_bd, s0, HI) + _dot(a_rb, u, HI) + _dot(a_rk, v_st, HI)
        decay_diag = jnp.where(di == dj, jnp.broadcast_to(w_end[:, sl], (MXU, MXU)), 0.0)
        state[g] = (_dot(decay_diag, s0, HI) + _dot(bd(b_rem[:, sl]).T, u, HI)
                    + _dot(bd(k_rem[:, sl]).T, v_st, HI))
        ys.append(unstack(y_st))
    y = jnp.concatenate(ys, axis=1)

    inv_n = 1.0 / HEAD_DIM
    mean = head_sum(y) * inv_n
    d = y - mean
    var = head_sum(d * d) * inv_n
    yn = d * lax.rsqrt(var + LNX_EPS) * lnx_w + lnx_b
    bonus = head_sum(r * k2 * r_k) * v
    y_ref[0] = ((yn + bonus) * gate).astype(y_ref.dtype)

    @pl.when(c == n_chunks - 1)
    def _():
        sfin_ref[0] = state[...]


def _rwkv_pad_cols(x, cw, nw, na, ng):
    o = 3 * cw
    parts = [x[..., :o + nw], x[..., o + nw:o + nw + na], x[..., o + nw + na:]]
    widths = [o + _rup(nw, LANES), _rup(na, LANES), _rup(ng, LANES)]
    out = []
    for part, wd in zip(parts, widths):
        pad = [(0, 0)] * (x.ndim - 1) + [(0, wd - part.shape[-1])]
        out.append(jnp.pad(part, pad))
    return jnp.concatenate(out, axis=-1)


def _rwkv_unpad_cols(x, cw, nw, na, ng):
    o = 3 * cw
    o2 = o + _rup(nw, LANES)
    o3 = o2 + _rup(na, LANES)
    return jnp.concatenate([x[..., :o + nw], x[..., o2:o2 + na], x[..., o3:o3 + ng]], axis=-1)


def _rup(x, m):
    return (x + m - 1) // m * m


def _rwkv_mix(p_r, shift_prev, wkv0, W, chunk, t_valid):
    b, tpad, pr = p_r.shape
    cw = W['rwkv_w0'].shape[0]
    nh = cw // HEAD_DIM
    n_groups = cw // MXU
    nw, na, ng = W['rwkv_w_up'].shape[0], W['rwkv_a_up'].shape[0], W['rwkv_g_up'].shape[0]
    n_chunks = tpad // chunk
    mu = _rwkv_pad_cols(W['rwkv_mu'], cw, nw, na, ng).reshape(1, pr)
    vecs = jnp.stack([W['rwkv_w0'], W['rwkv_a0'], W['rwkv_k_k'], W['rwkv_k_a'], W['rwkv_r_k'].reshape(cw),
                      W['lnx_w'], W['lnx_b'], jnp.zeros((cw,), F32)])
    wup = jnp.pad(W['rwkv_w_up'], ((0, _rup(nw, LANES) - nw), (0, 0)))
    aup = jnp.pad(W['rwkv_a_up'], ((0, _rup(na, LANES) - na), (0, 0)))
    gup = jnp.pad(W['rwkv_g_up'], ((0, _rup(ng, LANES) - ng), (0, 0)))
    shift3 = _rwkv_pad_cols(shift_prev, cw, nw, na, ng).reshape(b, 1, pr)
    s0 = wkv0.astype(F32).transpose(0, 1, 3, 2).reshape(b, n_groups, MXU, HEAD_DIM)
    const = lambda shape: pl.BlockSpec(shape, lambda i, c: (0,) * len(shape))
    y, sfin = pl.pallas_call(
        functools.partial(_rwkv_kernel, t_valid=t_valid, n_chunks=n_chunks),
        grid=(b, n_chunks),
        in_specs=[pl.BlockSpec((1, chunk, pr), lambda i, c: (i, c, 0)),
                  pl.BlockSpec((1, 1, pr), lambda i, c: (i, 0, 0)),
                  pl.BlockSpec((1, n_groups, MXU, HEAD_DIM), lambda i, c: (i, 0, 0, 0)),
                  const((1, pr)), const((8, cw)), const(wup.shape), const(aup.shape), const(gup.shape)],
        out_specs=[pl.BlockSpec((1, chunk, cw), lambda i, c: (i, c, 0)),
                   pl.BlockSpec((1, n_groups, MXU, HEAD_DIM), lambda i, c: (i, 0, 0, 0))],
        out_shape=[jax.ShapeDtypeStruct((b, tpad, cw), F32),
                   jax.ShapeDtypeStruct((b, n_groups, MXU, HEAD_DIM), F32)],
        scratch_shapes=[pltpu.VMEM((1, pr), F32), pltpu.VMEM((n_groups, MXU, HEAD_DIM), F32)],
        compiler_params=_cparams(("arbitrary", "arbitrary")),
        name="rwkv_mix",
    )(p_r, shift3, s0, mu, vecs, wup, aup, gup)
    s_fin = sfin.reshape(b, nh, HEAD_DIM, HEAD_DIM).transpose(0, 1, 3, 2)
    return y, s_fin


def _rms(x, g):
    return x * lax.rsqrt(jnp.mean(x * x, axis=-1, keepdims=True) + RMS_EPS) * g


def _masked_softmax(logits, mask):
    z = jnp.where(mask, logits, NEG)
    return jax.nn.softmax(z, axis=-1) * jnp.any(mask, axis=-1, keepdims=True)


def _rel_bucket(dist):
    d = jnp.maximum(dist, 0)
    exact = REL_BUCKETS // 2
    ratio = jnp.maximum(d, exact).astype(F32) / exact
    large = exact + (jnp.log(ratio) / math.log(REL_MAX_DIST / exact) * (REL_BUCKETS - exact)).astype(jnp.int32)
    return jnp.where(d < exact, d, jnp.minimum(large, REL_BUCKETS - 1))


def _compress(k, w1, b1, w2, b2):
    b, l, g, dk = k.shape
    ns = l // CMP_STRIDE
    r2 = CMP_LEN // CMP_STRIDE
    nc = ns - r2 + 1
    kb = k[:, :ns * CMP_STRIDE].reshape(b, ns, CMP_STRIDE, g, dk)
    w1r = w1.reshape(r2, CMP_STRIDE, dk, -1)
    part = jnp.einsum('bnsgd,hsde->hbnge', kb, w1r)
    hid = part[0, :, :nc] + b1
    for h in range(1, r2):
        hid = hid + part[h, :, h:h + nc]
    return jnp.einsum('bnge,ed->bngd', jax.nn.gelu(hid), w2) + b2


def _to_blocks(k):
    b, l, g, dk = k.shape
    nb = l // SEL_BLOCK
    return k.reshape(b, nb, SEL_BLOCK, g, dk).transpose(0, 3, 1, 2, 4).reshape(b, g, nb, SEL_BLOCK * dk)


def _slc_offsets():
    r1 = SEL_BLOCK // CMP_STRIDE
    r2 = CMP_LEN // CMP_STRIDE
    offs = np.arange(-(r2 - 1), r1)
    wts = np.array([sum(1 for m in range(r1) for n in range(r2) if m - n == o) for o in offs], np.float32)
    return offs, wts


def _nsa_project(p_n, qk_norm_g, nkv, ngrp):
    b, t, _ = p_n.shape
    nsa_w = nkv * ngrp * HEAD_DIM
    kvw = nkv * HEAD_DIM
    q = _rms(p_n[..., :nsa_w].reshape(b, t, nkv, ngrp, HEAD_DIM), qk_norm_g[0])
    kv = p_n[..., nsa_w:nsa_w + 6 * kvw].reshape(b, t, 6, nkv, HEAD_DIM)
    gates = jax.nn.sigmoid(p_n[..., nsa_w + 6 * kvw:nsa_w + 6 * kvw + 3 * nkv * ngrp]).reshape(b, t, nkv, ngrp, 3)
    rows = jnp.stack([kv[:, :, 0], kv[:, :, 1], _rms(kv[:, :, 2], qk_norm_g[2]), kv[:, :, 3]], axis=2)
    win = jnp.stack([_rms(kv[:, :, 4], qk_norm_g[3]), kv[:, :, 5]], axis=2)
    return q, gates, rows, win


def _nsa_context(rows, W):
    kc = _rms(_compress(rows[:, :, 0], W['cmp_w1'][0], W['cmp_b1'][0], W['cmp_w2'][0], W['cmp_b2'][0]), W['qk_norm_g'][1])
    vc = _compress(rows[:, :, 1], W['cmp_w1'][1], W['cmp_b1'][1], W['cmp_w2'][1], W['cmp_b2'][1])
    c_end = jnp.arange(kc.shape[1]) * CMP_STRIDE + CMP_LEN - 1
    return kc, vc, c_end, _to_blocks(rows[:, :, 2]), _to_blocks(rows[:, :, 3])


def _nsa_core(q, gates, q_pos, kc, vc, c_end, ks_blk, vs_blk, kw, vw, kw_pos, rel_bias):
    b, tq, g, r, dk = q.shape
    nb = ks_blk.shape[2]
    nc = kc.shape[1]
    scale = HEAD_DIM ** -0.5
    table = rel_bias.astype(F32).reshape(REL_BUCKETS, g, r)
    dist_c = q_pos[:, None] - c_end[None, :]
    bias_c = table[_rel_bucket(dist_c)].transpose(2, 3, 0, 1)
    lg_c = jnp.einsum('bqgrd,bcgd->bgrqc', q, kc) * scale + bias_c
    p_c = _masked_softmax(lg_c, dist_c >= 0)
    o_c = jnp.einsum('bgrqc,bcgd->bqgrd', p_c, vc)
    offs, wts = _slc_offsets()
    jb = jnp.arange(nb)
    cidx = (SEL_BLOCK // CMP_STRIDE) * jb[:, None] + offs[None, :]
    cval = (cidx >= 0) & (cidx < nc)
    p_grp = jnp.sum(p_c, axis=2)
    p_slc = jnp.sum(p_grp[..., jnp.clip(cidx, 0, nc - 1)] * (wts * cval), axis=-1)
    cur = q_pos // SEL_BLOCK
    forced = (jb[None] == 0) | (jb[None] == cur[:, None]) | (jb[None] == cur[:, None] - 1)
    future = jb[None] > cur[:, None]
    score = jnp.where(future, -1.0, jnp.where(forced, 1e6, p_slc))
    n_sel = min(SEL_TOPK, nb)
    _, sel = lax.top_k(score, n_sel)
    sel_ok = sel <= cur[:, None]
    bi = jnp.arange(b)[:, None, None]
    gi = jnp.arange(g)[None, :, None]
    flat = sel.reshape(b, g, tq * n_sel)
    ksg = ks_blk[bi, gi, flat].reshape(b, g, tq, n_sel * SEL_BLOCK, dk)
    vsg = vs_blk[bi, gi, flat].reshape(b, g, tq, n_sel * SEL_BLOCK, dk)
    pos4 = sel[..., None] * SEL_BLOCK + jnp.arange(SEL_BLOCK)
    ok_s = (sel_ok[..., None] & (pos4 <= q_pos[:, None, None])).reshape(b, g, tq, n_sel * SEL_BLOCK)
    pos_s = pos4.reshape(b, g, tq, n_sel * SEL_BLOCK)
    tg = table.transpose(1, 0, 2)
    bias_s = tg[jnp.arange(g)[None, :, None, None], _rel_bucket(q_pos[:, None] - pos_s)].transpose(0, 1, 4, 2, 3)
    lg_s = jnp.einsum('bqgrd,bgqkd->bgrqk', q, ksg) * scale + bias_s
    p_s = _masked_softmax(lg_s, ok_s[:, :, None])
    o_s = jnp.einsum('bgrqk,bgqkd->bqgrd', p_s, vsg)
    dist_w = q_pos[:, None] - kw_pos[None, :]
    ok_w = (dist_w >= 0) & (dist_w < WINDOW) & (kw_pos[None, :] >= 0)
    bias_w = table[_rel_bucket(dist_w)].transpose(2, 3, 0, 1)
    lg_w = jnp.einsum('bqgrd,bkgd->bgrqk', q, kw) * scale + bias_w
    p_w = _masked_softmax(lg_w, ok_w)
    o_w = jnp.einsum('bgrqk,bkgd->bqgrd', p_w, vw)
    o = gates[..., 0:1] * o_c + gates[..., 1:2] * o_s + gates[..., 2:3] * o_w
    return o.reshape(b, tq, g * r * dk)


def _nsa_prompt_jnp(p_n, W, nkv, ngrp):
    q, gates, rows, win = _nsa_project(p_n, W['qk_norm_g'], nkv, ngrp)
    b, s = q.shape[:2]
    kc, vc, c_end, ks_blk, vs_blk = _nsa_context(rows, W)
    win_pad = jnp.pad(win, ((0, 0), (WINDOW, 0), (0, 0), (0, 0), (0, 0)))
    qb_sz = 128

    def block(i):
        start = i * qb_sz
        qb = lax.dynamic_slice_in_dim(q, start, qb_sz, axis=1)
        gb = lax.dynamic_slice_in_dim(gates, start, qb_sz, axis=1)
        wb = lax.dynamic_slice_in_dim(win_pad, start, WINDOW + qb_sz, axis=1)
        q_pos = start + jnp.arange(qb_sz)
        kw_pos = start - WINDOW + jnp.arange(WINDOW + qb_sz)
        return _nsa_core(qb, gb, q_pos, kc, vc, c_end, ks_blk, vs_blk, wb[:, :, 0], wb[:, :, 1], kw_pos, W['rel_bias'])

    o = lax.map(block, jnp.arange(s // qb_sz))
    o = o.transpose(1, 0, 2, 3).reshape(b, s, -1)
    return o, rows, win[:, s - min(WINDOW, s):]


def _nsa_sample_jnp(p_n, cache_kv, cache_win, page_table, W, nkv, ngrp):
    q, gates, rows_new, win_new = _nsa_project(p_n, W['qk_norm_g'], nkv, ngrp)
    db, ds = q.shape[:2]
    past_len = page_table.shape[1] * PAGE_SIZE
    past_rows = cache_kv[page_table].reshape(db, past_len, 4, nkv, HEAD_DIM)
    pad = (-ds) % SEL_BLOCK
    rows = jnp.concatenate([past_rows, jnp.pad(rows_new, ((0, 0), (0, pad), (0, 0), (0, 0), (0, 0)))], axis=1)
    kc, vc, c_end, ks_blk, vs_blk = _nsa_context(rows, W)
    keep = cache_win.shape[1]
    win_all = jnp.concatenate([cache_win, win_new], axis=1)
    q_pos = past_len + jnp.arange(ds)
    kw_pos = past_len - keep + jnp.arange(keep + ds)
    o = _nsa_core(q, gates, q_pos, kc, vc, c_end, ks_blk, vs_blk, win_all[:, :, 0], win_all[:, :, 1], kw_pos, W['rel_bias'])
    n_keep = min(WINDOW, past_len + ds)
    return o, rows_new, win_all[:, win_all.shape[1] - n_keep:]


def _peer_jnp(h, W):
    n, d = h.shape
    sub_keys, expert_u, expert_v = W['peer_sub_keys'], W['peer_u'], W['peer_v']
    nk, half = sub_keys.shape[1], sub_keys.shape[2]
    nheads = W['peer_w_query'].shape[1] // (2 * half)
    cs = min(128, n)
    hc = h.reshape(-1, cs, d)

    def chunk(x):
        qh = (x @ W['peer_w_query']).reshape(cs, nheads, 2, half)
        s1 = jnp.einsum('chd,kd->chk', qh[:, :, 0], sub_keys[0])
        s2 = jnp.einsum('chd,kd->chk', qh[:, :, 1], sub_keys[1])
        v1, i1 = lax.top_k(s1, PEER_TOPK)
        v2, i2 = lax.top_k(s2, PEER_TOPK)
        cand = (v1[..., :, None] + v2[..., None, :]).reshape(cs, nheads, -1)
        cidx = (i1[..., :, None] * nk + i2[..., None, :]).reshape(cs, nheads, -1)
        top, pos = lax.top_k(cand, PEER_TOPK)
        eidx = jnp.take_along_axis(cidx, pos, axis=-1)
        gw = jax.nn.softmax(top, axis=-1)
        act = jax.nn.gelu(jnp.einsum('chkd,cd->chk', expert_u[eidx], x))
        return jnp.einsum('chk,chkd->cd', gw * act, expert_v[eidx])

    return lax.map(chunk, hc).reshape(-1, d)


def _layer(x, mods, nsa_fn, shift_prev, wkv0, W, tm, rwkv_chunk):
    b, t, d = x.shape
    cw = W['rwkv_w0'].shape[0]
    nw, na, ng = W['rwkv_w_up'].shape[0], W['rwkv_a_up'].shape[0], W['rwkv_g_up'].shape[0]
    rwkv_proj = 3 * cw + nw + na + ng
    sh1, sc1, ga1, sh2, sc2, ga2 = mods
    xf = x.reshape(b * t, d)
    if (b * t) % tm == 0 and t % tm == 0:
        as_mod = lambda m: m.reshape(b, 1, d)
    else:
        tm = b * t
        as_mod = lambda m: jnp.repeat(m, t, axis=0).reshape(1, b * t, d)
    w_r = _rwkv_pad_cols(W['w_in'][:, :rwkv_proj], cw, nw, na, ng).astype(BF16)
    nsa_cols = W['w_in'].shape[1] - rwkv_proj
    w_n = jnp.pad(W['w_in'][:, rwkv_proj:], ((0, 0), (0, _rup(nsa_cols, LANES) - nsa_cols))).astype(BF16)
    p_r, _ = _norm_mod_matmul(xf, W['norm1_g'], as_mod(sc1), as_mod(sh1), w_r, tm, 512)
    p_n, _ = _norm_mod_matmul(xf, W['norm1_g'], as_mod(sc1), as_mod(sh1), w_n, tm, w_n.shape[1] // 3)
    pr = p_r.shape[1]
    p_r = p_r.reshape(b, t, pr)
    shift_new = _rwkv_unpad_cols(p_r[:, -1], cw, nw, na, ng)
    tpad = _rup(t, rwkv_chunk)
    p_r_pad = jnp.pad(p_r, ((0, 0), (0, tpad - t), (0, 0)))
    y_r, wkv_new = _rwkv_mix(p_r_pad, shift_prev, wkv0, W, rwkv_chunk, t)
    y_r = y_r[:, :t].reshape(b * t, cw)
    y_n, rows, win = nsa_fn(p_n[:, :nsa_cols].reshape(b, t, nsa_cols))
    y_n = y_n.reshape(b * t, -1)
    w_out = W['w_out'].astype(BF16)
    x1 = _out_proj(xf, y_r, y_n, as_mod(ga1), w_out[:cw], w_out[cw:], tm, 512)
    _, h2 = _norm_mod_matmul(x1, W['norm2_g'], as_mod(sc2), as_mod(sh2), W['peer_w_query'].astype(BF16), tm, 512)
    pe = _peer_jnp(h2.astype(F32), W)
    ga2_tok = jnp.repeat(ga2, t, axis=0)
    out = x1 + ga2_tok * pe
    return out.reshape(b, t, d), rows, win, wkv_new, shift_new


def kernel(x_prompt, x_sample, c_prompt, c_sample, cache_kv, cache_win, state_wkv, state_shift, page_table,
           norm1_g, norm2_g, w_ada, b_ada, w_in, w_out,
           rwkv_mu, rwkv_w0, rwkv_w_up, rwkv_a0, rwkv_a_up, rwkv_g_up, rwkv_k_k, rwkv_k_a, rwkv_r_k, lnx_w, lnx_b,
           qk_norm_g, cmp_w1, cmp_b1, cmp_w2, cmp_b2, rel_bias,
           peer_w_query, peer_sub_keys, peer_u, peer_v):
    W = dict(norm1_g=norm1_g, norm2_g=norm2_g, w_ada=w_ada, b_ada=b_ada, w_in=w_in, w_out=w_out,
             rwkv_mu=rwkv_mu, rwkv_w0=rwkv_w0, rwkv_w_up=rwkv_w_up, rwkv_a0=rwkv_a0, rwkv_a_up=rwkv_a_up,
             rwkv_g_up=rwkv_g_up, rwkv_k_k=rwkv_k_k, rwkv_k_a=rwkv_k_a, rwkv_r_k=rwkv_r_k, lnx_w=lnx_w, lnx_b=lnx_b,
             qk_norm_g=qk_norm_g, cmp_w1=cmp_w1, cmp_b1=cmp_b1, cmp_w2=cmp_w2, cmp_b2=cmp_b2, rel_bias=rel_bias,
             peer_w_query=peer_w_query, peer_sub_keys=peer_sub_keys, peer_u=peer_u, peer_v=peer_v)
    bp, seq, d = x_prompt.shape
    db = x_sample.shape[0]
    nkv = cache_kv.shape[3]
    nh_r = rwkv_w0.shape[0] // HEAD_DIM
    ngrp = (w_out.shape[0] - rwkv_w0.shape[0]) // HEAD_DIM // nkv

    mods = _ada_mods(jnp.concatenate([c_prompt, c_sample], axis=0), w_ada, b_ada)
    mods = mods.reshape(bp + db, N_MODS, d)
    mods_p = [mods[:bp, i] for i in range(N_MODS)]
    mods_s = [mods[bp:, i] for i in range(N_MODS)]

    shift0 = jnp.zeros((bp, state_shift.shape[1]), F32)
    wkv0 = jnp.zeros((bp, nh_r, HEAD_DIM, HEAD_DIM), F32)
    y_p, rows_p, win_p, wkv_p, shift_p = _layer(
        x_prompt, mods_p, lambda pn: _nsa_prompt_jnp(pn, W, nkv, ngrp), shift0, wkv0, W, 512, 64)
    y_s, rows_s, win_s, wkv_s, shift_s = _layer(
        x_sample, mods_s, lambda pn: _nsa_sample_jnp(pn, cache_kv, cache_win, page_table, W, nkv, ngrp),
        state_shift, state_wkv, W, 512, 32)
    return (y_p, y_s, rows_p, win_p, wkv_p.astype(state_wkv.dtype), shift_p,
            rows_s, win_s, wkv_s.astype(state_wkv.dtype), shift_s)
```

```python
import functools
import math

import numpy as np
import jax
import jax.numpy as jnp
from jax import lax
from jax.experimental import pallas as pl
from jax.experimental.pallas import tpu as pltpu

F32 = jnp.float32
BF16 = jnp.bfloat16
HI = lax.Precision.HIGHEST

HEAD_DIM = 64
PAGE_SIZE = 128
CMP_LEN = 32
CMP_STRIDE = 16
SEL_BLOCK = 64
SEL_TOPK = 16
WINDOW = 512
REL_BUCKETS = 32
REL_MAX_DIST = 2048
PEER_TOPK = 16
N_MODS = 6
RMS_EPS = 1e-6
LNX_EPS = 64e-5
NEG = -1e30

LANES = 128
MXU = 256
HEADS_PER_GROUP = MXU // HEAD_DIM
VMEM_LIMIT = 56 * 1024 * 1024


def _cparams(sem):
    return pltpu.CompilerParams(dimension_semantics=sem, vmem_limit_bytes=VMEM_LIMIT)


def _dot(a, b, precision=None):
    return jnp.dot(a, b, preferred_element_type=F32, precision=precision)


def _dot_nt(a, b, precision=None):
    return lax.dot_general(a, b, (((1,), (1,)), ((), ())), preferred_element_type=F32, precision=precision)


def _ada_kernel(c_ref, w_ref, b_ref, o_ref):
    c = c_ref[...]
    s = c * jax.nn.sigmoid(c)
    o_ref[...] = _dot(s.astype(BF16), w_ref[...].astype(BF16)) + b_ref[...]


def _ada_mods(c, w_ada, b_ada):
    n, d = c.shape
    cols = w_ada.shape[1]
    tn = 1024
    return pl.pallas_call(
        _ada_kernel,
        grid=(cols // tn,),
        in_specs=[pl.BlockSpec((n, d), lambda j: (0, 0)),
                  pl.BlockSpec((d, tn), lambda j: (0, j)),
                  pl.BlockSpec((1, tn), lambda j: (0, j))],
        out_specs=pl.BlockSpec((n, tn), lambda j: (0, j)),
        out_shape=jax.ShapeDtypeStruct((n, cols), F32),
        compiler_params=_cparams(("arbitrary",)),
        name="ada_mods",
    )(c, w_ada, b_ada.reshape(1, cols))


def _nmm_kernel(x_ref, g_ref, sc_ref, sh_ref, w_ref, o_ref, h_ref):
    @pl.when(pl.program_id(1) == 0)
    def _():
        x = x_ref[...]
        ms = jnp.mean(x * x, axis=-1, keepdims=True)
        y = x * lax.rsqrt(ms + RMS_EPS) * g_ref[...]
        h_ref[...] = (y * (1.0 + sc_ref[0]) + sh_ref[0]).astype(h_ref.dtype)

    o_ref[...] = _dot(h_ref[...], w_ref[...])


def _norm_mod_matmul(x, g, sc, sh, w, tm, tn):
    t, d = x.shape
    n = w.shape[1]
    nmod, rows, _ = sc.shape
    tiles_per_mod = (t // tm) // nmod
    mod_spec = pl.BlockSpec((1, rows, d), lambda i, j: (i // tiles_per_mod, 0, 0))
    return pl.pallas_call(
        _nmm_kernel,
        grid=(t // tm, n // tn),
        in_specs=[pl.BlockSpec((tm, d), lambda i, j: (i, 0)),
                  pl.BlockSpec((1, d), lambda i, j: (0, 0)),
                  mod_spec, mod_spec,
                  pl.BlockSpec((d, tn), lambda i, j: (0, j))],
        out_specs=[pl.BlockSpec((tm, tn), lambda i, j: (i, j)),
                   pl.BlockSpec((tm, d), lambda i, j: (i, 0))],
        out_shape=[jax.ShapeDtypeStruct((t, n), F32), jax.ShapeDtypeStruct((t, d), BF16)],
        compiler_params=_cparams(("arbitrary", "arbitrary")),
        name="norm_mod_matmul",
    )(x, g.reshape(1, d), sc, sh, w)


def _nm_kernel(x_ref, g_ref, sc_ref, sh_ref, h_ref):
    x = x_ref[...]
    ms = jnp.mean(x * x, axis=-1, keepdims=True)
    y = x * lax.rsqrt(ms + RMS_EPS) * g_ref[...]
    h_ref[...] = (y * (1.0 + sc_ref[0]) + sh_ref[0]).astype(h_ref.dtype)


def _norm_mod(x, g, sc, sh, tm):
    t, d = x.shape
    nmod, rows, _ = sc.shape
    tiles_per_mod = (t // tm) // nmod
    mod_spec = pl.BlockSpec((1, rows, d), lambda i: (i // tiles_per_mod, 0, 0))
    return pl.pallas_call(
        _nm_kernel,
        grid=(t // tm,),
        in_specs=[pl.BlockSpec((tm, d), lambda i: (i, 0)), pl.BlockSpec((1, d), lambda i: (0, 0)), mod_spec, mod_spec],
        out_specs=pl.BlockSpec((tm, d), lambda i: (i, 0)),
        out_shape=jax.ShapeDtypeStruct((t, d), BF16),
        compiler_params=_cparams(("arbitrary",)),
        name="norm_mod",
    )(x, g.reshape(1, d), sc, sh)


def _outproj_kernel(x_ref, yr_ref, yn_ref, ga_ref, w1_ref, w2_ref, o_ref):
    acc = _dot(yr_ref[...].astype(BF16), w1_ref[...]) + _dot(yn_ref[...].astype(BF16), w2_ref[...])
    o_ref[...] = x_ref[...] + ga_ref[0] * acc


def _out_proj(x, y_r, y_n, ga, w1, w2, tm, tn):
    t, d = x.shape
    nmod, rows, _ = ga.shape
    tiles_per_mod = (t // tm) // nmod
    cr, cn = y_r.shape[1], y_n.shape[1]
    return pl.pallas_call(
        _outproj_kernel,
        grid=(t // tm, d // tn),
        in_specs=[pl.BlockSpec((tm, tn), lambda i, j: (i, j)),
                  pl.BlockSpec((tm, cr), lambda i, j: (i, 0)),
                  pl.BlockSpec((tm, cn), lambda i, j: (i, 0)),
                  pl.BlockSpec((1, rows, tn), lambda i, j: (i // tiles_per_mod, 0, j)),
                  pl.BlockSpec((cr, tn), lambda i, j: (0, j)),
                  pl.BlockSpec((cn, tn), lambda i, j: (0, j))],
        out_specs=pl.BlockSpec((tm, tn), lambda i, j: (i, j)),
        out_shape=jax.ShapeDtypeStruct((t, d), F32),
        compiler_params=_cparams(("arbitrary", "arbitrary")),
        name="out_proj",
    )(x, y_r, y_n, ga, w1, w2)


def _softplus(z):
    return jnp.maximum(z, 0.0) + jnp.log(1.0 + jnp.exp(-jnp.abs(z)))


def _rwkv_kernel(p_ref, shift_ref, s0_ref, mu_ref, vec_ref, wup_ref, aup_ref, gup_ref,
                 y_ref, sfin_ref, carry, state, *, t_valid, n_chunks):
    c = pl.program_id(1)
    chunk = p_ref.shape[1]
    cw = vec_ref.shape[1]
    n_groups = cw // MXU
    hg = HEADS_PER_GROUP
    rows_g = hg * chunk

    @pl.when(c == 0)
    def _():
        carry[...] = shift_ref[0]
        state[...] = s0_ref[0]

    p = p_ref[0]
    row = lax.broadcasted_iota(jnp.int32, (chunk, 1), 0)
    prev = jnp.where(row == 0, carry[...], pltpu.roll(p, 1, axis=0))
    carry[...] = p[chunk - 1:chunk, :]
    xs = p + (prev - p) * mu_ref[...]

    w0, a0, k_k, k_a, r_k, lnx_w, lnx_b = (vec_ref[i:i + 1, :] for i in range(7))
    r = xs[:, 0:cw]
    k = xs[:, cw:2 * cw]
    v = xs[:, 2 * cw:3 * cw]
    o = 3 * cw
    nw, na, ng = wup_ref.shape[0], aup_ref.shape[0], gup_ref.shape[0]
    xw = xs[:, o:o + nw]
    xa = xs[:, o + nw:o + nw + na]
    xg = xs[:, o + nw + na:o + nw + na + ng]
    w_log = -_softplus(-(w0 + _dot(jnp.tanh(xw), wup_ref[...], HI))) - 0.5
    a = jax.nn.sigmoid(a0 + _dot(xa, aup_ref[...], HI))
    gate = _dot(jax.nn.sigmoid(xg), gup_ref[...], HI)

    gi = lax.broadcasted_iota(jnp.int32, (MXU, MXU), 0) // HEAD_DIM
    gj = lax.broadcasted_iota(jnp.int32, (MXU, MXU), 1) // HEAD_DIM
    ones_bd = (gi == gj).astype(F32)

    def head_sum(x):
        return jnp.concatenate([_dot(x[:, g * MXU:(g + 1) * MXU], ones_bd, HI) for g in range(n_groups)], axis=1)

    kk = k * k_k
    kk = kk / jnp.maximum(jnp.sqrt(head_sum(kk * kk)), 1e-12)
    k2 = k * (1.0 + (a - 1.0) * k_a)
    logdec = -jnp.exp(w_log)
    if t_valid < chunk * n_chunks:
        valid = (row + c * chunk) < t_valid
        logdec = jnp.where(valid, logdec, 0.0)
        kk = jnp.where(valid, kk, 0.0)
        k2 = jnp.where(valid, k2, 0.0)
        v = jnp.where(valid, v, 0.0)

    ti = lax.broadcasted_iota(jnp.int32, (chunk, chunk), 0)
    tj = lax.broadcasted_iota(jnp.int32, (chunk, chunk), 1)
    cum = _dot((tj <= ti).astype(F32), logdec, HI)
    cum_end = cum[chunk - 1:chunk, :]
    e_neg = jnp.exp(-cum)
    e_rem = jnp.exp(cum_end - cum)
    r_t = r * jnp.exp(cum)
    a_t = -kk * jnp.exp(cum - logdec)
    b_vec = kk * a
    b_t = b_vec * e_neg
    k_t = k2 * e_neg
    b_rem = b_vec * e_rem
    k_rem = k2 * e_rem
    w_end = jnp.exp(cum_end)

    lane_head = lax.broadcasted_iota(jnp.int32, (chunk, MXU), 1) // HEAD_DIM
    ri = lax.broadcasted_iota(jnp.int32, (rows_g, rows_g), 0)
    rj = lax.broadcasted_iota(jnp.int32, (rows_g, rows_g), 1)
    strict = rj < ri
    incl = rj <= ri
    eye_r = (ri == rj).astype(F32)
    di = lax.broadcasted_iota(jnp.int32, (MXU, MXU), 0)
    dj = lax.broadcasted_iota(jnp.int32, (MXU, MXU), 1)
    n_double = max(int(math.ceil(math.log2(chunk))) - 1, 0)

    def bd(x):
        return jnp.concatenate([jnp.where(lane_head == h, x, 0.0) for h in range(hg)], axis=0)

    def stack(x):
        return jnp.concatenate([x[:, h * HEAD_DIM:(h + 1) * HEAD_DIM] for h in range(hg)], axis=0)

    def unstack(x):
        return jnp.concatenate([x[h * chunk:(h + 1) * chunk, :] for h in range(hg)], axis=1)

    ys = []
    for g in range(n_groups):
        sl = slice(g * MXU, (g + 1) * MXU)
        a_bd, r_bd = bd(a_t[:, sl]), bd(r_t[:, sl])
        b_bd, k_bd = bd(b_t[:, sl]), bd(k_t[:, sl])
        v_st = stack(v[:, sl])
        a_ab = jnp.where(strict, _dot_nt(a_bd, b_bd, HI), 0.0)
        a_ak = jnp.where(strict, _dot_nt(a_bd, k_bd, HI), 0.0)
        a_rb = jnp.where(incl, _dot_nt(r_bd, b_bd, HI), 0.0)
        a_rk = jnp.where(incl, _dot_nt(r_bd, k_bd, HI), 0.0)
        tinv = eye_r + a_ab
        pw = a_ab
        for _ in range(n_double):
            pw = _dot(pw, pw, HI)
            tinv = tinv + _dot(pw, tinv, HI)
        s0 = state[g]
        z = _dot(a_bd, s0, HI) + _dot(a_ak, v_st, HI)
        u = _dot(tinv, z, HI)
        y_st = _dot(r_bd, s0, HI) + _dot(a_rb, u, HI) + _dot(a_rk, v_st, HI)
        decay_diag = jnp.where(di == dj, jnp.broadcast_to(w_end[:, sl], (MXU, MXU)), 0.0)
        state[g] = (_dot(decay_diag, s0, HI) + _dot(bd(b_rem[:, sl]).T, u, HI)
                    + _dot(bd(k_rem[:, sl]).T, v_st, HI))
        ys.append(unstack(y_st))
    y = jnp.concatenate(ys, axis=1)

    inv_n = 1.0 / HEAD_DIM
    mean = head_sum(y) * inv_n
    d = y - mean
    var = head_sum(d * d) * inv_n
    yn = d * lax.rsqrt(var + LNX_EPS) * lnx_w + lnx_b
    bonus = head_sum(r * k2 * r_k) * v
    y_ref[0] = ((yn + bonus) * gate).astype(y_ref.dtype)

    @pl.when(c == n_chunks - 1)
    def _():
        sfin_ref[0] = state[...]


def _rwkv_pad_cols(x, cw, nw, na, ng):
    o = 3 * cw
    parts = [x[..., :o + nw], x[..., o + nw:o + nw + na], x[..., o + nw + na:]]
    widths = [o + _rup(nw, LANES), _rup(na, LANES), _rup(ng, LANES)]
    out = []
    for part, wd in zip(parts, widths):
        pad = [(0, 0)] * (x.ndim - 1) + [(0, wd - part.shape[-1])]
        out.append(jnp.pad(part, pad))
    return jnp.concatenate(out, axis=-1)


def _rwkv_unpad_cols(x, cw, nw, na, ng):
    o = 3 * cw
    o2 = o + _rup(nw, LANES)
    o3 = o2 + _rup(na, LANES)
    return jnp.concatenate([x[..., :o + nw], x[..., o2:o2 + na], x[..., o3:o3 + ng]], axis=-1)


def _rup(x, m):
    return (x + m - 1) // m * m


def _rwkv_mix(p_r, shift_prev, wkv0, W, chunk, t_valid):
    b, tpad, pr = p_r.shape
    cw = W['rwkv_w0'].shape[0]
    nh = cw // HEAD_DIM
    n_groups = cw // MXU
    nw, na, ng = W['rwkv_w_up'].shape[0], W['rwkv_a_up'].shape[0], W['rwkv_g_up'].shape[0]
    n_chunks = tpad // chunk
    mu = _rwkv_pad_cols(W['rwkv_mu'], cw, nw, na, ng).reshape(1, pr)
    vecs = jnp.stack([W['rwkv_w0'], W['rwkv_a0'], W['rwkv_k_k'], W['rwkv_k_a'], W['rwkv_r_k'].reshape(cw),
                      W['lnx_w'], W['lnx_b'], jnp.zeros((cw,), F32)])
    wup = jnp.pad(W['rwkv_w_up'], ((0, _rup(nw, LANES) - nw), (0, 0)))
    aup = jnp.pad(W['rwkv_a_up'], ((0, _rup(na, LANES) - na), (0, 0)))
    gup = jnp.pad(W['rwkv_g_up'], ((0, _rup(ng, LANES) - ng), (0, 0)))
    shift3 = _rwkv_pad_cols(shift_prev, cw, nw, na, ng).reshape(b, 1, pr)
    s0 = wkv0.astype(F32).transpose(0, 1, 3, 2).reshape(b, n_groups, MXU, HEAD_DIM)
    const = lambda shape: pl.BlockSpec(shape, lambda i, c: (0,) * len(shape))
    y, sfin = pl.pallas_call(
        functools.partial(_rwkv_kernel, t_valid=t_valid, n_chunks=n_chunks),
        grid=(b, n_chunks),
        in_specs=[pl.BlockSpec((1, chunk, pr), lambda i, c: (i, c, 0)),
                  pl.BlockSpec((1, 1, pr), lambda i, c: (i, 0, 0)),
                  pl.BlockSpec((1, n_groups, MXU, HEAD_DIM), lambda i, c: (i, 0, 0, 0)),
                  const((1, pr)), const((8, cw)), const(wup.shape), const(aup.shape), const(gup.shape)],
        out_specs=[pl.BlockSpec((1, chunk, cw), lambda i, c: (i, c, 0)),
                   pl.BlockSpec((1, n_groups, MXU, HEAD_DIM), lambda i, c: (i, 0, 0, 0))],
        out_shape=[jax.ShapeDtypeStruct((b, tpad, cw), F32),
                   jax.ShapeDtypeStruct((b, n_groups, MXU, HEAD_DIM), F32)],
        scratch_shapes=[pltpu.VMEM((1, pr), F32), pltpu.VMEM((n_groups, MXU, HEAD_DIM), F32)],
        compiler_params=_cparams(("arbitrary", "arbitrary")),
        name="rwkv_mix",
    )(p_r, shift3, s0, mu, vecs, wup, aup, gup)
    s_fin = sfin.reshape(b, nh, HEAD_DIM, HEAD_DIM).transpose(0, 1, 3, 2)
    return y, s_fin


def _head_ones():
    gi = lax.broadcasted_iota(jnp.int32, (MXU, MXU), 0) // HEAD_DIM
    gj = lax.broadcasted_iota(jnp.int32, (MXU, MXU), 1) // HEAD_DIM
    return (gi == gj).astype(F32)


def _nsa_proj_kernel(p_ref, g_ref, q_ref, rows_ref, win_ref, gate_ref, *, nsa_w, kvw):
    ones_bd = _head_ones()

    def hnorm(x, gvec):
        ms = _dot(x * x, ones_bd, HI) * (1.0 / HEAD_DIM)
        return x * lax.rsqrt(ms + RMS_EPS) * gvec

    for i in range(nsa_w // MXU):
        sl = slice(i * MXU, (i + 1) * MXU)
        q_ref[:, sl] = hnorm(p_ref[:, sl], g_ref[0:1, :])
    o = nsa_w
    rows_ref[:, 0:2 * kvw] = p_ref[:, o:o + 2 * kvw]
    rows_ref[:, 2 * kvw:3 * kvw] = hnorm(p_ref[:, o + 2 * kvw:o + 3 * kvw], g_ref[2:3, :])
    rows_ref[:, 3 * kvw:4 * kvw] = p_ref[:, o + 3 * kvw:o + 4 * kvw]
    win_ref[:, 0:kvw] = hnorm(p_ref[:, o + 4 * kvw:o + 5 * kvw], g_ref[3:4, :])
    win_ref[:, kvw:2 * kvw] = p_ref[:, o + 5 * kvw:o + 6 * kvw]
    gate_ref[...] = jax.nn.sigmoid(p_ref[:, o + 6 * kvw:])


def _nsa_project_call(p_n, qk_norm_g, nsa_w, kvw, tm):
    t, pc = p_n.shape
    assert kvw == MXU and nsa_w % MXU == 0
    gcols = pc - nsa_w - 6 * kvw
    gvec = jnp.tile(qk_norm_g, (1, MXU // HEAD_DIM))
    return pl.pallas_call(
        functools.partial(_nsa_proj_kernel, nsa_w=nsa_w, kvw=kvw),
        grid=(t // tm,),
        in_specs=[pl.BlockSpec((tm, pc), lambda i: (i, 0)),
                  pl.BlockSpec(gvec.shape, lambda i: (0, 0))],
        out_specs=[pl.BlockSpec((tm, nsa_w), lambda i: (i, 0)),
                   pl.BlockSpec((tm, 4 * kvw), lambda i: (i, 0)),
                   pl.BlockSpec((tm, 2 * kvw), lambda i: (i, 0)),
                   pl.BlockSpec((tm, gcols), lambda i: (i, 0))],
        out_shape=[jax.ShapeDtypeStruct((t, nsa_w), F32), jax.ShapeDtypeStruct((t, 4 * kvw), F32),
                   jax.ShapeDtypeStruct((t, 2 * kvw), F32), jax.ShapeDtypeStruct((t, gcols), F32)],
        compiler_params=_cparams(("arbitrary",)),
        name="nsa_project",
    )(p_n, gvec)


def _cmp_part_kernel(*refs, n_in, row_w, kvw):
    x_refs, w_ref, o_ref = refs[:n_in], refs[n_in], refs[n_in + 1]
    for typ in range(2):
        acc = None
        for s in range(CMP_STRIDE):
            lo = s * row_w + typ * kvw
            xs = jnp.concatenate([x[0, :, lo:lo + kvw] for x in x_refs], axis=0) if n_in > 1 else x_refs[0][0, :, lo:lo + kvw]
            d = _dot(xs.astype(BF16), w_ref[typ, s])
            acc = d if acc is None else acc + d
        o_ref[0, :, typ * 2 * kvw:(typ + 1) * 2 * kvw] = acc


def _cmp_first_weights(cmp_w1, nkv):
    r2 = CMP_LEN // CMP_STRIDE
    e = cmp_w1.shape[-1]
    w1r = cmp_w1.reshape(2, r2, CMP_STRIDE, HEAD_DIM, e)
    eye = jnp.eye(nkv, dtype=F32)
    big = jnp.einsum('yhsde,gk->ysgdkhe', w1r, eye)
    return big.reshape(2, CMP_STRIDE, nkv * HEAD_DIM, nkv * r2 * e).astype(BF16)


def _cmp_parts_prompt(rows2d, w_big, b, t, kvw):
    row_w = rows2d.shape[2]
    nsub = t // CMP_STRIDE
    blk = min(nsub, LANES)
    return pl.pallas_call(
        functools.partial(_cmp_part_kernel, n_in=1, row_w=row_w, kvw=kvw),
        grid=(b, nsub // blk),
        in_specs=[pl.BlockSpec((1, blk, CMP_STRIDE * row_w), lambda i, j: (i, j, 0)),
                  pl.BlockSpec(w_big.shape, lambda i, j: (0, 0, 0, 0))],
        out_specs=pl.BlockSpec((1, blk, 4 * kvw), lambda i, j: (i, j, 0)),
        out_shape=jax.ShapeDtypeStruct((b, nsub, 4 * kvw), F32),
        compiler_params=_cparams(("arbitrary", "arbitrary")),
        name="cmp_parts",
    )(rows2d.reshape(b, nsub, CMP_STRIDE * row_w), w_big)


def _cmp_finish_kernel(p_ref, b1_ref, w2_ref, b2_ref, g_ref, o_ref, *, nc, nkv):
    tg = pl.program_id(1)
    ns = p_ref.shape[1]
    e = p_ref.shape[2] // 2
    part = p_ref[0]
    nxt = pltpu.roll(part[:, e:], ns - 1, axis=0)
    hid = part[:, :e] + nxt + b1_ref[0]
    out = _dot(jax.nn.gelu(hid).astype(BF16), w2_ref[0].astype(BF16)) + b2_ref[0]
    normed = out * lax.rsqrt(jnp.mean(out * out, axis=-1, keepdims=True) + RMS_EPS) * g_ref[...]
    out = jnp.where(tg < nkv, normed, out)
    row = lax.broadcasted_iota(jnp.int32, (ns, 1), 0)
    o_ref[0, 0] = jnp.where(row < nc, out, 0.0)


def _cmp_finish(parts, cmp_b1, cmp_w2, cmp_b2, g1, nc, nkv):
    b, ns, _ = parts.shape
    e = cmp_b1.shape[1]
    return pl.pallas_call(
        functools.partial(_cmp_finish_kernel, nc=nc, nkv=nkv),
        grid=(b, 2 * nkv),
        in_specs=[pl.BlockSpec((1, ns, 2 * e), lambda i, j: (i, 0, j)),
                  pl.BlockSpec((1, 1, e), lambda i, j: (j // nkv, 0, 0)),
                  pl.BlockSpec((1, e, HEAD_DIM), lambda i, j: (j // nkv, 0, 0)),
                  pl.BlockSpec((1, 1, HEAD_DIM), lambda i, j: (j // nkv, 0, 0)),
                  pl.BlockSpec((1, HEAD_DIM), lambda i, j: (0, 0))],
        out_specs=pl.BlockSpec((1, 1, ns, HEAD_DIM), lambda i, j: (i, j, 0, 0)),
        out_shape=jax.ShapeDtypeStruct((b, 2 * nkv, ns, HEAD_DIM), F32),
        compiler_params=_cparams(("arbitrary", "arbitrary")),
        name="cmp_finish",
    )(parts, cmp_b1.reshape(2, 1, e), cmp_w2, cmp_b2.reshape(2, 1, HEAD_DIM), g1.reshape(1, HEAD_DIM))


QT = 128


def _rel_table_np_dist(dist, table):
    return table[_rel_bucket(dist)]


def _softmax_update(s, mask, m, l):
    m_new = jnp.maximum(m, jnp.max(jnp.where(mask, s, NEG), axis=0, keepdims=True))
    alpha = jnp.exp(m - m_new)
    p = jnp.where(mask, jnp.exp(s - m_new), 0.0)
    return p, m_new, alpha, alpha * l + jnp.sum(p, axis=0, keepdims=True)


def _rank_select(score, score_ref, cur, n_sel):
    nb = score.shape[0]
    score_ref[0:nb, :] = score
    jrow = lax.broadcasted_iota(jnp.int32, score.shape, 0)
    rank = jnp.zeros(score.shape, F32)
    for j in range(nb):
        other = score_ref[j:j + 1, :]
        beats = (other > score) | ((other == score) & (jrow > j))
        rank = rank + jnp.where(beats, 1.0, 0.0)
    return jnp.where((rank < n_sel) & (jrow <= cur), 1.0, 0.0)


def _nsa_prompt_kernel(qT_ref, gT_ref, kc_ref, vcT_ref, ks_ref, vsT_ref, kw_ref, vwT_ref, bc_ref, toep_ref,
                       o_ref, pg_ref, score_ref, sel_ref, *, nc, nb, n_sel, ngrp):
    qt = pl.program_id(2)
    scale = HEAD_DIM ** -0.5
    lanes = ngrp * QT
    q = qT_ref[0, 0, 0].astype(BF16)
    iq = lax.broadcasted_iota(jnp.int32, (1, QT), 1)
    q_pos = qt * QT + iq
    tile4 = lambda x: jnp.concatenate([x] * ngrp, axis=1)

    ncp = kc_ref.shape[2]
    s = _dot(kc_ref[0, 0].astype(BF16), q) * scale
    s = s + jnp.concatenate([bc_ref[0, r] for r in range(ngrp)], axis=1)
    crow = lax.broadcasted_iota(jnp.int32, (ncp, QT), 0)
    ok_c = tile4((crow * CMP_STRIDE + (CMP_LEN - 1) <= q_pos) & (crow < nc))
    p, _, _, l = _softmax_update(s, ok_c, jnp.full((1, lanes), NEG, F32), jnp.zeros((1, lanes), F32))
    p = p * jnp.where(l > 0.0, 1.0 / jnp.where(l > 0.0, l, 1.0), 0.0)
    o_c = _dot(vcT_ref[0, 0].astype(BF16), p.astype(BF16))

    p_grp = p[:, 0:QT]
    for r in range(1, ngrp):
        p_grp = p_grp + p[:, r * QT:(r + 1) * QT]
    pad = 8
    pg_ref[...] = jnp.zeros(pg_ref.shape, F32)
    pg_ref[pad:pad + ncp, :] = p_grp
    r1 = SEL_BLOCK // CMP_STRIDE
    offs, wts = _slc_offsets()
    p_slc = None
    for o, wt in zip(offs, wts):
        term = float(wt) * pg_ref[pl.ds(pad + int(o), nb, stride=r1), :]
        p_slc = term if p_slc is None else p_slc + term
    jrow = lax.broadcasted_iota(jnp.int32, (nb, QT), 0)
    cur = q_pos // SEL_BLOCK
    forced = (jrow == 0) | (jrow == cur) | (jrow == cur - 1)
    score = jnp.where(jrow > cur, -1.0, jnp.where(forced, 1e6, p_slc))
    sel_ref[0:nb, :] = _rank_select(score, score_ref, cur, n_sel)

    ik = lax.broadcasted_iota(jnp.int32, (QT, QT), 0)
    iqq = lax.broadcasted_iota(jnp.int32, (QT, QT), 1)
    blocks_per_tile = QT // SEL_BLOCK

    def attend(kt, carry, k_ref, vT_ref, mask_fn):
        m, l, acc = carry
        delta = qt - kt
        s = _dot(k_ref[0, 0, kt].astype(BF16), q) * scale
        s = s + jnp.concatenate([toep_ref[0, r, delta] for r in range(ngrp)], axis=1)
        mask = tile4(mask_fn(kt, delta))
        p, m, alpha, l = _softmax_update(s, mask, m, l)
        acc = alpha * acc + _dot(vT_ref[0, 0, kt].astype(BF16), p.astype(BF16))
        return m, l, acc

    def sel_mask(kt, delta):
        rows = [jnp.broadcast_to(sel_ref[pl.ds(kt * blocks_per_tile + i, 1), :], (SEL_BLOCK, QT))
                for i in range(blocks_per_tile)]
        chosen = jnp.concatenate(rows, axis=0) > 0.5
        return chosen & (ik - iqq <= delta * QT)

    def win_mask(kt, delta):
        dist = delta * QT + iqq - ik
        return (dist >= 0) & (dist < WINDOW)

    init = (jnp.full((1, lanes), NEG, F32), jnp.zeros((1, lanes), F32), jnp.zeros((HEAD_DIM, lanes), F32))
    finish = lambda c: c[2] * jnp.where(c[1] > 0.0, 1.0 / jnp.where(c[1] > 0.0, c[1], 1.0), 0.0)
    o_s = finish(lax.fori_loop(0, qt + 1, lambda kt, c: attend(kt, c, ks_ref, vsT_ref, sel_mask), init))
    first = jnp.maximum(qt - WINDOW // QT, 0)
    o_w = finish(lax.fori_loop(first, qt + 1, lambda kt, c: attend(kt, c, kw_ref, vwT_ref, win_mask), init))
    g = gT_ref[0, 0, 0]
    o_ref[0, 0, 0] = g[0:1, :] * o_c + g[1:2, :] * o_s + g[2:3, :] * o_w


def _nsa_prompt(p_n, W, b, t, nkv, ngrp, tm):
    nsa_w = nkv * ngrp * HEAD_DIM
    kvw = nkv * HEAD_DIM
    qn, rows2d, win2d, gates = _nsa_project_call(p_n, W['qk_norm_g'], nsa_w, kvw, tm)
    rows = rows2d.reshape(b, t, 4, nkv, HEAD_DIM)
    win = win2d.reshape(b, t, 2, nkv, HEAD_DIM)
    ns = t // CMP_STRIDE
    nc = ns - CMP_LEN // CMP_STRIDE + 1
    nb = t // SEL_BLOCK
    n_sel = min(SEL_TOPK, nb)
    nqt = t // QT
    parts = _cmp_parts_prompt(rows2d.reshape(b, t, 4 * kvw), _cmp_first_weights(W['cmp_w1'], nkv), b, t, kvw)
    kvc = _cmp_finish(parts, W['cmp_b1'], W['cmp_w2'], W['cmp_b2'], W['qk_norm_g'][1], nc, nkv)
    kc = kvc[:, :nkv]
    vcT = kvc[:, nkv:].transpose(0, 1, 3, 2)
    qT = qn.reshape(b, nqt, QT, nkv, ngrp, HEAD_DIM).transpose(0, 3, 1, 5, 4, 2).reshape(b, nkv, nqt, HEAD_DIM, ngrp * QT)
    ng = 3 * nkv * ngrp
    gT = gates[:, :ng].reshape(b, nqt, QT, nkv, ngrp, 3).transpose(0, 3, 1, 5, 4, 2).reshape(b, nkv, nqt, 3, ngrp * QT)
    gT = jnp.pad(gT, ((0, 0), (0, 0), (0, 0), (0, 5), (0, 0)))
    k_tiles = lambda x: x.transpose(0, 2, 1, 3).reshape(b, nkv, nqt, QT, HEAD_DIM).astype(BF16)
    vT_tiles = lambda x: x.reshape(b, nqt, QT, nkv, HEAD_DIM).transpose(0, 3, 1, 4, 2).astype(BF16)
    ks, vsT = k_tiles(rows[:, :, 2]), vT_tiles(rows[:, :, 3])
    kw, vwT = k_tiles(win[:, :, 0]), vT_tiles(win[:, :, 1])
    table = W['rel_bias'].astype(F32)
    c_end = jnp.arange(ns) * CMP_STRIDE + CMP_LEN - 1
    bias_c = _rel_table_np_dist(jnp.arange(t)[None, :] - c_end[:, None], table)
    bias_c = bias_c.transpose(2, 0, 1).reshape(nkv, ngrp, ns, t)
    dd = (jnp.arange(nqt)[:, None, None] * QT + jnp.arange(QT)[None, None, :] - jnp.arange(QT)[None, :, None])
    toep = _rel_table_np_dist(dd, table).transpose(3, 0, 1, 2).reshape(nkv, ngrp, nqt, QT, QT)
    lanes = ngrp * QT
    kv_spec = lambda shape: pl.BlockSpec((1, 1) + shape, lambda i, g, j: (i, g) + (0,) * len(shape))
    yT = pl.pallas_call(
        functools.partial(_nsa_prompt_kernel, nc=nc, nb=nb, n_sel=n_sel, ngrp=ngrp),
        grid=(b, nkv, nqt),
        in_specs=[pl.BlockSpec((1, 1, 1, HEAD_DIM, lanes), lambda i, g, j: (i, g, j, 0, 0)),
                  pl.BlockSpec((1, 1, 1, 8, lanes), lambda i, g, j: (i, g, j, 0, 0)),
                  kv_spec((ns, HEAD_DIM)), kv_spec((HEAD_DIM, ns)),
                  kv_spec((nqt, QT, HEAD_DIM)), kv_spec((nqt, HEAD_DIM, QT)),
                  kv_spec((nqt, QT, HEAD_DIM)), kv_spec((nqt, HEAD_DIM, QT)),
                  pl.BlockSpec((1, ngrp, ns, QT), lambda i, g, j: (g, 0, 0, j)),
                  pl.BlockSpec((1, ngrp, nqt, QT, QT), lambda i, g, j: (g, 0, 0, 0, 0))],
        out_specs=pl.BlockSpec((1, 1, 1, HEAD_DIM, lanes), lambda i, g, j: (i, g, j, 0, 0)),
        out_shape=jax.ShapeDtypeStruct((b, nkv, nqt, HEAD_DIM, lanes), F32),
        scratch_shapes=[pltpu.VMEM((ns + 16, QT), F32), pltpu.VMEM((_rup(nb, 8), QT), F32),
                        pltpu.VMEM((_rup(nb, 8), QT), F32)],
        compiler_params=_cparams(("arbitrary", "arbitrary", "arbitrary")),
        name="nsa_prompt_attn",
    )(qT, gT, kc, vcT, ks, vsT, kw, vwT, bias_c, toep)
    y = yT.reshape(b, nkv, nqt, HEAD_DIM, ngrp, QT).transpose(0, 2, 5, 1, 4, 3).reshape(b * t, nsa_w)
    return y, rows, win[:, t - min(WINDOW, t):]


def _rms(x, g):
    return x * lax.rsqrt(jnp.mean(x * x, axis=-1, keepdims=True) + RMS_EPS) * g


def _masked_softmax(logits, mask):
    z = jnp.where(mask, logits, NEG)
    return jax.nn.softmax(z, axis=-1) * jnp.any(mask, axis=-1, keepdims=True)


def _rel_bucket(dist):
    d = jnp.maximum(dist, 0)
    exact = REL_BUCKETS // 2
    ratio = jnp.maximum(d, exact).astype(F32) / exact
    large = exact + (jnp.log(ratio) / math.log(REL_MAX_DIST / exact) * (REL_BUCKETS - exact)).astype(jnp.int32)
    return jnp.where(d < exact, d, jnp.minimum(large, REL_BUCKETS - 1))


def _compress(k, w1, b1, w2, b2):
    b, l, g, dk = k.shape
    ns = l // CMP_STRIDE
    r2 = CMP_LEN // CMP_STRIDE
    nc = ns - r2 + 1
    kb = k[:, :ns * CMP_STRIDE].reshape(b, ns, CMP_STRIDE, g, dk)
    w1r = w1.reshape(r2, CMP_STRIDE, dk, -1)
    part = jnp.einsum('bnsgd,hsde->hbnge', kb, w1r)
    hid = part[0, :, :nc] + b1
    for h in range(1, r2):
        hid = hid + part[h, :, h:h + nc]
    return jnp.einsum('bnge,ed->bngd', jax.nn.gelu(hid), w2) + b2


def _to_blocks(k):
    b, l, g, dk = k.shape
    nb = l // SEL_BLOCK
    return k.reshape(b, nb, SEL_BLOCK, g, dk).transpose(0, 3, 1, 2, 4).reshape(b, g, nb, SEL_BLOCK * dk)


def _slc_offsets():
    r1 = SEL_BLOCK // CMP_STRIDE
    r2 = CMP_LEN // CMP_STRIDE
    offs = np.arange(-(r2 - 1), r1)
    wts = np.array([sum(1 for m in range(r1) for n in range(r2) if m - n == o) for o in offs], np.float32)
    return offs, wts


def _nsa_project(p_n, qk_norm_g, nkv, ngrp):
    b, t, _ = p_n.shape
    nsa_w = nkv * ngrp * HEAD_DIM
    kvw = nkv * HEAD_DIM
    q = _rms(p_n[..., :nsa_w].reshape(b, t, nkv, ngrp, HEAD_DIM), qk_norm_g[0])
    kv = p_n[..., nsa_w:nsa_w + 6 * kvw].reshape(b, t, 6, nkv, HEAD_DIM)
    gates = jax.nn.sigmoid(p_n[..., nsa_w + 6 * kvw:nsa_w + 6 * kvw + 3 * nkv * ngrp]).reshape(b, t, nkv, ngrp, 3)
    rows = jnp.stack([kv[:, :, 0], kv[:, :, 1], _rms(kv[:, :, 2], qk_norm_g[2]), kv[:, :, 3]], axis=2)
    win = jnp.stack([_rms(kv[:, :, 4], qk_norm_g[3]), kv[:, :, 5]], axis=2)
    return q, gates, rows, win


def _nsa_context(rows, W):
    kc = _rms(_compress(rows[:, :, 0], W['cmp_w1'][0], W['cmp_b1'][0], W['cmp_w2'][0], W['cmp_b2'][0]), W['qk_norm_g'][1])
    vc = _compress(rows[:, :, 1], W['cmp_w1'][1], W['cmp_b1'][1], W['cmp_w2'][1], W['cmp_b2'][1])
    c_end = jnp.arange(kc.shape[1]) * CMP_STRIDE + CMP_LEN - 1
    return kc, vc, c_end, _to_blocks(rows[:, :, 2]), _to_blocks(rows[:, :, 3])


def _nsa_core(q, gates, q_pos, kc, vc, c_end, ks_blk, vs_blk, kw, vw, kw_pos, rel_bias):
    b, tq, g, r, dk = q.shape
    nb = ks_blk.shape[2]
    nc = kc.shape[1]
    scale = HEAD_DIM ** -0.5
    table = rel_bias.astype(F32).reshape(REL_BUCKETS, g, r)
    dist_c = q_pos[:, None] - c_end[None, :]
    bias_c = table[_rel_bucket(dist_c)].transpose(2, 3, 0, 1)
    lg_c = jnp.einsum('bqgrd,bcgd->bgrqc', q, kc) * scale + bias_c
    p_c = _masked_softmax(lg_c, dist_c >= 0)
    o_c = jnp.einsum('bgrqc,bcgd->bqgrd', p_c, vc)
    offs, wts = _slc_offsets()
    jb = jnp.arange(nb)
    cidx = (SEL_BLOCK // CMP_STRIDE) * jb[:, None] + offs[None, :]
    cval = (cidx >= 0) & (cidx < nc)
    p_grp = jnp.sum(p_c, axis=2)
    p_slc = jnp.sum(p_grp[..., jnp.clip(cidx, 0, nc - 1)] * (wts * cval), axis=-1)
    cur = q_pos // SEL_BLOCK
    forced = (jb[None] == 0) | (jb[None] == cur[:, None]) | (jb[None] == cur[:, None] - 1)
    future = jb[None] > cur[:, None]
    score = jnp.where(future, -1.0, jnp.where(forced, 1e6, p_slc))
    n_sel = min(SEL_TOPK, nb)
    _, sel = lax.top_k(score, n_sel)
    sel_ok = sel <= cur[:, None]
    bi = jnp.arange(b)[:, None, None]
    gi = jnp.arange(g)[None, :, None]
    flat = sel.reshape(b, g, tq * n_sel)
    ksg = ks_blk[bi, gi, flat].reshape(b, g, tq, n_sel * SEL_BLOCK, dk)
    vsg = vs_blk[bi, gi, flat].reshape(b, g, tq, n_sel * SEL_BLOCK, dk)
    pos4 = sel[..., None] * SEL_BLOCK + jnp.arange(SEL_BLOCK)
    ok_s = (sel_ok[..., None] & (pos4 <= q_pos[:, None, None])).reshape(b, g, tq, n_sel * SEL_BLOCK)
    pos_s = pos4.reshape(b, g, tq, n_sel * SEL_BLOCK)
    tg = table.transpose(1, 0, 2)
    bias_s = tg[jnp.arange(g)[None, :, None, None], _rel_bucket(q_pos[:, None] - pos_s)].transpose(0, 1, 4, 2, 3)
    lg_s = jnp.einsum('bqgrd,bgqkd->bgrqk', q, ksg) * scale + bias_s
    p_s = _masked_softmax(lg_s, ok_s[:, :, None])
    o_s = jnp.einsum('bgrqk,bgqkd->bqgrd', p_s, vsg)
    dist_w = q_pos[:, None] - kw_pos[None, :]
    ok_w = (dist_w >= 0) & (dist_w < WINDOW) & (kw_pos[None, :] >= 0)
    bias_w = table[_rel_bucket(dist_w)].transpose(2, 3, 0, 1)
    lg_w = jnp.einsum('bqgrd,bkgd->bgrqk', q, kw) * scale + bias_w
    p_w = _masked_softmax(lg_w, ok_w)
    o_w = jnp.einsum('bgrqk,bkgd->bqgrd', p_w, vw)
    o = gates[..., 0:1] * o_c + gates[..., 1:2] * o_s + gates[..., 2:3] * o_w
    return o.reshape(b, tq, g * r * dk)


def _nsa_prompt_jnp(p_n, W, nkv, ngrp):
    q, gates, rows, win = _nsa_project(p_n, W['qk_norm_g'], nkv, ngrp)
    b, s = q.shape[:2]
    kc, vc, c_end, ks_blk, vs_blk = _nsa_context(rows, W)
    win_pad = jnp.pad(win, ((0, 0), (WINDOW, 0), (0, 0), (0, 0), (0, 0)))
    qb_sz = 128

    def block(i):
        start = i * qb_sz
        qb = lax.dynamic_slice_in_dim(q, start, qb_sz, axis=1)
        gb = lax.dynamic_slice_in_dim(gates, start, qb_sz, axis=1)
        wb = lax.dynamic_slice_in_dim(win_pad, start, WINDOW + qb_sz, axis=1)
        q_pos = start + jnp.arange(qb_sz)
        kw_pos = start - WINDOW + jnp.arange(WINDOW + qb_sz)
        return _nsa_core(qb, gb, q_pos, kc, vc, c_end, ks_blk, vs_blk, wb[:, :, 0], wb[:, :, 1], kw_pos, W['rel_bias'])

    o = lax.map(block, jnp.arange(s // qb_sz))
    o = o.transpose(1, 0, 2, 3).reshape(b, s, -1)
    return o, rows, win[:, s - min(WINDOW, s):]


def _nsa_sample_jnp(p_n, cache_kv, cache_win, page_table, W, nkv, ngrp, db):
    p_n = p_n.reshape(db, p_n.shape[0] // db, -1)
    q, gates, rows_new, win_new = _nsa_project(p_n, W['qk_norm_g'], nkv, ngrp)
    db, ds = q.shape[:2]
    past_len = page_table.shape[1] * PAGE_SIZE
    past_rows = cache_kv[page_table].reshape(db, past_len, 4, nkv, HEAD_DIM)
    pad = (-ds) % SEL_BLOCK
    rows = jnp.concatenate([past_rows, jnp.pad(rows_new, ((0, 0), (0, pad), (0, 0), (0, 0), (0, 0)))], axis=1)
    kc, vc, c_end, ks_blk, vs_blk = _nsa_context(rows, W)
    keep = cache_win.shape[1]
    win_all = jnp.concatenate([cache_win, win_new], axis=1)
    q_pos = past_len + jnp.arange(ds)
    kw_pos = past_len - keep + jnp.arange(keep + ds)
    o = _nsa_core(q, gates, q_pos, kc, vc, c_end, ks_blk, vs_blk, win_all[:, :, 0], win_all[:, :, 1], kw_pos, W['rel_bias'])
    n_keep = min(WINDOW, past_len + ds)
    return o.reshape(db * ds, -1), rows_new, win_all[:, win_all.shape[1] - n_keep:]


def _top_values(x, k):
    n = x.shape[0]
    row = lax.broadcasted_iota(jnp.int32, x.shape, 0)
    vals = []
    for _ in range(k):
        m = jnp.max(x, axis=0, keepdims=True)
        vals.append(m)
        first = jnp.min(jnp.where(x == m, row, n), axis=0, keepdims=True)
        x = jnp.where(row == first, -jnp.inf, x)
    return vals


def _peer_route_kernel(hT_ref, wq_ref, sk_ref, s1_ref, c1_ref, s2_ref, e2_ref, tau_ref, *, nheads, topk):
    nk, half = sk_ref.shape[1], sk_ref.shape[2]
    qT = _dot(wq_ref[...], hT_ref[...])
    for h in range(nheads):
        base = h * 2 * half
        s1 = _dot(sk_ref[0].astype(BF16), qT[base:base + half].astype(BF16))
        s2 = _dot(sk_ref[1].astype(BF16), qT[base + half:base + 2 * half].astype(BF16))
        v1 = _top_values(s1, topk)
        v2 = _top_values(s2, topk)
        cand = jnp.concatenate([v1[a] + v2[b] for a in range(topk) for b in range(topk) if (a + 1) * (b + 1) <= topk],
                               axis=0)
        tau = _top_values(cand, topk)[-1]
        e1 = jnp.exp(s1 - v1[0])
        e2 = jnp.exp(s2 - v2[0])
        z = jnp.sum(jnp.where(cand >= tau, jnp.exp(cand - (v1[0] + v2[0])), 0.0), axis=0, keepdims=True)
        s1_ref[h] = s1
        s2_ref[h] = s2
        c1_ref[h] = e1 / z
        e2_ref[h] = e2
        tau_ref[h] = tau


def _peer_expert_kernel(hT_ref, x1_ref, ga_ref, s1_ref, c1_ref, s2_ref, e2_ref, tau_ref, u_ref, v_ref,
                        o_ref, acc_ref, *, nheads, n_eblocks):
    eb = pl.program_id(1)
    nk = s2_ref.shape[1]
    rows_per_block = u_ref.shape[0] // nk

    @pl.when(eb == 0)
    def _():
        acc_ref[...] = jnp.zeros(acc_ref.shape, F32)

    act = jax.nn.gelu(_dot(u_ref[...], hT_ref[...]))
    gates = []
    for i in range(rows_per_block):
        i1 = eb * rows_per_block + i
        wd = None
        for h in range(nheads):
            cand = s1_ref[h, pl.ds(i1, 1), :] + s2_ref[h]
            term = jnp.where(cand >= tau_ref[h], e2_ref[h], 0.0) * c1_ref[h, pl.ds(i1, 1), :]
            wd = term if wd is None else wd + term
        gates.append(wd)
    g = jnp.concatenate(gates, axis=0) * act
    acc_ref[...] += _dot(g.T.astype(BF16), v_ref[...])

    @pl.when(eb == n_eblocks - 1)
    def _():
        o_ref[...] = x1_ref[...] + ga_ref[0] * acc_ref[...]


def _peer(h2, x1, ga, W, tm):
    t, d = h2.shape
    sub_keys = W['peer_sub_keys']
    nk, half = sub_keys.shape[1], sub_keys.shape[2]
    qd = W['peer_w_query'].shape[1]
    nheads = qd // (2 * half)
    hT = h2.T
    wqT = W['peer_w_query'].T.astype(BF16)
    route_shape = jax.ShapeDtypeStruct((nheads, nk, t), F32)
    rspec = pl.BlockSpec((nheads, nk, tm), lambda i: (0, 0, i))
    s1, c1, s2, e2, tau = pl.pallas_call(
        functools.partial(_peer_route_kernel, nheads=nheads, topk=PEER_TOPK),
        grid=(t // tm,),
        in_specs=[pl.BlockSpec((d, tm), lambda i: (0, i)),
                  pl.BlockSpec((qd, d), lambda i: (0, 0)),
                  pl.BlockSpec(sub_keys.shape, lambda i: (0, 0, 0))],
        out_specs=[rspec, rspec, rspec, rspec, pl.BlockSpec((nheads, 1, tm), lambda i: (0, 0, i))],
        out_shape=[route_shape] * 4 + [jax.ShapeDtypeStruct((nheads, 1, t), F32)],
        compiler_params=_cparams(("arbitrary",)),
        name="peer_route",
    )(hT, wqT, sub_keys)
    eblk = MXU
    n_eblocks = W['peer_u'].shape[0] // eblk
    nmod, rows, _ = ga.shape
    tiles_per_mod = (t // tm) // nmod
    rspec2 = pl.BlockSpec((nheads, nk, tm), lambda i, e: (0, 0, i))
    return pl.pallas_call(
        functools.partial(_peer_expert_kernel, nheads=nheads, n_eblocks=n_eblocks),
        grid=(t // tm, n_eblocks),
        in_specs=[pl.BlockSpec((d, tm), lambda i, e: (0, i)),
                  pl.BlockSpec((tm, d), lambda i, e: (i, 0)),
                  pl.BlockSpec((1, rows, d), lambda i, e: (i // tiles_per_mod, 0, 0)),
                  rspec2, rspec2, rspec2, rspec2,
                  pl.BlockSpec((nheads, 1, tm), lambda i, e: (0, 0, i)),
                  pl.BlockSpec((eblk, d), lambda i, e: (e, 0)),
                  pl.BlockSpec((eblk, d), lambda i, e: (e, 0))],
        out_specs=pl.BlockSpec((tm, d), lambda i, e: (i, 0)),
        out_shape=jax.ShapeDtypeStruct((t, d), F32),
        scratch_shapes=[pltpu.VMEM((tm, d), F32)],
        compiler_params=_cparams(("arbitrary", "arbitrary")),
        name="peer_experts",
    )(hT, x1, ga, s1, c1, s2, e2, tau, W['peer_u_bf16'], W['peer_v_bf16'])


def _peer_jnp(h, W):
    n, d = h.shape
    sub_keys, expert_u, expert_v = W['peer_sub_keys'], W['peer_u'], W['peer_v']
    nk, half = sub_keys.shape[1], sub_keys.shape[2]
    nheads = W['peer_w_query'].shape[1] // (2 * half)
    cs = min(128, n)
    hc = h.reshape(-1, cs, d)

    def chunk(x):
        qh = (x @ W['peer_w_query']).reshape(cs, nheads, 2, half)
        s1 = jnp.einsum('chd,kd->chk', qh[:, :, 0], sub_keys[0])
        s2 = jnp.einsum('chd,kd->chk', qh[:, :, 1], sub_keys[1])
        v1, i1 = lax.top_k(s1, PEER_TOPK)
        v2, i2 = lax.top_k(s2, PEER_TOPK)
        cand = (v1[..., :, None] + v2[..., None, :]).reshape(cs, nheads, -1)
        cidx = (i1[..., :, None] * nk + i2[..., None, :]).reshape(cs, nheads, -1)
        top, pos = lax.top_k(cand, PEER_TOPK)
        eidx = jnp.take_along_axis(cidx, pos, axis=-1)
        gw = jax.nn.softmax(top, axis=-1)
        act = jax.nn.gelu(jnp.einsum('chkd,cd->chk', expert_u[eidx], x))
        return jnp.einsum('chk,chkd->cd', gw * act, expert_v[eidx])

    return lax.map(chunk, hc).reshape(-1, d)


def _layer(x, mods, nsa_fn, shift_prev, wkv0, W, tm, rwkv_chunk):
    b, t, d = x.shape
    cw = W['rwkv_w0'].shape[0]
    nw, na, ng = W['rwkv_w_up'].shape[0], W['rwkv_a_up'].shape[0], W['rwkv_g_up'].shape[0]
    rwkv_proj = 3 * cw + nw + na + ng
    sh1, sc1, ga1, sh2, sc2, ga2 = mods
    xf = x.reshape(b * t, d)
    if (b * t) % tm == 0 and t % tm == 0:
        as_mod = lambda m: m.reshape(b, 1, d)
    else:
        tm = b * t
        as_mod = lambda m: jnp.repeat(m, t, axis=0).reshape(1, b * t, d)
    w_r = _rwkv_pad_cols(W['w_in'][:, :rwkv_proj], cw, nw, na, ng).astype(BF16)
    nsa_cols = W['w_in'].shape[1] - rwkv_proj
    w_n = jnp.pad(W['w_in'][:, rwkv_proj:], ((0, 0), (0, _rup(nsa_cols, LANES) - nsa_cols))).astype(BF16)
    p_r, _ = _norm_mod_matmul(xf, W['norm1_g'], as_mod(sc1), as_mod(sh1), w_r, tm, 512)
    p_n, _ = _norm_mod_matmul(xf, W['norm1_g'], as_mod(sc1), as_mod(sh1), w_n, tm, w_n.shape[1] // 3)
    pr = p_r.shape[1]
    p_r = p_r.reshape(b, t, pr)
    shift_new = _rwkv_unpad_cols(p_r[:, -1], cw, nw, na, ng)
    tpad = _rup(t, rwkv_chunk)
    p_r_pad = jnp.pad(p_r, ((0, 0), (0, tpad - t), (0, 0)))
    y_r, wkv_new = _rwkv_mix(p_r_pad, shift_prev, wkv0, W, rwkv_chunk, t)
    y_r = y_r[:, :t].reshape(b * t, cw)
    y_n, rows, win = nsa_fn(p_n)
    w_out = W['w_out'].astype(BF16)
    x1 = _out_proj(xf, y_r, y_n, as_mod(ga1), w_out[:cw], w_out[cw:], tm, 512)
    h2 = _norm_mod(x1, W['norm2_g'], as_mod(sc2), as_mod(sh2), tm)
    if (b * t) % LANES == 0:
        out = _peer(h2, x1, as_mod(ga2), W, tm)
    else:
        out = x1 + jnp.repeat(ga2, t, axis=0) * _peer_jnp(h2.astype(F32), W)
    return out.reshape(b, t, d), rows, win, wkv_new, shift_new


def kernel(x_prompt, x_sample, c_prompt, c_sample, cache_kv, cache_win, state_wkv, state_shift, page_table,
           norm1_g, norm2_g, w_ada, b_ada, w_in, w_out,
           rwkv_mu, rwkv_w0, rwkv_w_up, rwkv_a0, rwkv_a_up, rwkv_g_up, rwkv_k_k, rwkv_k_a, rwkv_r_k, lnx_w, lnx_b,
           qk_norm_g, cmp_w1, cmp_b1, cmp_w2, cmp_b2, rel_bias,
           peer_w_query, peer_sub_keys, peer_u, peer_v):
    W = dict(norm1_g=norm1_g, norm2_g=norm2_g, w_ada=w_ada, b_ada=b_ada, w_in=w_in, w_out=w_out,
             rwkv_mu=rwkv_mu, rwkv_w0=rwkv_w0, rwkv_w_up=rwkv_w_up, rwkv_a0=rwkv_a0, rwkv_a_up=rwkv_a_up,
             rwkv_g_up=rwkv_g_up, rwkv_k_k=rwkv_k_k, rwkv_k_a=rwkv_k_a, rwkv_r_k=rwkv_r_k, lnx_w=lnx_w, lnx_b=lnx_b,
             qk_norm_g=qk_norm_g, cmp_w1=cmp_w1, cmp_b1=cmp_b1, cmp_w2=cmp_w2, cmp_b2=cmp_b2, rel_bias=rel_bias,
             peer_w_query=peer_w_query, peer_sub_keys=peer_sub_keys, peer_u=peer_u, peer_v=peer_v)
    W['peer_u_bf16'] = peer_u.astype(BF16)
    W['peer_v_bf16'] = peer_v.astype(BF16)
    bp, seq, d = x_prompt.shape
    db = x_sample.shape[0]
    nkv = cache_kv.shape[3]
    nh_r = rwkv_w0.shape[0] // HEAD_DIM
    ngrp = (w_out.shape[0] - rwkv_w0.shape[0]) // HEAD_DIM // nkv

    mods = _ada_mods(jnp.concatenate([c_prompt, c_sample], axis=0), w_ada, b_ada)
    mods = mods.reshape(bp + db, N_MODS, d)
    mods_p = [mods[:bp, i] for i in range(N_MODS)]
    mods_s = [mods[bp:, i] for i in range(N_MODS)]

    shift0 = jnp.zeros((bp, state_shift.shape[1]), F32)
    wkv0 = jnp.zeros((bp, nh_r, HEAD_DIM, HEAD_DIM), F32)
    y_p, rows_p, win_p, wkv_p, shift_p = _layer(
        x_prompt, mods_p, lambda pn: _nsa_prompt(pn, W, bp, seq, nkv, ngrp, 512), shift0, wkv0, W, 512, 64)
    y_s, rows_s, win_s, wkv_s, shift_s = _layer(
        x_sample, mods_s, lambda pn: _nsa_sample_jnp(pn, cache_kv, cache_win, page_table, W, nkv, ngrp, db),
        state_shift, state_wkv, W, 512, 32)
    return (y_p, y_s, rows_p, win_p, wkv_p.astype(state_wkv.dtype), shift_p,
            rows_s, win_s, wkv_s.astype(state_wkv.dtype), shift_s)
```

```python
import functools
import math

import numpy as np
import jax
import jax.numpy as jnp
from jax import lax
from jax.experimental import pallas as pl
from jax.experimental.pallas import tpu as pltpu

F32 = jnp.float32
BF16 = jnp.bfloat16
HI = lax.Precision.HIGHEST

HEAD_DIM = 64
PAGE_SIZE = 128
CMP_LEN = 32
CMP_STRIDE = 16
SEL_BLOCK = 64
SEL_TOPK = 16
WINDOW = 512
REL_BUCKETS = 32
REL_MAX_DIST = 2048
PEER_TOPK = 16
N_MODS = 6
RMS_EPS = 1e-6
LNX_EPS = 64e-5
NEG = -1e30

LANES = 128
MXU = 256
HEADS_PER_GROUP = MXU // HEAD_DIM
VMEM_LIMIT = 56 * 1024 * 1024


def _cparams(sem):
    return pltpu.CompilerParams(dimension_semantics=sem, vmem_limit_bytes=VMEM_LIMIT)


def _dot(a, b, precision=None):
    return jnp.dot(a, b, preferred_element_type=F32, precision=precision)


def _dot_nt(a, b, precision=None):
    return lax.dot_general(a, b, (((1,), (1,)), ((), ())), preferred_element_type=F32, precision=precision)


def _ada_kernel(c_ref, w_ref, b_ref, o_ref):
    c = c_ref[...]
    s = c * jax.nn.sigmoid(c)
    o_ref[...] = _dot(s.astype(BF16), w_ref[...].astype(BF16)) + b_ref[...]


def _ada_mods(c, w_ada, b_ada):
    n, d = c.shape
    cols = w_ada.shape[1]
    tn = 1024
    return pl.pallas_call(
        _ada_kernel,
        grid=(cols // tn,),
        in_specs=[pl.BlockSpec((n, d), lambda j: (0, 0)),
                  pl.BlockSpec((d, tn), lambda j: (0, j)),
                  pl.BlockSpec((1, tn), lambda j: (0, j))],
        out_specs=pl.BlockSpec((n, tn), lambda j: (0, j)),
        out_shape=jax.ShapeDtypeStruct((n, cols), F32),
        compiler_params=_cparams(("arbitrary",)),
        name="ada_mods",
    )(c, w_ada, b_ada.reshape(1, cols))


def _nmm_kernel(x_ref, g_ref, sc_ref, sh_ref, w_ref, o_ref, h_ref):
    @pl.when(pl.program_id(1) == 0)
    def _():
        x = x_ref[...]
        ms = jnp.mean(x * x, axis=-1, keepdims=True)
        y = x * lax.rsqrt(ms + RMS_EPS) * g_ref[...]
        h_ref[...] = (y * (1.0 + sc_ref[0]) + sh_ref[0]).astype(h_ref.dtype)

    o_ref[...] = _dot(h_ref[...], w_ref[...])


def _norm_mod_matmul(x, g, sc, sh, w, tm, tn):
    t, d = x.shape
    n = w.shape[1]
    nmod, rows, _ = sc.shape
    tiles_per_mod = (t // tm) // nmod
    mod_spec = pl.BlockSpec((1, rows, d), lambda i, j: (i // tiles_per_mod, 0, 0))
    return pl.pallas_call(
        _nmm_kernel,
        grid=(t // tm, n // tn),
        in_specs=[pl.BlockSpec((tm, d), lambda i, j: (i, 0)),
                  pl.BlockSpec((1, d), lambda i, j: (0, 0)),
                  mod_spec, mod_spec,
                  pl.BlockSpec((d, tn), lambda i, j: (0, j))],
        out_specs=[pl.BlockSpec((tm, tn), lambda i, j: (i, j)),
                   pl.BlockSpec((tm, d), lambda i, j: (i, 0))],
        out_shape=[jax.ShapeDtypeStruct((t, n), F32), jax.ShapeDtypeStruct((t, d), BF16)],
        compiler_params=_cparams(("arbitrary", "arbitrary")),
        name="norm_mod_matmul",
    )(x, g.reshape(1, d), sc, sh, w)


def _nm_kernel(x_ref, g_ref, sc_ref, sh_ref, h_ref):
    x = x_ref[...]
    ms = jnp.mean(x * x, axis=-1, keepdims=True)
    y = x * lax.rsqrt(ms + RMS_EPS) * g_ref[...]
    h_ref[...] = (y * (1.0 + sc_ref[0]) + sh_ref[0]).astype(h_ref.dtype)


def _norm_mod(x, g, sc, sh, tm):
    t, d = x.shape
    nmod, rows, _ = sc.shape
    tiles_per_mod = (t // tm) // nmod
    mod_spec = pl.BlockSpec((1, rows, d), lambda i: (i // tiles_per_mod, 0, 0))
    return pl.pallas_call(
        _nm_kernel,
        grid=(t // tm,),
        in_specs=[pl.BlockSpec((tm, d), lambda i: (i, 0)), pl.BlockSpec((1, d), lambda i: (0, 0)), mod_spec, mod_spec],
        out_specs=pl.BlockSpec((tm, d), lambda i: (i, 0)),
        out_shape=jax.ShapeDtypeStruct((t, d), BF16),
        compiler_params=_cparams(("arbitrary",)),
        name="norm_mod",
    )(x, g.reshape(1, d), sc, sh)


def _outproj_kernel(x_ref, yr_ref, yn_ref, ga_ref, w1_ref, w2_ref, o_ref):
    acc = _dot(yr_ref[...].astype(BF16), w1_ref[...]) + _dot(yn_ref[...].astype(BF16), w2_ref[...])
    o_ref[...] = x_ref[...] + ga_ref[0] * acc


def _out_proj(x, y_r, y_n, ga, w1, w2, tm, tn):
    t, d = x.shape
    nmod, rows, _ = ga.shape
    tiles_per_mod = (t // tm) // nmod
    cr, cn = y_r.shape[1], y_n.shape[1]
    return pl.pallas_call(
        _outproj_kernel,
        grid=(t // tm, d // tn),
        in_specs=[pl.BlockSpec((tm, tn), lambda i, j: (i, j)),
                  pl.BlockSpec((tm, cr), lambda i, j: (i, 0)),
                  pl.BlockSpec((tm, cn), lambda i, j: (i, 0)),
                  pl.BlockSpec((1, rows, tn), lambda i, j: (i // tiles_per_mod, 0, j)),
                  pl.BlockSpec((cr, tn), lambda i, j: (0, j)),
                  pl.BlockSpec((cn, tn), lambda i, j: (0, j))],
        out_specs=pl.BlockSpec((tm, tn), lambda i, j: (i, j)),
        out_shape=jax.ShapeDtypeStruct((t, d), F32),
        compiler_params=_cparams(("arbitrary", "arbitrary")),
        name="out_proj",
    )(x, y_r, y_n, ga, w1, w2)


def _softplus(z):
    return jnp.maximum(z, 0.0) + jnp.log(1.0 + jnp.exp(-jnp.abs(z)))


def _rwkv_kernel(p_ref, shift_ref, s0_ref, mu_ref, vec_ref, wup_ref, aup_ref, gup_ref,
                 y_ref, sfin_ref, carry, state, *, t_valid, n_chunks):
    c = pl.program_id(1)
    chunk = p_ref.shape[1]
    cw = vec_ref.shape[1]
    n_groups = cw // MXU
    hg = HEADS_PER_GROUP
    rows_g = hg * chunk

    @pl.when(c == 0)
    def _():
        carry[...] = shift_ref[0]
        state[...] = s0_ref[0]

    p = p_ref[0]
    row = lax.broadcasted_iota(jnp.int32, (chunk, 1), 0)
    prev = jnp.where(row == 0, carry[...], pltpu.roll(p, 1, axis=0))
    carry[...] = p[chunk - 1:chunk, :]
    xs = p + (prev - p) * mu_ref[...]

    w0, a0, k_k, k_a, r_k, lnx_w, lnx_b = (vec_ref[i:i + 1, :] for i in range(7))
    r = xs[:, 0:cw]
    k = xs[:, cw:2 * cw]
    v = xs[:, 2 * cw:3 * cw]
    o = 3 * cw
    nw, na, ng = wup_ref.shape[0], aup_ref.shape[0], gup_ref.shape[0]
    xw = xs[:, o:o + nw]
    xa = xs[:, o + nw:o + nw + na]
    xg = xs[:, o + nw + na:o + nw + na + ng]
    w_log = -_softplus(-(w0 + _dot(jnp.tanh(xw), wup_ref[...], HI))) - 0.5
    a = jax.nn.sigmoid(a0 + _dot(xa, aup_ref[...], HI))
    gate = _dot(jax.nn.sigmoid(xg), gup_ref[...], HI)

    gi = lax.broadcasted_iota(jnp.int32, (MXU, MXU), 0) // HEAD_DIM
    gj = lax.broadcasted_iota(jnp.int32, (MXU, MXU), 1) // HEAD_DIM
    ones_bd = (gi == gj).astype(F32)

    def head_sum(x):
        return jnp.concatenate([_dot(x[:, g * MXU:(g + 1) * MXU], ones_bd, HI) for g in range(n_groups)], axis=1)

    kk = k * k_k
    kk = kk / jnp.maximum(jnp.sqrt(head_sum(kk * kk)), 1e-12)
    k2 = k * (1.0 + (a - 1.0) * k_a)
    logdec = -jnp.exp(w_log)
    if t_valid < chunk * n_chunks:
        valid = (row + c * chunk) < t_valid
        logdec = jnp.where(valid, logdec, 0.0)
        kk = jnp.where(valid, kk, 0.0)
        k2 = jnp.where(valid, k2, 0.0)
        v = jnp.where(valid, v, 0.0)

    ti = lax.broadcasted_iota(jnp.int32, (chunk, chunk), 0)
    tj = lax.broadcasted_iota(jnp.int32, (chunk, chunk), 1)
    cum = _dot((tj <= ti).astype(F32), logdec, HI)
    cum_end = cum[chunk - 1:chunk, :]
    e_neg = jnp.exp(-cum)
    e_rem = jnp.exp(cum_end - cum)
    r_t = r * jnp.exp(cum)
    a_t = -kk * jnp.exp(cum - logdec)
    b_vec = kk * a
    b_t = b_vec * e_neg
    k_t = k2 * e_neg
    b_rem = b_vec * e_rem
    k_rem = k2 * e_rem
    w_end = jnp.exp(cum_end)

    lane_head = lax.broadcasted_iota(jnp.int32, (chunk, MXU), 1) // HEAD_DIM
    ri = lax.broadcasted_iota(jnp.int32, (rows_g, rows_g), 0)
    rj = lax.broadcasted_iota(jnp.int32, (rows_g, rows_g), 1)
    strict = rj < ri
    incl = rj <= ri
    eye_r = (ri == rj).astype(F32)
    di = lax.broadcasted_iota(jnp.int32, (MXU, MXU), 0)
    dj = lax.broadcasted_iota(jnp.int32, (MXU, MXU), 1)
    n_double = max(int(math.ceil(math.log2(chunk))) - 1, 0)

    def bd(x):
        return jnp.concatenate([jnp.where(lane_head == h, x, 0.0) for h in range(hg)], axis=0)

    def stack(x):
        return jnp.concatenate([x[:, h * HEAD_DIM:(h + 1) * HEAD_DIM] for h in range(hg)], axis=0)

    def unstack(x):
        return jnp.concatenate([x[h * chunk:(h + 1) * chunk, :] for h in range(hg)], axis=1)

    ys = []
    for g in range(n_groups):
        sl = slice(g * MXU, (g + 1) * MXU)
        a_bd, r_bd = bd(a_t[:, sl]), bd(r_t[:, sl])
        b_bd, k_bd = bd(b_t[:, sl]), bd(k_t[:, sl])
        v_st = stack(v[:, sl])
        a_ab = jnp.where(strict, _dot_nt(a_bd, b_bd, HI), 0.0)
        a_ak = jnp.where(strict, _dot_nt(a_bd, k_bd, HI), 0.0)
        a_rb = jnp.where(incl, _dot_nt(r_bd, b_bd, HI), 0.0)
        a_rk = jnp.where(incl, _dot_nt(r_bd, k_bd, HI), 0.0)
        tinv = eye_r + a_ab
        pw = a_ab
        for _ in range(n_double):
            pw = _dot(pw, pw, HI)
            tinv = tinv + _dot(pw, tinv, HI)
        s0 = state[g]
        z = _dot(a_bd, s0, HI) + _dot(a_ak, v_st, HI)
        u = _dot(tinv, z, HI)
        y_st = _dot(r_bd, s0, HI) + _dot(a_rb, u, HI) + _dot(a_rk, v_st, HI)
        decay_diag = jnp.where(di == dj, jnp.broadcast_to(w_end[:, sl], (MXU, MXU)), 0.0)
        state[g] = (_dot(decay_diag, s0, HI) + _dot(bd(b_rem[:, sl]).T, u, HI)
                    + _dot(bd(k_rem[:, sl]).T, v_st, HI))
        ys.append(unstack(y_st))
    y = jnp.concatenate(ys, axis=1)

    inv_n = 1.0 / HEAD_DIM
    mean = head_sum(y) * inv_n
    d = y - mean
    var = head_sum(d * d) * inv_n
    yn = d * lax.rsqrt(var + LNX_EPS) * lnx_w + lnx_b
    bonus = head_sum(r * k2 * r_k) * v
    y_ref[0] = ((yn + bonus) * gate).astype(y_ref.dtype)

    @pl.when(c == n_chunks - 1)
    def _():
        sfin_ref[0] = state[...]


def _rwkv_pad_cols(x, cw, nw, na, ng):
    o = 3 * cw
    parts = [x[..., :o + nw], x[..., o + nw:o + nw + na], x[..., o + nw + na:]]
    widths = [o + _rup(nw, LANES), _rup(na, LANES), _rup(ng, LANES)]
    out = []
    for part, wd in zip(parts, widths):
        pad = [(0, 0)] * (x.ndim - 1) + [(0, wd - part.shape[-1])]
        out.append(jnp.pad(part, pad))
    return jnp.concatenate(out, axis=-1)


def _rwkv_unpad_cols(x, cw, nw, na, ng):
    o = 3 * cw
    o2 = o + _rup(nw, LANES)
    o3 = o2 + _rup(na, LANES)
    return jnp.concatenate([x[..., :o + nw], x[..., o2:o2 + na], x[..., o3:o3 + ng]], axis=-1)


def _rup(x, m):
    return (x + m - 1) // m * m


def _rwkv_mix(p_r, shift_prev, wkv0, W, chunk, t_valid):
    b, tpad, pr = p_r.shape
    cw = W['rwkv_w0'].shape[0]
    nh = cw // HEAD_DIM
    n_groups = cw // MXU
    nw, na, ng = W['rwkv_w_up'].shape[0], W['rwkv_a_up'].shape[0], W['rwkv_g_up'].shape[0]
    n_chunks = tpad // chunk
    mu = _rwkv_pad_cols(W['rwkv_mu'], cw, nw, na, ng).reshape(1, pr)
    vecs = jnp.stack([W['rwkv_w0'], W['rwkv_a0'], W['rwkv_k_k'], W['rwkv_k_a'], W['rwkv_r_k'].reshape(cw),
                      W['lnx_w'], W['lnx_b'], jnp.zeros((cw,), F32)])
    wup = jnp.pad(W['rwkv_w_up'], ((0, _rup(nw, LANES) - nw), (0, 0)))
    aup = jnp.pad(W['rwkv_a_up'], ((0, _rup(na, LANES) - na), (0, 0)))
    gup = jnp.pad(W['rwkv_g_up'], ((0, _rup(ng, LANES) - ng), (0, 0)))
    shift3 = _rwkv_pad_cols(shift_prev, cw, nw, na, ng).reshape(b, 1, pr)
    s0 = wkv0.astype(F32).transpose(0, 1, 3, 2).reshape(b, n_groups, MXU, HEAD_DIM)
    const = lambda shape: pl.BlockSpec(shape, lambda i, c: (0,) * len(shape))
    y, sfin = pl.pallas_call(
        functools.partial(_rwkv_kernel, t_valid=t_valid, n_chunks=n_chunks),
        grid=(b, n_chunks),
        in_specs=[pl.BlockSpec((1, chunk, pr), lambda i, c: (i, c, 0)),
                  pl.BlockSpec((1, 1, pr), lambda i, c: (i, 0, 0)),
                  pl.BlockSpec((1, n_groups, MXU, HEAD_DIM), lambda i, c: (i, 0, 0, 0)),
                  const((1, pr)), const((8, cw)), const(wup.shape), const(aup.shape), const(gup.shape)],
        out_specs=[pl.BlockSpec((1, chunk, cw), lambda i, c: (i, c, 0)),
                   pl.BlockSpec((1, n_groups, MXU, HEAD_DIM), lambda i, c: (i, 0, 0, 0))],
        out_shape=[jax.ShapeDtypeStruct((b, tpad, cw), F32),
                   jax.ShapeDtypeStruct((b, n_groups, MXU, HEAD_DIM), F32)],
        scratch_shapes=[pltpu.VMEM((1, pr), F32), pltpu.VMEM((n_groups, MXU, HEAD_DIM), F32)],
        compiler_params=_cparams(("arbitrary", "arbitrary")),
        name="rwkv_mix",
    )(p_r, shift3, s0, mu, vecs, wup, aup, gup)
    s_fin = sfin.reshape(b, nh, HEAD_DIM, HEAD_DIM).transpose(0, 1, 3, 2)
    return y, s_fin


def _head_ones():
    gi = lax.broadcasted_iota(jnp.int32, (MXU, MXU), 0) // HEAD_DIM
    gj = lax.broadcasted_iota(jnp.int32, (MXU, MXU), 1) // HEAD_DIM
    return (gi == gj).astype(F32)


def _nsa_proj_kernel(p_ref, g_ref, q_ref, rows_ref, win_ref, gate_ref, *, nsa_w, kvw):
    ones_bd = _head_ones()

    def hnorm(x, gvec):
        ms = _dot(x * x, ones_bd, HI) * (1.0 / HEAD_DIM)
        return x * lax.rsqrt(ms + RMS_EPS) * gvec

    for i in range(nsa_w // MXU):
        sl = slice(i * MXU, (i + 1) * MXU)
        q_ref[:, sl] = hnorm(p_ref[:, sl], g_ref[0:1, :])
    o = nsa_w
    rows_ref[:, 0:2 * kvw] = p_ref[:, o:o + 2 * kvw]
    rows_ref[:, 2 * kvw:3 * kvw] = hnorm(p_ref[:, o + 2 * kvw:o + 3 * kvw], g_ref[2:3, :])
    rows_ref[:, 3 * kvw:4 * kvw] = p_ref[:, o + 3 * kvw:o + 4 * kvw]
    win_ref[:, 0:kvw] = hnorm(p_ref[:, o + 4 * kvw:o + 5 * kvw], g_ref[3:4, :])
    win_ref[:, kvw:2 * kvw] = p_ref[:, o + 5 * kvw:o + 6 * kvw]
    gate_ref[...] = jax.nn.sigmoid(p_ref[:, o + 6 * kvw:])


def _nsa_project_call(p_n, qk_norm_g, nsa_w, kvw, tm):
    t, pc = p_n.shape
    assert kvw == MXU and nsa_w % MXU == 0
    gcols = pc - nsa_w - 6 * kvw
    gvec = jnp.tile(qk_norm_g, (1, MXU // HEAD_DIM))
    return pl.pallas_call(
        functools.partial(_nsa_proj_kernel, nsa_w=nsa_w, kvw=kvw),
        grid=(t // tm,),
        in_specs=[pl.BlockSpec((tm, pc), lambda i: (i, 0)),
                  pl.BlockSpec(gvec.shape, lambda i: (0, 0))],
        out_specs=[pl.BlockSpec((tm, nsa_w), lambda i: (i, 0)),
                   pl.BlockSpec((tm, 4 * kvw), lambda i: (i, 0)),
                   pl.BlockSpec((tm, 2 * kvw), lambda i: (i, 0)),
                   pl.BlockSpec((tm, gcols), lambda i: (i, 0))],
        out_shape=[jax.ShapeDtypeStruct((t, nsa_w), F32), jax.ShapeDtypeStruct((t, 4 * kvw), F32),
                   jax.ShapeDtypeStruct((t, 2 * kvw), F32), jax.ShapeDtypeStruct((t, gcols), F32)],
        compiler_params=_cparams(("arbitrary",)),
        name="nsa_project",
    )(p_n, gvec)


def _cmp_part_kernel(*refs, n_in, row_w, kvw, n_prefetch=0):
    refs = refs[n_prefetch:]
    x_refs, w_ref, o_ref = refs[:n_in], refs[n_in], refs[n_in + 1]
    for typ in range(2):
        acc = None
        for s in range(CMP_STRIDE):
            lo = s * row_w + typ * kvw
            xs = jnp.concatenate([x[0, :, lo:lo + kvw] for x in x_refs], axis=0) if n_in > 1 else x_refs[0][0, :, lo:lo + kvw]
            d = _dot(xs.astype(BF16), w_ref[typ, s])
            acc = d if acc is None else acc + d
        o_ref[0, :, typ * 2 * kvw:(typ + 1) * 2 * kvw] = acc


def _cmp_first_weights(cmp_w1, nkv):
    r2 = CMP_LEN // CMP_STRIDE
    e = cmp_w1.shape[-1]
    w1r = cmp_w1.reshape(2, r2, CMP_STRIDE, HEAD_DIM, e)
    eye = jnp.eye(nkv, dtype=F32)
    big = jnp.einsum('yhsde,gk->ysgdkhe', w1r, eye)
    return big.reshape(2, CMP_STRIDE, nkv * HEAD_DIM, nkv * r2 * e).astype(BF16)


def _cmp_parts_prompt(rows2d, w_big, b, t, kvw):
    row_w = rows2d.shape[2]
    nsub = t // CMP_STRIDE
    blk = min(nsub, LANES)
    return pl.pallas_call(
        functools.partial(_cmp_part_kernel, n_in=1, row_w=row_w, kvw=kvw),
        grid=(b, nsub // blk),
        in_specs=[pl.BlockSpec((1, blk, CMP_STRIDE * row_w), lambda i, j: (i, j, 0)),
                  pl.BlockSpec(w_big.shape, lambda i, j: (0, 0, 0, 0))],
        out_specs=pl.BlockSpec((1, blk, 4 * kvw), lambda i, j: (i, j, 0)),
        out_shape=jax.ShapeDtypeStruct((b, nsub, 4 * kvw), F32),
        compiler_params=_cparams(("arbitrary", "arbitrary")),
        name="cmp_parts",
    )(rows2d.reshape(b, nsub, CMP_STRIDE * row_w), w_big)


def _cmp_finish_kernel(p_ref, b1_ref, w2_ref, b2_ref, g_ref, o_ref, *, nc, nkv):
    tg = pl.program_id(1)
    ns = p_ref.shape[1]
    e = p_ref.shape[2] // 2
    part = p_ref[0]
    nxt = pltpu.roll(part[:, e:], ns - 1, axis=0)
    hid = part[:, :e] + nxt + b1_ref[0]
    out = _dot(jax.nn.gelu(hid).astype(BF16), w2_ref[0].astype(BF16)) + b2_ref[0]
    normed = out * lax.rsqrt(jnp.mean(out * out, axis=-1, keepdims=True) + RMS_EPS) * g_ref[...]
    out = jnp.where(tg < nkv, normed, out)
    row = lax.broadcasted_iota(jnp.int32, (ns, 1), 0)
    o_ref[0, 0] = jnp.where(row < nc, out, 0.0)


def _cmp_finish(parts, cmp_b1, cmp_w2, cmp_b2, g1, nc, nkv):
    b, ns, _ = parts.shape
    e = cmp_b1.shape[1]
    return pl.pallas_call(
        functools.partial(_cmp_finish_kernel, nc=nc, nkv=nkv),
        grid=(b, 2 * nkv),
        in_specs=[pl.BlockSpec((1, ns, 2 * e), lambda i, j: (i, 0, j)),
                  pl.BlockSpec((1, 1, e), lambda i, j: (j // nkv, 0, 0)),
                  pl.BlockSpec((1, e, HEAD_DIM), lambda i, j: (j // nkv, 0, 0)),
                  pl.BlockSpec((1, 1, HEAD_DIM), lambda i, j: (j // nkv, 0, 0)),
                  pl.BlockSpec((1, HEAD_DIM), lambda i, j: (0, 0))],
        out_specs=pl.BlockSpec((1, 1, ns, HEAD_DIM), lambda i, j: (i, j, 0, 0)),
        out_shape=jax.ShapeDtypeStruct((b, 2 * nkv, ns, HEAD_DIM), F32),
        compiler_params=_cparams(("arbitrary", "arbitrary")),
        name="cmp_finish",
    )(parts, cmp_b1.reshape(2, 1, e), cmp_w2, cmp_b2.reshape(2, 1, HEAD_DIM), g1.reshape(1, HEAD_DIM))


QT = 128


def _rel_table_np_dist(dist, table):
    return table[_rel_bucket(dist)]


def _softmax_update(s, mask, m, l):
    m_new = jnp.maximum(m, jnp.max(jnp.where(mask, s, NEG), axis=0, keepdims=True))
    alpha = jnp.exp(m - m_new)
    p = jnp.where(mask, jnp.exp(s - m_new), 0.0)
    return p, m_new, alpha, alpha * l + jnp.sum(p, axis=0, keepdims=True)


def _rank_select(score, score_ref, cur, n_sel):
    nb = score.shape[0]
    score_ref[0:nb, :] = score
    jrow = lax.broadcasted_iota(jnp.int32, score.shape, 0)

    def body(j, rank):
        other = score_ref[pl.ds(j, 1), :]
        beats = (other > score) | ((other == score) & (jrow > j))
        return rank + jnp.where(beats, 1.0, 0.0)

    rank = lax.fori_loop(0, nb, body, jnp.zeros(score.shape, F32), unroll=8)
    return jnp.where((rank < n_sel) & (jrow <= cur), 1.0, 0.0)


def _nsa_prompt_kernel(qT_ref, gT_ref, kc_ref, vcT_ref, ks_ref, vsT_ref, kw_ref, vwT_ref, bc_ref, toep_ref,
                       o_ref, pg_ref, score_ref, sel_ref, *, nc, nb, n_sel, ngrp):
    qt = pl.program_id(2)
    scale = HEAD_DIM ** -0.5
    lanes = ngrp * QT
    q = qT_ref[0, 0, 0].astype(BF16)
    iq = lax.broadcasted_iota(jnp.int32, (1, QT), 1)
    q_pos = qt * QT + iq
    tile4 = lambda x: jnp.concatenate([x] * ngrp, axis=1)

    ncp = kc_ref.shape[2]
    s = _dot(kc_ref[0, 0].astype(BF16), q) * scale
    s = s + jnp.concatenate([bc_ref[0, r] for r in range(ngrp)], axis=1)
    crow = lax.broadcasted_iota(jnp.int32, (ncp, QT), 0)
    ok_c = tile4((crow * CMP_STRIDE + (CMP_LEN - 1) <= q_pos) & (crow < nc))
    p, _, _, l = _softmax_update(s, ok_c, jnp.full((1, lanes), NEG, F32), jnp.zeros((1, lanes), F32))
    p = p * jnp.where(l > 0.0, 1.0 / jnp.where(l > 0.0, l, 1.0), 0.0)
    o_c = _dot(vcT_ref[0, 0].astype(BF16), p.astype(BF16))

    p_grp = p[:, 0:QT]
    for r in range(1, ngrp):
        p_grp = p_grp + p[:, r * QT:(r + 1) * QT]
    pad = 8
    pg_ref[...] = jnp.zeros(pg_ref.shape, F32)
    pg_ref[pad:pad + ncp, :] = p_grp
    r1 = SEL_BLOCK // CMP_STRIDE
    offs, wts = _slc_offsets()
    p_slc = None
    for o, wt in zip(offs, wts):
        term = float(wt) * pg_ref[pl.ds(pad + int(o), nb, stride=r1), :]
        p_slc = term if p_slc is None else p_slc + term
    jrow = lax.broadcasted_iota(jnp.int32, (nb, QT), 0)
    cur = q_pos // SEL_BLOCK
    forced = (jrow == 0) | (jrow == cur) | (jrow == cur - 1)
    score = jnp.where(jrow > cur, -1.0, jnp.where(forced, 1e6, p_slc))
    sel_ref[0:nb, :] = _rank_select(score, score_ref, cur, n_sel)

    ik = lax.broadcasted_iota(jnp.int32, (QT, QT), 0)
    iqq = lax.broadcasted_iota(jnp.int32, (QT, QT), 1)
    blocks_per_tile = QT // SEL_BLOCK

    def attend(kt, carry, k_ref, vT_ref, mask_fn):
        m, l, acc = carry
        delta = qt - kt
        s = _dot(k_ref[0, 0, kt].astype(BF16), q) * scale
        s = s + jnp.concatenate([toep_ref[0, r, delta] for r in range(ngrp)], axis=1)
        mask = tile4(mask_fn(kt, delta))
        p, m, alpha, l = _softmax_update(s, mask, m, l)
        acc = alpha * acc + _dot(vT_ref[0, 0, kt].astype(BF16), p.astype(BF16))
        return m, l, acc

    def sel_mask(kt, delta):
        rows = [jnp.broadcast_to(sel_ref[pl.ds(kt * blocks_per_tile + i, 1), :], (SEL_BLOCK, QT))
                for i in range(blocks_per_tile)]
        chosen = jnp.concatenate(rows, axis=0) > 0.5
        return chosen & (ik - iqq <= delta * QT)

    def win_mask(kt, delta):
        dist = delta * QT + iqq - ik
        return (dist >= 0) & (dist < WINDOW)

    init = (jnp.full((1, lanes), NEG, F32), jnp.zeros((1, lanes), F32), jnp.zeros((HEAD_DIM, lanes), F32))
    finish = lambda c: c[2] * jnp.where(c[1] > 0.0, 1.0 / jnp.where(c[1] > 0.0, c[1], 1.0), 0.0)
    o_s = finish(lax.fori_loop(0, qt + 1, lambda kt, c: attend(kt, c, ks_ref, vsT_ref, sel_mask), init))
    first = jnp.maximum(qt - WINDOW // QT, 0)
    o_w = finish(lax.fori_loop(first, qt + 1, lambda kt, c: attend(kt, c, kw_ref, vwT_ref, win_mask), init))
    g = gT_ref[0, 0, 0]
    o_ref[0, 0, 0] = g[0:1, :] * o_c + g[1:2, :] * o_s + g[2:3, :] * o_w


def _nsa_prompt(p_n, W, b, t, nkv, ngrp, tm):
    nsa_w = nkv * ngrp * HEAD_DIM
    kvw = nkv * HEAD_DIM
    qn, rows2d, win2d, gates = _nsa_project_call(p_n, W['qk_norm_g'], nsa_w, kvw, tm)
    rows = rows2d.reshape(b, t, 4, nkv, HEAD_DIM)
    win = win2d.reshape(b, t, 2, nkv, HEAD_DIM)
    ns = t // CMP_STRIDE
    nc = ns - CMP_LEN // CMP_STRIDE + 1
    nb = t // SEL_BLOCK
    n_sel = min(SEL_TOPK, nb)
    nqt = t // QT
    parts = _cmp_parts_prompt(rows2d.reshape(b, t, 4 * kvw), _cmp_first_weights(W['cmp_w1'], nkv), b, t, kvw)
    kvc = _cmp_finish(parts, W['cmp_b1'], W['cmp_w2'], W['cmp_b2'], W['qk_norm_g'][1], nc, nkv)
    kc = kvc[:, :nkv]
    vcT = kvc[:, nkv:].transpose(0, 1, 3, 2)
    qT = qn.reshape(b, nqt, QT, nkv, ngrp, HEAD_DIM).transpose(0, 3, 1, 5, 4, 2).reshape(b, nkv, nqt, HEAD_DIM, ngrp * QT)
    ng = 3 * nkv * ngrp
    gT = gates[:, :ng].reshape(b, nqt, QT, nkv, ngrp, 3).transpose(0, 3, 1, 5, 4, 2).reshape(b, nkv, nqt, 3, ngrp * QT)
    gT = jnp.pad(gT, ((0, 0), (0, 0), (0, 0), (0, 5), (0, 0)))
    k_tiles = lambda x: x.transpose(0, 2, 1, 3).reshape(b, nkv, nqt, QT, HEAD_DIM).astype(BF16)
    vT_tiles = lambda x: x.reshape(b, nqt, QT, nkv, HEAD_DIM).transpose(0, 3, 1, 4, 2).astype(BF16)
    ks, vsT = k_tiles(rows[:, :, 2]), vT_tiles(rows[:, :, 3])
    kw, vwT = k_tiles(win[:, :, 0]), vT_tiles(win[:, :, 1])
    table = W['rel_bias'].astype(F32)
    c_end = jnp.arange(ns) * CMP_STRIDE + CMP_LEN - 1
    bias_c = _rel_table_np_dist(jnp.arange(t)[None, :] - c_end[:, None], table)
    bias_c = bias_c.transpose(2, 0, 1).reshape(nkv, ngrp, ns, t)
    dd = (jnp.arange(nqt)[:, None, None] * QT + jnp.arange(QT)[None, None, :] - jnp.arange(QT)[None, :, None])
    toep = _rel_table_np_dist(dd, table).transpose(3, 0, 1, 2).reshape(nkv, ngrp, nqt, QT, QT)
    lanes = ngrp * QT
    kv_spec = lambda shape: pl.BlockSpec((1, 1) + shape, lambda i, g, j: (i, g) + (0,) * len(shape))
    yT = pl.pallas_call(
        functools.partial(_nsa_prompt_kernel, nc=nc, nb=nb, n_sel=n_sel, ngrp=ngrp),
        grid=(b, nkv, nqt),
        in_specs=[pl.BlockSpec((1, 1, 1, HEAD_DIM, lanes), lambda i, g, j: (i, g, j, 0, 0)),
                  pl.BlockSpec((1, 1, 1, 8, lanes), lambda i, g, j: (i, g, j, 0, 0)),
                  kv_spec((ns, HEAD_DIM)), kv_spec((HEAD_DIM, ns)),
                  kv_spec((nqt, QT, HEAD_DIM)), kv_spec((nqt, HEAD_DIM, QT)),
                  kv_spec((nqt, QT, HEAD_DIM)), kv_spec((nqt, HEAD_DIM, QT)),
                  pl.BlockSpec((1, ngrp, ns, QT), lambda i, g, j: (g, 0, 0, j)),
                  pl.BlockSpec((1, ngrp, nqt, QT, QT), lambda i, g, j: (g, 0, 0, 0, 0))],
        out_specs=pl.BlockSpec((1, 1, 1, HEAD_DIM, lanes), lambda i, g, j: (i, g, j, 0, 0)),
        out_shape=jax.ShapeDtypeStruct((b, nkv, nqt, HEAD_DIM, lanes), F32),
        scratch_shapes=[pltpu.VMEM((ns + 16, QT), F32), pltpu.VMEM((_rup(nb, 8), QT), F32),
                        pltpu.VMEM((_rup(nb, 8), QT), F32)],
        compiler_params=_cparams(("arbitrary", "arbitrary", "arbitrary")),
        name="nsa_prompt_attn",
    )(qT, gT, kc, vcT, ks, vsT, kw, vwT, bias_c, toep)
    y = yT.reshape(b, nkv, nqt, HEAD_DIM, ngrp, QT).transpose(0, 2, 5, 1, 4, 3).reshape(b * t, nsa_w)
    return y, rows, win[:, t - min(WINDOW, t):]


def _cmp_parts_sample(cache3, page_table, w_big, kvw, npg):
    b, n_pages = page_table.shape
    sub = cache3.shape[1]
    row_w = cache3.shape[2] // CMP_STRIDE
    in_specs = [pl.BlockSpec((1, sub, cache3.shape[2]), (lambda i, j, pt, k=k: (pt[i, j * npg + k], 0, 0)))
                for k in range(npg)]
    in_specs.append(pl.BlockSpec(w_big.shape, lambda i, j, pt: (0, 0, 0, 0)))
    return pl.pallas_call(
        functools.partial(_cmp_part_kernel, n_in=npg, row_w=row_w, kvw=kvw, n_prefetch=1),
        grid_spec=pltpu.PrefetchScalarGridSpec(
            num_scalar_prefetch=1, grid=(b, n_pages // npg), in_specs=in_specs,
            out_specs=pl.BlockSpec((1, npg * sub, 4 * kvw), lambda i, j, pt: (i, j, 0))),
        out_shape=jax.ShapeDtypeStruct((b, n_pages * sub, 4 * kvw), F32),
        compiler_params=_cparams(("arbitrary", "arbitrary")),
        name="cmp_parts_paged",
    )(page_table, *([cache3] * npg), w_big)


def _inv_pos(l):
    return jnp.where(l > 0.0, 1.0 / jnp.where(l > 0.0, l, 1.0), 0.0)


def _nsa_sample_kernel(*refs, npg, n_steps, nc, nb, n_sel, past, ds, keep, ngrp, n_lanes):
    pt_ref = refs[0]
    q_ref, g_ref, kc_ref, vcT_ref, bc_ref, win_ref, wnew_ref, bw_ref, rnew_ref, bs_ref = refs[1:11]
    page_refs = refs[11:11 + npg]
    o_ref = refs[11 + npg]
    (m_ref, l_ref, acc_ref, base_ref, sel_ref, pg_ref, score_ref,
     kw_ref, vw_ref, kn_ref, vn_ref) = refs[12 + npg:]
    del pt_ref
    j = pl.program_id(1)
    scale = HEAD_DIM ** -0.5
    kvw = q_ref.shape[1]
    qbd = q_ref[0]
    lane = lax.broadcasted_iota(jnp.int32, (1, LANES), 1)
    qi = (lane // ngrp) % ds
    q_pos = past + qi
    n_pages = npg * n_steps

    @pl.when(j == 0)
    def _():
        nsp = kc_ref.shape[1]
        s = _dot(kc_ref[0].astype(BF16), qbd) * scale + bc_ref[...]
        crow = lax.broadcasted_iota(jnp.int32, (nsp, LANES), 0)
        ok = (crow * CMP_STRIDE + (CMP_LEN - 1) <= q_pos) & (crow < nc)
        p, _, _, l = _softmax_update(s, ok, jnp.full((1, LANES), NEG, F32), jnp.zeros((1, LANES), F32))
        p = p * _inv_pos(l)
        o_c = _dot(vcT_ref[0].astype(BF16), p.astype(BF16))
        li = lax.broadcasted_iota(jnp.int32, (LANES, LANES), 0)
        lj = lax.broadcasted_iota(jnp.int32, (LANES, LANES), 1)
        fold = jnp.where((li // ngrp == lj) & (li < n_lanes), 1.0, 0.0)
        p_grp = _dot(p, fold, HI)
        pad = 8
        pg_ref[...] = jnp.zeros(pg_ref.shape, F32)
        pg_ref[pad:pad + nsp, :] = p_grp
        nbp = sel_ref.shape[0]
        r1 = SEL_BLOCK // CMP_STRIDE
        offs, wts = _slc_offsets()
        p_slc = None
        for o, wt in zip(offs, wts):
            term = float(wt) * pg_ref[pl.ds(pad + int(o), nbp, stride=r1), :]
            p_slc = term if p_slc is None else p_slc + term
        cur = (past + lane % ds) // SEL_BLOCK
        jrow = lax.broadcasted_iota(jnp.int32, (nbp, LANES), 0)
        forced = (jrow == 0) | (jrow == cur) | (jrow == cur - 1)
        score = jnp.where((jrow > cur) | (jrow >= nb), -1.0, jnp.where(forced, 1e6, p_slc))
        selg = _rank_select(score, score_ref, cur, n_sel)
        unfold = jnp.where((li == lj // ngrp) & (lj < n_lanes), 1.0, 0.0)
        sel_ref[...] = _dot(selg, unfold)
        wk = kw_ref.shape[0]
        kw_ref[...] = jnp.zeros(kw_ref.shape, F32)
        vw_ref[...] = jnp.zeros(vw_ref.shape, F32)
        kw_ref[0:keep, :] = win_ref[0, :, 0:kvw]
        vw_ref[0:keep, :] = win_ref[0, :, kvw:2 * kvw]
        nn = wnew_ref.shape[1]
        kw_ref[keep:keep + nn, :] = wnew_ref[0, :, 0:kvw]
        vw_ref[keep:keep + nn, :] = wnew_ref[0, :, kvw:2 * kvw]
        s = _dot(kw_ref[...].astype(BF16), qbd) * scale + bw_ref[...]
        irow = lax.broadcasted_iota(jnp.int32, (wk, LANES), 0)
        dist = jnp.where(irow < keep, keep + qi - irow, qi - (irow - keep))
        ok = (dist >= 0) & (dist < WINDOW) & (irow < keep + ds)
        p, _, _, l = _softmax_update(s, ok, jnp.full((1, LANES), NEG, F32), jnp.zeros((1, LANES), F32))
        p = p * _inv_pos(l)
        o_w = _dot(vw_ref[...].T.astype(BF16), p.astype(BF16))
        g = g_ref[0]
        base_ref[...] = g[0:1, :] * o_c + g[2:3, :] * o_w
        m_ref[...] = jnp.full(m_ref.shape, NEG, F32)
        l_ref[...] = jnp.zeros(l_ref.shape, F32)
        acc_ref[...] = jnp.zeros(acc_ref.shape, F32)
        kn_ref[...] = jnp.zeros(kn_ref.shape, F32)
        vn_ref[...] = jnp.zeros(vn_ref.shape, F32)
        kn_ref[0:nn, :] = rnew_ref[0, :, 2 * kvw:3 * kvw]
        vn_ref[0:nn, :] = rnew_ref[0, :, 3 * kvw:4 * kvw]

    ik = lax.broadcasted_iota(jnp.int32, (PAGE_SIZE, LANES), 0)
    blocks_per_page = PAGE_SIZE // SEL_BLOCK

    def page_update(k, v, page):
        s = _dot(k.astype(BF16), qbd) * scale + bs_ref[page]
        rows = [jnp.broadcast_to(sel_ref[pl.ds(page * blocks_per_page + i, 1), :], (SEL_BLOCK, LANES))
                for i in range(blocks_per_page)]
        mask = (jnp.concatenate(rows, axis=0) > 0.5) & (page * PAGE_SIZE + ik <= q_pos)
        p, m_new, alpha, l_new = _softmax_update(s, mask, m_ref[...], l_ref[...])
        m_ref[...] = m_new
        l_ref[...] = l_new
        acc_ref[...] = alpha * acc_ref[...] + _dot(v.T.astype(BF16), p.astype(BF16))

    for k in range(npg):
        blk = page_refs[k]
        page_update(blk[0, :, 0:kvw], blk[0, :, kvw:2 * kvw], j * npg + k)

    @pl.when(j == n_steps - 1)
    def _():
        page_update(kn_ref[...], vn_ref[...], n_pages)
        o_ref[0] = base_ref[...] + g_ref[0][1:2, :] * (acc_ref[...] * _inv_pos(l_ref[...]))


def _nsa_sample(p_n, cache_kv, cache_win, page_table, W, db, ds, nkv, ngrp):
    nsa_w = nkv * ngrp * HEAD_DIM
    kvw = nkv * HEAD_DIM
    row_w = 4 * kvw
    qn, rows2d, win2d, gates = _nsa_project_call(p_n, W['qk_norm_g'], nsa_w, kvw, db * ds)
    rows_new = rows2d.reshape(db, ds, 4, nkv, HEAD_DIM)
    win_new = win2d.reshape(db, ds, 2, nkv, HEAD_DIM)
    n_pool = cache_kv.shape[0]
    n_pages = page_table.shape[1]
    past = n_pages * PAGE_SIZE
    keep = cache_win.shape[1]
    tot = past + _rup(ds, SEL_BLOCK)
    ns = tot // CMP_STRIDE
    nc = ns - CMP_LEN // CMP_STRIDE + 1
    nb = tot // SEL_BLOCK
    n_sel = min(SEL_TOPK, nb)
    nsp = _rup(ns, LANES)
    nbp = _rup(nb, 8)
    n_lanes = nkv * ds * ngrp
    assert n_lanes <= LANES and ds <= 8
    w_big = _cmp_first_weights(W['cmp_w1'], nkv)
    sub = PAGE_SIZE // CMP_STRIDE
    cache3 = cache_kv.reshape(n_pool, sub, CMP_STRIDE * row_w)
    npg = min(16, n_pages)
    parts_past = _cmp_parts_sample(cache3, page_table, w_big, kvw, npg)
    rows_pad = jnp.pad(rows2d.reshape(db, ds, row_w), ((0, 0), (0, PAGE_SIZE - ds), (0, 0)))
    parts_new = _cmp_parts_prompt(rows_pad, w_big, db, PAGE_SIZE, kvw)
    n_new = _rup(ds, SEL_BLOCK) // CMP_STRIDE
    parts = jnp.concatenate([parts_past, parts_new[:, :n_new], jnp.zeros((db, nsp - ns, row_w), F32)], axis=1)
    kvc = _cmp_finish(parts, W['cmp_b1'], W['cmp_w2'], W['cmp_b2'], W['qk_norm_g'][1], nc, nkv)
    kc_cat = kvc[:, :nkv].transpose(0, 2, 1, 3).reshape(db, nsp, kvw)
    vcT_cat = kvc[:, nkv:].transpose(0, 1, 3, 2).reshape(db, kvw, nsp)
    lane_pad = LANES - n_lanes
    q5 = qn.reshape(db, ds, nkv, ngrp, HEAD_DIM)
    qbd = jnp.einsum('bqgrd,gk->bgdkqr', q5, jnp.eye(nkv, dtype=F32)).reshape(db, kvw, n_lanes)
    qbd = jnp.pad(qbd, ((0, 0), (0, 0), (0, lane_pad))).astype(BF16)
    ng = 3 * nkv * ngrp
    gT = gates[:, :ng].reshape(db, ds, nkv, ngrp, 3).transpose(0, 4, 2, 1, 3).reshape(db, 3, n_lanes)
    gT = jnp.pad(gT, ((0, 0), (0, 5), (0, lane_pad)))
    lane = np.arange(LANES)
    live = lane < n_lanes
    head_of_lane = np.where(live, (lane // (ds * ngrp)) * ngrp + lane % ngrp, 0)
    qi = np.where(live, (lane // ngrp) % ds, 0)
    table_l = W['rel_bias'].astype(F32)[:, head_of_lane]
    bias_of = lambda dist: jnp.take_along_axis(table_l, _rel_bucket(dist), axis=0)
    c_end = np.arange(nsp) * CMP_STRIDE + CMP_LEN - 1
    bias_c = bias_of(jnp.asarray(past + qi[None, :] - c_end[:, None], jnp.int32))
    pos = np.arange((n_pages + 1) * PAGE_SIZE)
    bias_s = bias_of(jnp.asarray(past + qi[None, :] - pos[:, None], jnp.int32)).reshape(n_pages + 1, PAGE_SIZE, LANES)
    wk = _rup(keep + 8, LANES)
    irow = np.arange(wk)[:, None]
    dist_w = np.where(irow < keep, keep + qi[None, :] - irow, qi[None, :] - (irow - keep))
    bias_w = bias_of(jnp.asarray(dist_w, jnp.int32))
    win_c = cache_win.reshape(db, keep, 2 * kvw)
    wnew8 = jnp.pad(win2d.reshape(db, ds, 2 * kvw), ((0, 0), (0, 8 - ds), (0, 0)))
    rnew8 = jnp.pad(rows2d.reshape(db, ds, row_w), ((0, 0), (0, 8 - ds), (0, 0)))
    cache_pages = cache_kv.reshape(n_pool, PAGE_SIZE, row_w)
    n_steps = n_pages // npg
    per_b = lambda shape: pl.BlockSpec((1,) + shape, lambda i, j, pt: (i,) + (0,) * len(shape))
    const = lambda shape: pl.BlockSpec(shape, lambda i, j, pt: (0,) * len(shape))
    in_specs = [per_b((kvw, LANES)), per_b((8, LANES)), per_b((nsp, kvw)), per_b((kvw, nsp)), const((nsp, LANES)),
                per_b((keep, 2 * kvw)), per_b((8, 2 * kvw)), const((wk, LANES)), per_b((8, row_w)),
                const((n_pages + 1, PAGE_SIZE, LANES))]
    in_specs += [pl.BlockSpec((1, PAGE_SIZE, 2 * kvw), (lambda i, j, pt, k=k: (pt[i, j * npg + k], 0, 1)))
                 for k in range(npg)]
    yT = pl.pallas_call(
        functools.partial(_nsa_sample_kernel, npg=npg, n_steps=n_steps, nc=nc, nb=nb, n_sel=n_sel, past=past,
                          ds=ds, keep=keep, ngrp=ngrp, n_lanes=n_lanes),
        grid_spec=pltpu.PrefetchScalarGridSpec(
            num_scalar_prefetch=1, grid=(db, n_steps), in_specs=in_specs,
            out_specs=pl.BlockSpec((1, kvw, LANES), lambda i, j, pt: (i, 0, 0)),
            scratch_shapes=[pltpu.VMEM((1, LANES), F32), pltpu.VMEM((1, LANES), F32), pltpu.VMEM((kvw, LANES), F32),
                            pltpu.VMEM((kvw, LANES), F32), pltpu.VMEM((nbp, LANES), F32),
                            pltpu.VMEM((nsp + 16, LANES), F32), pltpu.VMEM((nbp, LANES), F32),
                            pltpu.VMEM((wk, kvw), F32), pltpu.VMEM((wk, kvw), F32),
                            pltpu.VMEM((PAGE_SIZE, kvw), F32), pltpu.VMEM((PAGE_SIZE, kvw), F32)]),
        out_shape=jax.ShapeDtypeStruct((db, kvw, LANES), F32),
        compiler_params=_cparams(("arbitrary", "arbitrary")),
        name="nsa_sample_attn",
    )(page_table, qbd, gT, kc_cat, vcT_cat, bias_c, win_c, wnew8, bias_w, rnew8, bias_s, *([cache_pages] * npg))
    y6 = yT[:, :, :n_lanes].reshape(db, nkv, HEAD_DIM, nkv, ds, ngrp)
    y = jnp.einsum('bgdgqr->bqgrd', y6).reshape(db * ds, nsa_w)
    win_all = jnp.concatenate([cache_win, win_new.astype(cache_win.dtype)], axis=1)
    n_keep = min(WINDOW, past + ds)
    return y, rows_new, win_all[:, win_all.shape[1] - n_keep:]


def _rms(x, g):
    return x * lax.rsqrt(jnp.mean(x * x, axis=-1, keepdims=True) + RMS_EPS) * g


def _masked_softmax(logits, mask):
    z = jnp.where(mask, logits, NEG)
    return jax.nn.softmax(z, axis=-1) * jnp.any(mask, axis=-1, keepdims=True)


def _rel_bucket(dist):
    d = jnp.maximum(dist, 0)
    exact = REL_BUCKETS // 2
    ratio = jnp.maximum(d, exact).astype(F32) / exact
    large = exact + (jnp.log(ratio) / math.log(REL_MAX_DIST / exact) * (REL_BUCKETS - exact)).astype(jnp.int32)
    return jnp.where(d < exact, d, jnp.minimum(large, REL_BUCKETS - 1))


def _compress(k, w1, b1, w2, b2):
    b, l, g, dk = k.shape
    ns = l // CMP_STRIDE
    r2 = CMP_LEN // CMP_STRIDE
    nc = ns - r2 + 1
    kb = k[:, :ns * CMP_STRIDE].reshape(b, ns, CMP_STRIDE, g, dk)
    w1r = w1.reshape(r2, CMP_STRIDE, dk, -1)
    part = jnp.einsum('bnsgd,hsde->hbnge', kb, w1r)
    hid = part[0, :, :nc] + b1
    for h in range(1, r2):
        hid = hid + part[h, :, h:h + nc]
    return jnp.einsum('bnge,ed->bngd', jax.nn.gelu(hid), w2) + b2


def _to_blocks(k):
    b, l, g, dk = k.shape
    nb = l // SEL_BLOCK
    return k.reshape(b, nb, SEL_BLOCK, g, dk).transpose(0, 3, 1, 2, 4).reshape(b, g, nb, SEL_BLOCK * dk)


def _slc_offsets():
    r1 = SEL_BLOCK // CMP_STRIDE
    r2 = CMP_LEN // CMP_STRIDE
    offs = np.arange(-(r2 - 1), r1)
    wts = np.array([sum(1 for m in range(r1) for n in range(r2) if m - n == o) for o in offs], np.float32)
    return offs, wts


def _nsa_project(p_n, qk_norm_g, nkv, ngrp):
    b, t, _ = p_n.shape
    nsa_w = nkv * ngrp * HEAD_DIM
    kvw = nkv * HEAD_DIM
    q = _rms(p_n[..., :nsa_w].reshape(b, t, nkv, ngrp, HEAD_DIM), qk_norm_g[0])
    kv = p_n[..., nsa_w:nsa_w + 6 * kvw].reshape(b, t, 6, nkv, HEAD_DIM)
    gates = jax.nn.sigmoid(p_n[..., nsa_w + 6 * kvw:nsa_w + 6 * kvw + 3 * nkv * ngrp]).reshape(b, t, nkv, ngrp, 3)
    rows = jnp.stack([kv[:, :, 0], kv[:, :, 1], _rms(kv[:, :, 2], qk_norm_g[2]), kv[:, :, 3]], axis=2)
    win = jnp.stack([_rms(kv[:, :, 4], qk_norm_g[3]), kv[:, :, 5]], axis=2)
    return q, gates, rows, win


def _nsa_context(rows, W):
    kc = _rms(_compress(rows[:, :, 0], W['cmp_w1'][0], W['cmp_b1'][0], W['cmp_w2'][0], W['cmp_b2'][0]), W['qk_norm_g'][1])
    vc = _compress(rows[:, :, 1], W['cmp_w1'][1], W['cmp_b1'][1], W['cmp_w2'][1], W['cmp_b2'][1])
    c_end = jnp.arange(kc.shape[1]) * CMP_STRIDE + CMP_LEN - 1
    return kc, vc, c_end, _to_blocks(rows[:, :, 2]), _to_blocks(rows[:, :, 3])


def _nsa_core(q, gates, q_pos, kc, vc, c_end, ks_blk, vs_blk, kw, vw, kw_pos, rel_bias):
    b, tq, g, r, dk = q.shape
    nb = ks_blk.shape[2]
    nc = kc.shape[1]
    scale = HEAD_DIM ** -0.5
    table = rel_bias.astype(F32).reshape(REL_BUCKETS, g, r)
    dist_c = q_pos[:, None] - c_end[None, :]
    bias_c = table[_rel_bucket(dist_c)].transpose(2, 3, 0, 1)
    lg_c = jnp.einsum('bqgrd,bcgd->bgrqc', q, kc) * scale + bias_c
    p_c = _masked_softmax(lg_c, dist_c >= 0)
    o_c = jnp.einsum('bgrqc,bcgd->bqgrd', p_c, vc)
    offs, wts = _slc_offsets()
    jb = jnp.arange(nb)
    cidx = (SEL_BLOCK // CMP_STRIDE) * jb[:, None] + offs[None, :]
    cval = (cidx >= 0) & (cidx < nc)
    p_grp = jnp.sum(p_c, axis=2)
    p_slc = jnp.sum(p_grp[..., jnp.clip(cidx, 0, nc - 1)] * (wts * cval), axis=-1)
    cur = q_pos // SEL_BLOCK
    forced = (jb[None] == 0) | (jb[None] == cur[:, None]) | (jb[None] == cur[:, None] - 1)
    future = jb[None] > cur[:, None]
    score = jnp.where(future, -1.0, jnp.where(forced, 1e6, p_slc))
    n_sel = min(SEL_TOPK, nb)
    _, sel = lax.top_k(score, n_sel)
    sel_ok = sel <= cur[:, None]
    bi = jnp.arange(b)[:, None, None]
    gi = jnp.arange(g)[None, :, None]
    flat = sel.reshape(b, g, tq * n_sel)
    ksg = ks_blk[bi, gi, flat].reshape(b, g, tq, n_sel * SEL_BLOCK, dk)
    vsg = vs_blk[bi, gi, flat].reshape(b, g, tq, n_sel * SEL_BLOCK, dk)
    pos4 = sel[..., None] * SEL_BLOCK + jnp.arange(SEL_BLOCK)
    ok_s = (sel_ok[..., None] & (pos4 <= q_pos[:, None, None])).reshape(b, g, tq, n_sel * SEL_BLOCK)
    pos_s = pos4.reshape(b, g, tq, n_sel * SEL_BLOCK)
    tg = table.transpose(1, 0, 2)
    bias_s = tg[jnp.arange(g)[None, :, None, None], _rel_bucket(q_pos[:, None] - pos_s)].transpose(0, 1, 4, 2, 3)
    lg_s = jnp.einsum('bqgrd,bgqkd->bgrqk', q, ksg) * scale + bias_s
    p_s = _masked_softmax(lg_s, ok_s[:, :, None])
    o_s = jnp.einsum('bgrqk,bgqkd->bqgrd', p_s, vsg)
    dist_w = q_pos[:, None] - kw_pos[None, :]
    ok_w = (dist_w >= 0) & (dist_w < WINDOW) & (kw_pos[None, :] >= 0)
    bias_w = table[_rel_bucket(dist_w)].transpose(2, 3, 0, 1)
    lg_w = jnp.einsum('bqgrd,bkgd->bgrqk', q, kw) * scale + bias_w
    p_w = _masked_softmax(lg_w, ok_w)
    o_w = jnp.einsum('bgrqk,bkgd->bqgrd', p_w, vw)
    o = gates[..., 0:1] * o_c + gates[..., 1:2] * o_s + gates[..., 2:3] * o_w
    return o.reshape(b, tq, g * r * dk)


def _nsa_prompt_jnp(p_n, W, nkv, ngrp):
    q, gates, rows, win = _nsa_project(p_n, W['qk_norm_g'], nkv, ngrp)
    b, s = q.shape[:2]
    kc, vc, c_end, ks_blk, vs_blk = _nsa_context(rows, W)
    win_pad = jnp.pad(win, ((0, 0), (WINDOW, 0), (0, 0), (0, 0), (0, 0)))
    qb_sz = 128

    def block(i):
        start = i * qb_sz
        qb = lax.dynamic_slice_in_dim(q, start, qb_sz, axis=1)
        gb = lax.dynamic_slice_in_dim(gates, start, qb_sz, axis=1)
        wb = lax.dynamic_slice_in_dim(win_pad, start, WINDOW + qb_sz, axis=1)
        q_pos = start + jnp.arange(qb_sz)
        kw_pos = start - WINDOW + jnp.arange(WINDOW + qb_sz)
        return _nsa_core(qb, gb, q_pos, kc, vc, c_end, ks_blk, vs_blk, wb[:, :, 0], wb[:, :, 1], kw_pos, W['rel_bias'])

    o = lax.map(block, jnp.arange(s // qb_sz))
    o = o.transpose(1, 0, 2, 3).reshape(b, s, -1)
    return o, rows, win[:, s - min(WINDOW, s):]


def _nsa_sample_jnp(p_n, cache_kv, cache_win, page_table, W, nkv, ngrp, db):
    p_n = p_n.reshape(db, p_n.shape[0] // db, -1)
    q, gates, rows_new, win_new = _nsa_project(p_n, W['qk_norm_g'], nkv, ngrp)
    db, ds = q.shape[:2]
    past_len = page_table.shape[1] * PAGE_SIZE
    past_rows = cache_kv[page_table].reshape(db, past_len, 4, nkv, HEAD_DIM)
    pad = (-ds) % SEL_BLOCK
    rows = jnp.concatenate([past_rows, jnp.pad(rows_new, ((0, 0), (0, pad), (0, 0), (0, 0), (0, 0)))], axis=1)
    kc, vc, c_end, ks_blk, vs_blk = _nsa_context(rows, W)
    keep = cache_win.shape[1]
    win_all = jnp.concatenate([cache_win, win_new], axis=1)
    q_pos = past_len + jnp.arange(ds)
    kw_pos = past_len - keep + jnp.arange(keep + ds)
    o = _nsa_core(q, gates, q_pos, kc, vc, c_end, ks_blk, vs_blk, win_all[:, :, 0], win_all[:, :, 1], kw_pos, W['rel_bias'])
    n_keep = min(WINDOW, past_len + ds)
    return o.reshape(db * ds, -1), rows_new, win_all[:, win_all.shape[1] - n_keep:]


def _top_values(x, k):
    n = x.shape[0]
    row = lax.broadcasted_iota(jnp.int32, x.shape, 0)
    vals = []
    for _ in range(k):
        m = jnp.max(x, axis=0, keepdims=True)
        vals.append(m)
        first = jnp.min(jnp.where(x == m, row, n), axis=0, keepdims=True)
        x = jnp.where(row == first, -jnp.inf, x)
    return vals


def _peer_route_kernel(hT_ref, wq_ref, sk_ref, s1_ref, c1_ref, s2_ref, e2_ref, tau_ref, *, nheads, topk):
    nk, half = sk_ref.shape[1], sk_ref.shape[2]
    qT = _dot(wq_ref[...], hT_ref[...])
    for h in range(nheads):
        base = h * 2 * half
        s1 = _dot(sk_ref[0].astype(BF16), qT[base:base + half].astype(BF16))
        s2 = _dot(sk_ref[1].astype(BF16), qT[base + half:base + 2 * half].astype(BF16))
        v1 = _top_values(s1, topk)
        v2 = _top_values(s2, topk)
        cand = jnp.concatenate([v1[a] + v2[b] for a in range(topk) for b in range(topk) if (a + 1) * (b + 1) <= topk],
                               axis=0)
        tau = _top_values(cand, topk)[-1]
        e1 = jnp.exp(s1 - v1[0])
        e2 = jnp.exp(s2 - v2[0])
        z = jnp.sum(jnp.where(cand >= tau, jnp.exp(cand - (v1[0] + v2[0])), 0.0), axis=0, keepdims=True)
        s1_ref[h] = s1
        s2_ref[h] = s2
        c1_ref[h] = e1 / z
        e2_ref[h] = e2
        tau_ref[h] = tau


def _peer_expert_kernel(hT_ref, x1_ref, ga_ref, s1_ref, c1_ref, s2_ref, e2_ref, tau_ref, u_ref, v_ref,
                        o_ref, acc_ref, *, nheads, n_eblocks):
    eb = pl.program_id(1)
    nk = s2_ref.shape[1]
    rows_per_block = u_ref.shape[0] // nk

    @pl.when(eb == 0)
    def _():
        acc_ref[...] = jnp.zeros(acc_ref.shape, F32)

    act = jax.nn.gelu(_dot(u_ref[...], hT_ref[...]))
    gates = []
    for i in range(rows_per_block):
        i1 = eb * rows_per_block + i
        wd = None
        for h in range(nheads):
            cand = s1_ref[h, pl.ds(i1, 1), :] + s2_ref[h]
            term = jnp.where(cand >= tau_ref[h], e2_ref[h], 0.0) * c1_ref[h, pl.ds(i1, 1), :]
            wd = term if wd is None else wd + term
        gates.append(wd)
    g = jnp.concatenate(gates, axis=0) * act
    acc_ref[...] += _dot(g.T.astype(BF16), v_ref[...])

    @pl.when(eb == n_eblocks - 1)
    def _():
        o_ref[...] = x1_ref[...] + ga_ref[0] * acc_ref[...]


def _peer(h2, x1, ga, W, tm):
    t, d = h2.shape
    sub_keys = W['peer_sub_keys']
    nk, half = sub_keys.shape[1], sub_keys.shape[2]
    qd = W['peer_w_query'].shape[1]
    nheads = qd // (2 * half)
    hT = h2.T
    wqT = W['peer_w_query'].T.astype(BF16)
    route_shape = jax.ShapeDtypeStruct((nheads, nk, t), F32)
    rspec = pl.BlockSpec((nheads, nk, tm), lambda i: (0, 0, i))
    s1, c1, s2, e2, tau = pl.pallas_call(
        functools.partial(_peer_route_kernel, nheads=nheads, topk=PEER_TOPK),
        grid=(t // tm,),
        in_specs=[pl.BlockSpec((d, tm), lambda i: (0, i)),
                  pl.BlockSpec((qd, d), lambda i: (0, 0)),
                  pl.BlockSpec(sub_keys.shape, lambda i: (0, 0, 0))],
        out_specs=[rspec, rspec, rspec, rspec, pl.BlockSpec((nheads, 1, tm), lambda i: (0, 0, i))],
        out_shape=[route_shape] * 4 + [jax.ShapeDtypeStruct((nheads, 1, t), F32)],
        compiler_params=_cparams(("arbitrary",)),
        name="peer_route",
    )(hT, wqT, sub_keys)
    eblk = MXU
    n_eblocks = W['peer_u'].shape[0] // eblk
    nmod, rows, _ = ga.shape
    tiles_per_mod = (t // tm) // nmod
    rspec2 = pl.BlockSpec((nheads, nk, tm), lambda i, e: (0, 0, i))
    return pl.pallas_call(
        functools.partial(_peer_expert_kernel, nheads=nheads, n_eblocks=n_eblocks),
        grid=(t // tm, n_eblocks),
        in_specs=[pl.BlockSpec((d, tm), lambda i, e: (0, i)),
                  pl.BlockSpec((tm, d), lambda i, e: (i, 0)),
                  pl.BlockSpec((1, rows, d), lambda i, e: (i // tiles_per_mod, 0, 0)),
                  rspec2, rspec2, rspec2, rspec2,
                  pl.BlockSpec((nheads, 1, tm), lambda i, e: (0, 0, i)),
                  pl.BlockSpec((eblk, d), lambda i, e: (e, 0)),
                  pl.BlockSpec((eblk, d), lambda i, e: (e, 0))],
        out_specs=pl.BlockSpec((tm, d), lambda i, e: (i, 0)),
        out_shape=jax.ShapeDtypeStruct((t, d), F32),
        scratch_shapes=[pltpu.VMEM((tm, d), F32)],
        compiler_params=_cparams(("arbitrary", "arbitrary")),
        name="peer_experts",
    )(hT, x1, ga, s1, c1, s2, e2, tau, W['peer_u_bf16'], W['peer_v_bf16'])


def _peer_jnp(h, W):
    n, d = h.shape
    sub_keys, expert_u, expert_v = W['peer_sub_keys'], W['peer_u'], W['peer_v']
    nk, half = sub_keys.shape[1], sub_keys.shape[2]
    nheads = W['peer_w_query'].shape[1] // (2 * half)
    cs = min(128, n)
    hc = h.reshape(-1, cs, d)

    def chunk(x):
        qh = (x @ W['peer_w_query']).reshape(cs, nheads, 2, half)
        s1 = jnp.einsum('chd,kd->chk', qh[:, :, 0], sub_keys[0])
        s2 = jnp.einsum('chd,kd->chk', qh[:, :, 1], sub_keys[1])
        v1, i1 = lax.top_k(s1, PEER_TOPK)
        v2, i2 = lax.top_k(s2, PEER_TOPK)
        cand = (v1[..., :, None] + v2[..., None, :]).reshape(cs, nheads, -1)
        cidx = (i1[..., :, None] * nk + i2[..., None, :]).reshape(cs, nheads, -1)
        top, pos = lax.top_k(cand, PEER_TOPK)
        eidx = jnp.take_along_axis(cidx, pos, axis=-1)
        gw = jax.nn.softmax(top, axis=-1)
        act = jax.nn.gelu(jnp.einsum('chkd,cd->chk', expert_u[eidx], x))
        return jnp.einsum('chk,chkd->cd', gw * act, expert_v[eidx])

    return lax.map(chunk, hc).reshape(-1, d)


def _layer(x, mods, nsa_fn, shift_prev, wkv0, W, tm, rwkv_chunk):
    b, t, d = x.shape
    cw = W['rwkv_w0'].shape[0]
    nw, na, ng = W['rwkv_w_up'].shape[0], W['rwkv_a_up'].shape[0], W['rwkv_g_up'].shape[0]
    rwkv_proj = 3 * cw + nw + na + ng
    sh1, sc1, ga1, sh2, sc2, ga2 = mods
    xf = x.reshape(b * t, d)
    if (b * t) % tm == 0 and t % tm == 0:
        as_mod = lambda m: m.reshape(b, 1, d)
    else:
        tm = b * t
        as_mod = lambda m: jnp.repeat(m, t, axis=0).reshape(1, b * t, d)
    w_r = _rwkv_pad_cols(W['w_in'][:, :rwkv_proj], cw, nw, na, ng).astype(BF16)
    nsa_cols = W['w_in'].shape[1] - rwkv_proj
    w_n = jnp.pad(W['w_in'][:, rwkv_proj:], ((0, 0), (0, _rup(nsa_cols, LANES) - nsa_cols))).astype(BF16)
    p_r, _ = _norm_mod_matmul(xf, W['norm1_g'], as_mod(sc1), as_mod(sh1), w_r, tm, 512)
    p_n, _ = _norm_mod_matmul(xf, W['norm1_g'], as_mod(sc1), as_mod(sh1), w_n, tm, w_n.shape[1] // 3)
    pr = p_r.shape[1]
    p_r = p_r.reshape(b, t, pr)
    shift_new = _rwkv_unpad_cols(p_r[:, -1], cw, nw, na, ng)
    tpad = _rup(t, rwkv_chunk)
    p_r_pad = jnp.pad(p_r, ((0, 0), (0, tpad - t), (0, 0)))
    y_r, wkv_new = _rwkv_mix(p_r_pad, shift_prev, wkv0, W, rwkv_chunk, t)
    y_r = y_r[:, :t].reshape(b * t, cw)
    y_n, rows, win = nsa_fn(p_n)
    w_out = W['w_out'].astype(BF16)
    x1 = _out_proj(xf, y_r, y_n, as_mod(ga1), w_out[:cw], w_out[cw:], tm, 512)
    h2 = _norm_mod(x1, W['norm2_g'], as_mod(sc2), as_mod(sh2), tm)
    if (b * t) % LANES == 0:
        out = _peer(h2, x1, as_mod(ga2), W, tm)
    else:
        out = x1 + jnp.repeat(ga2, t, axis=0) * _peer_jnp(h2.astype(F32), W)
    return out.reshape(b, t, d), rows, win, wkv_new, shift_new


def kernel(x_prompt, x_sample, c_prompt, c_sample, cache_kv, cache_win, state_wkv, state_shift, page_table,
           norm1_g, norm2_g, w_ada, b_ada, w_in, w_out,
           rwkv_mu, rwkv_w0, rwkv_w_up, rwkv_a0, rwkv_a_up, rwkv_g_up, rwkv_k_k, rwkv_k_a, rwkv_r_k, lnx_w, lnx_b,
           qk_norm_g, cmp_w1, cmp_b1, cmp_w2, cmp_b2, rel_bias,
           peer_w_query, peer_sub_keys, peer_u, peer_v):
    W = dict(norm1_g=norm1_g, norm2_g=norm2_g, w_ada=w_ada, b_ada=b_ada, w_in=w_in, w_out=w_out,
             rwkv_mu=rwkv_mu, rwkv_w0=rwkv_w0, rwkv_w_up=rwkv_w_up, rwkv_a0=rwkv_a0, rwkv_a_up=rwkv_a_up,
             rwkv_g_up=rwkv_g_up, rwkv_k_k=rwkv_k_k, rwkv_k_a=rwkv_k_a, rwkv_r_k=rwkv_r_k, lnx_w=lnx_w, lnx_b=lnx_b,
             qk_norm_g=qk_norm_g, cmp_w1=cmp_w1, cmp_b1=cmp_b1, cmp_w2=cmp_w2, cmp_b2=cmp_b2, rel_bias=rel_bias,
             peer_w_query=peer_w_query, peer_sub_keys=peer_sub_keys, peer_u=peer_u, peer_v=peer_v)
    W['peer_u_bf16'] = peer_u.astype(BF16)
    W['peer_v_bf16'] = peer_v.astype(BF16)
    bp, seq, d = x_prompt.shape
    db = x_sample.shape[0]
    nkv = cache_kv.shape[3]
    nh_r = rwkv_w0.shape[0] // HEAD_DIM
    ngrp = (w_out.shape[0] - rwkv_w0.shape[0]) // HEAD_DIM // nkv

    mods = _ada_mods(jnp.concatenate([c_prompt, c_sample], axis=0), w_ada, b_ada)
    mods = mods.reshape(bp + db, N_MODS, d)
    mods_p = [mods[:bp, i] for i in range(N_MODS)]
    mods_s = [mods[bp:, i] for i in range(N_MODS)]

    shift0 = jnp.zeros((bp, state_shift.shape[1]), F32)
    wkv0 = jnp.zeros((bp, nh_r, HEAD_DIM, HEAD_DIM), F32)
    y_p, rows_p, win_p, wkv_p, shift_p = _layer(
        x_prompt, mods_p, lambda pn: _nsa_prompt(pn, W, bp, seq, nkv, ngrp, 512), shift0, wkv0, W, 512, 64)
    y_s, rows_s, win_s, wkv_s, shift_s = _layer(
        x_sample, mods_s,
        lambda pn: _nsa_sample(pn, cache_kv, cache_win, page_table, W, db, x_sample.shape[1], nkv, ngrp),
        state_shift, state_wkv, W, 512, 32)
    return (y_p, y_s, rows_p, win_p, wkv_p.astype(state_wkv.dtype), shift_p,
            rows_s, win_s, wkv_s.astype(state_wkv.dtype), shift_s)
```

```python
import functools
import math

import numpy as np
import jax
import jax.numpy as jnp
from jax import lax
from jax.experimental import pallas as pl
from jax.experimental.pallas import tpu as pltpu

F32 = jnp.float32
BF16 = jnp.bfloat16
HI = lax.Precision.HIGHEST

HEAD_DIM = 64
PAGE_SIZE = 128
CMP_LEN = 32
CMP_STRIDE = 16
SEL_BLOCK = 64
SEL_TOPK = 16
WINDOW = 512
REL_BUCKETS = 32
REL_MAX_DIST = 2048
PEER_TOPK = 16
N_MODS = 6
RMS_EPS = 1e-6
LNX_EPS = 64e-5
NEG = -1e30

LANES = 128
MXU = 256
HEADS_PER_GROUP = MXU // HEAD_DIM
VMEM_LIMIT = 56 * 1024 * 1024


def _cparams(sem):
    return pltpu.CompilerParams(dimension_semantics=sem, vmem_limit_bytes=VMEM_LIMIT)


def _dot(a, b, precision=None):
    return jnp.dot(a, b, preferred_element_type=F32, precision=precision)


def _dot_nt(a, b, precision=None):
    return lax.dot_general(a, b, (((1,), (1,)), ((), ())), preferred_element_type=F32, precision=precision)


def _split_bf16(x, parts):
    out = []
    for _ in range(parts):
        h = x.astype(BF16)
        out.append(h)
        x = x - h.astype(F32)
    return out


def _mm(a, b, mode, nt=False):
    f = _dot_nt if nt else _dot
    if mode == 6:
        return f(a, b, HI)
    if mode == 1:
        return f(a.astype(BF16), b.astype(BF16))
    if mode == 3:
        ah, al = _split_bf16(a, 2)
        bh, bl = _split_bf16(b, 2)
        return (f(al, bh) + f(ah, bl)) + f(ah, bh)
    if mode[0] == 'L':
        terms = [f(t, b.astype(BF16)) for t in _split_bf16(a, int(mode[1]))]
    else:
        terms = [f(a.astype(BF16), t) for t in _split_bf16(b, int(mode[1]))]
    out = terms[-1]
    for t in terms[-2::-1]:
        out = out + t
    return out


RWKV_MM = dict(lora=1, headsum='L2', cumsum='R3', gram=1, inverse=1, state=3)


def _ada_kernel(c_ref, w_ref, b_ref, o_ref):
    c = c_ref[...]
    s = c * jax.nn.sigmoid(c)
    o_ref[...] = _dot(s.astype(BF16), w_ref[...].astype(BF16)) + b_ref[...]


def _ada_mods(c, w_ada, b_ada):
    n, d = c.shape
    cols = w_ada.shape[1]
    tn = 1024
    return pl.pallas_call(
        _ada_kernel,
        grid=(cols // tn,),
        in_specs=[pl.BlockSpec((n, d), lambda j: (0, 0)),
                  pl.BlockSpec((d, tn), lambda j: (0, j)),
                  pl.BlockSpec((1, tn), lambda j: (0, j))],
        out_specs=pl.BlockSpec((n, tn), lambda j: (0, j)),
        out_shape=jax.ShapeDtypeStruct((n, cols), F32),
        compiler_params=_cparams(("arbitrary",)),
        name="ada_mods",
    )(c, w_ada, b_ada.reshape(1, cols))


def _nmm_kernel(x_ref, g_ref, sc_ref, sh_ref, w_ref, o_ref, h_ref):
    @pl.when(pl.program_id(1) == 0)
    def _():
        x = x_ref[...]
        ms = jnp.mean(x * x, axis=-1, keepdims=True)
        y = x * lax.rsqrt(ms + RMS_EPS) * g_ref[...]
        h_ref[...] = (y * (1.0 + sc_ref[0]) + sh_ref[0]).astype(h_ref.dtype)

    o_ref[...] = _dot(h_ref[...], w_ref[...])


def _norm_mod_matmul(x, g, sc, sh, w, tm, tn):
    t, d = x.shape
    n = w.shape[1]
    nmod, rows, _ = sc.shape
    tiles_per_mod = (t // tm) // nmod
    mod_spec = pl.BlockSpec((1, rows, d), lambda i, j: (i // tiles_per_mod, 0, 0))
    return pl.pallas_call(
        _nmm_kernel,
        grid=(t // tm, n // tn),
        in_specs=[pl.BlockSpec((tm, d), lambda i, j: (i, 0)),
                  pl.BlockSpec((1, d), lambda i, j: (0, 0)),
                  mod_spec, mod_spec,
                  pl.BlockSpec((d, tn), lambda i, j: (0, j))],
        out_specs=[pl.BlockSpec((tm, tn), lambda i, j: (i, j)),
                   pl.BlockSpec((tm, d), lambda i, j: (i, 0))],
        out_shape=[jax.ShapeDtypeStruct((t, n), F32), jax.ShapeDtypeStruct((t, d), BF16)],
        compiler_params=_cparams(("arbitrary", "arbitrary")),
        name="norm_mod_matmul",
    )(x, g.reshape(1, d), sc, sh, w)


def _nm_kernel(x_ref, g_ref, sc_ref, sh_ref, h_ref):
    x = x_ref[...]
    ms = jnp.mean(x * x, axis=-1, keepdims=True)
    y = x * lax.rsqrt(ms + RMS_EPS) * g_ref[...]
    h_ref[...] = (y * (1.0 + sc_ref[0]) + sh_ref[0]).astype(h_ref.dtype)


def _norm_mod(x, g, sc, sh, tm):
    t, d = x.shape
    nmod, rows, _ = sc.shape
    tiles_per_mod = (t // tm) // nmod
    mod_spec = pl.BlockSpec((1, rows, d), lambda i: (i // tiles_per_mod, 0, 0))
    return pl.pallas_call(
        _nm_kernel,
        grid=(t // tm,),
        in_specs=[pl.BlockSpec((tm, d), lambda i: (i, 0)), pl.BlockSpec((1, d), lambda i: (0, 0)), mod_spec, mod_spec],
        out_specs=pl.BlockSpec((tm, d), lambda i: (i, 0)),
        out_shape=jax.ShapeDtypeStruct((t, d), BF16),
        compiler_params=_cparams(("arbitrary",)),
        name="norm_mod",
    )(x, g.reshape(1, d), sc, sh)


def _outproj_kernel(x_ref, yr_ref, yn_ref, ga_ref, w1_ref, w2_ref, o_ref):
    acc = _dot(yr_ref[...].astype(BF16), w1_ref[...]) + _dot(yn_ref[...].astype(BF16), w2_ref[...])
    o_ref[...] = x_ref[...] + ga_ref[0] * acc


def _out_proj(x, y_r, y_n, ga, w1, w2, tm, tn):
    t, d = x.shape
    nmod, rows, _ = ga.shape
    tiles_per_mod = (t // tm) // nmod
    cr, cn = y_r.shape[1], y_n.shape[1]
    return pl.pallas_call(
        _outproj_kernel,
        grid=(t // tm, d // tn),
        in_specs=[pl.BlockSpec((tm, tn), lambda i, j: (i, j)),
                  pl.BlockSpec((tm, cr), lambda i, j: (i, 0)),
                  pl.BlockSpec((tm, cn), lambda i, j: (i, 0)),
                  pl.BlockSpec((1, rows, tn), lambda i, j: (i // tiles_per_mod, 0, j)),
                  pl.BlockSpec((cr, tn), lambda i, j: (0, j)),
                  pl.BlockSpec((cn, tn), lambda i, j: (0, j))],
        out_specs=pl.BlockSpec((tm, tn), lambda i, j: (i, j)),
        out_shape=jax.ShapeDtypeStruct((t, d), F32),
        compiler_params=_cparams(("arbitrary", "arbitrary")),
        name="out_proj",
    )(x, y_r, y_n, ga, w1, w2)


def _softplus(z):
    return jnp.maximum(z, 0.0) + jnp.log(1.0 + jnp.exp(-jnp.abs(z)))


def _rwkv_kernel(p_ref, shift_ref, s0_ref, mu_ref, vec_ref, wup_ref, aup_ref, gup_ref,
                 y_ref, sfin_ref, carry, state, *, t_valid, n_chunks):
    c = pl.program_id(1)
    chunk = p_ref.shape[1]
    cw = vec_ref.shape[1]
    n_groups = cw // MXU
    hg = HEADS_PER_GROUP
    rows_g = hg * chunk

    @pl.when(c == 0)
    def _():
        carry[...] = shift_ref[0]
        state[...] = s0_ref[0]

    p = p_ref[0]
    row = lax.broadcasted_iota(jnp.int32, (chunk, 1), 0)
    prev = jnp.where(row == 0, carry[...], pltpu.roll(p, 1, axis=0))
    carry[...] = p[chunk - 1:chunk, :]
    xs = p + (prev - p) * mu_ref[...]

    w0, a0, k_k, k_a, r_k, lnx_w, lnx_b = (vec_ref[i:i + 1, :] for i in range(7))
    r = xs[:, 0:cw]
    k = xs[:, cw:2 * cw]
    v = xs[:, 2 * cw:3 * cw]
    o = 3 * cw
    nw, na, ng = wup_ref.shape[0], aup_ref.shape[0], gup_ref.shape[0]
    xw = xs[:, o:o + nw]
    xa = xs[:, o + nw:o + nw + na]
    xg = xs[:, o + nw + na:o + nw + na + ng]
    pm = RWKV_MM
    w_log = -_softplus(-(w0 + _mm(jnp.tanh(xw), wup_ref[...], pm['lora']))) - 0.5
    a = jax.nn.sigmoid(a0 + _mm(xa, aup_ref[...], pm['lora']))
    gate = _mm(jax.nn.sigmoid(xg), gup_ref[...], pm['lora'])

    gi = lax.broadcasted_iota(jnp.int32, (MXU, MXU), 0) // HEAD_DIM
    gj = lax.broadcasted_iota(jnp.int32, (MXU, MXU), 1) // HEAD_DIM
    ones_bd = (gi == gj).astype(F32)

    def head_sum(x):
        return jnp.concatenate([_mm(x[:, g * MXU:(g + 1) * MXU], ones_bd, pm['headsum']) for g in range(n_groups)],
                               axis=1)

    kk = k * k_k
    kk = kk / jnp.maximum(jnp.sqrt(head_sum(kk * kk)), 1e-12)
    k2 = k * (1.0 + (a - 1.0) * k_a)
    logdec = -jnp.exp(w_log)
    if t_valid < chunk * n_chunks:
        valid = (row + c * chunk) < t_valid
        logdec = jnp.where(valid, logdec, 0.0)
        kk = jnp.where(valid, kk, 0.0)
        k2 = jnp.where(valid, k2, 0.0)
        v = jnp.where(valid, v, 0.0)

    ti = lax.broadcasted_iota(jnp.int32, (chunk, chunk), 0)
    tj = lax.broadcasted_iota(jnp.int32, (chunk, chunk), 1)
    cum = _mm((tj <= ti).astype(F32), logdec, pm['cumsum'])
    cum_end = cum[chunk - 1:chunk, :]
    e_neg = jnp.exp(-cum)
    e_rem = jnp.exp(cum_end - cum)
    r_t = r * jnp.exp(cum)
    a_t = -kk * jnp.exp(cum - logdec)
    b_vec = kk * a
    b_t = b_vec * e_neg
    k_t = k2 * e_neg
    b_rem = b_vec * e_rem
    k_rem = k2 * e_rem
    w_end = jnp.exp(cum_end)

    lane_head = lax.broadcasted_iota(jnp.int32, (chunk, MXU), 1) // HEAD_DIM
    ri = lax.broadcasted_iota(jnp.int32, (rows_g, rows_g), 0)
    rj = lax.broadcasted_iota(jnp.int32, (rows_g, rows_g), 1)
    strict = rj < ri
    incl = rj <= ri
    eye_r = (ri == rj).astype(F32)
    di = lax.broadcasted_iota(jnp.int32, (MXU, MXU), 0)
    dj = lax.broadcasted_iota(jnp.int32, (MXU, MXU), 1)
    n_double = max(int(math.ceil(math.log2(chunk))) - 1, 0)

    def bd(x):
        return jnp.concatenate([jnp.where(lane_head == h, x, 0.0) for h in range(hg)], axis=0)

    def stack(x):
        return jnp.concatenate([x[:, h * HEAD_DIM:(h + 1) * HEAD_DIM] for h in range(hg)], axis=0)

    def unstack(x):
        return jnp.concatenate([x[h * chunk:(h + 1) * chunk, :] for h in range(hg)], axis=1)

    ys = []
    for g in range(n_groups):
        sl = slice(g * MXU, (g + 1) * MXU)
        a_bd, r_bd = bd(a_t[:, sl]), bd(r_t[:, sl])
        b_bd, k_bd = bd(b_t[:, sl]), bd(k_t[:, sl])
        v_st = stack(v[:, sl])
        a_ab = jnp.where(strict, _mm(a_bd, b_bd, pm['gram'], nt=True), 0.0)
        a_ak = jnp.where(strict, _mm(a_bd, k_bd, pm['gram'], nt=True), 0.0)
        a_rb = jnp.where(incl, _mm(r_bd, b_bd, pm['gram'], nt=True), 0.0)
        a_rk = jnp.where(incl, _mm(r_bd, k_bd, pm['gram'], nt=True), 0.0)
        tinv = eye_r + a_ab
        pw = a_ab
        for _ in range(n_double):
            pw = _mm(pw, pw, pm['inverse'])
            tinv = tinv + _mm(pw, tinv, pm['inverse'])
        s0 = state[g]
        z = _mm(a_bd, s0, pm['state']) + _mm(a_ak, v_st, pm['state'])
        u = _mm(tinv, z, pm['state'])
        y_st = _mm(r_bd, s0, pm['state']) + _mm(a_rb, u, pm['state']) + _mm(a_rk, v_st, pm['state'])
        w_col = jnp.sum(jnp.where(di == dj, jnp.broadcast_to(w_end[:, sl], (MXU, MXU)), 0.0), axis=1, keepdims=True)
        state[g] = (w_col * s0 + _mm(bd(b_rem[:, sl]).T, u, pm['state'])
                    + _mm(bd(k_rem[:, sl]).T, v_st, pm['state']))
        ys.append(unstack(y_st))
    y = jnp.concatenate(ys, axis=1)

    inv_n = 1.0 / HEAD_DIM
    mean = head_sum(y) * inv_n
    d = y - mean
    var = head_sum(d * d) * inv_n
    yn = d * lax.rsqrt(var + LNX_EPS) * lnx_w + lnx_b
    bonus = head_sum(r * k2 * r_k) * v
    y_ref[0] = ((yn + bonus) * gate).astype(y_ref.dtype)

    @pl.when(c == n_chunks - 1)
    def _():
        sfin_ref[0] = state[...]


def _rwkv_pad_cols(x, cw, nw, na, ng):
    o = 3 * cw
    parts = [x[..., :o + nw], x[..., o + nw:o + nw + na], x[..., o + nw + na:]]
    widths = [o + _rup(nw, LANES), _rup(na, LANES), _rup(ng, LANES)]
    out = []
    for part, wd in zip(parts, widths):
        pad = [(0, 0)] * (x.ndim - 1) + [(0, wd - part.shape[-1])]
        out.append(jnp.pad(part, pad))
    return jnp.concatenate(out, axis=-1)


def _rwkv_unpad_cols(x, cw, nw, na, ng):
    o = 3 * cw
    o2 = o + _rup(nw, LANES)
    o3 = o2 + _rup(na, LANES)
    return jnp.concatenate([x[..., :o + nw], x[..., o2:o2 + na], x[..., o3:o3 + ng]], axis=-1)


def _rup(x, m):
    return (x + m - 1) // m * m


def _rwkv_mix(p_r, shift_prev, wkv0, W, chunk, t_valid):
    b, tpad, pr = p_r.shape
    cw = W['rwkv_w0'].shape[0]
    nh = cw // HEAD_DIM
    n_groups = cw // MXU
    nw, na, ng = W['rwkv_w_up'].shape[0], W['rwkv_a_up'].shape[0], W['rwkv_g_up'].shape[0]
    n_chunks = tpad // chunk
    mu = _rwkv_pad_cols(W['rwkv_mu'], cw, nw, na, ng).reshape(1, pr)
    vecs = jnp.stack([W['rwkv_w0'], W['rwkv_a0'], W['rwkv_k_k'], W['rwkv_k_a'], W['rwkv_r_k'].reshape(cw),
                      W['lnx_w'], W['lnx_b'], jnp.zeros((cw,), F32)])
    wup = jnp.pad(W['rwkv_w_up'], ((0, _rup(nw, LANES) - nw), (0, 0)))
    aup = jnp.pad(W['rwkv_a_up'], ((0, _rup(na, LANES) - na), (0, 0)))
    gup = jnp.pad(W['rwkv_g_up'], ((0, _rup(ng, LANES) - ng), (0, 0)))
    shift3 = _rwkv_pad_cols(shift_prev, cw, nw, na, ng).reshape(b, 1, pr)
    s0 = wkv0.astype(F32).transpose(0, 1, 3, 2).reshape(b, n_groups, MXU, HEAD_DIM)
    const = lambda shape: pl.BlockSpec(shape, lambda i, c: (0,) * len(shape))
    y, sfin = pl.pallas_call(
        functools.partial(_rwkv_kernel, t_valid=t_valid, n_chunks=n_chunks),
        grid=(b, n_chunks),
        in_specs=[pl.BlockSpec((1, chunk, pr), lambda i, c: (i, c, 0)),
                  pl.BlockSpec((1, 1, pr), lambda i, c: (i, 0, 0)),
                  pl.BlockSpec((1, n_groups, MXU, HEAD_DIM), lambda i, c: (i, 0, 0, 0)),
                  const((1, pr)), const((8, cw)), const(wup.shape), const(aup.shape), const(gup.shape)],
        out_specs=[pl.BlockSpec((1, chunk, cw), lambda i, c: (i, c, 0)),
                   pl.BlockSpec((1, n_groups, MXU, HEAD_DIM), lambda i, c: (i, 0, 0, 0))],
        out_shape=[jax.ShapeDtypeStruct((b, tpad, cw), F32),
                   jax.ShapeDtypeStruct((b, n_groups, MXU, HEAD_DIM), F32)],
        scratch_shapes=[pltpu.VMEM((1, pr), F32), pltpu.VMEM((n_groups, MXU, HEAD_DIM), F32)],
        compiler_params=_cparams(("arbitrary", "arbitrary")),
        name="rwkv_mix",
    )(p_r, shift3, s0, mu, vecs, wup, aup, gup)
    s_fin = sfin.reshape(b, nh, HEAD_DIM, HEAD_DIM).transpose(0, 1, 3, 2)
    return y, s_fin


def _head_ones():
    gi = lax.broadcasted_iota(jnp.int32, (MXU, MXU), 0) // HEAD_DIM
    gj = lax.broadcasted_iota(jnp.int32, (MXU, MXU), 1) // HEAD_DIM
    return (gi == gj).astype(F32)


def _nsa_proj_kernel(p_ref, g_ref, q_ref, rows_ref, win_ref, gate_ref, *, nsa_w, kvw):
    ones_bd = _head_ones()

    def hnorm(x, gvec):
        ms = _dot(x * x, ones_bd, HI) * (1.0 / HEAD_DIM)
        return x * lax.rsqrt(ms + RMS_EPS) * gvec

    for i in range(nsa_w // MXU):
        sl = slice(i * MXU, (i + 1) * MXU)
        q_ref[:, sl] = hnorm(p_ref[:, sl], g_ref[0:1, :])
    o = nsa_w
    rows_ref[:, 0:2 * kvw] = p_ref[:, o:o + 2 * kvw]
    rows_ref[:, 2 * kvw:3 * kvw] = hnorm(p_ref[:, o + 2 * kvw:o + 3 * kvw], g_ref[2:3, :])
    rows_ref[:, 3 * kvw:4 * kvw] = p_ref[:, o + 3 * kvw:o + 4 * kvw]
    win_ref[:, 0:kvw] = hnorm(p_ref[:, o + 4 * kvw:o + 5 * kvw], g_ref[3:4, :])
    win_ref[:, kvw:2 * kvw] = p_ref[:, o + 5 * kvw:o + 6 * kvw]
    gate_ref[...] = jax.nn.sigmoid(p_ref[:, o + 6 * kvw:])


def _nsa_project_call(p_n, qk_norm_g, nsa_w, kvw, tm):
    t, pc = p_n.shape
    assert kvw == MXU and nsa_w % MXU == 0
    gcols = pc - nsa_w - 6 * kvw
    gvec = jnp.tile(qk_norm_g, (1, MXU // HEAD_DIM))
    return pl.pallas_call(
        functools.partial(_nsa_proj_kernel, nsa_w=nsa_w, kvw=kvw),
        grid=(t // tm,),
        in_specs=[pl.BlockSpec((tm, pc), lambda i: (i, 0)),
                  pl.BlockSpec(gvec.shape, lambda i: (0, 0))],
        out_specs=[pl.BlockSpec((tm, nsa_w), lambda i: (i, 0)),
                   pl.BlockSpec((tm, 4 * kvw), lambda i: (i, 0)),
                   pl.BlockSpec((tm, 2 * kvw), lambda i: (i, 0)),
                   pl.BlockSpec((tm, gcols), lambda i: (i, 0))],
        out_shape=[jax.ShapeDtypeStruct((t, nsa_w), F32), jax.ShapeDtypeStruct((t, 4 * kvw), F32),
                   jax.ShapeDtypeStruct((t, 2 * kvw), F32), jax.ShapeDtypeStruct((t, gcols), F32)],
        compiler_params=_cparams(("arbitrary",)),
        name="nsa_project",
    )(p_n, gvec)


def _cmp_part_kernel(*refs, n_in, row_w, kvw, n_prefetch=0):
    refs = refs[n_prefetch:]
    x_refs, w_ref, o_ref = refs[:n_in], refs[n_in], refs[n_in + 1]
    for typ in range(2):
        acc = None
        for s in range(CMP_STRIDE):
            lo = s * row_w + typ * kvw
            xs = jnp.concatenate([x[0, :, lo:lo + kvw] for x in x_refs], axis=0) if n_in > 1 else x_refs[0][0, :, lo:lo + kvw]
            d = _dot(xs.astype(BF16), w_ref[typ, s])
            acc = d if acc is None else acc + d
        o_ref[0, :, typ * 2 * kvw:(typ + 1) * 2 * kvw] = acc


def _cmp_first_weights(cmp_w1, nkv):
    r2 = CMP_LEN // CMP_STRIDE
    e = cmp_w1.shape[-1]
    w1r = cmp_w1.reshape(2, r2, CMP_STRIDE, HEAD_DIM, e)
    eye = jnp.eye(nkv, dtype=F32)
    big = jnp.einsum('yhsde,gk->ysgdkhe', w1r, eye)
    return big.reshape(2, CMP_STRIDE, nkv * HEAD_DIM, nkv * r2 * e).astype(BF16)


def _cmp_parts_prompt(rows2d, w_big, b, t, kvw):
    row_w = rows2d.shape[2]
    nsub = t // CMP_STRIDE
    blk = min(nsub, LANES)
    return pl.pallas_call(
        functools.partial(_cmp_part_kernel, n_in=1, row_w=row_w, kvw=kvw),
        grid=(b, nsub // blk),
        in_specs=[pl.BlockSpec((1, blk, CMP_STRIDE * row_w), lambda i, j: (i, j, 0)),
                  pl.BlockSpec(w_big.shape, lambda i, j: (0, 0, 0, 0))],
        out_specs=pl.BlockSpec((1, blk, 4 * kvw), lambda i, j: (i, j, 0)),
        out_shape=jax.ShapeDtypeStruct((b, nsub, 4 * kvw), F32),
        compiler_params=_cparams(("arbitrary", "arbitrary")),
        name="cmp_parts",
    )(rows2d.reshape(b, nsub, CMP_STRIDE * row_w), w_big)


def _cmp_finish_kernel(p_ref, b1_ref, w2_ref, b2_ref, g_ref, o_ref, *, nc, nkv):
    tg = pl.program_id(1)
    ns = p_ref.shape[1]
    e = p_ref.shape[2] // 2
    part = p_ref[0]
    nxt = pltpu.roll(part[:, e:], ns - 1, axis=0)
    hid = part[:, :e] + nxt + b1_ref[0]
    out = _dot(jax.nn.gelu(hid).astype(BF16), w2_ref[0].astype(BF16)) + b2_ref[0]
    normed = out * lax.rsqrt(jnp.mean(out * out, axis=-1, keepdims=True) + RMS_EPS) * g_ref[...]
    out = jnp.where(tg < nkv, normed, out)
    row = lax.broadcasted_iota(jnp.int32, (ns, 1), 0)
    o_ref[0, 0] = jnp.where(row < nc, out, 0.0)


def _cmp_finish(parts, cmp_b1, cmp_w2, cmp_b2, g1, nc, nkv):
    b, ns, _ = parts.shape
    e = cmp_b1.shape[1]
    return pl.pallas_call(
        functools.partial(_cmp_finish_kernel, nc=nc, nkv=nkv),
        grid=(b, 2 * nkv),
        in_specs=[pl.BlockSpec((1, ns, 2 * e), lambda i, j: (i, 0, j)),
                  pl.BlockSpec((1, 1, e), lambda i, j: (j // nkv, 0, 0)),
                  pl.BlockSpec((1, e, HEAD_DIM), lambda i, j: (j // nkv, 0, 0)),
                  pl.BlockSpec((1, 1, HEAD_DIM), lambda i, j: (j // nkv, 0, 0)),
                  pl.BlockSpec((1, HEAD_DIM), lambda i, j: (0, 0))],
        out_specs=pl.BlockSpec((1, 1, ns, HEAD_DIM), lambda i, j: (i, j, 0, 0)),
        out_shape=jax.ShapeDtypeStruct((b, 2 * nkv, ns, HEAD_DIM), F32),
        compiler_params=_cparams(("arbitrary", "arbitrary")),
        name="cmp_finish",
    )(parts, cmp_b1.reshape(2, 1, e), cmp_w2, cmp_b2.reshape(2, 1, HEAD_DIM), g1.reshape(1, HEAD_DIM))


QT = 128


def _rel_table_np_dist(dist, table):
    onehot = (_rel_bucket(dist)[..., None] == jnp.arange(REL_BUCKETS)).astype(F32)
    return jnp.einsum('...b,bh->...h', onehot, table, precision=HI)


def _rel_table_per_lane(dist, table_l):
    bucket = _rel_bucket(dist)
    out = jnp.zeros(dist.shape, F32)
    for b in range(REL_BUCKETS):
        out = out + jnp.where(bucket == b, table_l[b][None, :], 0.0)
    return out


def _softmax_update(s, mask, m, l):
    m_new = jnp.maximum(m, jnp.max(jnp.where(mask, s, NEG), axis=0, keepdims=True))
    alpha = jnp.exp(m - m_new)
    p = jnp.where(mask, jnp.exp(s - m_new), 0.0)
    return p, m_new, alpha, alpha * l + jnp.sum(p, axis=0, keepdims=True)


def _rank_select(score, score_ref, cur, n_sel):
    nb = score.shape[0]
    score_ref[0:nb, :] = score
    jrow = lax.broadcasted_iota(jnp.int32, score.shape, 0)

    def body(j, rank):
        other = score_ref[pl.ds(j, 1), :]
        beats = (other > score) | ((other == score) & (jrow > j))
        return rank + jnp.where(beats, 1.0, 0.0)

    rank = lax.fori_loop(0, nb, body, jnp.zeros(score.shape, F32), unroll=8)
    return jnp.where((rank < n_sel) & (jrow <= cur), 1.0, 0.0)


def _nsa_prompt_kernel(qT_ref, gT_ref, kc_ref, vcT_ref, ks_ref, vsT_ref, kw_ref, vwT_ref, bc_ref, toep_ref,
                       o_ref, pg_ref, score_ref, sel_ref, *, nc, nb, n_sel, ngrp):
    qt = pl.program_id(2)
    scale = HEAD_DIM ** -0.5
    lanes = ngrp * QT
    q = qT_ref[0, 0, 0].astype(BF16)
    iq = lax.broadcasted_iota(jnp.int32, (1, QT), 1)
    q_pos = qt * QT + iq
    tile4 = lambda x: jnp.concatenate([x] * ngrp, axis=1)

    ncp = kc_ref.shape[2]
    s = _dot(kc_ref[0, 0].astype(BF16), q) * scale
    s = s + jnp.concatenate([bc_ref[0, r] for r in range(ngrp)], axis=1)
    crow = lax.broadcasted_iota(jnp.int32, (ncp, QT), 0)
    ok_c = tile4((crow * CMP_STRIDE + (CMP_LEN - 1) <= q_pos) & (crow < nc))
    p, _, _, l = _softmax_update(s, ok_c, jnp.full((1, lanes), NEG, F32), jnp.zeros((1, lanes), F32))
    p = p * jnp.where(l > 0.0, 1.0 / jnp.where(l > 0.0, l, 1.0), 0.0)
    o_c = _dot(vcT_ref[0, 0].astype(BF16), p.astype(BF16))

    p_grp = p[:, 0:QT]
    for r in range(1, ngrp):
        p_grp = p_grp + p[:, r * QT:(r + 1) * QT]
    pad = 8
    pg_ref[...] = jnp.zeros(pg_ref.shape, F32)
    pg_ref[pad:pad + ncp, :] = p_grp
    r1 = SEL_BLOCK // CMP_STRIDE
    offs, wts = _slc_offsets()
    p_slc = None
    for o, wt in zip(offs, wts):
        term = float(wt) * pg_ref[pl.ds(pad + int(o), nb, stride=r1), :]
        p_slc = term if p_slc is None else p_slc + term
    jrow = lax.broadcasted_iota(jnp.int32, (nb, QT), 0)
    cur = q_pos // SEL_BLOCK
    forced = (jrow == 0) | (jrow == cur) | (jrow == cur - 1)
    score = jnp.where(jrow > cur, -1.0, jnp.where(forced, 1e6, p_slc))
    sel_ref[0:nb, :] = _rank_select(score, score_ref, cur, n_sel)

    ik = lax.broadcasted_iota(jnp.int32, (QT, QT), 0)
    iqq = lax.broadcasted_iota(jnp.int32, (QT, QT), 1)
    blocks_per_tile = QT // SEL_BLOCK

    def attend(kt, carry, k_ref, vT_ref, mask_fn):
        m, l, acc = carry
        delta = qt - kt
        s = _dot(k_ref[0, 0, kt].astype(BF16), q) * scale
        s = s + jnp.concatenate([toep_ref[0, r, delta] for r in range(ngrp)], axis=1)
        mask = tile4(mask_fn(kt, delta))
        p, m, alpha, l = _softmax_update(s, mask, m, l)
        acc = alpha * acc + _dot(vT_ref[0, 0, kt].astype(BF16), p.astype(BF16))
        return m, l, acc

    def sel_mask(kt, delta):
        rows = [jnp.broadcast_to(sel_ref[pl.ds(kt * blocks_per_tile + i, 1), :], (SEL_BLOCK, QT))
                for i in range(blocks_per_tile)]
        chosen = jnp.concatenate(rows, axis=0) > 0.5
        return chosen & (ik - iqq <= delta * QT)

    def win_mask(kt, delta):
        dist = delta * QT + iqq - ik
        return (dist >= 0) & (dist < WINDOW)

    init = (jnp.full((1, lanes), NEG, F32), jnp.zeros((1, lanes), F32), jnp.zeros((HEAD_DIM, lanes), F32))
    finish = lambda c: c[2] * jnp.where(c[1] > 0.0, 1.0 / jnp.where(c[1] > 0.0, c[1], 1.0), 0.0)
    o_s = finish(lax.fori_loop(0, qt + 1, lambda kt, c: attend(kt, c, ks_ref, vsT_ref, sel_mask), init))
    first = jnp.maximum(qt - WINDOW // QT, 0)
    o_w = finish(lax.fori_loop(first, qt + 1, lambda kt, c: attend(kt, c, kw_ref, vwT_ref, win_mask), init))
    g = gT_ref[0, 0, 0]
    o_ref[0, 0, 0] = g[0:1, :] * o_c + g[1:2, :] * o_s + g[2:3, :] * o_w


def _nsa_prompt(p_n, W, b, t, nkv, ngrp, tm):
    nsa_w = nkv * ngrp * HEAD_DIM
    kvw = nkv * HEAD_DIM
    qn, rows2d, win2d, gates = _nsa_project_call(p_n, W['qk_norm_g'], nsa_w, kvw, tm)
    rows = rows2d.reshape(b, t, 4, nkv, HEAD_DIM)
    win = win2d.reshape(b, t, 2, nkv, HEAD_DIM)
    ns = t // CMP_STRIDE
    nc = ns - CMP_LEN // CMP_STRIDE + 1
    nb = t // SEL_BLOCK
    n_sel = min(SEL_TOPK, nb)
    nqt = t // QT
    parts = _cmp_parts_prompt(rows2d.reshape(b, t, 4 * kvw), _cmp_first_weights(W['cmp_w1'], nkv), b, t, kvw)
    kvc = _cmp_finish(parts, W['cmp_b1'], W['cmp_w2'], W['cmp_b2'], W['qk_norm_g'][1], nc, nkv)
    kc = kvc[:, :nkv]
    vcT = kvc[:, nkv:].transpose(0, 1, 3, 2)
    qT = qn.reshape(b, nqt, QT, nkv, ngrp, HEAD_DIM).transpose(0, 3, 1, 5, 4, 2).reshape(b, nkv, nqt, HEAD_DIM, ngrp * QT)
    ng = 3 * nkv * ngrp
    gT = gates[:, :ng].reshape(b, nqt, QT, nkv, ngrp, 3).transpose(0, 3, 1, 5, 4, 2).reshape(b, nkv, nqt, 3, ngrp * QT)
    gT = jnp.pad(gT, ((0, 0), (0, 0), (0, 0), (0, 5), (0, 0)))
    k_tiles = lambda x: x.transpose(0, 2, 1, 3).reshape(b, nkv, nqt, QT, HEAD_DIM).astype(BF16)
    vT_tiles = lambda x: x.reshape(b, nqt, QT, nkv, HEAD_DIM).transpose(0, 3, 1, 4, 2).astype(BF16)
    ks, vsT = k_tiles(rows[:, :, 2]), vT_tiles(rows[:, :, 3])
    kw, vwT = k_tiles(win[:, :, 0]), vT_tiles(win[:, :, 1])
    table = W['rel_bias'].astype(F32)
    c_end = jnp.arange(ns) * CMP_STRIDE + CMP_LEN - 1
    bias_c = _rel_table_np_dist(jnp.arange(t)[None, :] - c_end[:, None], table)
    bias_c = bias_c.transpose(2, 0, 1).reshape(nkv, ngrp, ns, t)
    dd = (jnp.arange(nqt)[:, None, None] * QT + jnp.arange(QT)[None, None, :] - jnp.arange(QT)[None, :, None])
    toep = _rel_table_np_dist(dd, table).transpose(3, 0, 1, 2).reshape(nkv, ngrp, nqt, QT, QT)
    lanes = ngrp * QT
    kv_spec = lambda shape: pl.BlockSpec((1, 1) + shape, lambda i, g, j: (i, g) + (0,) * len(shape))
    yT = pl.pallas_call(
        functools.partial(_nsa_prompt_kernel, nc=nc, nb=nb, n_sel=n_sel, ngrp=ngrp),
        grid=(b, nkv, nqt),
        in_specs=[pl.BlockSpec((1, 1, 1, HEAD_DIM, lanes), lambda i, g, j: (i, g, j, 0, 0)),
                  pl.BlockSpec((1, 1, 1, 8, lanes), lambda i, g, j: (i, g, j, 0, 0)),
                  kv_spec((ns, HEAD_DIM)), kv_spec((HEAD_DIM, ns)),
                  kv_spec((nqt, QT, HEAD_DIM)), kv_spec((nqt, HEAD_DIM, QT)),
                  kv_spec((nqt, QT, HEAD_DIM)), kv_spec((nqt, HEAD_DIM, QT)),
                  pl.BlockSpec((1, ngrp, ns, QT), lambda i, g, j: (g, 0, 0, j)),
                  pl.BlockSpec((1, ngrp, nqt, QT, QT), lambda i, g, j: (g, 0, 0, 0, 0))],
        out_specs=pl.BlockSpec((1, 1, 1, HEAD_DIM, lanes), lambda i, g, j: (i, g, j, 0, 0)),
        out_shape=jax.ShapeDtypeStruct((b, nkv, nqt, HEAD_DIM, lanes), F32),
        scratch_shapes=[pltpu.VMEM((ns + 16, QT), F32), pltpu.VMEM((_rup(nb, 8), QT), F32),
                        pltpu.VMEM((_rup(nb, 8), QT), F32)],
        compiler_params=_cparams(("arbitrary", "arbitrary", "arbitrary")),
        name="nsa_prompt_attn",
    )(qT, gT, kc, vcT, ks, vsT, kw, vwT, bias_c, toep)
    y = yT.reshape(b, nkv, nqt, HEAD_DIM, ngrp, QT).transpose(0, 2, 5, 1, 4, 3).reshape(b * t, nsa_w)
    return y, rows, win[:, t - min(WINDOW, t):]


def _cmp_part_paged_kernel(*refs, npg, kvw):
    x_refs, w_ref, o_ref = refs[1:1 + npg], refs[1 + npg], refs[2 + npg]
    rows = x_refs[0].shape[1]
    sub = rows // CMP_STRIDE
    ri = lax.broadcasted_iota(jnp.int32, (rows, rows), 0)
    ci = lax.broadcasted_iota(jnp.int32, (rows, rows), 1)
    perm = jnp.where(ci == (ri % sub) * CMP_STRIDE + ri // sub, 1.0, 0.0).astype(BF16)
    xp = [_dot(perm, x[0].astype(BF16)) for x in x_refs]
    for typ in range(2):
        acc = None
        for s in range(CMP_STRIDE):
            xs = jnp.concatenate([p[s * sub:(s + 1) * sub, typ * kvw:(typ + 1) * kvw] for p in xp], axis=0)
            d = _dot(xs.astype(BF16), w_ref[typ, s])
            acc = d if acc is None else acc + d
        o_ref[0, :, typ * 2 * kvw:(typ + 1) * 2 * kvw] = acc


def _cmp_parts_sample(cache_pages, page_table, w_big, kvw, npg):
    b, n_pages = page_table.shape
    rows = cache_pages.shape[1]
    sub = rows // CMP_STRIDE
    in_specs = [pl.BlockSpec((1, rows, 2 * kvw), (lambda i, j, pt, k=k: (pt[i, j * npg + k], 0, 0)))
                for k in range(npg)]
    in_specs.append(pl.BlockSpec(w_big.shape, lambda i, j, pt: (0, 0, 0, 0)))
    return pl.pallas_call(
        functools.partial(_cmp_part_paged_kernel, npg=npg, kvw=kvw),
        grid_spec=pltpu.PrefetchScalarGridSpec(
            num_scalar_prefetch=1, grid=(b, n_pages // npg), in_specs=in_specs,
            out_specs=pl.BlockSpec((1, npg * sub, 4 * kvw), lambda i, j, pt: (i, j, 0))),
        out_shape=jax.ShapeDtypeStruct((b, n_pages * sub, 4 * kvw), F32),
        compiler_params=_cparams(("arbitrary", "arbitrary")),
        name="cmp_parts_paged",
    )(page_table, *([cache_pages] * npg), w_big)


def _inv_pos(l):
    return jnp.where(l > 0.0, 1.0 / jnp.where(l > 0.0, l, 1.0), 0.0)


def _nsa_sample_kernel(*refs, npg, n_steps, nc, nb, n_sel, past, ds, keep, ngrp, n_lanes):
    pt_ref = refs[0]
    q_ref, g_ref, kc_ref, vcT_ref, bc_ref, win_ref, wnew_ref, bw_ref, rnew_ref, bs_ref = refs[1:11]
    page_refs = refs[11:11 + npg]
    o_ref = refs[11 + npg]
    (m_ref, l_ref, acc_ref, base_ref, sel_ref, pg_ref, score_ref,
     kw_ref, vw_ref, kn_ref, vn_ref) = refs[12 + npg:]
    del pt_ref
    j = pl.program_id(1)
    scale = HEAD_DIM ** -0.5
    kvw = q_ref.shape[1]
    qbd = q_ref[0]
    lane = lax.broadcasted_iota(jnp.int32, (1, LANES), 1)
    qi = (lane // ngrp) % ds
    q_pos = past + qi
    n_pages = npg * n_steps

    @pl.when(j == 0)
    def _():
        nsp = kc_ref.shape[1]
        s = _dot(kc_ref[0].astype(BF16), qbd) * scale + bc_ref[...]
        crow = lax.broadcasted_iota(jnp.int32, (nsp, LANES), 0)
        ok = (crow * CMP_STRIDE + (CMP_LEN - 1) <= q_pos) & (crow < nc)
        p, _, _, l = _softmax_update(s, ok, jnp.full((1, LANES), NEG, F32), jnp.zeros((1, LANES), F32))
        p = p * _inv_pos(l)
        o_c = _dot(vcT_ref[0].astype(BF16), p.astype(BF16))
        li = lax.broadcasted_iota(jnp.int32, (LANES, LANES), 0)
        lj = lax.broadcasted_iota(jnp.int32, (LANES, LANES), 1)
        fold = jnp.where((li // ngrp == lj) & (li < n_lanes), 1.0, 0.0)
        p_grp = _dot(p, fold, HI)
        pad = 8
        pg_ref[...] = jnp.zeros(pg_ref.shape, F32)
        pg_ref[pad:pad + nsp, :] = p_grp
        nbp = sel_ref.shape[0]
        r1 = SEL_BLOCK // CMP_STRIDE
        offs, wts = _slc_offsets()
        p_slc = None
        for o, wt in zip(offs, wts):
            term = float(wt) * pg_ref[pl.ds(pad + int(o), nbp, stride=r1), :]
            p_slc = term if p_slc is None else p_slc + term
        cur = (past + lane % ds) // SEL_BLOCK
        jrow = lax.broadcasted_iota(jnp.int32, (nbp, LANES), 0)
        forced = (jrow == 0) | (jrow == cur) | (jrow == cur - 1)
        score = jnp.where((jrow > cur) | (jrow >= nb), -1.0, jnp.where(forced, 1e6, p_slc))
        selg = _rank_select(score, score_ref, cur, n_sel)
        unfold = jnp.where((li == lj // ngrp) & (lj < n_lanes), 1.0, 0.0)
        sel_ref[...] = _dot(selg, unfold)
        wk = kw_ref.shape[0]
        kw_ref[...] = jnp.zeros(kw_ref.shape, F32)
        vw_ref[...] = jnp.zeros(vw_ref.shape, F32)
        kw_ref[0:keep, :] = win_ref[0, :, 0:kvw]
        vw_ref[0:keep, :] = win_ref[0, :, kvw:2 * kvw]
        nn = wnew_ref.shape[1]
        kw_ref[keep:keep + nn, :] = wnew_ref[0, :, 0:kvw]
        vw_ref[keep:keep + nn, :] = wnew_ref[0, :, kvw:2 * kvw]
        s = _dot(kw_ref[...].astype(BF16), qbd) * scale + bw_ref[...]
        irow = lax.broadcasted_iota(jnp.int32, (wk, LANES), 0)
        dist = jnp.where(irow < keep, keep + qi - irow, qi - (irow - keep))
        ok = (dist >= 0) & (dist < WINDOW) & (irow < keep + ds)
        p, _, _, l = _softmax_update(s, ok, jnp.full((1, LANES), NEG, F32), jnp.zeros((1, LANES), F32))
        p = p * _inv_pos(l)
        o_w = _dot(vw_ref[...].T.astype(BF16), p.astype(BF16))
        g = g_ref[0]
        base_ref[...] = g[0:1, :] * o_c + g[2:3, :] * o_w
        m_ref[...] = jnp.full(m_ref.shape, NEG, F32)
        l_ref[...] = jnp.zeros(l_ref.shape, F32)
        acc_ref[...] = jnp.zeros(acc_ref.shape, F32)
        kn_ref[...] = jnp.zeros(kn_ref.shape, F32)
        vn_ref[...] = jnp.zeros(vn_ref.shape, F32)
        kn_ref[0:nn, :] = rnew_ref[0, :, 2 * kvw:3 * kvw]
        vn_ref[0:nn, :] = rnew_ref[0, :, 3 * kvw:4 * kvw]

    ik = lax.broadcasted_iota(jnp.int32, (PAGE_SIZE, LANES), 0)
    blocks_per_page = PAGE_SIZE // SEL_BLOCK

    def page_update(k, v, page):
        s = _dot(k.astype(BF16), qbd) * scale + bs_ref[page]
        rows = [jnp.broadcast_to(sel_ref[pl.ds(page * blocks_per_page + i, 1), :], (SEL_BLOCK, LANES))
                for i in range(blocks_per_page)]
        mask = (jnp.concatenate(rows, axis=0) > 0.5) & (page * PAGE_SIZE + ik <= q_pos)
        p, m_new, alpha, l_new = _softmax_update(s, mask, m_ref[...], l_ref[...])
        m_ref[...] = m_new
        l_ref[...] = l_new
        acc_ref[...] = alpha * acc_ref[...] + _dot(v.T.astype(BF16), p.astype(BF16))

    for k in range(npg):
        blk = page_refs[k]
        page_update(blk[0, :, 0:kvw], blk[0, :, kvw:2 * kvw], j * npg + k)

    @pl.when(j == n_steps - 1)
    def _():
        page_update(kn_ref[...], vn_ref[...], n_pages)
        o_ref[0] = base_ref[...] + g_ref[0][1:2, :] * (acc_ref[...] * _inv_pos(l_ref[...]))


def _nsa_sample(p_n, cache_kv, cache_win, page_table, W, db, ds, nkv, ngrp):
    nsa_w = nkv * ngrp * HEAD_DIM
    kvw = nkv * HEAD_DIM
    row_w = 4 * kvw
    qn, rows2d, win2d, gates = _nsa_project_call(p_n, W['qk_norm_g'], nsa_w, kvw, db * ds)
    rows_new = rows2d.reshape(db, ds, 4, nkv, HEAD_DIM)
    win_new = win2d.reshape(db, ds, 2, nkv, HEAD_DIM)
    n_pool = cache_kv.shape[0]
    n_pages = page_table.shape[1]
    past = n_pages * PAGE_SIZE
    keep = cache_win.shape[1]
    tot = past + _rup(ds, SEL_BLOCK)
    ns = tot // CMP_STRIDE
    nc = ns - CMP_LEN // CMP_STRIDE + 1
    nb = tot // SEL_BLOCK
    n_sel = min(SEL_TOPK, nb)
    nsp = _rup(ns, LANES)
    nbp = _rup(nb, 8)
    n_lanes = nkv * ds * ngrp
    assert n_lanes <= LANES and ds <= 8
    w_big = _cmp_first_weights(W['cmp_w1'], nkv)
    cache_pages = cache_kv.reshape(n_pool, PAGE_SIZE, row_w)
    npg = min(16, n_pages)
    parts_past = _cmp_parts_sample(cache_pages, page_table, w_big, kvw, npg)
    rows_pad = jnp.pad(rows2d.reshape(db, ds, row_w), ((0, 0), (0, PAGE_SIZE - ds), (0, 0)))
    parts_new = _cmp_parts_prompt(rows_pad, w_big, db, PAGE_SIZE, kvw)
    n_new = _rup(ds, SEL_BLOCK) // CMP_STRIDE
    parts = jnp.concatenate([parts_past, parts_new[:, :n_new], jnp.zeros((db, nsp - ns, row_w), F32)], axis=1)
    kvc = _cmp_finish(parts, W['cmp_b1'], W['cmp_w2'], W['cmp_b2'], W['qk_norm_g'][1], nc, nkv)
    kc_cat = kvc[:, :nkv].transpose(0, 2, 1, 3).reshape(db, nsp, kvw)
    vcT_cat = kvc[:, nkv:].transpose(0, 1, 3, 2).reshape(db, kvw, nsp)
    lane_pad = LANES - n_lanes
    q5 = qn.reshape(db, ds, nkv, ngrp, HEAD_DIM)
    qbd = jnp.einsum('bqgrd,gk->bgdkqr', q5, jnp.eye(nkv, dtype=F32)).reshape(db, kvw, n_lanes)
    qbd = jnp.pad(qbd, ((0, 0), (0, 0), (0, lane_pad))).astype(BF16)
    ng = 3 * nkv * ngrp
    gT = gates[:, :ng].reshape(db, ds, nkv, ngrp, 3).transpose(0, 4, 2, 1, 3).reshape(db, 3, n_lanes)
    gT = jnp.pad(gT, ((0, 0), (0, 5), (0, lane_pad)))
    lane = np.arange(LANES)
    live = lane < n_lanes
    head_of_lane = np.where(live, (lane // (ds * ngrp)) * ngrp + lane % ngrp, 0)
    qi = np.where(live, (lane // ngrp) % ds, 0)
    table_l = W['rel_bias'].astype(F32)[:, head_of_lane]
    bias_of = lambda dist: _rel_table_per_lane(dist, table_l)
    c_end = np.arange(nsp) * CMP_STRIDE + CMP_LEN - 1
    bias_c = bias_of(jnp.asarray(past + qi[None, :] - c_end[:, None], jnp.int32))
    pos = np.arange((n_pages + 1) * PAGE_SIZE)
    bias_s = bias_of(jnp.asarray(past + qi[None, :] - pos[:, None], jnp.int32)).reshape(n_pages + 1, PAGE_SIZE, LANES)
    wk = _rup(keep + 8, LANES)
    irow = np.arange(wk)[:, None]
    dist_w = np.where(irow < keep, keep + qi[None, :] - irow, qi[None, :] - (irow - keep))
    bias_w = bias_of(jnp.asarray(dist_w, jnp.int32))
    win_c = cache_win.reshape(db, keep, 2 * kvw)
    wnew8 = jnp.pad(win2d.reshape(db, ds, 2 * kvw), ((0, 0), (0, 8 - ds), (0, 0)))
    rnew8 = jnp.pad(rows2d.reshape(db, ds, row_w), ((0, 0), (0, 8 - ds), (0, 0)))
    n_steps = n_pages // npg
    per_b = lambda shape: pl.BlockSpec((1,) + shape, lambda i, j, pt: (i,) + (0,) * len(shape))
    const = lambda shape: pl.BlockSpec(shape, lambda i, j, pt: (0,) * len(shape))
    in_specs = [per_b((kvw, LANES)), per_b((8, LANES)), per_b((nsp, kvw)), per_b((kvw, nsp)), const((nsp, LANES)),
                per_b((keep, 2 * kvw)), per_b((8, 2 * kvw)), const((wk, LANES)), per_b((8, row_w)),
                const((n_pages + 1, PAGE_SIZE, LANES))]
    in_specs += [pl.BlockSpec((1, PAGE_SIZE, 2 * kvw), (lambda i, j, pt, k=k: (pt[i, j * npg + k], 0, 1)))
                 for k in range(npg)]
    yT = pl.pallas_call(
        functools.partial(_nsa_sample_kernel, npg=npg, n_steps=n_steps, nc=nc, nb=nb, n_sel=n_sel, past=past,
                          ds=ds, keep=keep, ngrp=ngrp, n_lanes=n_lanes),
        grid_spec=pltpu.PrefetchScalarGridSpec(
            num_scalar_prefetch=1, grid=(db, n_steps), in_specs=in_specs,
            out_specs=pl.BlockSpec((1, kvw, LANES), lambda i, j, pt: (i, 0, 0)),
            scratch_shapes=[pltpu.VMEM((1, LANES), F32), pltpu.VMEM((1, LANES), F32), pltpu.VMEM((kvw, LANES), F32),
                            pltpu.VMEM((kvw, LANES), F32), pltpu.VMEM((nbp, LANES), F32),
                            pltpu.VMEM((nsp + 16, LANES), F32), pltpu.VMEM((nbp, LANES), F32),
                            pltpu.VMEM((wk, kvw), F32), pltpu.VMEM((wk, kvw), F32),
                            pltpu.VMEM((PAGE_SIZE, kvw), F32), pltpu.VMEM((PAGE_SIZE, kvw), F32)]),
        out_shape=jax.ShapeDtypeStruct((db, kvw, LANES), F32),
        compiler_params=_cparams(("arbitrary", "arbitrary")),
        name="nsa_sample_attn",
    )(page_table, qbd, gT, kc_cat, vcT_cat, bias_c, win_c, wnew8, bias_w, rnew8, bias_s, *([cache_pages] * npg))
    y6 = yT[:, :, :n_lanes].reshape(db, nkv, HEAD_DIM, nkv, ds, ngrp)
    y = jnp.einsum('bgdgqr->bqgrd', y6).reshape(db * ds, nsa_w)
    win_all = jnp.concatenate([cache_win, win_new.astype(cache_win.dtype)], axis=1)
    n_keep = min(WINDOW, past + ds)
    return y, rows_new, win_all[:, win_all.shape[1] - n_keep:]


def _rms(x, g):
    return x * lax.rsqrt(jnp.mean(x * x, axis=-1, keepdims=True) + RMS_EPS) * g


def _masked_softmax(logits, mask):
    z = jnp.where(mask, logits, NEG)
    return jax.nn.softmax(z, axis=-1) * jnp.any(mask, axis=-1, keepdims=True)


def _rel_bucket(dist):
    d = jnp.maximum(dist, 0)
    exact = REL_BUCKETS // 2
    ratio = jnp.maximum(d, exact).astype(F32) / exact
    large = exact + (jnp.log(ratio) / math.log(REL_MAX_DIST / exact) * (REL_BUCKETS - exact)).astype(jnp.int32)
    return jnp.where(d < exact, d, jnp.minimum(large, REL_BUCKETS - 1))


def _compress(k, w1, b1, w2, b2):
    b, l, g, dk = k.shape
    ns = l // CMP_STRIDE
    r2 = CMP_LEN // CMP_STRIDE
    nc = ns - r2 + 1
    kb = k[:, :ns * CMP_STRIDE].reshape(b, ns, CMP_STRIDE, g, dk)
    w1r = w1.reshape(r2, CMP_STRIDE, dk, -1)
    part = jnp.einsum('bnsgd,hsde->hbnge', kb, w1r)
    hid = part[0, :, :nc] + b1
    for h in range(1, r2):
        hid = hid + part[h, :, h:h + nc]
    return jnp.einsum('bnge,ed->bngd', jax.nn.gelu(hid), w2) + b2


def _to_blocks(k):
    b, l, g, dk = k.shape
    nb = l // SEL_BLOCK
    return k.reshape(b, nb, SEL_BLOCK, g, dk).transpose(0, 3, 1, 2, 4).reshape(b, g, nb, SEL_BLOCK * dk)


def _slc_offsets():
    r1 = SEL_BLOCK // CMP_STRIDE
    r2 = CMP_LEN // CMP_STRIDE
    offs = np.arange(-(r2 - 1), r1)
    wts = np.array([sum(1 for m in range(r1) for n in range(r2) if m - n == o) for o in offs], np.float32)
    return offs, wts


def _nsa_project(p_n, qk_norm_g, nkv, ngrp):
    b, t, _ = p_n.shape
    nsa_w = nkv * ngrp * HEAD_DIM
    kvw = nkv * HEAD_DIM
    q = _rms(p_n[..., :nsa_w].reshape(b, t, nkv, ngrp, HEAD_DIM), qk_norm_g[0])
    kv = p_n[..., nsa_w:nsa_w + 6 * kvw].reshape(b, t, 6, nkv, HEAD_DIM)
    gates = jax.nn.sigmoid(p_n[..., nsa_w + 6 * kvw:nsa_w + 6 * kvw + 3 * nkv * ngrp]).reshape(b, t, nkv, ngrp, 3)
    rows = jnp.stack([kv[:, :, 0], kv[:, :, 1], _rms(kv[:, :, 2], qk_norm_g[2]), kv[:, :, 3]], axis=2)
    win = jnp.stack([_rms(kv[:, :, 4], qk_norm_g[3]), kv[:, :, 5]], axis=2)
    return q, gates, rows, win


def _nsa_context(rows, W):
    kc = _rms(_compress(rows[:, :, 0], W['cmp_w1'][0], W['cmp_b1'][0], W['cmp_w2'][0], W['cmp_b2'][0]), W['qk_norm_g'][1])
    vc = _compress(rows[:, :, 1], W['cmp_w1'][1], W['cmp_b1'][1], W['cmp_w2'][1], W['cmp_b2'][1])
    c_end = jnp.arange(kc.shape[1]) * CMP_STRIDE + CMP_LEN - 1
    return kc, vc, c_end, _to_blocks(rows[:, :, 2]), _to_blocks(rows[:, :, 3])


def _nsa_core(q, gates, q_pos, kc, vc, c_end, ks_blk, vs_blk, kw, vw, kw_pos, rel_bias):
    b, tq, g, r, dk = q.shape
    nb = ks_blk.shape[2]
    nc = kc.shape[1]
    scale = HEAD_DIM ** -0.5
    table = rel_bias.astype(F32).reshape(REL_BUCKETS, g, r)
    dist_c = q_pos[:, None] - c_end[None, :]
    bias_c = table[_rel_bucket(dist_c)].transpose(2, 3, 0, 1)
    lg_c = jnp.einsum('bqgrd,bcgd->bgrqc', q, kc) * scale + bias_c
    p_c = _masked_softmax(lg_c, dist_c >= 0)
    o_c = jnp.einsum('bgrqc,bcgd->bqgrd', p_c, vc)
    offs, wts = _slc_offsets()
    jb = jnp.arange(nb)
    cidx = (SEL_BLOCK // CMP_STRIDE) * jb[:, None] + offs[None, :]
    cval = (cidx >= 0) & (cidx < nc)
    p_grp = jnp.sum(p_c, axis=2)
    p_slc = jnp.sum(p_grp[..., jnp.clip(cidx, 0, nc - 1)] * (wts * cval), axis=-1)
    cur = q_pos // SEL_BLOCK
    forced = (jb[None] == 0) | (jb[None] == cur[:, None]) | (jb[None] == cur[:, None] - 1)
    future = jb[None] > cur[:, None]
    score = jnp.where(future, -1.0, jnp.where(forced, 1e6, p_slc))
    n_sel = min(SEL_TOPK, nb)
    _, sel = lax.top_k(score, n_sel)
    sel_ok = sel <= cur[:, None]
    bi = jnp.arange(b)[:, None, None]
    gi = jnp.arange(g)[None, :, None]
    flat = sel.reshape(b, g, tq * n_sel)
    ksg = ks_blk[bi, gi, flat].reshape(b, g, tq, n_sel * SEL_BLOCK, dk)
    vsg = vs_blk[bi, gi, flat].reshape(b, g, tq, n_sel * SEL_BLOCK, dk)
    pos4 = sel[..., None] * SEL_BLOCK + jnp.arange(SEL_BLOCK)
    ok_s = (sel_ok[..., None] & (pos4 <= q_pos[:, None, None])).reshape(b, g, tq, n_sel * SEL_BLOCK)
    pos_s = pos4.reshape(b, g, tq, n_sel * SEL_BLOCK)
    tg = table.transpose(1, 0, 2)
    bias_s = tg[jnp.arange(g)[None, :, None, None], _rel_bucket(q_pos[:, None] - pos_s)].transpose(0, 1, 4, 2, 3)
    lg_s = jnp.einsum('bqgrd,bgqkd->bgrqk', q, ksg) * scale + bias_s
    p_s = _masked_softmax(lg_s, ok_s[:, :, None])
    o_s = jnp.einsum('bgrqk,bgqkd->bqgrd', p_s, vsg)
    dist_w = q_pos[:, None] - kw_pos[None, :]
    ok_w = (dist_w >= 0) & (dist_w < WINDOW) & (kw_pos[None, :] >= 0)
    bias_w = table[_rel_bucket(dist_w)].transpose(2, 3, 0, 1)
    lg_w = jnp.einsum('bqgrd,bkgd->bgrqk', q, kw) * scale + bias_w
    p_w = _masked_softmax(lg_w, ok_w)
    o_w = jnp.einsum('bgrqk,bkgd->bqgrd', p_w, vw)
    o = gates[..., 0:1] * o_c + gates[..., 1:2] * o_s + gates[..., 2:3] * o_w
    return o.reshape(b, tq, g * r * dk)


def _nsa_prompt_jnp(p_n, W, nkv, ngrp):
    q, gates, rows, win = _nsa_project(p_n, W['qk_norm_g'], nkv, ngrp)
    b, s = q.shape[:2]
    kc, vc, c_end, ks_blk, vs_blk = _nsa_context(rows, W)
    win_pad = jnp.pad(win, ((0, 0), (WINDOW, 0), (0, 0), (0, 0), (0, 0)))
    qb_sz = 128

    def block(i):
        start = i * qb_sz
        qb = lax.dynamic_slice_in_dim(q, start, qb_sz, axis=1)
        gb = lax.dynamic_slice_in_dim(gates, start, qb_sz, axis=1)
        wb = lax.dynamic_slice_in_dim(win_pad, start, WINDOW + qb_sz, axis=1)
        q_pos = start + jnp.arange(qb_sz)
        kw_pos = start - WINDOW + jnp.arange(WINDOW + qb_sz)
        return _nsa_core(qb, gb, q_pos, kc, vc, c_end, ks_blk, vs_blk, wb[:, :, 0], wb[:, :, 1], kw_pos, W['rel_bias'])

    o = lax.map(block, jnp.arange(s // qb_sz))
    o = o.transpose(1, 0, 2, 3).reshape(b, s, -1)
    return o, rows, win[:, s - min(WINDOW, s):]


def _nsa_sample_jnp(p_n, cache_kv, cache_win, page_table, W, nkv, ngrp, db):
    p_n = p_n.reshape(db, p_n.shape[0] // db, -1)
    q, gates, rows_new, win_new = _nsa_project(p_n, W['qk_norm_g'], nkv, ngrp)
    db, ds = q.shape[:2]
    past_len = page_table.shape[1] * PAGE_SIZE
    past_rows = cache_kv[page_table].reshape(db, past_len, 4, nkv, HEAD_DIM)
    pad = (-ds) % SEL_BLOCK
    rows = jnp.concatenate([past_rows, jnp.pad(rows_new, ((0, 0), (0, pad), (0, 0), (0, 0), (0, 0)))], axis=1)
    kc, vc, c_end, ks_blk, vs_blk = _nsa_context(rows, W)
    keep = cache_win.shape[1]
    win_all = jnp.concatenate([cache_win, win_new], axis=1)
    q_pos = past_len + jnp.arange(ds)
    kw_pos = past_len - keep + jnp.arange(keep + ds)
    o = _nsa_core(q, gates, q_pos, kc, vc, c_end, ks_blk, vs_blk, win_all[:, :, 0], win_all[:, :, 1], kw_pos, W['rel_bias'])
    n_keep = min(WINDOW, past_len + ds)
    return o.reshape(db * ds, -1), rows_new, win_all[:, win_all.shape[1] - n_keep:]


def _top_values(x, k):
    n = x.shape[0]
    row = lax.broadcasted_iota(jnp.int32, x.shape, 0)
    vals = []
    for _ in range(k):
        m = jnp.max(x, axis=0, keepdims=True)
        vals.append(m)
        first = jnp.min(jnp.where(x == m, row, n), axis=0, keepdims=True)
        x = jnp.where(row == first, -jnp.inf, x)
    return vals


def _peer_route_kernel(hT_ref, wq_ref, sk_ref, s1_ref, c1_ref, s2_ref, e2_ref, tau_ref, *, nheads, topk):
    nk, half = sk_ref.shape[1], sk_ref.shape[2]
    qT = _dot(wq_ref[...], hT_ref[...])
    for h in range(nheads):
        base = h * 2 * half
        s1 = _dot(sk_ref[0].astype(BF16), qT[base:base + half].astype(BF16))
        s2 = _dot(sk_ref[1].astype(BF16), qT[base + half:base + 2 * half].astype(BF16))
        v1 = _top_values(s1, topk)
        v2 = _top_values(s2, topk)
        cand = jnp.concatenate([v1[a] + v2[b] for a in range(topk) for b in range(topk) if (a + 1) * (b + 1) <= topk],
                               axis=0)
        tau = _top_values(cand, topk)[-1]
        e1 = jnp.exp(s1 - v1[0])
        e2 = jnp.exp(s2 - v2[0])
        z = jnp.sum(jnp.where(cand >= tau, jnp.exp(cand - (v1[0] + v2[0])), 0.0), axis=0, keepdims=True)
        s1_ref[h] = s1
        s2_ref[h] = s2
        c1_ref[h] = e1 / z
        e2_ref[h] = e2
        tau_ref[h] = tau


def _peer_expert_kernel(hT_ref, x1_ref, ga_ref, s1_ref, c1_ref, s2_ref, e2_ref, tau_ref, u_ref, v_ref,
                        o_ref, acc_ref, *, nheads, n_eblocks):
    eb = pl.program_id(1)
    nk = s2_ref.shape[1]
    rows_per_block = u_ref.shape[0] // nk

    @pl.when(eb == 0)
    def _():
        acc_ref[...] = jnp.zeros(acc_ref.shape, F32)

    act = jax.nn.gelu(_dot(u_ref[...], hT_ref[...]))
    gates = []
    for i in range(rows_per_block):
        i1 = eb * rows_per_block + i
        wd = None
        for h in range(nheads):
            cand = s1_ref[h, pl.ds(i1, 1), :] + s2_ref[h]
            term = jnp.where(cand >= tau_ref[h], e2_ref[h], 0.0) * c1_ref[h, pl.ds(i1, 1), :]
            wd = term if wd is None else wd + term
        gates.append(wd)
    g = jnp.concatenate(gates, axis=0) * act
    acc_ref[...] += _dot(g.T.astype(BF16), v_ref[...])

    @pl.when(eb == n_eblocks - 1)
    def _():
        o_ref[...] = x1_ref[...] + ga_ref[0] * acc_ref[...]


def _peer(h2, x1, ga, W, tm):
    t, d = h2.shape
    sub_keys = W['peer_sub_keys']
    nk, half = sub_keys.shape[1], sub_keys.shape[2]
    qd = W['peer_w_query'].shape[1]
    nheads = qd // (2 * half)
    hT = h2.T
    wqT = W['peer_w_query'].T.astype(BF16)
    route_shape = jax.ShapeDtypeStruct((nheads, nk, t), F32)
    rspec = pl.BlockSpec((nheads, nk, tm), lambda i: (0, 0, i))
    s1, c1, s2, e2, tau = pl.pallas_call(
        functools.partial(_peer_route_kernel, nheads=nheads, topk=PEER_TOPK),
        grid=(t // tm,),
        in_specs=[pl.BlockSpec((d, tm), lambda i: (0, i)),
                  pl.BlockSpec((qd, d), lambda i: (0, 0)),
                  pl.BlockSpec(sub_keys.shape, lambda i: (0, 0, 0))],
        out_specs=[rspec, rspec, rspec, rspec, pl.BlockSpec((nheads, 1, tm), lambda i: (0, 0, i))],
        out_shape=[route_shape] * 4 + [jax.ShapeDtypeStruct((nheads, 1, t), F32)],
        compiler_params=_cparams(("arbitrary",)),
        name="peer_route",
    )(hT, wqT, sub_keys)
    eblk = MXU
    n_eblocks = W['peer_u'].shape[0] // eblk
    nmod, rows, _ = ga.shape
    tiles_per_mod = (t // tm) // nmod
    rspec2 = pl.BlockSpec((nheads, nk, tm), lambda i, e: (0, 0, i))
    return pl.pallas_call(
        functools.partial(_peer_expert_kernel, nheads=nheads, n_eblocks=n_eblocks),
        grid=(t // tm, n_eblocks),
        in_specs=[pl.BlockSpec((d, tm), lambda i, e: (0, i)),
                  pl.BlockSpec((tm, d), lambda i, e: (i, 0)),
                  pl.BlockSpec((1, rows, d), lambda i, e: (i // tiles_per_mod, 0, 0)),
                  rspec2, rspec2, rspec2, rspec2,
                  pl.BlockSpec((nheads, 1, tm), lambda i, e: (0, 0, i)),
                  pl.BlockSpec((eblk, d), lambda i, e: (e, 0)),
                  pl.BlockSpec((eblk, d), lambda i, e: (e, 0))],
        out_specs=pl.BlockSpec((tm, d), lambda i, e: (i, 0)),
        out_shape=jax.ShapeDtypeStruct((t, d), F32),
        scratch_shapes=[pltpu.VMEM((tm, d), F32)],
        compiler_params=_cparams(("arbitrary", "arbitrary")),
        name="peer_experts",
    )(hT, x1, ga, s1, c1, s2, e2, tau, W['peer_u_bf16'], W['peer_v_bf16'])


def _peer_jnp(h, W):
    n, d = h.shape
    sub_keys, expert_u, expert_v = W['peer_sub_keys'], W['peer_u'], W['peer_v']
    nk, half = sub_keys.shape[1], sub_keys.shape[2]
    nheads = W['peer_w_query'].shape[1] // (2 * half)
    cs = min(128, n)
    hc = h.reshape(-1, cs, d)

    def chunk(x):
        qh = (x @ W['peer_w_query']).reshape(cs, nheads, 2, half)
        s1 = jnp.einsum('chd,kd->chk', qh[:, :, 0], sub_keys[0])
        s2 = jnp.einsum('chd,kd->chk', qh[:, :, 1], sub_keys[1])
        v1, i1 = lax.top_k(s1, PEER_TOPK)
        v2, i2 = lax.top_k(s2, PEER_TOPK)
        cand = (v1[..., :, None] + v2[..., None, :]).reshape(cs, nheads, -1)
        cidx = (i1[..., :, None] * nk + i2[..., None, :]).reshape(cs, nheads, -1)
        top, pos = lax.top_k(cand, PEER_TOPK)
        eidx = jnp.take_along_axis(cidx, pos, axis=-1)
        gw = jax.nn.softmax(top, axis=-1)
        act = jax.nn.gelu(jnp.einsum('chkd,cd->chk', expert_u[eidx], x))
        return jnp.einsum('chk,chkd->cd', gw * act, expert_v[eidx])

    return lax.map(chunk, hc).reshape(-1, d)


def _layer(x, mods, nsa_fn, shift_prev, wkv0, W, tm, rwkv_chunk):
    b, t, d = x.shape
    cw = W['rwkv_w0'].shape[0]
    nw, na, ng = W['rwkv_w_up'].shape[0], W['rwkv_a_up'].shape[0], W['rwkv_g_up'].shape[0]
    rwkv_proj = 3 * cw + nw + na + ng
    sh1, sc1, ga1, sh2, sc2, ga2 = mods
    xf = x.reshape(b * t, d)
    if (b * t) % tm == 0 and t % tm == 0:
        as_mod = lambda m: m.reshape(b, 1, d)
    else:
        tm = b * t
        as_mod = lambda m: jnp.repeat(m, t, axis=0).reshape(1, b * t, d)
    w_r = _rwkv_pad_cols(W['w_in'][:, :rwkv_proj], cw, nw, na, ng).astype(BF16)
    nsa_cols = W['w_in'].shape[1] - rwkv_proj
    w_n = jnp.pad(W['w_in'][:, rwkv_proj:], ((0, 0), (0, _rup(nsa_cols, LANES) - nsa_cols))).astype(BF16)
    p_r, _ = _norm_mod_matmul(xf, W['norm1_g'], as_mod(sc1), as_mod(sh1), w_r, tm, 512)
    p_n, _ = _norm_mod_matmul(xf, W['norm1_g'], as_mod(sc1), as_mod(sh1), w_n, tm, w_n.shape[1] // 3)
    pr = p_r.shape[1]
    p_r = p_r.reshape(b, t, pr)
    shift_new = _rwkv_unpad_cols(p_r[:, -1], cw, nw, na, ng)
    tpad = _rup(t, rwkv_chunk)
    p_r_pad = jnp.pad(p_r, ((0, 0), (0, tpad - t), (0, 0)))
    y_r, wkv_new = _rwkv_mix(p_r_pad, shift_prev, wkv0, W, rwkv_chunk, t)
    y_r = y_r[:, :t].reshape(b * t, cw)
    y_n, rows, win = nsa_fn(p_n)
    w_out = W['w_out'].astype(BF16)
    x1 = _out_proj(xf, y_r, y_n, as_mod(ga1), w_out[:cw], w_out[cw:], tm, 512)
    h2 = _norm_mod(x1, W['norm2_g'], as_mod(sc2), as_mod(sh2), tm)
    if (b * t) % LANES == 0:
        out = _peer(h2, x1, as_mod(ga2), W, tm)
    else:
        out = x1 + jnp.repeat(ga2, t, axis=0) * _peer_jnp(h2.astype(F32), W)
    return out.reshape(b, t, d), rows, win, wkv_new, shift_new


def kernel(x_prompt, x_sample, c_prompt, c_sample, cache_kv, cache_win, state_wkv, state_shift, page_table,
           norm1_g, norm2_g, w_ada, b_ada, w_in, w_out,
           rwkv_mu, rwkv_w0, rwkv_w_up, rwkv_a0, rwkv_a_up, rwkv_g_up, rwkv_k_k, rwkv_k_a, rwkv_r_k, lnx_w, lnx_b,
           qk_norm_g, cmp_w1, cmp_b1, cmp_w2, cmp_b2, rel_bias,
           peer_w_query, peer_sub_keys, peer_u, peer_v):
    W = dict(norm1_g=norm1_g, norm2_g=norm2_g, w_ada=w_ada, b_ada=b_ada, w_in=w_in, w_out=w_out,
             rwkv_mu=rwkv_mu, rwkv_w0=rwkv_w0, rwkv_w_up=rwkv_w_up, rwkv_a0=rwkv_a0, rwkv_a_up=rwkv_a_up,
             rwkv_g_up=rwkv_g_up, rwkv_k_k=rwkv_k_k, rwkv_k_a=rwkv_k_a, rwkv_r_k=rwkv_r_k, lnx_w=lnx_w, lnx_b=lnx_b,
             qk_norm_g=qk_norm_g, cmp_w1=cmp_w1, cmp_b1=cmp_b1, cmp_w2=cmp_w2, cmp_b2=cmp_b2, rel_bias=rel_bias,
             peer_w_query=peer_w_query, peer_sub_keys=peer_sub_keys, peer_u=peer_u, peer_v=peer_v)
    W['peer_u_bf16'] = peer_u.astype(BF16)
    W['peer_v_bf16'] = peer_v.astype(BF16)
    bp, seq, d = x_prompt.shape
    db = x_sample.shape[0]
    nkv = cache_kv.shape[3]
    nh_r = rwkv_w0.shape[0] // HEAD_DIM
    ngrp = (w_out.shape[0] - rwkv_w0.shape[0]) // HEAD_DIM // nkv

    mods = _ada_mods(jnp.concatenate([c_prompt, c_sample], axis=0), w_ada, b_ada)
    mods = mods.reshape(bp + db, N_MODS, d)
    mods_p = [mods[:bp, i] for i in range(N_MODS)]
    mods_s = [mods[bp:, i] for i in range(N_MODS)]

    shift0 = jnp.zeros((bp, state_shift.shape[1]), F32)
    wkv0 = jnp.zeros((bp, nh_r, HEAD_DIM, HEAD_DIM), F32)
    y_p, rows_p, win_p, wkv_p, shift_p = _layer(
        x_prompt, mods_p, lambda pn: _nsa_prompt(pn, W, bp, seq, nkv, ngrp, 512), shift0, wkv0, W, 512, 64)
    y_s, rows_s, win_s, wkv_s, shift_s = _layer(
        x_sample, mods_s,
        lambda pn: _nsa_sample(pn, cache_kv, cache_win, page_table, W, db, x_sample.shape[1], nkv, ngrp),
        state_shift, state_wkv, W, 512, 32)
    return (y_p, y_s, rows_p, win_p, wkv_p.astype(state_wkv.dtype), shift_p,
            rows_s, win_s, wkv_s.astype(state_wkv.dtype), shift_s)
```

```python
import functools
import math

import numpy as np
import jax
import jax.numpy as jnp
from jax import lax
from jax.experimental import pallas as pl
from jax.experimental.pallas import tpu as pltpu

F32 = jnp.float32
BF16 = jnp.bfloat16
HI = lax.Precision.HIGHEST

HEAD_DIM = 64
PAGE_SIZE = 128
CMP_LEN = 32
CMP_STRIDE = 16
SEL_BLOCK = 64
SEL_TOPK = 16
WINDOW = 512
REL_BUCKETS = 32
REL_MAX_DIST = 2048
PEER_TOPK = 16
N_MODS = 6
RMS_EPS = 1e-6
LNX_EPS = 64e-5
NEG = -1e30

LANES = 128
MXU = 256
HEADS_PER_GROUP = MXU // HEAD_DIM
VMEM_LIMIT = 56 * 1024 * 1024


def _cparams(sem):
    return pltpu.CompilerParams(dimension_semantics=sem, vmem_limit_bytes=VMEM_LIMIT)


def _dot(a, b, precision=None):
    return jnp.dot(a, b, preferred_element_type=F32, precision=precision)


def _dot_nt(a, b, precision=None):
    return lax.dot_general(a, b, (((1,), (1,)), ((), ())), preferred_element_type=F32, precision=precision)


def _split_bf16(x, parts):
    out = []
    for _ in range(parts):
        h = x.astype(BF16)
        out.append(h)
        x = x - h.astype(F32)
    return out


def _mm(a, b, mode, nt=False):
    f = _dot_nt if nt else _dot
    if mode == 6:
        return f(a, b, HI)
    if mode == 1:
        return f(a.astype(BF16), b.astype(BF16))
    if mode == 3:
        ah, al = _split_bf16(a, 2)
        bh, bl = _split_bf16(b, 2)
        return (f(al, bh) + f(ah, bl)) + f(ah, bh)
    if mode[0] == 'L':
        terms = [f(t, b.astype(BF16)) for t in _split_bf16(a, int(mode[1]))]
    else:
        terms = [f(a.astype(BF16), t) for t in _split_bf16(b, int(mode[1]))]
    out = terms[-1]
    for t in terms[-2::-1]:
        out = out + t
    return out


RWKV_MM = dict(lora=1, headsum='L2', cumsum='R3', gram=1, inverse=1, state=3)


def _ada_kernel(c_ref, w_ref, b_ref, o_ref):
    c = c_ref[...]
    s = c * jax.nn.sigmoid(c)
    o_ref[...] = _dot(s.astype(BF16), w_ref[...].astype(BF16)) + b_ref[...]


def _ada_mods(c, w_ada, b_ada):
    n, d = c.shape
    cols = w_ada.shape[1]
    tn = 1024
    return pl.pallas_call(
        _ada_kernel,
        grid=(cols // tn,),
        in_specs=[pl.BlockSpec((n, d), lambda j: (0, 0)),
                  pl.BlockSpec((d, tn), lambda j: (0, j)),
                  pl.BlockSpec((1, tn), lambda j: (0, j))],
        out_specs=pl.BlockSpec((n, tn), lambda j: (0, j)),
        out_shape=jax.ShapeDtypeStruct((n, cols), F32),
        compiler_params=_cparams(("arbitrary",)),
        name="ada_mods",
    )(c, w_ada, b_ada.reshape(1, cols))


def _nmm_kernel(x_ref, g_ref, sc_ref, sh_ref, w_ref, o_ref, h_ref):
    @pl.when(pl.program_id(1) == 0)
    def _():
        x = x_ref[...]
        ms = jnp.mean(x * x, axis=-1, keepdims=True)
        y = x * lax.rsqrt(ms + RMS_EPS) * g_ref[...]
        h_ref[...] = (y * (1.0 + sc_ref[0]) + sh_ref[0]).astype(h_ref.dtype)

    o_ref[...] = _dot(h_ref[...], w_ref[...])


def _norm_mod_matmul(x, g, sc, sh, w, tm, tn):
    t, d = x.shape
    n = w.shape[1]
    nmod, rows, _ = sc.shape
    tiles_per_mod = (t // tm) // nmod
    mod_spec = pl.BlockSpec((1, rows, d), lambda i, j: (i // tiles_per_mod, 0, 0))
    return pl.pallas_call(
        _nmm_kernel,
        grid=(t // tm, n // tn),
        in_specs=[pl.BlockSpec((tm, d), lambda i, j: (i, 0)),
                  pl.BlockSpec((1, d), lambda i, j: (0, 0)),
                  mod_spec, mod_spec,
                  pl.BlockSpec((d, tn), lambda i, j: (0, j))],
        out_specs=[pl.BlockSpec((tm, tn), lambda i, j: (i, j)),
                   pl.BlockSpec((tm, d), lambda i, j: (i, 0))],
        out_shape=[jax.ShapeDtypeStruct((t, n), F32), jax.ShapeDtypeStruct((t, d), BF16)],
        compiler_params=_cparams(("arbitrary", "arbitrary")),
        name="norm_mod_matmul",
    )(x, g.reshape(1, d), sc, sh, w)


def _nm_kernel(x_ref, g_ref, sc_ref, sh_ref, h_ref):
    x = x_ref[...]
    ms = jnp.mean(x * x, axis=-1, keepdims=True)
    y = x * lax.rsqrt(ms + RMS_EPS) * g_ref[...]
    h_ref[...] = (y * (1.0 + sc_ref[0]) + sh_ref[0]).astype(h_ref.dtype)


def _norm_mod(x, g, sc, sh, tm):
    t, d = x.shape
    nmod, rows, _ = sc.shape
    tiles_per_mod = (t // tm) // nmod
    mod_spec = pl.BlockSpec((1, rows, d), lambda i: (i // tiles_per_mod, 0, 0))
    return pl.pallas_call(
        _nm_kernel,
        grid=(t // tm,),
        in_specs=[pl.BlockSpec((tm, d), lambda i: (i, 0)), pl.BlockSpec((1, d), lambda i: (0, 0)), mod_spec, mod_spec],
        out_specs=pl.BlockSpec((tm, d), lambda i: (i, 0)),
        out_shape=jax.ShapeDtypeStruct((t, d), BF16),
        compiler_params=_cparams(("arbitrary",)),
        name="norm_mod",
    )(x, g.reshape(1, d), sc, sh)


def _outproj_kernel(x_ref, yr_ref, yn_ref, ga_ref, w1_ref, w2_ref, o_ref):
    acc = _dot(yr_ref[...].astype(BF16), w1_ref[...]) + _dot(yn_ref[...].astype(BF16), w2_ref[...])
    o_ref[...] = x_ref[...] + ga_ref[0] * acc


def _out_proj(x, y_r, y_n, ga, w1, w2, tm, tn):
    t, d = x.shape
    nmod, rows, _ = ga.shape
    tiles_per_mod = (t // tm) // nmod
    cr, cn = y_r.shape[1], y_n.shape[1]
    return pl.pallas_call(
        _outproj_kernel,
        grid=(t // tm, d // tn),
        in_specs=[pl.BlockSpec((tm, tn), lambda i, j: (i, j)),
                  pl.BlockSpec((tm, cr), lambda i, j: (i, 0)),
                  pl.BlockSpec((tm, cn), lambda i, j: (i, 0)),
                  pl.BlockSpec((1, rows, tn), lambda i, j: (i // tiles_per_mod, 0, j)),
                  pl.BlockSpec((cr, tn), lambda i, j: (0, j)),
                  pl.BlockSpec((cn, tn), lambda i, j: (0, j))],
        out_specs=pl.BlockSpec((tm, tn), lambda i, j: (i, j)),
        out_shape=jax.ShapeDtypeStruct((t, d), F32),
        compiler_params=_cparams(("arbitrary", "arbitrary")),
        name="out_proj",
    )(x, y_r, y_n, ga, w1, w2)


def _softplus(z):
    return jnp.maximum(z, 0.0) + jnp.log(1.0 + jnp.exp(-jnp.abs(z)))


def _rwkv_kernel(p_ref, shift_ref, s0_ref, mu_ref, vec_ref, wup_ref, aup_ref, gup_ref,
                 y_ref, sfin_ref, carry, state, *, t_valid, n_chunks):
    c = pl.program_id(1)
    chunk = p_ref.shape[1]
    cw = vec_ref.shape[1]
    n_groups = cw // MXU
    hg = HEADS_PER_GROUP
    rows_g = hg * chunk

    @pl.when(c == 0)
    def _():
        carry[...] = shift_ref[0]
        state[...] = s0_ref[0]

    p = p_ref[0]
    row = lax.broadcasted_iota(jnp.int32, (chunk, 1), 0)
    prev = jnp.where(row == 0, carry[...], pltpu.roll(p, 1, axis=0))
    carry[...] = p[chunk - 1:chunk, :]
    xs = p + (prev - p) * mu_ref[...]

    w0, a0, k_k, k_a, r_k, lnx_w, lnx_b = (vec_ref[i:i + 1, :] for i in range(7))
    r = xs[:, 0:cw]
    k = xs[:, cw:2 * cw]
    v = xs[:, 2 * cw:3 * cw]
    o = 3 * cw
    nw, na, ng = wup_ref.shape[0], aup_ref.shape[0], gup_ref.shape[0]
    xw = xs[:, o:o + nw]
    xa = xs[:, o + nw:o + nw + na]
    xg = xs[:, o + nw + na:o + nw + na + ng]
    pm = RWKV_MM
    w_log = -_softplus(-(w0 + _mm(jnp.tanh(xw), wup_ref[...], pm['lora']))) - 0.5
    a = jax.nn.sigmoid(a0 + _mm(xa, aup_ref[...], pm['lora']))
    gate = _mm(jax.nn.sigmoid(xg), gup_ref[...], pm['lora'])

    gi = lax.broadcasted_iota(jnp.int32, (MXU, MXU), 0) // HEAD_DIM
    gj = lax.broadcasted_iota(jnp.int32, (MXU, MXU), 1) // HEAD_DIM
    ones_bd = (gi == gj).astype(F32)

    def head_sum(x):
        return jnp.concatenate([_mm(x[:, g * MXU:(g + 1) * MXU], ones_bd, pm['headsum']) for g in range(n_groups)],
                               axis=1)

    kk = k * k_k
    kk = kk / jnp.maximum(jnp.sqrt(head_sum(kk * kk)), 1e-12)
    k2 = k * (1.0 + (a - 1.0) * k_a)
    logdec = -jnp.exp(w_log)
    if t_valid < chunk * n_chunks:
        valid = (row + c * chunk) < t_valid
        logdec = jnp.where(valid, logdec, 0.0)
        kk = jnp.where(valid, kk, 0.0)
        k2 = jnp.where(valid, k2, 0.0)
        v = jnp.where(valid, v, 0.0)

    ti = lax.broadcasted_iota(jnp.int32, (chunk, chunk), 0)
    tj = lax.broadcasted_iota(jnp.int32, (chunk, chunk), 1)
    cum = _mm((tj <= ti).astype(F32), logdec, pm['cumsum'])
    cum_end = cum[chunk - 1:chunk, :]
    e_neg = jnp.exp(-cum)
    e_rem = jnp.exp(cum_end - cum)
    r_t = r * jnp.exp(cum)
    a_t = -kk * jnp.exp(cum - logdec)
    b_vec = kk * a
    b_t = b_vec * e_neg
    k_t = k2 * e_neg
    b_rem = b_vec * e_rem
    k_rem = k2 * e_rem
    w_end = jnp.exp(cum_end)

    lane_head = lax.broadcasted_iota(jnp.int32, (chunk, MXU), 1) // HEAD_DIM
    ri = lax.broadcasted_iota(jnp.int32, (rows_g, rows_g), 0)
    rj = lax.broadcasted_iota(jnp.int32, (rows_g, rows_g), 1)
    strict = rj < ri
    incl = rj <= ri
    eye_r = (ri == rj).astype(F32)
    di = lax.broadcasted_iota(jnp.int32, (MXU, MXU), 0)
    dj = lax.broadcasted_iota(jnp.int32, (MXU, MXU), 1)
    n_double = max(int(math.ceil(math.log2(chunk))) - 1, 0)

    def bd(x):
        return jnp.concatenate([jnp.where(lane_head == h, x, 0.0) for h in range(hg)], axis=0)

    def stack(x):
        return jnp.concatenate([x[:, h * HEAD_DIM:(h + 1) * HEAD_DIM] for h in range(hg)], axis=0)

    def unstack(x):
        return jnp.concatenate([x[h * chunk:(h + 1) * chunk, :] for h in range(hg)], axis=1)

    ys = []
    for g in range(n_groups):
        sl = slice(g * MXU, (g + 1) * MXU)
        a_bd, r_bd = bd(a_t[:, sl]), bd(r_t[:, sl])
        b_bd, k_bd = bd(b_t[:, sl]), bd(k_t[:, sl])
        v_st = stack(v[:, sl])
        a_ab = jnp.where(strict, _mm(a_bd, b_bd, pm['gram'], nt=True), 0.0)
        a_ak = jnp.where(strict, _mm(a_bd, k_bd, pm['gram'], nt=True), 0.0)
        a_rb = jnp.where(incl, _mm(r_bd, b_bd, pm['gram'], nt=True), 0.0)
        a_rk = jnp.where(incl, _mm(r_bd, k_bd, pm['gram'], nt=True), 0.0)
        tinv = eye_r + a_ab
        pw = a_ab
        for _ in range(n_double):
            pw = _mm(pw, pw, pm['inverse'])
            tinv = tinv + _mm(pw, tinv, pm['inverse'])
        s0 = state[g]
        z = _mm(a_bd, s0, pm['state']) + _mm(a_ak, v_st, pm['state'])
        u = _mm(tinv, z, pm['state'])
        y_st = _mm(r_bd, s0, pm['state']) + _mm(a_rb, u, pm['state']) + _mm(a_rk, v_st, pm['state'])
        w_col = jnp.sum(jnp.where(di == dj, jnp.broadcast_to(w_end[:, sl], (MXU, MXU)), 0.0), axis=1, keepdims=True)
        state[g] = (w_col * s0 + _mm(bd(b_rem[:, sl]).T, u, pm['state'])
                    + _mm(bd(k_rem[:, sl]).T, v_st, pm['state']))
        ys.append(unstack(y_st))
    y = jnp.concatenate(ys, axis=1)

    inv_n = 1.0 / HEAD_DIM
    mean = head_sum(y) * inv_n
    d = y - mean
    var = head_sum(d * d) * inv_n
    yn = d * lax.rsqrt(var + LNX_EPS) * lnx_w + lnx_b
    bonus = head_sum(r * k2 * r_k) * v
    y_ref[0] = ((yn + bonus) * gate).astype(y_ref.dtype)

    @pl.when(c == n_chunks - 1)
    def _():
        sfin_ref[0] = state[...]


def _rwkv_pad_cols(x, cw, nw, na, ng):
    o = 3 * cw
    parts = [x[..., :o + nw], x[..., o + nw:o + nw + na], x[..., o + nw + na:]]
    widths = [o + _rup(nw, LANES), _rup(na, LANES), _rup(ng, LANES)]
    out = []
    for part, wd in zip(parts, widths):
        pad = [(0, 0)] * (x.ndim - 1) + [(0, wd - part.shape[-1])]
        out.append(jnp.pad(part, pad))
    return jnp.concatenate(out, axis=-1)


def _rwkv_unpad_cols(x, cw, nw, na, ng):
    o = 3 * cw
    o2 = o + _rup(nw, LANES)
    o3 = o2 + _rup(na, LANES)
    return jnp.concatenate([x[..., :o + nw], x[..., o2:o2 + na], x[..., o3:o3 + ng]], axis=-1)


def _rup(x, m):
    return (x + m - 1) // m * m


def _rwkv_mix(p_r, shift_prev, wkv0, W, chunk, t_valid):
    b, tpad, pr = p_r.shape
    cw = W['rwkv_w0'].shape[0]
    nh = cw // HEAD_DIM
    n_groups = cw // MXU
    nw, na, ng = W['rwkv_w_up'].shape[0], W['rwkv_a_up'].shape[0], W['rwkv_g_up'].shape[0]
    n_chunks = tpad // chunk
    mu = _rwkv_pad_cols(W['rwkv_mu'], cw, nw, na, ng).reshape(1, pr)
    vecs = jnp.stack([W['rwkv_w0'], W['rwkv_a0'], W['rwkv_k_k'], W['rwkv_k_a'], W['rwkv_r_k'].reshape(cw),
                      W['lnx_w'], W['lnx_b'], jnp.zeros((cw,), F32)])
    wup = jnp.pad(W['rwkv_w_up'], ((0, _rup(nw, LANES) - nw), (0, 0)))
    aup = jnp.pad(W['rwkv_a_up'], ((0, _rup(na, LANES) - na), (0, 0)))
    gup = jnp.pad(W['rwkv_g_up'], ((0, _rup(ng, LANES) - ng), (0, 0)))
    shift3 = _rwkv_pad_cols(shift_prev, cw, nw, na, ng).reshape(b, 1, pr)
    s0 = wkv0.astype(F32).transpose(0, 1, 3, 2).reshape(b, n_groups, MXU, HEAD_DIM)
    const = lambda shape: pl.BlockSpec(shape, lambda i, c: (0,) * len(shape))
    y, sfin = pl.pallas_call(
        functools.partial(_rwkv_kernel, t_valid=t_valid, n_chunks=n_chunks),
        grid=(b, n_chunks),
        in_specs=[pl.BlockSpec((1, chunk, pr), lambda i, c: (i, c, 0)),
                  pl.BlockSpec((1, 1, pr), lambda i, c: (i, 0, 0)),
                  pl.BlockSpec((1, n_groups, MXU, HEAD_DIM), lambda i, c: (i, 0, 0, 0)),
                  const((1, pr)), const((8, cw)), const(wup.shape), const(aup.shape), const(gup.shape)],
        out_specs=[pl.BlockSpec((1, chunk, cw), lambda i, c: (i, c, 0)),
                   pl.BlockSpec((1, n_groups, MXU, HEAD_DIM), lambda i, c: (i, 0, 0, 0))],
        out_shape=[jax.ShapeDtypeStruct((b, tpad, cw), F32),
                   jax.ShapeDtypeStruct((b, n_groups, MXU, HEAD_DIM), F32)],
        scratch_shapes=[pltpu.VMEM((1, pr), F32), pltpu.VMEM((n_groups, MXU, HEAD_DIM), F32)],
        compiler_params=_cparams(("arbitrary", "arbitrary")),
        name="rwkv_mix",
    )(p_r, shift3, s0, mu, vecs, wup, aup, gup)
    s_fin = sfin.reshape(b, nh, HEAD_DIM, HEAD_DIM).transpose(0, 1, 3, 2)
    return y, s_fin


def _head_ones():
    gi = lax.broadcasted_iota(jnp.int32, (MXU, MXU), 0) // HEAD_DIM
    gj = lax.broadcasted_iota(jnp.int32, (MXU, MXU), 1) // HEAD_DIM
    return (gi == gj).astype(F32)


def _nsa_proj_kernel(p_ref, g_ref, q_ref, rows_ref, win_ref, gate_ref, *, nsa_w, kvw):
    ones_bd = _head_ones()

    def hnorm(x, gvec):
        ms = _dot(x * x, ones_bd, HI) * (1.0 / HEAD_DIM)
        return x * lax.rsqrt(ms + RMS_EPS) * gvec

    for i in range(nsa_w // MXU):
        sl = slice(i * MXU, (i + 1) * MXU)
        q_ref[:, sl] = hnorm(p_ref[:, sl], g_ref[0:1, :])
    o = nsa_w
    rows_ref[:, 0:2 * kvw] = p_ref[:, o:o + 2 * kvw]
    rows_ref[:, 2 * kvw:3 * kvw] = hnorm(p_ref[:, o + 2 * kvw:o + 3 * kvw], g_ref[2:3, :])
    rows_ref[:, 3 * kvw:4 * kvw] = p_ref[:, o + 3 * kvw:o + 4 * kvw]
    win_ref[:, 0:kvw] = hnorm(p_ref[:, o + 4 * kvw:o + 5 * kvw], g_ref[3:4, :])
    win_ref[:, kvw:2 * kvw] = p_ref[:, o + 5 * kvw:o + 6 * kvw]
    gate_ref[...] = jax.nn.sigmoid(p_ref[:, o + 6 * kvw:])


def _nsa_project_call(p_n, qk_norm_g, nsa_w, kvw, tm):
    t, pc = p_n.shape
    assert kvw == MXU and nsa_w % MXU == 0
    gcols = pc - nsa_w - 6 * kvw
    gvec = jnp.tile(qk_norm_g, (1, MXU // HEAD_DIM))
    return pl.pallas_call(
        functools.partial(_nsa_proj_kernel, nsa_w=nsa_w, kvw=kvw),
        grid=(t // tm,),
        in_specs=[pl.BlockSpec((tm, pc), lambda i: (i, 0)),
                  pl.BlockSpec(gvec.shape, lambda i: (0, 0))],
        out_specs=[pl.BlockSpec((tm, nsa_w), lambda i: (i, 0)),
                   pl.BlockSpec((tm, 4 * kvw), lambda i: (i, 0)),
                   pl.BlockSpec((tm, 2 * kvw), lambda i: (i, 0)),
                   pl.BlockSpec((tm, gcols), lambda i: (i, 0))],
        out_shape=[jax.ShapeDtypeStruct((t, nsa_w), F32), jax.ShapeDtypeStruct((t, 4 * kvw), F32),
                   jax.ShapeDtypeStruct((t, 2 * kvw), F32), jax.ShapeDtypeStruct((t, gcols), F32)],
        compiler_params=_cparams(("arbitrary",)),
        name="nsa_project",
    )(p_n, gvec)


def _cmp_part_kernel(*refs, n_in, row_w, kvw, n_prefetch=0):
    refs = refs[n_prefetch:]
    x_refs, w_ref, o_ref = refs[:n_in], refs[n_in], refs[n_in + 1]
    for typ in range(2):
        acc = None
        for s in range(CMP_STRIDE):
            lo = s * row_w + typ * kvw
            xs = jnp.concatenate([x[0, :, lo:lo + kvw] for x in x_refs], axis=0) if n_in > 1 else x_refs[0][0, :, lo:lo + kvw]
            d = _dot(xs.astype(BF16), w_ref[typ, s])
            acc = d if acc is None else acc + d
        o_ref[0, :, typ * 2 * kvw:(typ + 1) * 2 * kvw] = acc


def _cmp_first_weights(cmp_w1, nkv):
    r2 = CMP_LEN // CMP_STRIDE
    e = cmp_w1.shape[-1]
    w1r = cmp_w1.reshape(2, r2, CMP_STRIDE, HEAD_DIM, e)
    eye = jnp.eye(nkv, dtype=F32)
    big = jnp.einsum('yhsde,gk->ysgdkhe', w1r, eye)
    return big.reshape(2, CMP_STRIDE, nkv * HEAD_DIM, nkv * r2 * e).astype(BF16)


def _cmp_parts_prompt(rows2d, w_big, b, t, kvw):
    row_w = rows2d.shape[2]
    nsub = t // CMP_STRIDE
    blk = min(nsub, LANES)
    return pl.pallas_call(
        functools.partial(_cmp_part_kernel, n_in=1, row_w=row_w, kvw=kvw),
        grid=(b, nsub // blk),
        in_specs=[pl.BlockSpec((1, blk, CMP_STRIDE * row_w), lambda i, j: (i, j, 0)),
                  pl.BlockSpec(w_big.shape, lambda i, j: (0, 0, 0, 0))],
        out_specs=pl.BlockSpec((1, blk, 4 * kvw), lambda i, j: (i, j, 0)),
        out_shape=jax.ShapeDtypeStruct((b, nsub, 4 * kvw), F32),
        compiler_params=_cparams(("arbitrary", "arbitrary")),
        name="cmp_parts",
    )(rows2d.reshape(b, nsub, CMP_STRIDE * row_w), w_big)


def _cmp_finish_kernel(p_ref, b1_ref, w2_ref, b2_ref, g_ref, o_ref, *, nc, nkv):
    tg = pl.program_id(1)
    ns = p_ref.shape[1]
    e = p_ref.shape[2] // 2
    part = p_ref[0]
    nxt = pltpu.roll(part[:, e:], ns - 1, axis=0)
    hid = part[:, :e] + nxt + b1_ref[0]
    out = _dot(jax.nn.gelu(hid).astype(BF16), w2_ref[0].astype(BF16)) + b2_ref[0]
    normed = out * lax.rsqrt(jnp.mean(out * out, axis=-1, keepdims=True) + RMS_EPS) * g_ref[...]
    out = jnp.where(tg < nkv, normed, out)
    row = lax.broadcasted_iota(jnp.int32, (ns, 1), 0)
    o_ref[0, 0] = jnp.where(row < nc, out, 0.0)


def _cmp_finish(parts, cmp_b1, cmp_w2, cmp_b2, g1, nc, nkv):
    b, ns, _ = parts.shape
    e = cmp_b1.shape[1]
    return pl.pallas_call(
        functools.partial(_cmp_finish_kernel, nc=nc, nkv=nkv),
        grid=(b, 2 * nkv),
        in_specs=[pl.BlockSpec((1, ns, 2 * e), lambda i, j: (i, 0, j)),
                  pl.BlockSpec((1, 1, e), lambda i, j: (j // nkv, 0, 0)),
                  pl.BlockSpec((1, e, HEAD_DIM), lambda i, j: (j // nkv, 0, 0)),
                  pl.BlockSpec((1, 1, HEAD_DIM), lambda i, j: (j // nkv, 0, 0)),
                  pl.BlockSpec((1, HEAD_DIM), lambda i, j: (0, 0))],
        out_specs=pl.BlockSpec((1, 1, ns, HEAD_DIM), lambda i, j: (i, j, 0, 0)),
        out_shape=jax.ShapeDtypeStruct((b, 2 * nkv, ns, HEAD_DIM), F32),
        compiler_params=_cparams(("arbitrary", "arbitrary")),
        name="cmp_finish",
    )(parts, cmp_b1.reshape(2, 1, e), cmp_w2, cmp_b2.reshape(2, 1, HEAD_DIM), g1.reshape(1, HEAD_DIM))


QT = 128
QK_SCALE = HEAD_DIM ** -0.5
assert math.log2(HEAD_DIM) % 2 == 0, "QK_SCALE must be a power of two to be folded into q exactly"


def _rel_table_np_dist(dist, table):
    onehot = (_rel_bucket(dist)[..., None] == jnp.arange(REL_BUCKETS)).astype(F32)
    return jnp.einsum('...b,bh->...h', onehot, table, precision=HI)


def _rel_table_per_lane(dist, table_l):
    bucket = _rel_bucket(dist)
    out = jnp.zeros(dist.shape, F32)
    for b in range(REL_BUCKETS):
        out = out + jnp.where(bucket == b, table_l[b][None, :], 0.0)
    return out


def _softmax_update(s, mask, m, l):
    m_new = jnp.maximum(m, jnp.max(jnp.where(mask, s, NEG), axis=0, keepdims=True))
    alpha = jnp.exp(m - m_new)
    p = jnp.where(mask, jnp.exp(s - m_new), 0.0)
    return p, m_new, alpha, alpha * l + jnp.sum(p, axis=0, keepdims=True)


def _rank_select(score, score_ref, cur, n_sel):
    nb = score.shape[0]
    score_ref[0:nb, :] = score
    jrow = lax.broadcasted_iota(jnp.int32, score.shape, 0)

    def body(j, rank):
        other = score_ref[pl.ds(j, 1), :]
        beats = (other > score) | ((other == score) & (jrow > j))
        return rank + jnp.where(beats, 1.0, 0.0)

    rank = lax.fori_loop(0, nb, body, jnp.zeros(score.shape, F32), unroll=8)
    return jnp.where((rank < n_sel) & (jrow <= cur), 1.0, 0.0)


def _nsa_prompt_kernel(qT_ref, gT_ref, kc_ref, vcT_ref, ks_ref, vsT_ref, kw_ref, vwT_ref, bc_ref, toep_ref,
                       o_ref, pg_ref, score_ref, sel_ref, *, nc, nb, n_sel, ngrp):
    qt = pl.program_id(2)
    lanes = ngrp * QT
    q = (qT_ref[0, 0, 0] * QK_SCALE).astype(BF16)
    iq = lax.broadcasted_iota(jnp.int32, (1, QT), 1)
    q_pos = qt * QT + iq
    tile4 = lambda x: jnp.concatenate([x] * ngrp, axis=1)

    ncp = kc_ref.shape[2]
    s = _dot(kc_ref[0, 0].astype(BF16), q)
    s = s + jnp.concatenate([bc_ref[0, r] for r in range(ngrp)], axis=1)
    crow = lax.broadcasted_iota(jnp.int32, (ncp, QT), 0)
    ok_c = tile4((crow * CMP_STRIDE + (CMP_LEN - 1) <= q_pos) & (crow < nc))
    p, _, _, l = _softmax_update(s, ok_c, jnp.full((1, lanes), NEG, F32), jnp.zeros((1, lanes), F32))
    p = p * jnp.where(l > 0.0, 1.0 / jnp.where(l > 0.0, l, 1.0), 0.0)
    o_c = _dot(vcT_ref[0, 0].astype(BF16), p.astype(BF16))

    p_grp = p[:, 0:QT]
    for r in range(1, ngrp):
        p_grp = p_grp + p[:, r * QT:(r + 1) * QT]
    pad = 8
    pg_ref[...] = jnp.zeros(pg_ref.shape, F32)
    pg_ref[pad:pad + ncp, :] = p_grp
    r1 = SEL_BLOCK // CMP_STRIDE
    offs, wts = _slc_offsets()
    p_slc = None
    for o, wt in zip(offs, wts):
        term = float(wt) * pg_ref[pl.ds(pad + int(o), nb, stride=r1), :]
        p_slc = term if p_slc is None else p_slc + term
    jrow = lax.broadcasted_iota(jnp.int32, (nb, QT), 0)
    cur = q_pos // SEL_BLOCK
    forced = (jrow == 0) | (jrow == cur) | (jrow == cur - 1)
    score = jnp.where(jrow > cur, -1.0, jnp.where(forced, 1e6, p_slc))
    sel_ref[0:nb, :] = _rank_select(score, score_ref, cur, n_sel)

    ik = lax.broadcasted_iota(jnp.int32, (QT, QT), 0)
    iqq = lax.broadcasted_iota(jnp.int32, (QT, QT), 1)
    blocks_per_tile = QT // SEL_BLOCK

    def attend(kp, carry, k_ref, vT_ref, mask_fn):
        m, l, acc = carry
        s = _dot(k_ref[0, 0, kp], q)
        bias, mask = [], []
        for i in range(2):
            delta = qt - (2 * kp + i)
            dc = jnp.maximum(delta, 0)
            bias.append(jnp.concatenate([toep_ref[0, r, dc] for r in range(ngrp)], axis=1))
            mask.append(tile4(mask_fn(2 * kp + i, delta)))
        s = jnp.where(jnp.concatenate(mask, axis=0), s + jnp.concatenate(bias, axis=0), NEG)
        m_new = jnp.maximum(m, jnp.max(s, axis=0, keepdims=True))
        alpha = jnp.exp(m - m_new)
        p = jnp.exp(s - m_new)
        l = alpha * l + jnp.sum(p, axis=0, keepdims=True)
        acc = alpha * acc + _dot(vT_ref[0, 0, kp], p.astype(BF16))
        return m_new, l, acc

    def sel_mask(kt, delta):
        rows = [jnp.broadcast_to(sel_ref[pl.ds(kt * blocks_per_tile + i, 1), :], (SEL_BLOCK, QT))
                for i in range(blocks_per_tile)]
        chosen = jnp.concatenate(rows, axis=0) > 0.5
        return chosen & (ik - iqq <= delta * QT)

    def win_mask(kt, delta):
        dist = delta * QT + iqq - ik
        return (dist >= 0) & (dist < WINDOW)

    init = (jnp.full((1, lanes), NEG, F32), jnp.zeros((1, lanes), F32), jnp.zeros((HEAD_DIM, lanes), F32))
    finish = lambda c: c[2] * jnp.where(c[1] > 0.0, 1.0 / jnp.where(c[1] > 0.0, c[1], 1.0), 0.0)
    diag = qt // 2
    o_s = finish(lax.fori_loop(0, diag + 1, lambda kp, c: attend(kp, c, ks_ref, vsT_ref, sel_mask), init))
    first = jnp.maximum(qt - WINDOW // QT, 0) // 2
    o_w = finish(lax.fori_loop(0, diag - first + 1,
                               lambda i, c: attend(diag - i, c, kw_ref, vwT_ref, win_mask), init))
    g = gT_ref[0, 0, 0]
    o_ref[0, 0, 0] = g[0:1, :] * o_c + g[1:2, :] * o_s + g[2:3, :] * o_w


def _nsa_prompt(p_n, W, b, t, nkv, ngrp, tm):
    nsa_w = nkv * ngrp * HEAD_DIM
    kvw = nkv * HEAD_DIM
    qn, rows2d, win2d, gates = _nsa_project_call(p_n, W['qk_norm_g'], nsa_w, kvw, tm)
    rows = rows2d.reshape(b, t, 4, nkv, HEAD_DIM)
    win = win2d.reshape(b, t, 2, nkv, HEAD_DIM)
    ns = t // CMP_STRIDE
    nc = ns - CMP_LEN // CMP_STRIDE + 1
    nb = t // SEL_BLOCK
    n_sel = min(SEL_TOPK, nb)
    nqt = t // QT
    parts = _cmp_parts_prompt(rows2d.reshape(b, t, 4 * kvw), _cmp_first_weights(W['cmp_w1'], nkv), b, t, kvw)
    kvc = _cmp_finish(parts, W['cmp_b1'], W['cmp_w2'], W['cmp_b2'], W['qk_norm_g'][1], nc, nkv)
    kc = kvc[:, :nkv]
    vcT = kvc[:, nkv:].transpose(0, 1, 3, 2)
    qT = qn.reshape(b, nqt, QT, nkv, ngrp, HEAD_DIM).transpose(0, 3, 1, 5, 4, 2).reshape(b, nkv, nqt, HEAD_DIM, ngrp * QT)
    ng = 3 * nkv * ngrp
    gT = gates[:, :ng].reshape(b, nqt, QT, nkv, ngrp, 3).transpose(0, 3, 1, 5, 4, 2).reshape(b, nkv, nqt, 3, ngrp * QT)
    gT = jnp.pad(gT, ((0, 0), (0, 0), (0, 0), (0, 5), (0, 0)))
    assert nqt % 2 == 0
    npair = nqt // 2
    k_tiles = lambda x: x.transpose(0, 2, 1, 3).reshape(b, nkv, npair, 2 * QT, HEAD_DIM).astype(BF16)
    vT_tiles = lambda x: x.reshape(b, npair, 2 * QT, nkv, HEAD_DIM).transpose(0, 3, 1, 4, 2).astype(BF16)
    ks, vsT = k_tiles(rows[:, :, 2]), vT_tiles(rows[:, :, 3])
    kw, vwT = k_tiles(win[:, :, 0]), vT_tiles(win[:, :, 1])
    table = W['rel_bias'].astype(F32)
    c_end = jnp.arange(ns) * CMP_STRIDE + CMP_LEN - 1
    bias_c = _rel_table_np_dist(jnp.arange(t)[None, :] - c_end[:, None], table)
    bias_c = bias_c.transpose(2, 0, 1).reshape(nkv, ngrp, ns, t)
    dd = (jnp.arange(nqt)[:, None, None] * QT + jnp.arange(QT)[None, None, :] - jnp.arange(QT)[None, :, None])
    toep = _rel_table_np_dist(dd, table).transpose(3, 0, 1, 2).reshape(nkv, ngrp, nqt, QT, QT)
    lanes = ngrp * QT
    kv_spec = lambda shape: pl.BlockSpec((1, 1) + shape, lambda i, g, j: (i, g) + (0,) * len(shape))
    yT = pl.pallas_call(
        functools.partial(_nsa_prompt_kernel, nc=nc, nb=nb, n_sel=n_sel, ngrp=ngrp),
        grid=(b, nkv, nqt),
        in_specs=[pl.BlockSpec((1, 1, 1, HEAD_DIM, lanes), lambda i, g, j: (i, g, j, 0, 0)),
                  pl.BlockSpec((1, 1, 1, 8, lanes), lambda i, g, j: (i, g, j, 0, 0)),
                  kv_spec((ns, HEAD_DIM)), kv_spec((HEAD_DIM, ns)),
                  kv_spec((npair, 2 * QT, HEAD_DIM)), kv_spec((npair, HEAD_DIM, 2 * QT)),
                  kv_spec((npair, 2 * QT, HEAD_DIM)), kv_spec((npair, HEAD_DIM, 2 * QT)),
                  pl.BlockSpec((1, ngrp, ns, QT), lambda i, g, j: (g, 0, 0, j)),
                  pl.BlockSpec((1, ngrp, nqt, QT, QT), lambda i, g, j: (g, 0, 0, 0, 0))],
        out_specs=pl.BlockSpec((1, 1, 1, HEAD_DIM, lanes), lambda i, g, j: (i, g, j, 0, 0)),
        out_shape=jax.ShapeDtypeStruct((b, nkv, nqt, HEAD_DIM, lanes), F32),
        scratch_shapes=[pltpu.VMEM((ns + 16, QT), F32), pltpu.VMEM((_rup(nb, 8), QT), F32),
                        pltpu.VMEM((_rup(nb, 8), QT), F32)],
        compiler_params=_cparams(("arbitrary", "arbitrary", "arbitrary")),
        name="nsa_prompt_attn",
    )(qT, gT, kc, vcT, ks, vsT, kw, vwT, bias_c, toep)
    y = yT.reshape(b, nkv, nqt, HEAD_DIM, ngrp, QT).transpose(0, 2, 5, 1, 4, 3).reshape(b * t, nsa_w)
    return y, rows, win[:, t - min(WINDOW, t):]


def _cmp_part_paged_kernel(*refs, npg, kvw):
    x_refs, w_ref, o_ref = refs[1:1 + npg], refs[1 + npg], refs[2 + npg]
    rows = x_refs[0].shape[2]
    sub = rows // CMP_STRIDE
    ri = lax.broadcasted_iota(jnp.int32, (rows, rows), 0)
    ci = lax.broadcasted_iota(jnp.int32, (rows, rows), 1)
    perm = jnp.where(ci == (ri % sub) * CMP_STRIDE + ri // sub, 1.0, 0.0).astype(BF16)
    xp = [_dot_nt(perm, x[0].astype(BF16)) for x in x_refs]
    for typ in range(2):
        acc = None
        for s in range(CMP_STRIDE):
            xs = jnp.concatenate([p[s * sub:(s + 1) * sub, typ * kvw:(typ + 1) * kvw] for p in xp], axis=0)
            d = _dot(xs.astype(BF16), w_ref[typ, s])
            acc = d if acc is None else acc + d
        o_ref[0, :, typ * 2 * kvw:(typ + 1) * 2 * kvw] = acc


def _cmp_parts_sample(cache_pages, page_table, w_big, kvw, npg):
    b, n_pages = page_table.shape
    rows = cache_pages.shape[2]
    sub = rows // CMP_STRIDE
    in_specs = [pl.BlockSpec((1, 2 * kvw, rows), (lambda i, j, pt, k=k: (pt[i, j * npg + k], 0, 0)))
                for k in range(npg)]
    in_specs.append(pl.BlockSpec(w_big.shape, lambda i, j, pt: (0, 0, 0, 0)))
    return pl.pallas_call(
        functools.partial(_cmp_part_paged_kernel, npg=npg, kvw=kvw),
        grid_spec=pltpu.PrefetchScalarGridSpec(
            num_scalar_prefetch=1, grid=(b, n_pages // npg), in_specs=in_specs,
            out_specs=pl.BlockSpec((1, npg * sub, 4 * kvw), lambda i, j, pt: (i, j, 0))),
        out_shape=jax.ShapeDtypeStruct((b, n_pages * sub, 4 * kvw), F32),
        compiler_params=_cparams(("arbitrary", "arbitrary")),
        name="cmp_parts_paged",
    )(page_table, *([cache_pages] * npg), w_big)


def _inv_pos(l):
    return jnp.where(l > 0.0, 1.0 / jnp.where(l > 0.0, l, 1.0), 0.0)


def _nsa_sample_kernel(*refs, npg, n_steps, nc, nb, n_sel, past, ds, keep, ngrp, n_lanes):
    pt_ref = refs[0]
    q_ref, g_ref, kc_ref, vcT_ref, bc_ref, win_ref, wnew_ref, bw_ref, rnew_ref, bs_ref = refs[1:11]
    page_refs = refs[11:11 + npg]
    o_ref = refs[11 + npg]
    (m_ref, l_ref, acc_ref, base_ref, sel_ref, pg_ref, score_ref,
     kw_ref, kn_ref, vn_ref) = refs[12 + npg:]
    del pt_ref
    j = pl.program_id(1)
    scale = HEAD_DIM ** -0.5
    kvw = q_ref.shape[1]
    eye = jnp.where(lax.broadcasted_iota(jnp.int32, (LANES, LANES), 0)
                    == lax.broadcasted_iota(jnp.int32, (LANES, LANES), 1), 1.0, 0.0).astype(BF16)
    qbd = q_ref[0]
    lane = lax.broadcasted_iota(jnp.int32, (1, LANES), 1)
    qi = (lane // ngrp) % ds
    q_pos = past + qi
    n_pages = npg * n_steps

    @pl.when(j == 0)
    def _():
        nsp = kc_ref.shape[1]
        s = _dot(kc_ref[0].astype(BF16), qbd) * scale + bc_ref[...]
        crow = lax.broadcasted_iota(jnp.int32, (nsp, LANES), 0)
        ok = (crow * CMP_STRIDE + (CMP_LEN - 1) <= q_pos) & (crow < nc)
        p, _, _, l = _softmax_update(s, ok, jnp.full((1, LANES), NEG, F32), jnp.zeros((1, LANES), F32))
        p = p * _inv_pos(l)
        o_c = _dot(vcT_ref[0].astype(BF16), p.astype(BF16))
        li = lax.broadcasted_iota(jnp.int32, (LANES, LANES), 0)
        lj = lax.broadcasted_iota(jnp.int32, (LANES, LANES), 1)
        fold = jnp.where((li // ngrp == lj) & (li < n_lanes), 1.0, 0.0)
        p_grp = _dot(p, fold, HI)
        pad = 8
        pg_ref[...] = jnp.zeros(pg_ref.shape, F32)
        pg_ref[pad:pad + nsp, :] = p_grp
        nbp = sel_ref.shape[0]
        r1 = SEL_BLOCK // CMP_STRIDE
        offs, wts = _slc_offsets()
        p_slc = None
        for o, wt in zip(offs, wts):
            term = float(wt) * pg_ref[pl.ds(pad + int(o), nbp, stride=r1), :]
            p_slc = term if p_slc is None else p_slc + term
        cur = (past + lane % ds) // SEL_BLOCK
        jrow = lax.broadcasted_iota(jnp.int32, (nbp, LANES), 0)
        forced = (jrow == 0) | (jrow == cur) | (jrow == cur - 1)
        score = jnp.where((jrow > cur) | (jrow >= nb), -1.0, jnp.where(forced, 1e6, p_slc))
        selg = _rank_select(score, score_ref, cur, n_sel)
        unfold = jnp.where((li == lj // ngrp) & (lj < n_lanes), 1.0, 0.0)
        sel_ref[...] = _dot(selg, unfold)
        wk = kw_ref.shape[0]
        nn = wnew_ref.shape[1]
        kw_ref[...] = jnp.zeros(kw_ref.shape, F32)
        for c0 in range(0, keep, LANES):
            kw_ref[c0:c0 + LANES, :] = _dot_nt(eye, win_ref[0, 0:kvw, c0:c0 + LANES].astype(BF16))
        kw_ref[keep:keep + nn, :] = wnew_ref[0, :, 0:kvw]
        vn_ref[...] = jnp.zeros(vn_ref.shape, F32)
        vn_ref[0:nn, :] = wnew_ref[0, :, kvw:2 * kvw]
        s = _dot(kw_ref[...].astype(BF16), qbd) * scale + bw_ref[...]
        irow = lax.broadcasted_iota(jnp.int32, (wk, LANES), 0)
        dist = jnp.where(irow < keep, keep + qi - irow, qi - (irow - keep))
        ok = (dist >= 0) & (dist < WINDOW) & (irow < keep + ds)
        p, _, _, l = _softmax_update(s, ok, jnp.full((1, LANES), NEG, F32), jnp.zeros((1, LANES), F32))
        p = (p * _inv_pos(l)).astype(BF16)
        o_w = (_dot(win_ref[0, kvw:2 * kvw, :].astype(BF16), p[0:keep, :])
               + _dot(vn_ref[...].T.astype(BF16), p[keep:keep + PAGE_SIZE, :]))
        g = g_ref[0]
        base_ref[...] = g[0:1, :] * o_c + g[2:3, :] * o_w
        m_ref[...] = jnp.full(m_ref.shape, NEG, F32)
        l_ref[...] = jnp.zeros(l_ref.shape, F32)
        acc_ref[...] = jnp.zeros(acc_ref.shape, F32)
        kn_ref[...] = jnp.zeros(kn_ref.shape, F32)
        vn_ref[...] = jnp.zeros(vn_ref.shape, F32)
        kn_ref[0:nn, :] = rnew_ref[0, :, 2 * kvw:3 * kvw]
        vn_ref[0:nn, :] = rnew_ref[0, :, 3 * kvw:4 * kvw]

    ik = lax.broadcasted_iota(jnp.int32, (PAGE_SIZE, LANES), 0)
    blocks_per_page = PAGE_SIZE // SEL_BLOCK

    def page_update(k, v_t, page):
        s = _dot(k, qbd) * scale + bs_ref[page]
        rows = [jnp.broadcast_to(sel_ref[pl.ds(page * blocks_per_page + i, 1), :], (SEL_BLOCK, LANES))
                for i in range(blocks_per_page)]
        mask = (jnp.concatenate(rows, axis=0) > 0.5) & (page * PAGE_SIZE + ik <= q_pos)
        p, m_new, alpha, l_new = _softmax_update(s, mask, m_ref[...], l_ref[...])
        m_ref[...] = m_new
        l_ref[...] = l_new
        acc_ref[...] = alpha * acc_ref[...] + _dot(v_t, p.astype(BF16))

    for k in range(npg):
        blk = page_refs[k]
        k_rows = _dot_nt(eye, blk[0, 0:kvw, :].astype(BF16)).astype(BF16)
        page_update(k_rows, blk[0, kvw:2 * kvw, :].astype(BF16), j * npg + k)

    @pl.when(j == n_steps - 1)
    def _():
        page_update(kn_ref[...].astype(BF16), vn_ref[...].T.astype(BF16), n_pages)
        o_ref[0] = base_ref[...] + g_ref[0][1:2, :] * (acc_ref[...] * _inv_pos(l_ref[...]))


def _nsa_sample(p_n, cache_kv, cache_win, page_table, W, db, ds, nkv, ngrp):
    nsa_w = nkv * ngrp * HEAD_DIM
    kvw = nkv * HEAD_DIM
    row_w = 4 * kvw
    qn, rows2d, win2d, gates = _nsa_project_call(p_n, W['qk_norm_g'], nsa_w, kvw, db * ds)
    rows_new = rows2d.reshape(db, ds, 4, nkv, HEAD_DIM)
    win_new = win2d.reshape(db, ds, 2, nkv, HEAD_DIM)
    n_pool = cache_kv.shape[0]
    n_pages = page_table.shape[1]
    past = n_pages * PAGE_SIZE
    keep = cache_win.shape[1]
    tot = past + _rup(ds, SEL_BLOCK)
    ns = tot // CMP_STRIDE
    nc = ns - CMP_LEN // CMP_STRIDE + 1
    nb = tot // SEL_BLOCK
    n_sel = min(SEL_TOPK, nb)
    nsp = _rup(ns, LANES)
    nbp = _rup(nb, 8)
    n_lanes = nkv * ds * ngrp
    assert n_lanes <= LANES and ds <= 8
    w_big = _cmp_first_weights(W['cmp_w1'], nkv)
    cache_pages = cache_kv.transpose(0, 2, 3, 4, 1).reshape(n_pool, row_w, PAGE_SIZE)
    npg = min(16, n_pages)
    parts_past = _cmp_parts_sample(cache_pages, page_table, w_big, kvw, npg)
    rows_pad = jnp.pad(rows2d.reshape(db, ds, row_w), ((0, 0), (0, PAGE_SIZE - ds), (0, 0)))
    parts_new = _cmp_parts_prompt(rows_pad, w_big, db, PAGE_SIZE, kvw)
    n_new = _rup(ds, SEL_BLOCK) // CMP_STRIDE
    parts = jnp.concatenate([parts_past, parts_new[:, :n_new], jnp.zeros((db, nsp - ns, row_w), F32)], axis=1)
    kvc = _cmp_finish(parts, W['cmp_b1'], W['cmp_w2'], W['cmp_b2'], W['qk_norm_g'][1], nc, nkv)
    kc_cat = kvc[:, :nkv].transpose(0, 2, 1, 3).reshape(db, nsp, kvw)
    vcT_cat = kvc[:, nkv:].transpose(0, 1, 3, 2).reshape(db, kvw, nsp)
    lane_pad = LANES - n_lanes
    q5 = qn.reshape(db, ds, nkv, ngrp, HEAD_DIM)
    qbd = jnp.einsum('bqgrd,gk->bgdkqr', q5, jnp.eye(nkv, dtype=F32)).reshape(db, kvw, n_lanes)
    qbd = jnp.pad(qbd, ((0, 0), (0, 0), (0, lane_pad))).astype(BF16)
    ng = 3 * nkv * ngrp
    gT = gates[:, :ng].reshape(db, ds, nkv, ngrp, 3).transpose(0, 4, 2, 1, 3).reshape(db, 3, n_lanes)
    gT = jnp.pad(gT, ((0, 0), (0, 5), (0, lane_pad)))
    lane = np.arange(LANES)
    live = lane < n_lanes
    head_of_lane = np.where(live, (lane // (ds * ngrp)) * ngrp + lane % ngrp, 0)
    qi = np.where(live, (lane // ngrp) % ds, 0)
    table_l = W['rel_bias'].astype(F32)[:, head_of_lane]
    bias_of = lambda dist: _rel_table_per_lane(dist, table_l)
    c_end = np.arange(nsp) * CMP_STRIDE + CMP_LEN - 1
    bias_c = bias_of(jnp.asarray(past + qi[None, :] - c_end[:, None], jnp.int32))
    pos = np.arange((n_pages + 1) * PAGE_SIZE)
    bias_s = bias_of(jnp.asarray(past + qi[None, :] - pos[:, None], jnp.int32)).reshape(n_pages + 1, PAGE_SIZE, LANES)
    wk = _rup(keep + 8, LANES)
    irow = np.arange(wk)[:, None]
    dist_w = np.where(irow < keep, keep + qi[None, :] - irow, qi[None, :] - (irow - keep))
    bias_w = bias_of(jnp.asarray(dist_w, jnp.int32))
    win_c = cache_win.transpose(0, 2, 3, 4, 1).reshape(db, 2 * kvw, keep)
    wnew8 = jnp.pad(win2d.reshape(db, ds, 2 * kvw), ((0, 0), (0, 8 - ds), (0, 0)))
    rnew8 = jnp.pad(rows2d.reshape(db, ds, row_w), ((0, 0), (0, 8 - ds), (0, 0)))
    n_steps = n_pages // npg
    per_b = lambda shape: pl.BlockSpec((1,) + shape, lambda i, j, pt: (i,) + (0,) * len(shape))
    const = lambda shape: pl.BlockSpec(shape, lambda i, j, pt: (0,) * len(shape))
    in_specs = [per_b((kvw, LANES)), per_b((8, LANES)), per_b((nsp, kvw)), per_b((kvw, nsp)), const((nsp, LANES)),
                per_b((2 * kvw, keep)), per_b((8, 2 * kvw)), const((wk, LANES)), per_b((8, row_w)),
                const((n_pages + 1, PAGE_SIZE, LANES))]
    in_specs += [pl.BlockSpec((1, 2 * kvw, PAGE_SIZE), (lambda i, j, pt, k=k: (pt[i, j * npg + k], 1, 0)))
                 for k in range(npg)]
    yT = pl.pallas_call(
        functools.partial(_nsa_sample_kernel, npg=npg, n_steps=n_steps, nc=nc, nb=nb, n_sel=n_sel, past=past,
                          ds=ds, keep=keep, ngrp=ngrp, n_lanes=n_lanes),
        grid_spec=pltpu.PrefetchScalarGridSpec(
            num_scalar_prefetch=1, grid=(db, n_steps), in_specs=in_specs,
            out_specs=pl.BlockSpec((1, kvw, LANES), lambda i, j, pt: (i, 0, 0)),
            scratch_shapes=[pltpu.VMEM((1, LANES), F32), pltpu.VMEM((1, LANES), F32), pltpu.VMEM((kvw, LANES), F32),
                            pltpu.VMEM((kvw, LANES), F32), pltpu.VMEM((nbp, LANES), F32),
                            pltpu.VMEM((nsp + 16, LANES), F32), pltpu.VMEM((nbp, LANES), F32),
                            pltpu.VMEM((wk, kvw), F32),
                            pltpu.VMEM((PAGE_SIZE, kvw), F32), pltpu.VMEM((PAGE_SIZE, kvw), F32)]),
        out_shape=jax.ShapeDtypeStruct((db, kvw, LANES), F32),
        compiler_params=_cparams(("arbitrary", "arbitrary")),
        name="nsa_sample_attn",
    )(page_table, qbd, gT, kc_cat, vcT_cat, bias_c, win_c, wnew8, bias_w, rnew8, bias_s, *([cache_pages] * npg))
    y6 = yT[:, :, :n_lanes].reshape(db, nkv, HEAD_DIM, nkv, ds, ngrp)
    y = jnp.einsum('bgdgqr->bqgrd', y6).reshape(db * ds, nsa_w)
    win_all = jnp.concatenate([cache_win, win_new.astype(cache_win.dtype)], axis=1)
    n_keep = min(WINDOW, past + ds)
    return y, rows_new, win_all[:, win_all.shape[1] - n_keep:]


def _rms(x, g):
    return x * lax.rsqrt(jnp.mean(x * x, axis=-1, keepdims=True) + RMS_EPS) * g


def _masked_softmax(logits, mask):
    z = jnp.where(mask, logits, NEG)
    return jax.nn.softmax(z, axis=-1) * jnp.any(mask, axis=-1, keepdims=True)


def _rel_bucket(dist):
    d = jnp.maximum(dist, 0)
    exact = REL_BUCKETS // 2
    ratio = jnp.maximum(d, exact).astype(F32) / exact
    large = exact + (jnp.log(ratio) / math.log(REL_MAX_DIST / exact) * (REL_BUCKETS - exact)).astype(jnp.int32)
    return jnp.where(d < exact, d, jnp.minimum(large, REL_BUCKETS - 1))


def _compress(k, w1, b1, w2, b2):
    b, l, g, dk = k.shape
    ns = l // CMP_STRIDE
    r2 = CMP_LEN // CMP_STRIDE
    nc = ns - r2 + 1
    kb = k[:, :ns * CMP_STRIDE].reshape(b, ns, CMP_STRIDE, g, dk)
    w1r = w1.reshape(r2, CMP_STRIDE, dk, -1)
    part = jnp.einsum('bnsgd,hsde->hbnge', kb, w1r)
    hid = part[0, :, :nc] + b1
    for h in range(1, r2):
        hid = hid + part[h, :, h:h + nc]
    return jnp.einsum('bnge,ed->bngd', jax.nn.gelu(hid), w2) + b2


def _to_blocks(k):
    b, l, g, dk = k.shape
    nb = l // SEL_BLOCK
    return k.reshape(b, nb, SEL_BLOCK, g, dk).transpose(0, 3, 1, 2, 4).reshape(b, g, nb, SEL_BLOCK * dk)


def _slc_offsets():
    r1 = SEL_BLOCK // CMP_STRIDE
    r2 = CMP_LEN // CMP_STRIDE
    offs = np.arange(-(r2 - 1), r1)
    wts = np.array([sum(1 for m in range(r1) for n in range(r2) if m - n == o) for o in offs], np.float32)
    return offs, wts


def _nsa_project(p_n, qk_norm_g, nkv, ngrp):
    b, t, _ = p_n.shape
    nsa_w = nkv * ngrp * HEAD_DIM
    kvw = nkv * HEAD_DIM
    q = _rms(p_n[..., :nsa_w].reshape(b, t, nkv, ngrp, HEAD_DIM), qk_norm_g[0])
    kv = p_n[..., nsa_w:nsa_w + 6 * kvw].reshape(b, t, 6, nkv, HEAD_DIM)
    gates = jax.nn.sigmoid(p_n[..., nsa_w + 6 * kvw:nsa_w + 6 * kvw + 3 * nkv * ngrp]).reshape(b, t, nkv, ngrp, 3)
    rows = jnp.stack([kv[:, :, 0], kv[:, :, 1], _rms(kv[:, :, 2], qk_norm_g[2]), kv[:, :, 3]], axis=2)
    win = jnp.stack([_rms(kv[:, :, 4], qk_norm_g[3]), kv[:, :, 5]], axis=2)
    return q, gates, rows, win


def _nsa_context(rows, W):
    kc = _rms(_compress(rows[:, :, 0], W['cmp_w1'][0], W['cmp_b1'][0], W['cmp_w2'][0], W['cmp_b2'][0]), W['qk_norm_g'][1])
    vc = _compress(rows[:, :, 1], W['cmp_w1'][1], W['cmp_b1'][1], W['cmp_w2'][1], W['cmp_b2'][1])
    c_end = jnp.arange(kc.shape[1]) * CMP_STRIDE + CMP_LEN - 1
    return kc, vc, c_end, _to_blocks(rows[:, :, 2]), _to_blocks(rows[:, :, 3])


def _nsa_core(q, gates, q_pos, kc, vc, c_end, ks_blk, vs_blk, kw, vw, kw_pos, rel_bias):
    b, tq, g, r, dk = q.shape
    nb = ks_blk.shape[2]
    nc = kc.shape[1]
    scale = HEAD_DIM ** -0.5
    table = rel_bias.astype(F32).reshape(REL_BUCKETS, g, r)
    dist_c = q_pos[:, None] - c_end[None, :]
    bias_c = table[_rel_bucket(dist_c)].transpose(2, 3, 0, 1)
    lg_c = jnp.einsum('bqgrd,bcgd->bgrqc', q, kc) * scale + bias_c
    p_c = _masked_softmax(lg_c, dist_c >= 0)
    o_c = jnp.einsum('bgrqc,bcgd->bqgrd', p_c, vc)
    offs, wts = _slc_offsets()
    jb = jnp.arange(nb)
    cidx = (SEL_BLOCK // CMP_STRIDE) * jb[:, None] + offs[None, :]
    cval = (cidx >= 0) & (cidx < nc)
    p_grp = jnp.sum(p_c, axis=2)
    p_slc = jnp.sum(p_grp[..., jnp.clip(cidx, 0, nc - 1)] * (wts * cval), axis=-1)
    cur = q_pos // SEL_BLOCK
    forced = (jb[None] == 0) | (jb[None] == cur[:, None]) | (jb[None] == cur[:, None] - 1)
    future = jb[None] > cur[:, None]
    score = jnp.where(future, -1.0, jnp.where(forced, 1e6, p_slc))
    n_sel = min(SEL_TOPK, nb)
    _, sel = lax.top_k(score, n_sel)
    sel_ok = sel <= cur[:, None]
    bi = jnp.arange(b)[:, None, None]
    gi = jnp.arange(g)[None, :, None]
    flat = sel.reshape(b, g, tq * n_sel)
    ksg = ks_blk[bi, gi, flat].reshape(b, g, tq, n_sel * SEL_BLOCK, dk)
    vsg = vs_blk[bi, gi, flat].reshape(b, g, tq, n_sel * SEL_BLOCK, dk)
    pos4 = sel[..., None] * SEL_BLOCK + jnp.arange(SEL_BLOCK)
    ok_s = (sel_ok[..., None] & (pos4 <= q_pos[:, None, None])).reshape(b, g, tq, n_sel * SEL_BLOCK)
    pos_s = pos4.reshape(b, g, tq, n_sel * SEL_BLOCK)
    tg = table.transpose(1, 0, 2)
    bias_s = tg[jnp.arange(g)[None, :, None, None], _rel_bucket(q_pos[:, None] - pos_s)].transpose(0, 1, 4, 2, 3)
    lg_s = jnp.einsum('bqgrd,bgqkd->bgrqk', q, ksg) * scale + bias_s
    p_s = _masked_softmax(lg_s, ok_s[:, :, None])
    o_s = jnp.einsum('bgrqk,bgqkd->bqgrd', p_s, vsg)
    dist_w = q_pos[:, None] - kw_pos[None, :]
    ok_w = (dist_w >= 0) & (dist_w < WINDOW) & (kw_pos[None, :] >= 0)
    bias_w = table[_rel_bucket(dist_w)].transpose(2, 3, 0, 1)
    lg_w = jnp.einsum('bqgrd,bkgd->bgrqk', q, kw) * scale + bias_w
    p_w = _masked_softmax(lg_w, ok_w)
    o_w = jnp.einsum('bgrqk,bkgd->bqgrd', p_w, vw)
    o = gates[..., 0:1] * o_c + gates[..., 1:2] * o_s + gates[..., 2:3] * o_w
    return o.reshape(b, tq, g * r * dk)


def _nsa_prompt_jnp(p_n, W, nkv, ngrp):
    q, gates, rows, win = _nsa_project(p_n, W['qk_norm_g'], nkv, ngrp)
    b, s = q.shape[:2]
    kc, vc, c_end, ks_blk, vs_blk = _nsa_context(rows, W)
    win_pad = jnp.pad(win, ((0, 0), (WINDOW, 0), (0, 0), (0, 0), (0, 0)))
    qb_sz = 128

    def block(i):
        start = i * qb_sz
        qb = lax.dynamic_slice_in_dim(q, start, qb_sz, axis=1)
        gb = lax.dynamic_slice_in_dim(gates, start, qb_sz, axis=1)
        wb = lax.dynamic_slice_in_dim(win_pad, start, WINDOW + qb_sz, axis=1)
        q_pos = start + jnp.arange(qb_sz)
        kw_pos = start - WINDOW + jnp.arange(WINDOW + qb_sz)
        return _nsa_core(qb, gb, q_pos, kc, vc, c_end, ks_blk, vs_blk, wb[:, :, 0], wb[:, :, 1], kw_pos, W['rel_bias'])

    o = lax.map(block, jnp.arange(s // qb_sz))
    o = o.transpose(1, 0, 2, 3).reshape(b, s, -1)
    return o, rows, win[:, s - min(WINDOW, s):]


def _nsa_sample_jnp(p_n, cache_kv, cache_win, page_table, W, nkv, ngrp, db):
    p_n = p_n.reshape(db, p_n.shape[0] // db, -1)
    q, gates, rows_new, win_new = _nsa_project(p_n, W['qk_norm_g'], nkv, ngrp)
    db, ds = q.shape[:2]
    past_len = page_table.shape[1] * PAGE_SIZE
    past_rows = cache_kv[page_table].reshape(db, past_len, 4, nkv, HEAD_DIM)
    pad = (-ds) % SEL_BLOCK
    rows = jnp.concatenate([past_rows, jnp.pad(rows_new, ((0, 0), (0, pad), (0, 0), (0, 0), (0, 0)))], axis=1)
    kc, vc, c_end, ks_blk, vs_blk = _nsa_context(rows, W)
    keep = cache_win.shape[1]
    win_all = jnp.concatenate([cache_win, win_new], axis=1)
    q_pos = past_len + jnp.arange(ds)
    kw_pos = past_len - keep + jnp.arange(keep + ds)
    o = _nsa_core(q, gates, q_pos, kc, vc, c_end, ks_blk, vs_blk, win_all[:, :, 0], win_all[:, :, 1], kw_pos, W['rel_bias'])
    n_keep = min(WINDOW, past_len + ds)
    return o.reshape(db * ds, -1), rows_new, win_all[:, win_all.shape[1] - n_keep:]


def _top_values(x, k):
    n = x.shape[0]
    row = lax.broadcasted_iota(jnp.int32, x.shape, 0)
    vals = []
    for _ in range(k):
        m = jnp.max(x, axis=0, keepdims=True)
        vals.append(m)
        first = jnp.min(jnp.where(x == m, row, n), axis=0, keepdims=True)
        x = jnp.where(row == first, -jnp.inf, x)
    return vals


def _peer_route_kernel(hT_ref, wq_ref, sk_ref, s1_ref, c1_ref, s2_ref, e2_ref, tau_ref, *, nheads, topk):
    nk, half = sk_ref.shape[1], sk_ref.shape[2]
    qT = _dot(wq_ref[...], hT_ref[...])
    for h in range(nheads):
        base = h * 2 * half
        s1 = _dot(sk_ref[0].astype(BF16), qT[base:base + half].astype(BF16))
        s2 = _dot(sk_ref[1].astype(BF16), qT[base + half:base + 2 * half].astype(BF16))
        v1 = _top_values(s1, topk)
        v2 = _top_values(s2, topk)
        cand = jnp.concatenate([v1[a] + v2[b] for a in range(topk) for b in range(topk) if (a + 1) * (b + 1) <= topk],
                               axis=0)
        tau = _top_values(cand, topk)[-1]
        e1 = jnp.exp(s1 - v1[0])
        e2 = jnp.exp(s2 - v2[0])
        z = jnp.sum(jnp.where(cand >= tau, jnp.exp(cand - (v1[0] + v2[0])), 0.0), axis=0, keepdims=True)
        c1 = e1 / z
        for c in range(s1_ref.shape[1]):
            cs = slice(c * LANES, (c + 1) * LANES)
            s1_ref[h, c] = s1[:, cs]
            s2_ref[h, c] = s2[:, cs]
            c1_ref[h, c] = c1[:, cs]
            e2_ref[h, c] = e2[:, cs]
            tau_ref[h, c] = tau[:, cs]


def _peer_expert_kernel(hT_ref, x1_ref, ga_ref, s1_ref, c1_ref, s2_ref, e2_ref, tau_ref, u_ref, v_ref,
                        o_ref, acc_ref, gate_ref, *, nheads, n_eblocks):
    eb = pl.program_id(1)
    n_chunks, nk = s2_ref.shape[1], s2_ref.shape[2]
    rows_per_block = u_ref.shape[0] // nk

    @pl.when(eb == 0)
    def _():
        acc_ref[...] = jnp.zeros(acc_ref.shape, F32)

    def gate_tile(it, carry):
        i = it // n_chunks
        c = it % n_chunks
        i1 = eb * rows_per_block + i
        wd = None
        for h in range(nheads):
            cand = s1_ref[h, c, pl.ds(i1, 1), :] + s2_ref[h, c]
            term = jnp.where(cand >= tau_ref[h, c], e2_ref[h, c], 0.0) * c1_ref[h, c, pl.ds(i1, 1), :]
            wd = term if wd is None else wd + term
        gate_ref[i, c] = wd
        return carry

    lax.fori_loop(0, rows_per_block * n_chunks, gate_tile, 0)
    act = jax.nn.gelu(_dot(u_ref[...], hT_ref[...]))
    gates = jnp.concatenate([jnp.concatenate([gate_ref[i, c] for c in range(n_chunks)], axis=1)
                             for i in range(rows_per_block)], axis=0)
    acc_ref[...] += _dot((gates * act).T.astype(BF16), v_ref[...])

    @pl.when(eb == n_eblocks - 1)
    def _():
        o_ref[...] = x1_ref[...] + ga_ref[0] * acc_ref[...]


def _peer(h2, x1, ga, W, tm):
    t, d = h2.shape
    sub_keys = W['peer_sub_keys']
    nk, half = sub_keys.shape[1], sub_keys.shape[2]
    qd = W['peer_w_query'].shape[1]
    nheads = qd // (2 * half)
    hT = h2.T
    wqT = W['peer_w_query'].T.astype(BF16)
    cpt = tm // LANES
    route_shape = jax.ShapeDtypeStruct((nheads, t // LANES, nk, LANES), F32)
    rspec = pl.BlockSpec((nheads, cpt, nk, LANES), lambda i: (0, i, 0, 0))
    s1, c1, s2, e2, tau = pl.pallas_call(
        functools.partial(_peer_route_kernel, nheads=nheads, topk=PEER_TOPK),
        grid=(t // tm,),
        in_specs=[pl.BlockSpec((d, tm), lambda i: (0, i)),
                  pl.BlockSpec((qd, d), lambda i: (0, 0)),
                  pl.BlockSpec(sub_keys.shape, lambda i: (0, 0, 0))],
        out_specs=[rspec, rspec, rspec, rspec, pl.BlockSpec((nheads, cpt, 1, LANES), lambda i: (0, i, 0, 0))],
        out_shape=[route_shape] * 4 + [jax.ShapeDtypeStruct((nheads, t // LANES, 1, LANES), F32)],
        compiler_params=_cparams(("arbitrary",)),
        name="peer_route",
    )(hT, wqT, sub_keys)
    eblk = 2 * MXU
    n_eblocks = W['peer_u'].shape[0] // eblk
    nmod, rows, _ = ga.shape
    tiles_per_mod = (t // tm) // nmod
    rspec2 = pl.BlockSpec((nheads, cpt, nk, LANES), lambda i, e: (0, i, 0, 0))
    return pl.pallas_call(
        functools.partial(_peer_expert_kernel, nheads=nheads, n_eblocks=n_eblocks),
        grid=(t // tm, n_eblocks),
        in_specs=[pl.BlockSpec((d, tm), lambda i, e: (0, i)),
                  pl.BlockSpec((tm, d), lambda i, e: (i, 0)),
                  pl.BlockSpec((1, rows, d), lambda i, e: (i // tiles_per_mod, 0, 0)),
                  rspec2, rspec2, rspec2, rspec2,
                  pl.BlockSpec((nheads, cpt, 1, LANES), lambda i, e: (0, i, 0, 0)),
                  pl.BlockSpec((eblk, d), lambda i, e: (e, 0)),
                  pl.BlockSpec((eblk, d), lambda i, e: (e, 0))],
        out_specs=pl.BlockSpec((tm, d), lambda i, e: (i, 0)),
        out_shape=jax.ShapeDtypeStruct((t, d), F32),
        scratch_shapes=[pltpu.VMEM((tm, d), F32), pltpu.VMEM((eblk // nk, cpt, nk, LANES), F32)],
        compiler_params=_cparams(("arbitrary", "arbitrary")),
        name="peer_experts",
    )(hT, x1, ga, s1, c1, s2, e2, tau, W['peer_u_bf16'], W['peer_v_bf16'])


def _peer_jnp(h, W):
    n, d = h.shape
    sub_keys, expert_u, expert_v = W['peer_sub_keys'], W['peer_u'], W['peer_v']
    nk, half = sub_keys.shape[1], sub_keys.shape[2]
    nheads = W['peer_w_query'].shape[1] // (2 * half)
    cs = min(128, n)
    hc = h.reshape(-1, cs, d)

    def chunk(x):
        qh = (x @ W['peer_w_query']).reshape(cs, nheads, 2, half)
        s1 = jnp.einsum('chd,kd->chk', qh[:, :, 0], sub_keys[0])
        s2 = jnp.einsum('chd,kd->chk', qh[:, :, 1], sub_keys[1])
        v1, i1 = lax.top_k(s1, PEER_TOPK)
        v2, i2 = lax.top_k(s2, PEER_TOPK)
        cand = (v1[..., :, None] + v2[..., None, :]).reshape(cs, nheads, -1)
        cidx = (i1[..., :, None] * nk + i2[..., None, :]).reshape(cs, nheads, -1)
        top, pos = lax.top_k(cand, PEER_TOPK)
        eidx = jnp.take_along_axis(cidx, pos, axis=-1)
        gw = jax.nn.softmax(top, axis=-1)
        act = jax.nn.gelu(jnp.einsum('chkd,cd->chk', expert_u[eidx], x))
        return jnp.einsum('chk,chkd->cd', gw * act, expert_v[eidx])

    return lax.map(chunk, hc).reshape(-1, d)


def _layer(x, mods, nsa_fn, shift_prev, wkv0, W, tm, rwkv_chunk):
    b, t, d = x.shape
    cw = W['rwkv_w0'].shape[0]
    nw, na, ng = W['rwkv_w_up'].shape[0], W['rwkv_a_up'].shape[0], W['rwkv_g_up'].shape[0]
    rwkv_proj = 3 * cw + nw + na + ng
    sh1, sc1, ga1, sh2, sc2, ga2 = mods
    xf = x.reshape(b * t, d)
    if (b * t) % tm == 0 and t % tm == 0:
        as_mod = lambda m: m.reshape(b, 1, d)
    else:
        tm = b * t
        as_mod = lambda m: jnp.repeat(m, t, axis=0).reshape(1, b * t, d)
    w_r = _rwkv_pad_cols(W['w_in'][:, :rwkv_proj], cw, nw, na, ng).astype(BF16)
    nsa_cols = W['w_in'].shape[1] - rwkv_proj
    w_n = jnp.pad(W['w_in'][:, rwkv_proj:], ((0, 0), (0, _rup(nsa_cols, LANES) - nsa_cols))).astype(BF16)
    p_r, _ = _norm_mod_matmul(xf, W['norm1_g'], as_mod(sc1), as_mod(sh1), w_r, tm, 512)
    p_n, _ = _norm_mod_matmul(xf, W['norm1_g'], as_mod(sc1), as_mod(sh1), w_n, tm, w_n.shape[1] // 3)
    pr = p_r.shape[1]
    p_r = p_r.reshape(b, t, pr)
    shift_new = _rwkv_unpad_cols(p_r[:, -1], cw, nw, na, ng)
    tpad = _rup(t, rwkv_chunk)
    p_r_pad = jnp.pad(p_r, ((0, 0), (0, tpad - t), (0, 0)))
    y_r, wkv_new = _rwkv_mix(p_r_pad, shift_prev, wkv0, W, rwkv_chunk, t)
    y_r = y_r[:, :t].reshape(b * t, cw)
    y_n, rows, win = nsa_fn(p_n)
    w_out = W['w_out'].astype(BF16)
    x1 = _out_proj(xf, y_r, y_n, as_mod(ga1), w_out[:cw], w_out[cw:], tm, 512)
    h2 = _norm_mod(x1, W['norm2_g'], as_mod(sc2), as_mod(sh2), tm)
    if (b * t) % LANES == 0:
        out = _peer(h2, x1, as_mod(ga2), W, tm)
    else:
        out = x1 + jnp.repeat(ga2, t, axis=0) * _peer_jnp(h2.astype(F32), W)
    return out.reshape(b, t, d), rows, win, wkv_new, shift_new


def kernel(x_prompt, x_sample, c_prompt, c_sample, cache_kv, cache_win, state_wkv, state_shift, page_table,
           norm1_g, norm2_g, w_ada, b_ada, w_in, w_out,
           rwkv_mu, rwkv_w0, rwkv_w_up, rwkv_a0, rwkv_a_up, rwkv_g_up, rwkv_k_k, rwkv_k_a, rwkv_r_k, lnx_w, lnx_b,
           qk_norm_g, cmp_w1, cmp_b1, cmp_w2, cmp_b2, rel_bias,
           peer_w_query, peer_sub_keys, peer_u, peer_v):
    W = dict(norm1_g=norm1_g, norm2_g=norm2_g, w_ada=w_ada, b_ada=b_ada, w_in=w_in, w_out=w_out,
             rwkv_mu=rwkv_mu, rwkv_w0=rwkv_w0, rwkv_w_up=rwkv_w_up, rwkv_a0=rwkv_a0, rwkv_a_up=rwkv_a_up,
             rwkv_g_up=rwkv_g_up, rwkv_k_k=rwkv_k_k, rwkv_k_a=rwkv_k_a, rwkv_r_k=rwkv_r_k, lnx_w=lnx_w, lnx_b=lnx_b,
             qk_norm_g=qk_norm_g, cmp_w1=cmp_w1, cmp_b1=cmp_b1, cmp_w2=cmp_w2, cmp_b2=cmp_b2, rel_bias=rel_bias,
             peer_w_query=peer_w_query, peer_sub_keys=peer_sub_keys, peer_u=peer_u, peer_v=peer_v)
    W['peer_u_bf16'] = peer_u.astype(BF16)
    W['peer_v_bf16'] = peer_v.astype(BF16)
    bp, seq, d = x_prompt.shape
    db = x_sample.shape[0]
    nkv = cache_kv.shape[3]
    nh_r = rwkv_w0.shape[0] // HEAD_DIM
    ngrp = (w_out.shape[0] - rwkv_w0.shape[0]) // HEAD_DIM // nkv

    mods = _ada_mods(jnp.concatenate([c_prompt, c_sample], axis=0), w_ada, b_ada)
    mods = mods.reshape(bp + db, N_MODS, d)
    mods_p = [mods[:bp, i] for i in range(N_MODS)]
    mods_s = [mods[bp:, i] for i in range(N_MODS)]

    shift0 = jnp.zeros((bp, state_shift.shape[1]), F32)
    wkv0 = jnp.zeros((bp, nh_r, HEAD_DIM, HEAD_DIM), F32)
    y_p, rows_p, win_p, wkv_p, shift_p = _layer(
        x_prompt, mods_p, lambda pn: _nsa_prompt(pn, W, bp, seq, nkv, ngrp, 512), shift0, wkv0, W, 512, 64)
    y_s, rows_s, win_s, wkv_s, shift_s = _layer(
        x_sample, mods_s,
        lambda pn: _nsa_sample(pn, cache_kv, cache_win, page_table, W, db, x_sample.shape[1], nkv, ngrp),
        state_shift, state_wkv, W, 512, 32)
    return (y_p, y_s, rows_p, win_p, wkv_p.astype(state_wkv.dtype), shift_p,
            rows_s, win_s, wkv_s.astype(state_wkv.dtype), shift_s)
```

```python
import functools
import math

import numpy as np
import jax
import jax.numpy as jnp
from jax import lax
from jax.experimental import pallas as pl
from jax.experimental.pallas import tpu as pltpu

F32 = jnp.float32
BF16 = jnp.bfloat16
HI = lax.Precision.HIGHEST

HEAD_DIM = 64
PAGE_SIZE = 128
CMP_LEN = 32
CMP_STRIDE = 16
SEL_BLOCK = 64
SEL_TOPK = 16
WINDOW = 512
REL_BUCKETS = 32
REL_MAX_DIST = 2048
PEER_TOPK = 16
N_MODS = 6
RMS_EPS = 1e-6
LNX_EPS = 64e-5
NEG = -1e30

LANES = 128
MXU = 256
HEADS_PER_GROUP = MXU // HEAD_DIM
VMEM_LIMIT = 56 * 1024 * 1024


def _cparams(sem):
    return pltpu.CompilerParams(dimension_semantics=sem, vmem_limit_bytes=VMEM_LIMIT)


def _dot(a, b, precision=None):
    return jnp.dot(a, b, preferred_element_type=F32, precision=precision)


def _dot_nt(a, b, precision=None):
    return lax.dot_general(a, b, (((1,), (1,)), ((), ())), preferred_element_type=F32, precision=precision)


def _split_bf16(x, parts):
    out = []
    for _ in range(parts):
        h = x.astype(BF16)
        out.append(h)
        x = x - h.astype(F32)
    return out


def _mm(a, b, mode, nt=False):
    f = _dot_nt if nt else _dot
    if mode == 6:
        return f(a, b, HI)
    if mode == 1:
        return f(a.astype(BF16), b.astype(BF16))
    if mode == 3:
        ah, al = _split_bf16(a, 2)
        bh, bl = _split_bf16(b, 2)
        return (f(al, bh) + f(ah, bl)) + f(ah, bh)
    if mode[0] == 'L':
        terms = [f(t, b.astype(BF16)) for t in _split_bf16(a, int(mode[1]))]
    else:
        terms = [f(a.astype(BF16), t) for t in _split_bf16(b, int(mode[1]))]
    out = terms[-1]
    for t in terms[-2::-1]:
        out = out + t
    return out


RWKV_MM = dict(lora=1, headsum='L2', cumsum='R3', gram=1, inverse=1, state=1)


def _ada_kernel(c_ref, w_ref, b_ref, o_ref):
    c = c_ref[...]
    s = c * jax.nn.sigmoid(c)
    o_ref[...] = _dot(s.astype(BF16), w_ref[...].astype(BF16)) + b_ref[...]


def _ada_mods(c, w_ada, b_ada):
    n, d = c.shape
    cols = w_ada.shape[1]
    tn = 1024
    return pl.pallas_call(
        _ada_kernel,
        grid=(cols // tn,),
        in_specs=[pl.BlockSpec((n, d), lambda j: (0, 0)),
                  pl.BlockSpec((d, tn), lambda j: (0, j)),
                  pl.BlockSpec((1, tn), lambda j: (0, j))],
        out_specs=pl.BlockSpec((n, tn), lambda j: (0, j)),
        out_shape=jax.ShapeDtypeStruct((n, cols), F32),
        compiler_params=_cparams(("arbitrary",)),
        name="ada_mods",
    )(c, w_ada, b_ada.reshape(1, cols))


def _nmm_kernel(x_ref, g_ref, sc_ref, sh_ref, w_ref, o_ref, h_ref):
    @pl.when(pl.program_id(1) == 0)
    def _():
        x = x_ref[...]
        ms = jnp.mean(x * x, axis=-1, keepdims=True)
        y = x * lax.rsqrt(ms + RMS_EPS) * g_ref[...]
        h_ref[...] = (y * (1.0 + sc_ref[0]) + sh_ref[0]).astype(h_ref.dtype)

    o_ref[...] = _dot(h_ref[...], w_ref[...])


def _norm_mod_matmul(x, g, sc, sh, w, tm, tn):
    t, d = x.shape
    n = w.shape[1]
    nmod, rows, _ = sc.shape
    tiles_per_mod = (t // tm) // nmod
    mod_spec = pl.BlockSpec((1, rows, d), lambda i, j: (i // tiles_per_mod, 0, 0))
    return pl.pallas_call(
        _nmm_kernel,
        grid=(t // tm, n // tn),
        in_specs=[pl.BlockSpec((tm, d), lambda i, j: (i, 0)),
                  pl.BlockSpec((1, d), lambda i, j: (0, 0)),
                  mod_spec, mod_spec,
                  pl.BlockSpec((d, tn), lambda i, j: (0, j))],
        out_specs=[pl.BlockSpec((tm, tn), lambda i, j: (i, j)),
                   pl.BlockSpec((tm, d), lambda i, j: (i, 0))],
        out_shape=[jax.ShapeDtypeStruct((t, n), F32), jax.ShapeDtypeStruct((t, d), BF16)],
        compiler_params=_cparams(("arbitrary", "arbitrary")),
        name="norm_mod_matmul",
    )(x, g.reshape(1, d), sc, sh, w)


def _nm_kernel(x_ref, g_ref, sc_ref, sh_ref, h_ref):
    x = x_ref[...]
    ms = jnp.mean(x * x, axis=-1, keepdims=True)
    y = x * lax.rsqrt(ms + RMS_EPS) * g_ref[...]
    h_ref[...] = (y * (1.0 + sc_ref[0]) + sh_ref[0]).astype(h_ref.dtype)


def _norm_mod(x, g, sc, sh, tm):
    t, d = x.shape
    nmod, rows, _ = sc.shape
    tiles_per_mod = (t // tm) // nmod
    mod_spec = pl.BlockSpec((1, rows, d), lambda i: (i // tiles_per_mod, 0, 0))
    return pl.pallas_call(
        _nm_kernel,
        grid=(t // tm,),
        in_specs=[pl.BlockSpec((tm, d), lambda i: (i, 0)), pl.BlockSpec((1, d), lambda i: (0, 0)), mod_spec, mod_spec],
        out_specs=pl.BlockSpec((tm, d), lambda i: (i, 0)),
        out_shape=jax.ShapeDtypeStruct((t, d), BF16),
        compiler_params=_cparams(("arbitrary",)),
        name="norm_mod",
    )(x, g.reshape(1, d), sc, sh)


def _outproj_kernel(x_ref, yr_ref, yn_ref, ga_ref, w1_ref, w2_ref, o_ref):
    acc = _dot(yr_ref[...].astype(BF16), w1_ref[...]) + _dot(yn_ref[...].astype(BF16), w2_ref[...])
    o_ref[...] = x_ref[...] + ga_ref[0] * acc


def _out_proj(x, y_r, y_n, ga, w1, w2, tm, tn):
    t, d = x.shape
    nmod, rows, _ = ga.shape
    tiles_per_mod = (t // tm) // nmod
    cr, cn = y_r.shape[1], y_n.shape[1]
    return pl.pallas_call(
        _outproj_kernel,
        grid=(t // tm, d // tn),
        in_specs=[pl.BlockSpec((tm, tn), lambda i, j: (i, j)),
                  pl.BlockSpec((tm, cr), lambda i, j: (i, 0)),
                  pl.BlockSpec((tm, cn), lambda i, j: (i, 0)),
                  pl.BlockSpec((1, rows, tn), lambda i, j: (i // tiles_per_mod, 0, j)),
                  pl.BlockSpec((cr, tn), lambda i, j: (0, j)),
                  pl.BlockSpec((cn, tn), lambda i, j: (0, j))],
        out_specs=pl.BlockSpec((tm, tn), lambda i, j: (i, j)),
        out_shape=jax.ShapeDtypeStruct((t, d), F32),
        compiler_params=_cparams(("arbitrary", "arbitrary")),
        name="out_proj",
    )(x, y_r, y_n, ga, w1, w2)


def _softplus(z):
    return jnp.maximum(z, 0.0) + jnp.log(1.0 + jnp.exp(-jnp.abs(z)))


def _rwkv_kernel(p_ref, shift_ref, s0_ref, mu_ref, vec_ref, wup_ref, aup_ref, gup_ref,
                 y_ref, sfin_ref, carry, state, *, t_valid, n_chunks):
    c = pl.program_id(1)
    chunk = p_ref.shape[1]
    cw = vec_ref.shape[1]
    n_groups = cw // MXU
    hg = HEADS_PER_GROUP
    rows_g = hg * chunk

    @pl.when(c == 0)
    def _():
        carry[...] = shift_ref[0]
        state[...] = s0_ref[0]

    p = p_ref[0]
    row = lax.broadcasted_iota(jnp.int32, (chunk, 1), 0)
    prev = jnp.where(row == 0, carry[...], pltpu.roll(p, 1, axis=0))
    carry[...] = p[chunk - 1:chunk, :]
    xs = p + (prev - p) * mu_ref[...]

    w0, a0, k_k, k_a, r_k, lnx_w, lnx_b = (vec_ref[i:i + 1, :] for i in range(7))
    r = xs[:, 0:cw]
    k = xs[:, cw:2 * cw]
    v = xs[:, 2 * cw:3 * cw]
    o = 3 * cw
    nw, na, ng = wup_ref.shape[0], aup_ref.shape[0], gup_ref.shape[0]
    xw = xs[:, o:o + nw]
    xa = xs[:, o + nw:o + nw + na]
    xg = xs[:, o + nw + na:o + nw + na + ng]
    pm = RWKV_MM
    w_log = -_softplus(-(w0 + _mm(jnp.tanh(xw), wup_ref[...], pm['lora']))) - 0.5
    a = jax.nn.sigmoid(a0 + _mm(xa, aup_ref[...], pm['lora']))
    gate = _mm(jax.nn.sigmoid(xg), gup_ref[...], pm['lora'])

    gi = lax.broadcasted_iota(jnp.int32, (MXU, MXU), 0) // HEAD_DIM
    gj = lax.broadcasted_iota(jnp.int32, (MXU, MXU), 1) // HEAD_DIM
    ones_bd = (gi == gj).astype(F32)

    def head_sum(x):
        return jnp.concatenate([_mm(x[:, g * MXU:(g + 1) * MXU], ones_bd, pm['headsum']) for g in range(n_groups)],
                               axis=1)

    kk = k * k_k
    kk = kk / jnp.maximum(jnp.sqrt(head_sum(kk * kk)), 1e-12)
    k2 = k * (1.0 + (a - 1.0) * k_a)
    logdec = -jnp.exp(w_log)
    if t_valid < chunk * n_chunks:
        valid = (row + c * chunk) < t_valid
        logdec = jnp.where(valid, logdec, 0.0)
        kk = jnp.where(valid, kk, 0.0)
        k2 = jnp.where(valid, k2, 0.0)
        v = jnp.where(valid, v, 0.0)

    ti = lax.broadcasted_iota(jnp.int32, (chunk, chunk), 0)
    tj = lax.broadcasted_iota(jnp.int32, (chunk, chunk), 1)
    cum = _mm((tj <= ti).astype(F32), logdec, pm['cumsum'])
    cum_end = cum[chunk - 1:chunk, :]
    e_neg = jnp.exp(-cum)
    e_rem = jnp.exp(cum_end - cum)
    r_t = r * jnp.exp(cum)
    a_t = -kk * jnp.exp(cum - logdec)
    b_vec = kk * a
    b_t = b_vec * e_neg
    k_t = k2 * e_neg
    b_rem = b_vec * e_rem
    k_rem = k2 * e_rem
    w_end = jnp.exp(cum_end)

    lane_head = lax.broadcasted_iota(jnp.int32, (chunk, MXU), 1) // HEAD_DIM
    ri = lax.broadcasted_iota(jnp.int32, (rows_g, rows_g), 0)
    rj = lax.broadcasted_iota(jnp.int32, (rows_g, rows_g), 1)
    strict = rj < ri
    incl = rj <= ri
    eye_r = (ri == rj).astype(F32)
    di = lax.broadcasted_iota(jnp.int32, (MXU, MXU), 0)
    dj = lax.broadcasted_iota(jnp.int32, (MXU, MXU), 1)
    n_double = max(int(math.ceil(math.log2(chunk))) - 1, 0)

    def bd(x):
        return jnp.concatenate([jnp.where(lane_head == h, x, 0.0) for h in range(hg)], axis=0)

    def stack(x):
        return jnp.concatenate([x[:, h * HEAD_DIM:(h + 1) * HEAD_DIM] for h in range(hg)], axis=0)

    def unstack(x):
        return jnp.concatenate([x[h * chunk:(h + 1) * chunk, :] for h in range(hg)], axis=1)

    ys = []
    for g in range(n_groups):
        sl = slice(g * MXU, (g + 1) * MXU)
        a_bd, r_bd = bd(a_t[:, sl]), bd(r_t[:, sl])
        b_bd, k_bd = bd(b_t[:, sl]), bd(k_t[:, sl])
        v_st = stack(v[:, sl])
        a_ab = jnp.where(strict, _mm(a_bd, b_bd, pm['gram'], nt=True), 0.0)
        a_ak = jnp.where(strict, _mm(a_bd, k_bd, pm['gram'], nt=True), 0.0)
        a_rb = jnp.where(incl, _mm(r_bd, b_bd, pm['gram'], nt=True), 0.0)
        a_rk = jnp.where(incl, _mm(r_bd, k_bd, pm['gram'], nt=True), 0.0)
        tinv = eye_r + a_ab
        pw = a_ab
        for _ in range(n_double):
            pw = _mm(pw, pw, pm['inverse'])
            tinv = tinv + _mm(pw, tinv, pm['inverse'])
        s0 = state[g]
        z = _mm(a_bd, s0, pm['state']) + _mm(a_ak, v_st, pm['state'])
        u = _mm(tinv, z, pm['state'])
        y_st = _mm(r_bd, s0, pm['state']) + _mm(a_rb, u, pm['state']) + _mm(a_rk, v_st, pm['state'])
        w_col = jnp.sum(jnp.where(di == dj, jnp.broadcast_to(w_end[:, sl], (MXU, MXU)), 0.0), axis=1, keepdims=True)
        state[g] = (w_col * s0 + _mm(bd(b_rem[:, sl]).T, u, pm['state'])
                    + _mm(bd(k_rem[:, sl]).T, v_st, pm['state']))
        ys.append(unstack(y_st))
    y = jnp.concatenate(ys, axis=1)

    inv_n = 1.0 / HEAD_DIM
    mean = head_sum(y) * inv_n
    d = y - mean
    var = head_sum(d * d) * inv_n
    yn = d * lax.rsqrt(var + LNX_EPS) * lnx_w + lnx_b
    bonus = head_sum(r * k2 * r_k) * v
    y_ref[0] = ((yn + bonus) * gate).astype(y_ref.dtype)

    @pl.when(c == n_chunks - 1)
    def _():
        sfin_ref[0] = state[...]


def _rwkv_pad_cols(x, cw, nw, na, ng):
    o = 3 * cw
    parts = [x[..., :o + nw], x[..., o + nw:o + nw + na], x[..., o + nw + na:]]
    widths = [o + _rup(nw, LANES), _rup(na, LANES), _rup(ng, LANES)]
    out = []
    for part, wd in zip(parts, widths):
        pad = [(0, 0)] * (x.ndim - 1) + [(0, wd - part.shape[-1])]
        out.append(jnp.pad(part, pad))
    return jnp.concatenate(out, axis=-1)


def _rwkv_unpad_cols(x, cw, nw, na, ng):
    o = 3 * cw
    o2 = o + _rup(nw, LANES)
    o3 = o2 + _rup(na, LANES)
    return jnp.concatenate([x[..., :o + nw], x[..., o2:o2 + na], x[..., o3:o3 + ng]], axis=-1)


def _rup(x, m):
    return (x + m - 1) // m * m


def _rwkv_mix(p_r, shift_prev, wkv0, W, chunk, t_valid):
    b, tpad, pr = p_r.shape
    cw = W['rwkv_w0'].shape[0]
    nh = cw // HEAD_DIM
    n_groups = cw // MXU
    nw, na, ng = W['rwkv_w_up'].shape[0], W['rwkv_a_up'].shape[0], W['rwkv_g_up'].shape[0]
    n_chunks = tpad // chunk
    mu = _rwkv_pad_cols(W['rwkv_mu'], cw, nw, na, ng).reshape(1, pr)
    vecs = jnp.stack([W['rwkv_w0'], W['rwkv_a0'], W['rwkv_k_k'], W['rwkv_k_a'], W['rwkv_r_k'].reshape(cw),
                      W['lnx_w'], W['lnx_b'], jnp.zeros((cw,), F32)])
    wup = jnp.pad(W['rwkv_w_up'], ((0, _rup(nw, LANES) - nw), (0, 0)))
    aup = jnp.pad(W['rwkv_a_up'], ((0, _rup(na, LANES) - na), (0, 0)))
    gup = jnp.pad(W['rwkv_g_up'], ((0, _rup(ng, LANES) - ng), (0, 0)))
    shift3 = _rwkv_pad_cols(shift_prev, cw, nw, na, ng).reshape(b, 1, pr)
    s0 = wkv0.astype(F32).transpose(0, 1, 3, 2).reshape(b, n_groups, MXU, HEAD_DIM)
    const = lambda shape: pl.BlockSpec(shape, lambda i, c: (0,) * len(shape))
    y, sfin = pl.pallas_call(
        functools.partial(_rwkv_kernel, t_valid=t_valid, n_chunks=n_chunks),
        grid=(b, n_chunks),
        in_specs=[pl.BlockSpec((1, chunk, pr), lambda i, c: (i, c, 0)),
                  pl.BlockSpec((1, 1, pr), lambda i, c: (i, 0, 0)),
                  pl.BlockSpec((1, n_groups, MXU, HEAD_DIM), lambda i, c: (i, 0, 0, 0)),
                  const((1, pr)), const((8, cw)), const(wup.shape), const(aup.shape), const(gup.shape)],
        out_specs=[pl.BlockSpec((1, chunk, cw), lambda i, c: (i, c, 0)),
                   pl.BlockSpec((1, n_groups, MXU, HEAD_DIM), lambda i, c: (i, 0, 0, 0))],
        out_shape=[jax.ShapeDtypeStruct((b, tpad, cw), F32),
                   jax.ShapeDtypeStruct((b, n_groups, MXU, HEAD_DIM), F32)],
        scratch_shapes=[pltpu.VMEM((1, pr), F32), pltpu.VMEM((n_groups, MXU, HEAD_DIM), F32)],
        compiler_params=_cparams(("arbitrary", "arbitrary")),
        name="rwkv_mix",
    )(p_r, shift3, s0, mu, vecs, wup, aup, gup)
    s_fin = sfin.reshape(b, nh, HEAD_DIM, HEAD_DIM).transpose(0, 1, 3, 2)
    return y, s_fin


def _head_ones():
    gi = lax.broadcasted_iota(jnp.int32, (MXU, MXU), 0) // HEAD_DIM
    gj = lax.broadcasted_iota(jnp.int32, (MXU, MXU), 1) // HEAD_DIM
    return (gi == gj).astype(F32)


def _nsa_proj_kernel(p_ref, g_ref, q_ref, rows_ref, win_ref, gate_ref, *, nsa_w, kvw):
    ones_bd = _head_ones()

    def hnorm(x, gvec):
        ms = _dot(x * x, ones_bd, HI) * (1.0 / HEAD_DIM)
        return x * lax.rsqrt(ms + RMS_EPS) * gvec

    for i in range(nsa_w // MXU):
        sl = slice(i * MXU, (i + 1) * MXU)
        q_ref[:, sl] = hnorm(p_ref[:, sl], g_ref[0:1, :])
    o = nsa_w
    rows_ref[:, 0:2 * kvw] = p_ref[:, o:o + 2 * kvw]
    rows_ref[:, 2 * kvw:3 * kvw] = hnorm(p_ref[:, o + 2 * kvw:o + 3 * kvw], g_ref[2:3, :])
    rows_ref[:, 3 * kvw:4 * kvw] = p_ref[:, o + 3 * kvw:o + 4 * kvw]
    win_ref[:, 0:kvw] = hnorm(p_ref[:, o + 4 * kvw:o + 5 * kvw], g_ref[3:4, :])
    win_ref[:, kvw:2 * kvw] = p_ref[:, o + 5 * kvw:o + 6 * kvw]
    gate_ref[...] = jax.nn.sigmoid(p_ref[:, o + 6 * kvw:])


def _nsa_project_call(p_n, qk_norm_g, nsa_w, kvw, tm):
    t, pc = p_n.shape
    assert kvw == MXU and nsa_w % MXU == 0
    gcols = pc - nsa_w - 6 * kvw
    gvec = jnp.tile(qk_norm_g, (1, MXU // HEAD_DIM))
    return pl.pallas_call(
        functools.partial(_nsa_proj_kernel, nsa_w=nsa_w, kvw=kvw),
        grid=(t // tm,),
        in_specs=[pl.BlockSpec((tm, pc), lambda i: (i, 0)),
                  pl.BlockSpec(gvec.shape, lambda i: (0, 0))],
        out_specs=[pl.BlockSpec((tm, nsa_w), lambda i: (i, 0)),
                   pl.BlockSpec((tm, 4 * kvw), lambda i: (i, 0)),
                   pl.BlockSpec((tm, 2 * kvw), lambda i: (i, 0)),
                   pl.BlockSpec((tm, gcols), lambda i: (i, 0))],
        out_shape=[jax.ShapeDtypeStruct((t, nsa_w), F32), jax.ShapeDtypeStruct((t, 4 * kvw), F32),
                   jax.ShapeDtypeStruct((t, 2 * kvw), F32), jax.ShapeDtypeStruct((t, gcols), F32)],
        compiler_params=_cparams(("arbitrary",)),
        name="nsa_project",
    )(p_n, gvec)


def _cmp_part_kernel(*refs, n_in, row_w, kvw, n_prefetch=0):
    refs = refs[n_prefetch:]
    x_refs, w_ref, o_ref = refs[:n_in], refs[n_in], refs[n_in + 1]
    for typ in range(2):
        acc = None
        for s in range(CMP_STRIDE):
            lo = s * row_w + typ * kvw
            xs = jnp.concatenate([x[0, :, lo:lo + kvw] for x in x_refs], axis=0) if n_in > 1 else x_refs[0][0, :, lo:lo + kvw]
            d = _dot(xs.astype(BF16), w_ref[typ, s])
            acc = d if acc is None else acc + d
        o_ref[0, :, typ * 2 * kvw:(typ + 1) * 2 * kvw] = acc


def _cmp_first_weights(cmp_w1, nkv):
    r2 = CMP_LEN // CMP_STRIDE
    e = cmp_w1.shape[-1]
    w1r = cmp_w1.reshape(2, r2, CMP_STRIDE, HEAD_DIM, e)
    eye = jnp.eye(nkv, dtype=F32)
    big = jnp.einsum('yhsde,gk->ysgdkhe', w1r, eye)
    return big.reshape(2, CMP_STRIDE, nkv * HEAD_DIM, nkv * r2 * e).astype(BF16)


def _cmp_parts_prompt(rows2d, w_big, b, t, kvw):
    row_w = rows2d.shape[2]
    nsub = t // CMP_STRIDE
    blk = min(nsub, LANES)
    return pl.pallas_call(
        functools.partial(_cmp_part_kernel, n_in=1, row_w=row_w, kvw=kvw),
        grid=(b, nsub // blk),
        in_specs=[pl.BlockSpec((1, blk, CMP_STRIDE * row_w), lambda i, j: (i, j, 0)),
                  pl.BlockSpec(w_big.shape, lambda i, j: (0, 0, 0, 0))],
        out_specs=pl.BlockSpec((1, blk, 4 * kvw), lambda i, j: (i, j, 0)),
        out_shape=jax.ShapeDtypeStruct((b, nsub, 4 * kvw), F32),
        compiler_params=_cparams(("arbitrary", "arbitrary")),
        name="cmp_parts",
    )(rows2d.reshape(b, nsub, CMP_STRIDE * row_w), w_big)


def _cmp_finish_kernel(*refs, n_parts, nc, nkv):
    p_refs = refs[:n_parts]
    b1_ref, w2_ref, b2_ref, g_ref, o_ref = refs[n_parts:]
    tg = pl.program_id(1)
    ns = o_ref.shape[2]
    e = p_refs[0].shape[2] // 2
    pieces = [p[0] for p in p_refs]
    have = sum(p.shape[0] for p in pieces)
    if have < ns:
        pieces.append(jnp.zeros((ns - have, 2 * e), F32))
    part = jnp.concatenate(pieces, axis=0) if len(pieces) > 1 else pieces[0]
    nxt = pltpu.roll(part[:, e:], ns - 1, axis=0)
    hid = part[:, :e] + nxt + b1_ref[0]
    out = _dot(jax.nn.gelu(hid).astype(BF16), w2_ref[0].astype(BF16)) + b2_ref[0]
    normed = out * lax.rsqrt(jnp.mean(out * out, axis=-1, keepdims=True) + RMS_EPS) * g_ref[...]
    out = jnp.where(tg < nkv, normed, out)
    row = lax.broadcasted_iota(jnp.int32, (ns, 1), 0)
    o_ref[0, 0] = jnp.where(row < nc, out, 0.0)


def _cmp_finish(parts_list, ns, cmp_b1, cmp_w2, cmp_b2, g1, nc, nkv):
    b = parts_list[0].shape[0]
    e = cmp_b1.shape[1]
    return pl.pallas_call(
        functools.partial(_cmp_finish_kernel, n_parts=len(parts_list), nc=nc, nkv=nkv),
        grid=(b, 2 * nkv),
        in_specs=[pl.BlockSpec((1, p.shape[1], 2 * e), lambda i, j: (i, 0, j)) for p in parts_list] + [
                  pl.BlockSpec((1, 1, e), lambda i, j: (j // nkv, 0, 0)),
                  pl.BlockSpec((1, e, HEAD_DIM), lambda i, j: (j // nkv, 0, 0)),
                  pl.BlockSpec((1, 1, HEAD_DIM), lambda i, j: (j // nkv, 0, 0)),
                  pl.BlockSpec((1, HEAD_DIM), lambda i, j: (0, 0))],
        out_specs=pl.BlockSpec((1, 1, ns, HEAD_DIM), lambda i, j: (i, j, 0, 0)),
        out_shape=jax.ShapeDtypeStruct((b, 2 * nkv, ns, HEAD_DIM), F32),
        compiler_params=_cparams(("arbitrary", "arbitrary")),
        name="cmp_finish",
    )(*parts_list, cmp_b1.reshape(2, 1, e), cmp_w2, cmp_b2.reshape(2, 1, HEAD_DIM), g1.reshape(1, HEAD_DIM))


QT = 128
QK_SCALE = HEAD_DIM ** -0.5
assert math.log2(HEAD_DIM) % 2 == 0, "QK_SCALE must be a power of two to be folded into q exactly"


def _rel_table_np_dist(dist, table):
    onehot = (_rel_bucket(dist)[..., None] == jnp.arange(REL_BUCKETS)).astype(F32)
    return jnp.einsum('...b,bh->...h', onehot, table, precision=HI)


def _rel_table_per_lane(dist, table_l):
    bucket = _rel_bucket(dist)
    out = jnp.zeros(dist.shape, F32)
    for b in range(REL_BUCKETS):
        out = out + jnp.where(bucket == b, table_l[b][None, :], 0.0)
    return out


def _softmax_update(s, mask, m, l):
    m_new = jnp.maximum(m, jnp.max(jnp.where(mask, s, NEG), axis=0, keepdims=True))
    alpha = jnp.exp(m - m_new)
    p = jnp.where(mask, jnp.exp(s - m_new), 0.0)
    return p, m_new, alpha, alpha * l + jnp.sum(p, axis=0, keepdims=True)


def _rank_select(score, score_ref, cur, n_sel):
    nb = score.shape[0]
    score_ref[0:nb, :] = score
    jrow = lax.broadcasted_iota(jnp.int32, score.shape, 0)

    def body(j, rank):
        other = score_ref[pl.ds(j, 1), :]
        beats = (other > score) | ((other == score) & (jrow > j))
        return rank + jnp.where(beats, 1.0, 0.0)

    rank = lax.fori_loop(0, nb, body, jnp.zeros(score.shape, F32), unroll=8)
    return jnp.where((rank < n_sel) & (jrow <= cur), 1.0, 0.0)


def _nsa_prompt_kernel(qT_ref, gT_ref, kc_ref, vcT_ref, ks_ref, vsT_ref, kw_ref, vwT_ref, bc_ref, toep_ref,
                       o_ref, pg_ref, score_ref, sel_ref, *, nc, nb, n_sel, ngrp):
    qt = pl.program_id(2)
    lanes = ngrp * QT
    q = (qT_ref[0, 0, 0] * QK_SCALE).astype(BF16)
    iq = lax.broadcasted_iota(jnp.int32, (1, QT), 1)
    q_pos = qt * QT + iq
    tile4 = lambda x: jnp.concatenate([x] * ngrp, axis=1)

    ncp = kc_ref.shape[2]
    s = _dot(kc_ref[0, 0].astype(BF16), q)
    s = s + jnp.concatenate([bc_ref[0, r] for r in range(ngrp)], axis=1)
    crow = lax.broadcasted_iota(jnp.int32, (ncp, QT), 0)
    ok_c = tile4((crow * CMP_STRIDE + (CMP_LEN - 1) <= q_pos) & (crow < nc))
    p, _, _, l = _softmax_update(s, ok_c, jnp.full((1, lanes), NEG, F32), jnp.zeros((1, lanes), F32))
    p = p * jnp.where(l > 0.0, 1.0 / jnp.where(l > 0.0, l, 1.0), 0.0)
    o_c = _dot(vcT_ref[0, 0].astype(BF16), p.astype(BF16))

    p_grp = p[:, 0:QT]
    for r in range(1, ngrp):
        p_grp = p_grp + p[:, r * QT:(r + 1) * QT]
    pad = 8
    pg_ref[...] = jnp.zeros(pg_ref.shape, F32)
    pg_ref[pad:pad + ncp, :] = p_grp
    r1 = SEL_BLOCK // CMP_STRIDE
    offs, wts = _slc_offsets()
    p_slc = None
    for o, wt in zip(offs, wts):
        term = float(wt) * pg_ref[pl.ds(pad + int(o), nb, stride=r1), :]
        p_slc = term if p_slc is None else p_slc + term
    jrow = lax.broadcasted_iota(jnp.int32, (nb, QT), 0)
    cur = q_pos // SEL_BLOCK
    forced = (jrow == 0) | (jrow == cur) | (jrow == cur - 1)
    score = jnp.where(jrow > cur, -1.0, jnp.where(forced, 1e6, p_slc))
    sel_ref[0:nb, :] = _rank_select(score, score_ref, cur, n_sel)

    ik = lax.broadcasted_iota(jnp.int32, (QT, QT), 0)
    iqq = lax.broadcasted_iota(jnp.int32, (QT, QT), 1)
    blocks_per_tile = QT // SEL_BLOCK

    def attend(kp, carry, k_ref, vT_ref, mask_fn):
        m, l, acc = carry
        s_all = _dot(k_ref[0, 0, kp], q)
        deltas = [qt - (2 * kp + i) for i in range(2)]
        mask = jnp.concatenate([mask_fn(2 * kp + i, deltas[i]) for i in range(2)], axis=0)
        m_out, l_out, alphas, ps = [], [], [], []
        for r in range(ngrp):
            ls = slice(r * QT, (r + 1) * QT)
            bias = jnp.concatenate([toep_ref[0, r, jnp.maximum(d, 0)] for d in deltas], axis=0)
            s = jnp.where(mask, s_all[:, ls] + bias, NEG)
            m_new = jnp.maximum(m[:, ls], jnp.max(s, axis=0, keepdims=True))
            alpha = jnp.exp(m[:, ls] - m_new)
            p = jnp.exp(s - m_new)
            m_out.append(m_new)
            l_out.append(alpha * l[:, ls] + jnp.sum(p, axis=0, keepdims=True))
            alphas.append(alpha)
            ps.append(p.astype(BF16))
        cat = lambda xs: jnp.concatenate(xs, axis=1)
        acc = cat(alphas) * acc + _dot(vT_ref[0, 0, kp], cat(ps))
        return cat(m_out), cat(l_out), acc

    def sel_mask(kt, delta):
        rows = [jnp.broadcast_to(sel_ref[pl.ds(kt * blocks_per_tile + i, 1), :], (SEL_BLOCK, QT))
                for i in range(blocks_per_tile)]
        chosen = jnp.concatenate(rows, axis=0) > 0.5
        return chosen & (ik - iqq <= delta * QT)

    def win_mask(kt, delta):
        dist = delta * QT + iqq - ik
        return (dist >= 0) & (dist < WINDOW)

    init = (jnp.full((1, lanes), NEG, F32), jnp.zeros((1, lanes), F32), jnp.zeros((HEAD_DIM, lanes), F32))
    finish = lambda c: c[2] * jnp.where(c[1] > 0.0, 1.0 / jnp.where(c[1] > 0.0, c[1], 1.0), 0.0)
    diag = qt // 2
    o_s = finish(lax.fori_loop(0, diag + 1, lambda kp, c: attend(kp, c, ks_ref, vsT_ref, sel_mask), init))
    first = jnp.maximum(qt - WINDOW // QT, 0) // 2
    o_w = finish(lax.fori_loop(0, diag - first + 1,
                               lambda i, c: attend(diag - i, c, kw_ref, vwT_ref, win_mask), init))
    g = gT_ref[0, 0, 0]
    o_ref[0, 0, 0] = g[0:1, :] * o_c + g[1:2, :] * o_s + g[2:3, :] * o_w


def _nsa_prompt(p_n, W, b, t, nkv, ngrp, tm):
    nsa_w = nkv * ngrp * HEAD_DIM
    kvw = nkv * HEAD_DIM
    qn, rows2d, win2d, gates = _nsa_project_call(p_n, W['qk_norm_g'], nsa_w, kvw, tm)
    rows = rows2d.reshape(b, t, 4, nkv, HEAD_DIM)
    win = win2d.reshape(b, t, 2, nkv, HEAD_DIM)
    ns = t // CMP_STRIDE
    nc = ns - CMP_LEN // CMP_STRIDE + 1
    nb = t // SEL_BLOCK
    n_sel = min(SEL_TOPK, nb)
    nqt = t // QT
    parts = _cmp_parts_prompt(rows2d.reshape(b, t, 4 * kvw), _cmp_first_weights(W['cmp_w1'], nkv), b, t, kvw)
    kvc = _cmp_finish([parts], ns, W['cmp_b1'], W['cmp_w2'], W['cmp_b2'], W['qk_norm_g'][1], nc, nkv)
    kc = kvc[:, :nkv]
    vcT = kvc[:, nkv:].transpose(0, 1, 3, 2)
    qT = qn.reshape(b, nqt, QT, nkv, ngrp, HEAD_DIM).transpose(0, 3, 1, 5, 4, 2).reshape(b, nkv, nqt, HEAD_DIM, ngrp * QT)
    ng = 3 * nkv * ngrp
    gT = gates[:, :ng].reshape(b, nqt, QT, nkv, ngrp, 3).transpose(0, 3, 1, 5, 4, 2).reshape(b, nkv, nqt, 3, ngrp * QT)
    gT = jnp.pad(gT, ((0, 0), (0, 0), (0, 0), (0, 5), (0, 0)))
    assert nqt % 2 == 0
    npair = nqt // 2
    k_tiles = lambda x: x.transpose(0, 2, 1, 3).reshape(b, nkv, npair, 2 * QT, HEAD_DIM).astype(BF16)
    vT_tiles = lambda x: x.reshape(b, npair, 2 * QT, nkv, HEAD_DIM).transpose(0, 3, 1, 4, 2).astype(BF16)
    ks, vsT = k_tiles(rows[:, :, 2]), vT_tiles(rows[:, :, 3])
    kw, vwT = k_tiles(win[:, :, 0]), vT_tiles(win[:, :, 1])
    table = W['rel_bias'].astype(F32)
    c_end = jnp.arange(ns) * CMP_STRIDE + CMP_LEN - 1
    bias_c = _rel_table_np_dist(jnp.arange(t)[None, :] - c_end[:, None], table)
    bias_c = bias_c.transpose(2, 0, 1).reshape(nkv, ngrp, ns, t)
    dd = (jnp.arange(nqt)[:, None, None] * QT + jnp.arange(QT)[None, None, :] - jnp.arange(QT)[None, :, None])
    toep = _rel_table_np_dist(dd, table).transpose(3, 0, 1, 2).reshape(nkv, ngrp, nqt, QT, QT)
    lanes = ngrp * QT
    kv_spec = lambda shape: pl.BlockSpec((1, 1) + shape, lambda i, g, j: (i, g) + (0,) * len(shape))
    yT = pl.pallas_call(
        functools.partial(_nsa_prompt_kernel, nc=nc, nb=nb, n_sel=n_sel, ngrp=ngrp),
        grid=(b, nkv, nqt),
        in_specs=[pl.BlockSpec((1, 1, 1, HEAD_DIM, lanes), lambda i, g, j: (i, g, j, 0, 0)),
                  pl.BlockSpec((1, 1, 1, 8, lanes), lambda i, g, j: (i, g, j, 0, 0)),
                  kv_spec((ns, HEAD_DIM)), kv_spec((HEAD_DIM, ns)),
                  kv_spec((npair, 2 * QT, HEAD_DIM)), kv_spec((npair, HEAD_DIM, 2 * QT)),
                  kv_spec((npair, 2 * QT, HEAD_DIM)), kv_spec((npair, HEAD_DIM, 2 * QT)),
                  pl.BlockSpec((1, ngrp, ns, QT), lambda i, g, j: (g, 0, 0, j)),
                  pl.BlockSpec((1, ngrp, nqt, QT, QT), lambda i, g, j: (g, 0, 0, 0, 0))],
        out_specs=pl.BlockSpec((1, 1, 1, HEAD_DIM, lanes), lambda i, g, j: (i, g, j, 0, 0)),
        out_shape=jax.ShapeDtypeStruct((b, nkv, nqt, HEAD_DIM, lanes), F32),
        scratch_shapes=[pltpu.VMEM((ns + 16, QT), F32), pltpu.VMEM((_rup(nb, 8), QT), F32),
                        pltpu.VMEM((_rup(nb, 8), QT), F32)],
        compiler_params=_cparams(("arbitrary", "arbitrary", "arbitrary")),
        name="nsa_prompt_attn",
    )(qT, gT, kc, vcT, ks, vsT, kw, vwT, bias_c, toep)
    y = yT.reshape(b, nkv, nqt, HEAD_DIM, ngrp, QT).transpose(0, 2, 5, 1, 4, 3).reshape(b * t, nsa_w)
    return y, rows, win[:, t - min(WINDOW, t):]


def _cmp_part_paged_kernel(*refs, npg, kvw):
    x_refs, w_ref, o_ref = refs[1:1 + npg], refs[1 + npg], refs[2 + npg]
    rows = x_refs[0].shape[2]
    sub = rows // CMP_STRIDE
    ri = lax.broadcasted_iota(jnp.int32, (rows, rows), 0)
    ci = lax.broadcasted_iota(jnp.int32, (rows, rows), 1)
    perm = jnp.where(ci == (ri % sub) * CMP_STRIDE + ri // sub, 1.0, 0.0).astype(BF16)
    xp = [_dot_nt(perm, x[0].astype(BF16)) for x in x_refs]
    for typ in range(2):
        acc = None
        for s in range(CMP_STRIDE):
            xs = jnp.concatenate([p[s * sub:(s + 1) * sub, typ * kvw:(typ + 1) * kvw] for p in xp], axis=0)
            d = _dot(xs.astype(BF16), w_ref[typ, s])
            acc = d if acc is None else acc + d
        o_ref[0, :, typ * 2 * kvw:(typ + 1) * 2 * kvw] = acc


def _cmp_parts_sample(cache_pages, page_table, w_big, kvw, npg):
    b, n_pages = page_table.shape
    rows = cache_pages.shape[2]
    sub = rows // CMP_STRIDE
    in_specs = [pl.BlockSpec((1, 2 * kvw, rows), (lambda i, j, pt, k=k: (pt[i, j * npg + k], 0, 0)))
                for k in range(npg)]
    in_specs.append(pl.BlockSpec(w_big.shape, lambda i, j, pt: (0, 0, 0, 0)))
    return pl.pallas_call(
        functools.partial(_cmp_part_paged_kernel, npg=npg, kvw=kvw),
        grid_spec=pltpu.PrefetchScalarGridSpec(
            num_scalar_prefetch=1, grid=(b, n_pages // npg), in_specs=in_specs,
            out_specs=pl.BlockSpec((1, npg * sub, 4 * kvw), lambda i, j, pt: (i, j, 0))),
        out_shape=jax.ShapeDtypeStruct((b, n_pages * sub, 4 * kvw), F32),
        compiler_params=_cparams(("arbitrary", "arbitrary")),
        name="cmp_parts_paged",
    )(page_table, *([cache_pages] * npg), w_big)


def _inv_pos(l):
    return jnp.where(l > 0.0, 1.0 / jnp.where(l > 0.0, l, 1.0), 0.0)


def _nsa_sample_kernel(*refs, npg, n_steps, nc, nb, n_sel, past, ds, keep, ngrp, n_lanes):
    pt_ref = refs[0]
    q_ref, g_ref, kc_ref, vcT_ref, bc_ref, win_ref, wnew_ref, bw_ref, rnew_ref, bs_ref = refs[1:11]
    page_refs = refs[11:11 + npg]
    o_ref = refs[11 + npg]
    (m_ref, l_ref, acc_ref, base_ref, sel_ref, pg_ref, score_ref,
     kw_ref, kn_ref, vn_ref) = refs[12 + npg:]
    del pt_ref
    j = pl.program_id(1)
    scale = HEAD_DIM ** -0.5
    kvw = q_ref.shape[1]
    eye = jnp.where(lax.broadcasted_iota(jnp.int32, (LANES, LANES), 0)
                    == lax.broadcasted_iota(jnp.int32, (LANES, LANES), 1), 1.0, 0.0).astype(BF16)
    qbd = q_ref[0]
    lane = lax.broadcasted_iota(jnp.int32, (1, LANES), 1)
    qi = (lane // ngrp) % ds
    q_pos = past + qi
    n_pages = npg * n_steps

    @pl.when(j == 0)
    def _():
        nsp = kc_ref.shape[1]
        s = _dot(kc_ref[0].astype(BF16), qbd) * scale + bc_ref[...]
        crow = lax.broadcasted_iota(jnp.int32, (nsp, LANES), 0)
        ok = (crow * CMP_STRIDE + (CMP_LEN - 1) <= q_pos) & (crow < nc)
        p, _, _, l = _softmax_update(s, ok, jnp.full((1, LANES), NEG, F32), jnp.zeros((1, LANES), F32))
        p = p * _inv_pos(l)
        o_c = _dot(vcT_ref[0].astype(BF16), p.astype(BF16))
        li = lax.broadcasted_iota(jnp.int32, (LANES, LANES), 0)
        lj = lax.broadcasted_iota(jnp.int32, (LANES, LANES), 1)
        fold = jnp.where((li // ngrp == lj) & (li < n_lanes), 1.0, 0.0)
        p_grp = _dot(p, fold, HI)
        pad = 8
        pg_ref[...] = jnp.zeros(pg_ref.shape, F32)
        pg_ref[pad:pad + nsp, :] = p_grp
        nbp = sel_ref.shape[0]
        r1 = SEL_BLOCK // CMP_STRIDE
        offs, wts = _slc_offsets()
        p_slc = None
        for o, wt in zip(offs, wts):
            term = float(wt) * pg_ref[pl.ds(pad + int(o), nbp, stride=r1), :]
            p_slc = term if p_slc is None else p_slc + term
        cur = (past + lane % ds) // SEL_BLOCK
        jrow = lax.broadcasted_iota(jnp.int32, (nbp, LANES), 0)
        forced = (jrow == 0) | (jrow == cur) | (jrow == cur - 1)
        score = jnp.where((jrow > cur) | (jrow >= nb), -1.0, jnp.where(forced, 1e6, p_slc))
        selg = _rank_select(score, score_ref, cur, n_sel)
        unfold = jnp.where((li == lj // ngrp) & (lj < n_lanes), 1.0, 0.0)
        sel_ref[...] = _dot(selg, unfold)
        wk = kw_ref.shape[0]
        nn = wnew_ref.shape[1]
        kw_ref[...] = jnp.zeros(kw_ref.shape, F32)
        for c0 in range(0, keep, LANES):
            kw_ref[c0:c0 + LANES, :] = _dot_nt(eye, win_ref[0, 0:kvw, c0:c0 + LANES].astype(BF16))
        kw_ref[keep:keep + nn, :] = wnew_ref[0, :, 0:kvw]
        vn_ref[...] = jnp.zeros(vn_ref.shape, F32)
        vn_ref[0:nn, :] = wnew_ref[0, :, kvw:2 * kvw]
        s = _dot(kw_ref[...].astype(BF16), qbd) * scale + bw_ref[...]
        irow = lax.broadcasted_iota(jnp.int32, (wk, LANES), 0)
        dist = jnp.where(irow < keep, keep + qi - irow, qi - (irow - keep))
        ok = (dist >= 0) & (dist < WINDOW) & (irow < keep + ds)
        p, _, _, l = _softmax_update(s, ok, jnp.full((1, LANES), NEG, F32), jnp.zeros((1, LANES), F32))
        p = (p * _inv_pos(l)).astype(BF16)
        o_w = (_dot(win_ref[0, kvw:2 * kvw, :].astype(BF16), p[0:keep, :])
               + _dot(vn_ref[...].T.astype(BF16), p[keep:keep + PAGE_SIZE, :]))
        g = g_ref[0]
        base_ref[...] = g[0:1, :] * o_c + g[2:3, :] * o_w
        m_ref[...] = jnp.full(m_ref.shape, NEG, F32)
        l_ref[...] = jnp.zeros(l_ref.shape, F32)
        acc_ref[...] = jnp.zeros(acc_ref.shape, F32)
        kn_ref[...] = jnp.zeros(kn_ref.shape, F32)
        vn_ref[...] = jnp.zeros(vn_ref.shape, F32)
        kn_ref[0:nn, :] = rnew_ref[0, :, 2 * kvw:3 * kvw]
        vn_ref[0:nn, :] = rnew_ref[0, :, 3 * kvw:4 * kvw]

    ik = lax.broadcasted_iota(jnp.int32, (PAGE_SIZE, LANES), 0)
    blocks_per_page = PAGE_SIZE // SEL_BLOCK

    def pages_update(k, v_t, first_page, count):
        bias = jnp.concatenate([bs_ref[first_page + i] for i in range(count)], axis=0)
        s = _dot(k, qbd) * scale + bias
        rows = [jnp.broadcast_to(sel_ref[pl.ds(first_page * blocks_per_page + i, 1), :], (SEL_BLOCK, LANES))
                for i in range(count * blocks_per_page)]
        key_pos = first_page * PAGE_SIZE + lax.broadcasted_iota(jnp.int32, (count * PAGE_SIZE, LANES), 0)
        mask = (jnp.concatenate(rows, axis=0) > 0.5) & (key_pos <= q_pos)
        p, m_new, alpha, l_new = _softmax_update(s, mask, m_ref[...], l_ref[...])
        m_ref[...] = m_new
        l_ref[...] = l_new
        acc_ref[...] = alpha * acc_ref[...] + _dot(v_t, p.astype(BF16))

    k_rows = jnp.concatenate([_dot_nt(eye, blk[0, 0:kvw, :].astype(BF16)) for blk in page_refs], axis=0)
    v_cols = jnp.concatenate([blk[0, kvw:2 * kvw, :].astype(BF16) for blk in page_refs], axis=1)
    pages_update(k_rows.astype(BF16), v_cols, j * npg, npg)

    @pl.when(j == n_steps - 1)
    def _():
        pages_update(kn_ref[...].astype(BF16), vn_ref[...].T.astype(BF16), n_pages, 1)
        o_ref[0] = base_ref[...] + g_ref[0][1:2, :] * (acc_ref[...] * _inv_pos(l_ref[...]))


def _nsa_sample(p_n, cache_kv, cache_win, page_table, W, db, ds, nkv, ngrp):
    nsa_w = nkv * ngrp * HEAD_DIM
    kvw = nkv * HEAD_DIM
    row_w = 4 * kvw
    qn, rows2d, win2d, gates = _nsa_project_call(p_n, W['qk_norm_g'], nsa_w, kvw, db * ds)
    rows_new = rows2d.reshape(db, ds, 4, nkv, HEAD_DIM)
    win_new = win2d.reshape(db, ds, 2, nkv, HEAD_DIM)
    n_pool = cache_kv.shape[0]
    n_pages = page_table.shape[1]
    past = n_pages * PAGE_SIZE
    keep = cache_win.shape[1]
    tot = past + _rup(ds, SEL_BLOCK)
    ns = tot // CMP_STRIDE
    nc = ns - CMP_LEN // CMP_STRIDE + 1
    nb = tot // SEL_BLOCK
    n_sel = min(SEL_TOPK, nb)
    nsp = _rup(ns, LANES)
    nbp = _rup(nb, 8)
    n_lanes = nkv * ds * ngrp
    assert n_lanes <= LANES and ds <= 8
    w_big = _cmp_first_weights(W['cmp_w1'], nkv)
    cache_pages = cache_kv.transpose(0, 2, 3, 4, 1).reshape(n_pool, row_w, PAGE_SIZE)
    npg = min(16, n_pages)
    parts_past = _cmp_parts_sample(cache_pages, page_table, w_big, kvw, npg)
    rows_pad = jnp.pad(rows2d.reshape(db, ds, row_w), ((0, 0), (0, PAGE_SIZE - ds), (0, 0)))
    parts_new = _cmp_parts_prompt(rows_pad, w_big, db, PAGE_SIZE, kvw)
    kvc = _cmp_finish([parts_past, parts_new], nsp, W['cmp_b1'], W['cmp_w2'], W['cmp_b2'], W['qk_norm_g'][1], nc, nkv)
    kc_cat = kvc[:, :nkv].transpose(0, 2, 1, 3).reshape(db, nsp, kvw)
    vcT_cat = kvc[:, nkv:].transpose(0, 1, 3, 2).reshape(db, kvw, nsp)
    lane_pad = LANES - n_lanes
    q5 = qn.reshape(db, ds, nkv, ngrp, HEAD_DIM)
    qbd = jnp.einsum('bqgrd,gk->bgdkqr', q5, jnp.eye(nkv, dtype=F32)).reshape(db, kvw, n_lanes)
    qbd = jnp.pad(qbd, ((0, 0), (0, 0), (0, lane_pad))).astype(BF16)
    ng = 3 * nkv * ngrp
    gT = gates[:, :ng].reshape(db, ds, nkv, ngrp, 3).transpose(0, 4, 2, 1, 3).reshape(db, 3, n_lanes)
    gT = jnp.pad(gT, ((0, 0), (0, 5), (0, lane_pad)))
    lane = np.arange(LANES)
    live = lane < n_lanes
    head_of_lane = np.where(live, (lane // (ds * ngrp)) * ngrp + lane % ngrp, 0)
    qi = np.where(live, (lane // ngrp) % ds, 0)
    table_l = W['rel_bias'].astype(F32)[:, head_of_lane]
    bias_of = lambda dist: _rel_table_per_lane(dist, table_l)
    c_end = np.arange(nsp) * CMP_STRIDE + CMP_LEN - 1
    bias_c = bias_of(jnp.asarray(past + qi[None, :] - c_end[:, None], jnp.int32))
    pos = np.arange((n_pages + 1) * PAGE_SIZE)
    bias_s = bias_of(jnp.asarray(past + qi[None, :] - pos[:, None], jnp.int32)).reshape(n_pages + 1, PAGE_SIZE, LANES)
    wk = _rup(keep + 8, LANES)
    irow = np.arange(wk)[:, None]
    dist_w = np.where(irow < keep, keep + qi[None, :] - irow, qi[None, :] - (irow - keep))
    bias_w = bias_of(jnp.asarray(dist_w, jnp.int32))
    win_c = cache_win.transpose(0, 2, 3, 4, 1).reshape(db, 2 * kvw, keep)
    wnew8 = jnp.pad(win2d.reshape(db, ds, 2 * kvw), ((0, 0), (0, 8 - ds), (0, 0)))
    rnew8 = jnp.pad(rows2d.reshape(db, ds, row_w), ((0, 0), (0, 8 - ds), (0, 0)))
    n_steps = n_pages // npg
    per_b = lambda shape: pl.BlockSpec((1,) + shape, lambda i, j, pt: (i,) + (0,) * len(shape))
    const = lambda shape: pl.BlockSpec(shape, lambda i, j, pt: (0,) * len(shape))
    in_specs = [per_b((kvw, LANES)), per_b((8, LANES)), per_b((nsp, kvw)), per_b((kvw, nsp)), const((nsp, LANES)),
                per_b((2 * kvw, keep)), per_b((8, 2 * kvw)), const((wk, LANES)), per_b((8, row_w)),
                const((n_pages + 1, PAGE_SIZE, LANES))]
    in_specs += [pl.BlockSpec((1, 2 * kvw, PAGE_SIZE), (lambda i, j, pt, k=k: (pt[i, j * npg + k], 1, 0)))
                 for k in range(npg)]
    yT = pl.pallas_call(
        functools.partial(_nsa_sample_kernel, npg=npg, n_steps=n_steps, nc=nc, nb=nb, n_sel=n_sel, past=past,
                          ds=ds, keep=keep, ngrp=ngrp, n_lanes=n_lanes),
        grid_spec=pltpu.PrefetchScalarGridSpec(
            num_scalar_prefetch=1, grid=(db, n_steps), in_specs=in_specs,
            out_specs=pl.BlockSpec((1, kvw, LANES), lambda i, j, pt: (i, 0, 0)),
            scratch_shapes=[pltpu.VMEM((1, LANES), F32), pltpu.VMEM((1, LANES), F32), pltpu.VMEM((kvw, LANES), F32),
                            pltpu.VMEM((kvw, LANES), F32), pltpu.VMEM((nbp, LANES), F32),
                            pltpu.VMEM((nsp + 16, LANES), F32), pltpu.VMEM((nbp, LANES), F32),
                            pltpu.VMEM((wk, kvw), F32),
                            pltpu.VMEM((PAGE_SIZE, kvw), F32), pltpu.VMEM((PAGE_SIZE, kvw), F32)]),
        out_shape=jax.ShapeDtypeStruct((db, kvw, LANES), F32),
        compiler_params=_cparams(("arbitrary", "arbitrary")),
        name="nsa_sample_attn",
    )(page_table, qbd, gT, kc_cat, vcT_cat, bias_c, win_c, wnew8, bias_w, rnew8, bias_s, *([cache_pages] * npg))
    y6 = yT[:, :, :n_lanes].reshape(db, nkv, HEAD_DIM, nkv, ds, ngrp)
    y = jnp.einsum('bgdgqr->bqgrd', y6).reshape(db * ds, nsa_w)
    win_all = jnp.concatenate([cache_win, win_new.astype(cache_win.dtype)], axis=1)
    n_keep = min(WINDOW, past + ds)
    return y, rows_new, win_all[:, win_all.shape[1] - n_keep:]


def _rms(x, g):
    return x * lax.rsqrt(jnp.mean(x * x, axis=-1, keepdims=True) + RMS_EPS) * g


def _masked_softmax(logits, mask):
    z = jnp.where(mask, logits, NEG)
    return jax.nn.softmax(z, axis=-1) * jnp.any(mask, axis=-1, keepdims=True)


def _rel_bucket(dist):
    d = jnp.maximum(dist, 0)
    exact = REL_BUCKETS // 2
    ratio = jnp.maximum(d, exact).astype(F32) / exact
    large = exact + (jnp.log(ratio) / math.log(REL_MAX_DIST / exact) * (REL_BUCKETS - exact)).astype(jnp.int32)
    return jnp.where(d < exact, d, jnp.minimum(large, REL_BUCKETS - 1))


def _compress(k, w1, b1, w2, b2):
    b, l, g, dk = k.shape
    ns = l // CMP_STRIDE
    r2 = CMP_LEN // CMP_STRIDE
    nc = ns - r2 + 1
    kb = k[:, :ns * CMP_STRIDE].reshape(b, ns, CMP_STRIDE, g, dk)
    w1r = w1.reshape(r2, CMP_STRIDE, dk, -1)
    part = jnp.einsum('bnsgd,hsde->hbnge', kb, w1r)
    hid = part[0, :, :nc] + b1
    for h in range(1, r2):
        hid = hid + part[h, :, h:h + nc]
    return jnp.einsum('bnge,ed->bngd', jax.nn.gelu(hid), w2) + b2


def _to_blocks(k):
    b, l, g, dk = k.shape
    nb = l // SEL_BLOCK
    return k.reshape(b, nb, SEL_BLOCK, g, dk).transpose(0, 3, 1, 2, 4).reshape(b, g, nb, SEL_BLOCK * dk)


def _slc_offsets():
    r1 = SEL_BLOCK // CMP_STRIDE
    r2 = CMP_LEN // CMP_STRIDE
    offs = np.arange(-(r2 - 1), r1)
    wts = np.array([sum(1 for m in range(r1) for n in range(r2) if m - n == o) for o in offs], np.float32)
    return offs, wts


def _nsa_project(p_n, qk_norm_g, nkv, ngrp):
    b, t, _ = p_n.shape
    nsa_w = nkv * ngrp * HEAD_DIM
    kvw = nkv * HEAD_DIM
    q = _rms(p_n[..., :nsa_w].reshape(b, t, nkv, ngrp, HEAD_DIM), qk_norm_g[0])
    kv = p_n[..., nsa_w:nsa_w + 6 * kvw].reshape(b, t, 6, nkv, HEAD_DIM)
    gates = jax.nn.sigmoid(p_n[..., nsa_w + 6 * kvw:nsa_w + 6 * kvw + 3 * nkv * ngrp]).reshape(b, t, nkv, ngrp, 3)
    rows = jnp.stack([kv[:, :, 0], kv[:, :, 1], _rms(kv[:, :, 2], qk_norm_g[2]), kv[:, :, 3]], axis=2)
    win = jnp.stack([_rms(kv[:, :, 4], qk_norm_g[3]), kv[:, :, 5]], axis=2)
    return q, gates, rows, win


def _nsa_context(rows, W):
    kc = _rms(_compress(rows[:, :, 0], W['cmp_w1'][0], W['cmp_b1'][0], W['cmp_w2'][0], W['cmp_b2'][0]), W['qk_norm_g'][1])
    vc = _compress(rows[:, :, 1], W['cmp_w1'][1], W['cmp_b1'][1], W['cmp_w2'][1], W['cmp_b2'][1])
    c_end = jnp.arange(kc.shape[1]) * CMP_STRIDE + CMP_LEN - 1
    return kc, vc, c_end, _to_blocks(rows[:, :, 2]), _to_blocks(rows[:, :, 3])


def _nsa_core(q, gates, q_pos, kc, vc, c_end, ks_blk, vs_blk, kw, vw, kw_pos, rel_bias):
    b, tq, g, r, dk = q.shape
    nb = ks_blk.shape[2]
    nc = kc.shape[1]
    scale = HEAD_DIM ** -0.5
    table = rel_bias.astype(F32).reshape(REL_BUCKETS, g, r)
    dist_c = q_pos[:, None] - c_end[None, :]
    bias_c = table[_rel_bucket(dist_c)].transpose(2, 3, 0, 1)
    lg_c = jnp.einsum('bqgrd,bcgd->bgrqc', q, kc) * scale + bias_c
    p_c = _masked_softmax(lg_c, dist_c >= 0)
    o_c = jnp.einsum('bgrqc,bcgd->bqgrd', p_c, vc)
    offs, wts = _slc_offsets()
    jb = jnp.arange(nb)
    cidx = (SEL_BLOCK // CMP_STRIDE) * jb[:, None] + offs[None, :]
    cval = (cidx >= 0) & (cidx < nc)
    p_grp = jnp.sum(p_c, axis=2)
    p_slc = jnp.sum(p_grp[..., jnp.clip(cidx, 0, nc - 1)] * (wts * cval), axis=-1)
    cur = q_pos // SEL_BLOCK
    forced = (jb[None] == 0) | (jb[None] == cur[:, None]) | (jb[None] == cur[:, None] - 1)
    future = jb[None] > cur[:, None]
    score = jnp.where(future, -1.0, jnp.where(forced, 1e6, p_slc))
    n_sel = min(SEL_TOPK, nb)
    _, sel = lax.top_k(score, n_sel)
    sel_ok = sel <= cur[:, None]
    bi = jnp.arange(b)[:, None, None]
    gi = jnp.arange(g)[None, :, None]
    flat = sel.reshape(b, g, tq * n_sel)
    ksg = ks_blk[bi, gi, flat].reshape(b, g, tq, n_sel * SEL_BLOCK, dk)
    vsg = vs_blk[bi, gi, flat].reshape(b, g, tq, n_sel * SEL_BLOCK, dk)
    pos4 = sel[..., None] * SEL_BLOCK + jnp.arange(SEL_BLOCK)
    ok_s = (sel_ok[..., None] & (pos4 <= q_pos[:, None, None])).reshape(b, g, tq, n_sel * SEL_BLOCK)
    pos_s = pos4.reshape(b, g, tq, n_sel * SEL_BLOCK)
    tg = table.transpose(1, 0, 2)
    bias_s = tg[jnp.arange(g)[None, :, None, None], _rel_bucket(q_pos[:, None] - pos_s)].transpose(0, 1, 4, 2, 3)
    lg_s = jnp.einsum('bqgrd,bgqkd->bgrqk', q, ksg) * scale + bias_s
    p_s = _masked_softmax(lg_s, ok_s[:, :, None])
    o_s = jnp.einsum('bgrqk,bgqkd->bqgrd', p_s, vsg)
    dist_w = q_pos[:, None] - kw_pos[None, :]
    ok_w = (dist_w >= 0) & (dist_w < WINDOW) & (kw_pos[None, :] >= 0)
    bias_w = table[_rel_bucket(dist_w)].transpose(2, 3, 0, 1)
    lg_w = jnp.einsum('bqgrd,bkgd->bgrqk', q, kw) * scale + bias_w
    p_w = _masked_softmax(lg_w, ok_w)
    o_w = jnp.einsum('bgrqk,bkgd->bqgrd', p_w, vw)
    o = gates[..., 0:1] * o_c + gates[..., 1:2] * o_s + gates[..., 2:3] * o_w
    return o.reshape(b, tq, g * r * dk)


def _nsa_prompt_jnp(p_n, W, nkv, ngrp):
    q, gates, rows, win = _nsa_project(p_n, W['qk_norm_g'], nkv, ngrp)
    b, s = q.shape[:2]
    kc, vc, c_end, ks_blk, vs_blk = _nsa_context(rows, W)
    win_pad = jnp.pad(win, ((0, 0), (WINDOW, 0), (0, 0), (0, 0), (0, 0)))
    qb_sz = 128

    def block(i):
        start = i * qb_sz
        qb = lax.dynamic_slice_in_dim(q, start, qb_sz, axis=1)
        gb = lax.dynamic_slice_in_dim(gates, start, qb_sz, axis=1)
        wb = lax.dynamic_slice_in_dim(win_pad, start, WINDOW + qb_sz, axis=1)
        q_pos = start + jnp.arange(qb_sz)
        kw_pos = start - WINDOW + jnp.arange(WINDOW + qb_sz)
        return _nsa_core(qb, gb, q_pos, kc, vc, c_end, ks_blk, vs_blk, wb[:, :, 0], wb[:, :, 1], kw_pos, W['rel_bias'])

    o = lax.map(block, jnp.arange(s // qb_sz))
    o = o.transpose(1, 0, 2, 3).reshape(b, s, -1)
    return o, rows, win[:, s - min(WINDOW, s):]


def _nsa_sample_jnp(p_n, cache_kv, cache_win, page_table, W, nkv, ngrp, db):
    p_n = p_n.reshape(db, p_n.shape[0] // db, -1)
    q, gates, rows_new, win_new = _nsa_project(p_n, W['qk_norm_g'], nkv, ngrp)
    db, ds = q.shape[:2]
    past_len = page_table.shape[1] * PAGE_SIZE
    past_rows = cache_kv[page_table].reshape(db, past_len, 4, nkv, HEAD_DIM)
    pad = (-ds) % SEL_BLOCK
    rows = jnp.concatenate([past_rows, jnp.pad(rows_new, ((0, 0), (0, pad), (0, 0), (0, 0), (0, 0)))], axis=1)
    kc, vc, c_end, ks_blk, vs_blk = _nsa_context(rows, W)
    keep = cache_win.shape[1]
    win_all = jnp.concatenate([cache_win, win_new], axis=1)
    q_pos = past_len + jnp.arange(ds)
    kw_pos = past_len - keep + jnp.arange(keep + ds)
    o = _nsa_core(q, gates, q_pos, kc, vc, c_end, ks_blk, vs_blk, win_all[:, :, 0], win_all[:, :, 1], kw_pos, W['rel_bias'])
    n_keep = min(WINDOW, past_len + ds)
    return o.reshape(db * ds, -1), rows_new, win_all[:, win_all.shape[1] - n_keep:]


def _top_values(x, k):
    n = x.shape[0]
    row = lax.broadcasted_iota(jnp.int32, x.shape, 0)
    vals = []
    for _ in range(k):
        m = jnp.max(x, axis=0, keepdims=True)
        vals.append(m)
        first = jnp.min(jnp.where(x == m, row, n), axis=0, keepdims=True)
        x = jnp.where(row == first, -jnp.inf, x)
    return vals


def _peer_route_kernel(hT_ref, wq_ref, sk_ref, s1_ref, c1_ref, s2_ref, e2_ref, tau_ref, *, nheads, topk):
    nk, half = sk_ref.shape[1], sk_ref.shape[2]
    qT = _dot(wq_ref[...], hT_ref[...])
    for h in range(nheads):
        base = h * 2 * half
        s1 = _dot(sk_ref[0].astype(BF16), qT[base:base + half].astype(BF16))
        s2 = _dot(sk_ref[1].astype(BF16), qT[base + half:base + 2 * half].astype(BF16))
        v1 = _top_values(s1, topk)
        v2 = _top_values(s2, topk)
        cand = jnp.concatenate([v1[a] + v2[b] for a in range(topk) for b in range(topk) if (a + 1) * (b + 1) <= topk],
                               axis=0)
        tau = _top_values(cand, topk)[-1]
        e1 = jnp.exp(s1 - v1[0])
        e2 = jnp.exp(s2 - v2[0])
        z = jnp.sum(jnp.where(cand >= tau, jnp.exp(cand - (v1[0] + v2[0])), 0.0), axis=0, keepdims=True)
        c1 = e1 / z
        for c in range(s1_ref.shape[1]):
            cs = slice(c * LANES, (c + 1) * LANES)
            s1_ref[h, c] = s1[:, cs]
            s2_ref[h, c] = s2[:, cs]
            c1_ref[h, c] = c1[:, cs]
            e2_ref[h, c] = e2[:, cs]
            tau_ref[h, c] = tau[:, cs]


def _peer_expert_kernel(hT_ref, x1_ref, ga_ref, s1_ref, c1_ref, s2_ref, e2_ref, tau_ref, u_ref, v_ref,
                        o_ref, acc_ref, gate_ref, *, nheads, n_eblocks):
    eb = pl.program_id(1)
    n_chunks, nk = s2_ref.shape[1], s2_ref.shape[2]
    rows_per_block = u_ref.shape[0] // nk

    @pl.when(eb == 0)
    def _():
        acc_ref[...] = jnp.zeros(acc_ref.shape, F32)

    def gate_tile(it, carry):
        i = it // n_chunks
        c = it % n_chunks
        i1 = eb * rows_per_block + i
        wd = None
        for h in range(nheads):
            cand = s1_ref[h, c, pl.ds(i1, 1), :] + s2_ref[h, c]
            term = jnp.where(cand >= tau_ref[h, c], e2_ref[h, c], 0.0) * c1_ref[h, c, pl.ds(i1, 1), :]
            wd = term if wd is None else wd + term
        gate_ref[i, c] = wd
        return carry

    lax.fori_loop(0, rows_per_block * n_chunks, gate_tile, 0)
    act = jax.nn.gelu(_dot(u_ref[...], hT_ref[...]))
    gates = jnp.concatenate([jnp.concatenate([gate_ref[i, c] for c in range(n_chunks)], axis=1)
                             for i in range(rows_per_block)], axis=0)
    acc_ref[...] += _dot((gates * act).T.astype(BF16), v_ref[...])

    @pl.when(eb == n_eblocks - 1)
    def _():
        o_ref[...] = x1_ref[...] + ga_ref[0] * acc_ref[...]


def _peer(h2, x1, ga, W, tm):
    t, d = h2.shape
    sub_keys = W['peer_sub_keys']
    nk, half = sub_keys.shape[1], sub_keys.shape[2]
    qd = W['peer_w_query'].shape[1]
    nheads = qd // (2 * half)
    hT = h2.T
    wqT = W['peer_w_query'].T.astype(BF16)
    cpt = tm // LANES
    route_shape = jax.ShapeDtypeStruct((nheads, t // LANES, nk, LANES), F32)
    rspec = pl.BlockSpec((nheads, cpt, nk, LANES), lambda i: (0, i, 0, 0))
    s1, c1, s2, e2, tau = pl.pallas_call(
        functools.partial(_peer_route_kernel, nheads=nheads, topk=PEER_TOPK),
        grid=(t // tm,),
        in_specs=[pl.BlockSpec((d, tm), lambda i: (0, i)),
                  pl.BlockSpec((qd, d), lambda i: (0, 0)),
                  pl.BlockSpec(sub_keys.shape, lambda i: (0, 0, 0))],
        out_specs=[rspec, rspec, rspec, rspec, pl.BlockSpec((nheads, cpt, 1, LANES), lambda i: (0, i, 0, 0))],
        out_shape=[route_shape] * 4 + [jax.ShapeDtypeStruct((nheads, t // LANES, 1, LANES), F32)],
        compiler_params=_cparams(("arbitrary",)),
        name="peer_route",
    )(hT, wqT, sub_keys)
    eblk = 2 * MXU
    n_eblocks = W['peer_u'].shape[0] // eblk
    nmod, rows, _ = ga.shape
    tiles_per_mod = (t // tm) // nmod
    rspec2 = pl.BlockSpec((nheads, cpt, nk, LANES), lambda i, e: (0, i, 0, 0))
    return pl.pallas_call(
        functools.partial(_peer_expert_kernel, nheads=nheads, n_eblocks=n_eblocks),
        grid=(t // tm, n_eblocks),
        in_specs=[pl.BlockSpec((d, tm), lambda i, e: (0, i)),
                  pl.BlockSpec((tm, d), lambda i, e: (i, 0)),
                  pl.BlockSpec((1, rows, d), lambda i, e: (i // tiles_per_mod, 0, 0)),
                  rspec2, rspec2, rspec2, rspec2,
                  pl.BlockSpec((nheads, cpt, 1, LANES), lambda i, e: (0, i, 0, 0)),
                  pl.BlockSpec((eblk, d), lambda i, e: (e, 0)),
                  pl.BlockSpec((eblk, d), lambda i, e: (e, 0))],
        out_specs=pl.BlockSpec((tm, d), lambda i, e: (i, 0)),
        out_shape=jax.ShapeDtypeStruct((t, d), F32),
        scratch_shapes=[pltpu.VMEM((tm, d), F32), pltpu.VMEM((eblk // nk, cpt, nk, LANES), F32)],
        compiler_params=_cparams(("arbitrary", "arbitrary")),
        name="peer_experts",
    )(hT, x1, ga, s1, c1, s2, e2, tau, W['peer_u_bf16'], W['peer_v_bf16'])


def _peer_jnp(h, W):
    n, d = h.shape
    sub_keys, expert_u, expert_v = W['peer_sub_keys'], W['peer_u'], W['peer_v']
    nk, half = sub_keys.shape[1], sub_keys.shape[2]
    nheads = W['peer_w_query'].shape[1] // (2 * half)
    cs = min(128, n)
    hc = h.reshape(-1, cs, d)

    def chunk(x):
        qh = (x @ W['peer_w_query']).reshape(cs, nheads, 2, half)
        s1 = jnp.einsum('chd,kd->chk', qh[:, :, 0], sub_keys[0])
        s2 = jnp.einsum('chd,kd->chk', qh[:, :, 1], sub_keys[1])
        v1, i1 = lax.top_k(s1, PEER_TOPK)
        v2, i2 = lax.top_k(s2, PEER_TOPK)
        cand = (v1[..., :, None] + v2[..., None, :]).reshape(cs, nheads, -1)
        cidx = (i1[..., :, None] * nk + i2[..., None, :]).reshape(cs, nheads, -1)
        top, pos = lax.top_k(cand, PEER_TOPK)
        eidx = jnp.take_along_axis(cidx, pos, axis=-1)
        gw = jax.nn.softmax(top, axis=-1)
        act = jax.nn.gelu(jnp.einsum('chkd,cd->chk', expert_u[eidx], x))
        return jnp.einsum('chk,chkd->cd', gw * act, expert_v[eidx])

    return lax.map(chunk, hc).reshape(-1, d)


def _layer(x, mods, nsa_fn, shift_prev, wkv0, W, tm, rwkv_chunk):
    b, t, d = x.shape
    cw = W['rwkv_w0'].shape[0]
    nw, na, ng = W['rwkv_w_up'].shape[0], W['rwkv_a_up'].shape[0], W['rwkv_g_up'].shape[0]
    rwkv_proj = 3 * cw + nw + na + ng
    sh1, sc1, ga1, sh2, sc2, ga2 = mods
    xf = x.reshape(b * t, d)
    if (b * t) % tm == 0 and t % tm == 0:
        as_mod = lambda m: m.reshape(b, 1, d)
    else:
        tm = b * t
        as_mod = lambda m: jnp.repeat(m, t, axis=0).reshape(1, b * t, d)
    w_r = _rwkv_pad_cols(W['w_in'][:, :rwkv_proj], cw, nw, na, ng).astype(BF16)
    nsa_cols = W['w_in'].shape[1] - rwkv_proj
    w_n = jnp.pad(W['w_in'][:, rwkv_proj:], ((0, 0), (0, _rup(nsa_cols, LANES) - nsa_cols))).astype(BF16)
    p_r, _ = _norm_mod_matmul(xf, W['norm1_g'], as_mod(sc1), as_mod(sh1), w_r, tm, 512)
    p_n, _ = _norm_mod_matmul(xf, W['norm1_g'], as_mod(sc1), as_mod(sh1), w_n, tm, w_n.shape[1] // 3)
    pr = p_r.shape[1]
    p_r = p_r.reshape(b, t, pr)
    shift_new = _rwkv_unpad_cols(p_r[:, -1], cw, nw, na, ng)
    tpad = _rup(t, rwkv_chunk)
    p_r_pad = jnp.pad(p_r, ((0, 0), (0, tpad - t), (0, 0)))
    y_r, wkv_new = _rwkv_mix(p_r_pad, shift_prev, wkv0, W, rwkv_chunk, t)
    y_r = y_r[:, :t].reshape(b * t, cw)
    y_n, rows, win = nsa_fn(p_n)
    w_out = W['w_out'].astype(BF16)
    x1 = _out_proj(xf, y_r, y_n, as_mod(ga1), w_out[:cw], w_out[cw:], tm, 512)
    h2 = _norm_mod(x1, W['norm2_g'], as_mod(sc2), as_mod(sh2), tm)
    if (b * t) % LANES == 0:
        out = _peer(h2, x1, as_mod(ga2), W, tm)
    else:
        out = x1 + jnp.repeat(ga2, t, axis=0) * _peer_jnp(h2.astype(F32), W)
    return out.reshape(b, t, d), rows, win, wkv_new, shift_new


def kernel(x_prompt, x_sample, c_prompt, c_sample, cache_kv, cache_win, state_wkv, state_shift, page_table,
           norm1_g, norm2_g, w_ada, b_ada, w_in, w_out,
           rwkv_mu, rwkv_w0, rwkv_w_up, rwkv_a0, rwkv_a_up, rwkv_g_up, rwkv_k_k, rwkv_k_a, rwkv_r_k, lnx_w, lnx_b,
           qk_norm_g, cmp_w1, cmp_b1, cmp_w2, cmp_b2, rel_bias,
           peer_w_query, peer_sub_keys, peer_u, peer_v):
    W = dict(norm1_g=norm1_g, norm2_g=norm2_g, w_ada=w_ada, b_ada=b_ada, w_in=w_in, w_out=w_out,
             rwkv_mu=rwkv_mu, rwkv_w0=rwkv_w0, rwkv_w_up=rwkv_w_up, rwkv_a0=rwkv_a0, rwkv_a_up=rwkv_a_up,
             rwkv_g_up=rwkv_g_up, rwkv_k_k=rwkv_k_k, rwkv_k_a=rwkv_k_a, rwkv_r_k=rwkv_r_k, lnx_w=lnx_w, lnx_b=lnx_b,
             qk_norm_g=qk_norm_g, cmp_w1=cmp_w1, cmp_b1=cmp_b1, cmp_w2=cmp_w2, cmp_b2=cmp_b2, rel_bias=rel_bias,
             peer_w_query=peer_w_query, peer_sub_keys=peer_sub_keys, peer_u=peer_u, peer_v=peer_v)
    W['peer_u_bf16'] = peer_u.astype(BF16)
    W['peer_v_bf16'] = peer_v.astype(BF16)
    bp, seq, d = x_prompt.shape
    db = x_sample.shape[0]
    nkv = cache_kv.shape[3]
    nh_r = rwkv_w0.shape[0] // HEAD_DIM
    ngrp = (w_out.shape[0] - rwkv_w0.shape[0]) // HEAD_DIM // nkv

    mods = _ada_mods(jnp.concatenate([c_prompt, c_sample], axis=0), w_ada, b_ada)
    mods = mods.reshape(bp + db, N_MODS, d)
    mods_p = [mods[:bp, i] for i in range(N_MODS)]
    mods_s = [mods[bp:, i] for i in range(N_MODS)]

    shift0 = jnp.zeros((bp, state_shift.shape[1]), F32)
    wkv0 = jnp.zeros((bp, nh_r, HEAD_DIM, HEAD_DIM), F32)
    y_p, rows_p, win_p, wkv_p, shift_p = _layer(
        x_prompt, mods_p, lambda pn: _nsa_prompt(pn, W, bp, seq, nkv, ngrp, 512), shift0, wkv0, W, 512, 64)
    y_s, rows_s, win_s, wkv_s, shift_s = _layer(
        x_sample, mods_s,
        lambda pn: _nsa_sample(pn, cache_kv, cache_win, page_table, W, db, x_sample.shape[1], nkv, ngrp),
        state_shift, state_wkv, W, 512, 32)
    return (y_p, y_s, rows_p, win_p, wkv_p.astype(state_wkv.dtype), shift_p,
            rows_s, win_s, wkv_s.astype(state_wkv.dtype), shift_s)
```

```python
import functools
import math

import numpy as np
import jax
import jax.numpy as jnp
from jax import lax
from jax.experimental import pallas as pl
from jax.experimental.pallas import tpu as pltpu

F32 = jnp.float32
BF16 = jnp.bfloat16
HI = lax.Precision.HIGHEST

HEAD_DIM = 64
PAGE_SIZE = 128
CMP_LEN = 32
CMP_STRIDE = 16
SEL_BLOCK = 64
SEL_TOPK = 16
WINDOW = 512
REL_BUCKETS = 32
REL_MAX_DIST = 2048
PEER_TOPK = 16
N_MODS = 6
RMS_EPS = 1e-6
LNX_EPS = 64e-5
NEG = -1e30

LANES = 128
MXU = 256
HEADS_PER_GROUP = MXU // HEAD_DIM
VMEM_LIMIT = 56 * 1024 * 1024


def _cparams(sem):
    return pltpu.CompilerParams(dimension_semantics=sem, vmem_limit_bytes=VMEM_LIMIT)


def _dot(a, b, precision=None):
    return jnp.dot(a, b, preferred_element_type=F32, precision=precision)


def _dot_nt(a, b, precision=None):
    return lax.dot_general(a, b, (((1,), (1,)), ((), ())), preferred_element_type=F32, precision=precision)


def _split_bf16(x, parts):
    out = []
    for _ in range(parts):
        h = x.astype(BF16)
        out.append(h)
        x = x - h.astype(F32)
    return out


def _mm(a, b, mode, nt=False):
    f = _dot_nt if nt else _dot
    if mode == 6:
        return f(a, b, HI)
    if mode == 1:
        return f(a.astype(BF16), b.astype(BF16))
    if mode == 3:
        ah, al = _split_bf16(a, 2)
        bh, bl = _split_bf16(b, 2)
        return (f(al, bh) + f(ah, bl)) + f(ah, bh)
    if mode[0] == 'L':
        terms = [f(t, b.astype(BF16)) for t in _split_bf16(a, int(mode[1]))]
    else:
        terms = [f(a.astype(BF16), t) for t in _split_bf16(b, int(mode[1]))]
    out = terms[-1]
    for t in terms[-2::-1]:
        out = out + t
    return out


RWKV_MM = dict(lora=1, headsum='L2', cumsum='R3', gram=1, inverse=1, state=1)


def _ada_kernel(c_ref, w_ref, b_ref, o_ref):
    c = c_ref[...]
    s = c * jax.nn.sigmoid(c)
    o_ref[...] = _dot(s.astype(BF16), w_ref[...].astype(BF16)) + b_ref[...]


def _ada_mods(c, w_ada, b_ada):
    n, d = c.shape
    cols = w_ada.shape[1]
    tn = 1024
    return pl.pallas_call(
        _ada_kernel,
        grid=(cols // tn,),
        in_specs=[pl.BlockSpec((n, d), lambda j: (0, 0)),
                  pl.BlockSpec((d, tn), lambda j: (0, j)),
                  pl.BlockSpec((1, tn), lambda j: (0, j))],
        out_specs=pl.BlockSpec((n, tn), lambda j: (0, j)),
        out_shape=jax.ShapeDtypeStruct((n, cols), F32),
        compiler_params=_cparams(("arbitrary",)),
        name="ada_mods",
    )(c, w_ada, b_ada.reshape(1, cols))


def _nmm_kernel(x_ref, g_ref, sc_ref, sh_ref, w_ref, o_ref, h_ref):
    @pl.when(pl.program_id(1) == 0)
    def _():
        x = x_ref[...]
        ms = jnp.mean(x * x, axis=-1, keepdims=True)
        y = x * lax.rsqrt(ms + RMS_EPS) * g_ref[...]
        h_ref[...] = (y * (1.0 + sc_ref[0]) + sh_ref[0]).astype(h_ref.dtype)

    o_ref[...] = _dot(h_ref[...], w_ref[...])


def _norm_mod_matmul(x, g, sc, sh, w, tm, tn):
    t, d = x.shape
    n = w.shape[1]
    nmod, rows, _ = sc.shape
    tiles_per_mod = (t // tm) // nmod
    mod_spec = pl.BlockSpec((1, rows, d), lambda i, j: (i // tiles_per_mod, 0, 0))
    return pl.pallas_call(
        _nmm_kernel,
        grid=(t // tm, n // tn),
        in_specs=[pl.BlockSpec((tm, d), lambda i, j: (i, 0)),
                  pl.BlockSpec((1, d), lambda i, j: (0, 0)),
                  mod_spec, mod_spec,
                  pl.BlockSpec((d, tn), lambda i, j: (0, j))],
        out_specs=[pl.BlockSpec((tm, tn), lambda i, j: (i, j)),
                   pl.BlockSpec((tm, d), lambda i, j: (i, 0))],
        out_shape=[jax.ShapeDtypeStruct((t, n), F32), jax.ShapeDtypeStruct((t, d), BF16)],
        compiler_params=_cparams(("arbitrary", "arbitrary")),
        name="norm_mod_matmul",
    )(x, g.reshape(1, d), sc, sh, w)


def _nm_kernel(x_ref, g_ref, sc_ref, sh_ref, h_ref):
    x = x_ref[...]
    ms = jnp.mean(x * x, axis=-1, keepdims=True)
    y = x * lax.rsqrt(ms + RMS_EPS) * g_ref[...]
    h_ref[...] = (y * (1.0 + sc_ref[0]) + sh_ref[0]).astype(h_ref.dtype)


def _norm_mod(x, g, sc, sh, tm):
    t, d = x.shape
    nmod, rows, _ = sc.shape
    tiles_per_mod = (t // tm) // nmod
    mod_spec = pl.BlockSpec((1, rows, d), lambda i: (i // tiles_per_mod, 0, 0))
    return pl.pallas_call(
        _nm_kernel,
        grid=(t // tm,),
        in_specs=[pl.BlockSpec((tm, d), lambda i: (i, 0)), pl.BlockSpec((1, d), lambda i: (0, 0)), mod_spec, mod_spec],
        out_specs=pl.BlockSpec((tm, d), lambda i: (i, 0)),
        out_shape=jax.ShapeDtypeStruct((t, d), BF16),
        compiler_params=_cparams(("arbitrary",)),
        name="norm_mod",
    )(x, g.reshape(1, d), sc, sh)


def _outproj_kernel(x_ref, yr_ref, yn_ref, ga_ref, w1_ref, w2_ref, o_ref):
    acc = _dot(yr_ref[...].astype(BF16), w1_ref[...]) + _dot(yn_ref[...].astype(BF16), w2_ref[...])
    o_ref[...] = x_ref[...] + ga_ref[0] * acc


def _out_proj(x, y_r, y_n, ga, w1, w2, tm, tn):
    t, d = x.shape
    nmod, rows, _ = ga.shape
    tiles_per_mod = (t // tm) // nmod
    cr, cn = y_r.shape[1], y_n.shape[1]
    return pl.pallas_call(
        _outproj_kernel,
        grid=(t // tm, d // tn),
        in_specs=[pl.BlockSpec((tm, tn), lambda i, j: (i, j)),
                  pl.BlockSpec((tm, cr), lambda i, j: (i, 0)),
                  pl.BlockSpec((tm, cn), lambda i, j: (i, 0)),
                  pl.BlockSpec((1, rows, tn), lambda i, j: (i // tiles_per_mod, 0, j)),
                  pl.BlockSpec((cr, tn), lambda i, j: (0, j)),
                  pl.BlockSpec((cn, tn), lambda i, j: (0, j))],
        out_specs=pl.BlockSpec((tm, tn), lambda i, j: (i, j)),
        out_shape=jax.ShapeDtypeStruct((t, d), F32),
        compiler_params=_cparams(("arbitrary", "arbitrary")),
        name="out_proj",
    )(x, y_r, y_n, ga, w1, w2)


def _softplus(z):
    return jnp.maximum(z, 0.0) + jnp.log(1.0 + jnp.exp(-jnp.abs(z)))


def _rwkv_kernel(p_ref, shift_ref, s0_ref, mu_ref, vec_ref, wup_ref, aup_ref, gup_ref,
                 y_ref, sfin_ref, carry, state, *, t_valid, n_chunks):
    c = pl.program_id(1)
    chunk = p_ref.shape[1]
    cw = vec_ref.shape[1]
    n_groups = cw // MXU
    hg = HEADS_PER_GROUP
    rows_g = hg * chunk

    @pl.when(c == 0)
    def _():
        carry[...] = shift_ref[0]
        state[...] = s0_ref[0]

    p = p_ref[0]
    row = lax.broadcasted_iota(jnp.int32, (chunk, 1), 0)
    prev = jnp.where(row == 0, carry[...], pltpu.roll(p, 1, axis=0))
    carry[...] = p[chunk - 1:chunk, :]
    xs = p + (prev - p) * mu_ref[...]

    w0, a0, k_k, k_a, r_k, lnx_w, lnx_b = (vec_ref[i:i + 1, :] for i in range(7))
    r = xs[:, 0:cw]
    k = xs[:, cw:2 * cw]
    v = xs[:, 2 * cw:3 * cw]
    o = 3 * cw
    nw, na, ng = wup_ref.shape[0], aup_ref.shape[0], gup_ref.shape[0]
    xw = xs[:, o:o + nw]
    xa = xs[:, o + nw:o + nw + na]
    xg = xs[:, o + nw + na:o + nw + na + ng]
    pm = RWKV_MM
    w_log = -_softplus(-(w0 + _mm(jnp.tanh(xw), wup_ref[...], pm['lora']))) - 0.5
    a = jax.nn.sigmoid(a0 + _mm(xa, aup_ref[...], pm['lora']))
    gate = _mm(jax.nn.sigmoid(xg), gup_ref[...], pm['lora'])

    gi = lax.broadcasted_iota(jnp.int32, (MXU, MXU), 0) // HEAD_DIM
    gj = lax.broadcasted_iota(jnp.int32, (MXU, MXU), 1) // HEAD_DIM
    ones_bd = (gi == gj).astype(F32)

    def head_sum(x):
        return jnp.concatenate([_mm(x[:, g * MXU:(g + 1) * MXU], ones_bd, pm['headsum']) for g in range(n_groups)],
                               axis=1)

    kk = k * k_k
    kk = kk / jnp.maximum(jnp.sqrt(head_sum(kk * kk)), 1e-12)
    k2 = k * (1.0 + (a - 1.0) * k_a)
    logdec = -jnp.exp(w_log)
    if t_valid < chunk * n_chunks:
        valid = (row + c * chunk) < t_valid
        logdec = jnp.where(valid, logdec, 0.0)
        kk = jnp.where(valid, kk, 0.0)
        k2 = jnp.where(valid, k2, 0.0)
        v = jnp.where(valid, v, 0.0)

    ti = lax.broadcasted_iota(jnp.int32, (chunk, chunk), 0)
    tj = lax.broadcasted_iota(jnp.int32, (chunk, chunk), 1)
    cum = _mm((tj <= ti).astype(F32), logdec, pm['cumsum'])
    cum_end = cum[chunk - 1:chunk, :]
    e_neg = jnp.exp(-cum)
    e_rem = jnp.exp(cum_end - cum)
    r_t = r * jnp.exp(cum)
    a_t = -kk * jnp.exp(cum - logdec)
    b_vec = kk * a
    b_t = b_vec * e_neg
    k_t = k2 * e_neg
    b_rem = b_vec * e_rem
    k_rem = k2 * e_rem
    w_end = jnp.exp(cum_end)

    lane_head = lax.broadcasted_iota(jnp.int32, (chunk, MXU), 1) // HEAD_DIM
    ri = lax.broadcasted_iota(jnp.int32, (rows_g, rows_g), 0)
    rj = lax.broadcasted_iota(jnp.int32, (rows_g, rows_g), 1)
    strict = rj < ri
    incl = rj <= ri
    eye_r = (ri == rj).astype(F32)
    di = lax.broadcasted_iota(jnp.int32, (MXU, MXU), 0)
    dj = lax.broadcasted_iota(jnp.int32, (MXU, MXU), 1)
    n_double = max(int(math.ceil(math.log2(chunk))) - 1, 0)

    def bd(x):
        return jnp.concatenate([jnp.where(lane_head == h, x, 0.0) for h in range(hg)], axis=0)

    def stack(x):
        return jnp.concatenate([x[:, h * HEAD_DIM:(h + 1) * HEAD_DIM] for h in range(hg)], axis=0)

    def unstack(x):
        return jnp.concatenate([x[h * chunk:(h + 1) * chunk, :] for h in range(hg)], axis=1)

    ys = []
    for g in range(n_groups):
        sl = slice(g * MXU, (g + 1) * MXU)
        a_bd, r_bd = bd(a_t[:, sl]), bd(r_t[:, sl])
        b_bd, k_bd = bd(b_t[:, sl]), bd(k_t[:, sl])
        v_st = stack(v[:, sl])
        a_ab = jnp.where(strict, _mm(a_bd, b_bd, pm['gram'], nt=True), 0.0)
        a_ak = jnp.where(strict, _mm(a_bd, k_bd, pm['gram'], nt=True), 0.0)
        a_rb = jnp.where(incl, _mm(r_bd, b_bd, pm['gram'], nt=True), 0.0)
        a_rk = jnp.where(incl, _mm(r_bd, k_bd, pm['gram'], nt=True), 0.0)
        tinv = eye_r + a_ab
        pw = a_ab
        for _ in range(n_double):
            pw = _mm(pw, pw, pm['inverse'])
            tinv = tinv + _mm(pw, tinv, pm['inverse'])
        s0 = state[g]
        z = _mm(a_bd, s0, pm['state']) + _mm(a_ak, v_st, pm['state'])
        u = _mm(tinv, z, pm['state'])
        y_st = _mm(r_bd, s0, pm['state']) + _mm(a_rb, u, pm['state']) + _mm(a_rk, v_st, pm['state'])
        w_col = jnp.sum(jnp.where(di == dj, jnp.broadcast_to(w_end[:, sl], (MXU, MXU)), 0.0), axis=1, keepdims=True)
        state[g] = (w_col * s0 + _mm(bd(b_rem[:, sl]).T, u, pm['state'])
                    + _mm(bd(k_rem[:, sl]).T, v_st, pm['state']))
        ys.append(unstack(y_st))
    y = jnp.concatenate(ys, axis=1)

    inv_n = 1.0 / HEAD_DIM
    mean = head_sum(y) * inv_n
    d = y - mean
    var = head_sum(d * d) * inv_n
    yn = d * lax.rsqrt(var + LNX_EPS) * lnx_w + lnx_b
    bonus = head_sum(r * k2 * r_k) * v
    y_ref[0] = ((yn + bonus) * gate).astype(y_ref.dtype)

    @pl.when(c == n_chunks - 1)
    def _():
        sfin_ref[0] = state[...]


def _rwkv_pad_cols(x, cw, nw, na, ng):
    o = 3 * cw
    parts = [x[..., :o + nw], x[..., o + nw:o + nw + na], x[..., o + nw + na:]]
    widths = [o + _rup(nw, LANES), _rup(na, LANES), _rup(ng, LANES)]
    out = []
    for part, wd in zip(parts, widths):
        pad = [(0, 0)] * (x.ndim - 1) + [(0, wd - part.shape[-1])]
        out.append(jnp.pad(part, pad))
    return jnp.concatenate(out, axis=-1)


def _rwkv_unpad_cols(x, cw, nw, na, ng):
    o = 3 * cw
    o2 = o + _rup(nw, LANES)
    o3 = o2 + _rup(na, LANES)
    return jnp.concatenate([x[..., :o + nw], x[..., o2:o2 + na], x[..., o3:o3 + ng]], axis=-1)


def _rup(x, m):
    return (x + m - 1) // m * m


def _rwkv_mix(p_r, shift_prev, wkv0, W, chunk, t_valid):
    b, tpad, pr = p_r.shape
    cw = W['rwkv_w0'].shape[0]
    nh = cw // HEAD_DIM
    n_groups = cw // MXU
    nw, na, ng = W['rwkv_w_up'].shape[0], W['rwkv_a_up'].shape[0], W['rwkv_g_up'].shape[0]
    n_chunks = tpad // chunk
    mu = _rwkv_pad_cols(W['rwkv_mu'], cw, nw, na, ng).reshape(1, pr)
    vecs = jnp.stack([W['rwkv_w0'], W['rwkv_a0'], W['rwkv_k_k'], W['rwkv_k_a'], W['rwkv_r_k'].reshape(cw),
                      W['lnx_w'], W['lnx_b'], jnp.zeros((cw,), F32)])
    wup = jnp.pad(W['rwkv_w_up'], ((0, _rup(nw, LANES) - nw), (0, 0)))
    aup = jnp.pad(W['rwkv_a_up'], ((0, _rup(na, LANES) - na), (0, 0)))
    gup = jnp.pad(W['rwkv_g_up'], ((0, _rup(ng, LANES) - ng), (0, 0)))
    shift3 = _rwkv_pad_cols(shift_prev, cw, nw, na, ng).reshape(b, 1, pr)
    s0 = wkv0.astype(F32).transpose(0, 1, 3, 2).reshape(b, n_groups, MXU, HEAD_DIM)
    const = lambda shape: pl.BlockSpec(shape, lambda i, c: (0,) * len(shape))
    y, sfin = pl.pallas_call(
        functools.partial(_rwkv_kernel, t_valid=t_valid, n_chunks=n_chunks),
        grid=(b, n_chunks),
        in_specs=[pl.BlockSpec((1, chunk, pr), lambda i, c: (i, c, 0)),
                  pl.BlockSpec((1, 1, pr), lambda i, c: (i, 0, 0)),
                  pl.BlockSpec((1, n_groups, MXU, HEAD_DIM), lambda i, c: (i, 0, 0, 0)),
                  const((1, pr)), const((8, cw)), const(wup.shape), const(aup.shape), const(gup.shape)],
        out_specs=[pl.BlockSpec((1, chunk, cw), lambda i, c: (i, c, 0)),
                   pl.BlockSpec((1, n_groups, MXU, HEAD_DIM), lambda i, c: (i, 0, 0, 0))],
        out_shape=[jax.ShapeDtypeStruct((b, tpad, cw), F32),
                   jax.ShapeDtypeStruct((b, n_groups, MXU, HEAD_DIM), F32)],
        scratch_shapes=[pltpu.VMEM((1, pr), F32), pltpu.VMEM((n_groups, MXU, HEAD_DIM), F32)],
        compiler_params=_cparams(("arbitrary", "arbitrary")),
        name="rwkv_mix",
    )(p_r, shift3, s0, mu, vecs, wup, aup, gup)
    s_fin = sfin.reshape(b, nh, HEAD_DIM, HEAD_DIM).transpose(0, 1, 3, 2)
    return y, s_fin


def _head_ones():
    gi = lax.broadcasted_iota(jnp.int32, (MXU, MXU), 0) // HEAD_DIM
    gj = lax.broadcasted_iota(jnp.int32, (MXU, MXU), 1) // HEAD_DIM
    return (gi == gj).astype(F32)


def _nsa_proj_kernel(p_ref, g_ref, q_ref, rows_ref, win_ref, gate_ref, *, nsa_w, kvw):
    ones_bd = _head_ones()

    def hnorm(x, gvec):
        ms = _dot(x * x, ones_bd, HI) * (1.0 / HEAD_DIM)
        return x * lax.rsqrt(ms + RMS_EPS) * gvec

    for i in range(nsa_w // MXU):
        sl = slice(i * MXU, (i + 1) * MXU)
        q_ref[:, sl] = hnorm(p_ref[:, sl], g_ref[0:1, :])
    o = nsa_w
    rows_ref[:, 0:2 * kvw] = p_ref[:, o:o + 2 * kvw]
    rows_ref[:, 2 * kvw:3 * kvw] = hnorm(p_ref[:, o + 2 * kvw:o + 3 * kvw], g_ref[2:3, :])
    rows_ref[:, 3 * kvw:4 * kvw] = p_ref[:, o + 3 * kvw:o + 4 * kvw]
    win_ref[:, 0:kvw] = hnorm(p_ref[:, o + 4 * kvw:o + 5 * kvw], g_ref[3:4, :])
    win_ref[:, kvw:2 * kvw] = p_ref[:, o + 5 * kvw:o + 6 * kvw]
    gate_ref[...] = jax.nn.sigmoid(p_ref[:, o + 6 * kvw:])


def _nsa_project_call(p_n, qk_norm_g, nsa_w, kvw, tm):
    t, pc = p_n.shape
    assert kvw == MXU and nsa_w % MXU == 0
    gcols = pc - nsa_w - 6 * kvw
    gvec = jnp.tile(qk_norm_g, (1, MXU // HEAD_DIM))
    return pl.pallas_call(
        functools.partial(_nsa_proj_kernel, nsa_w=nsa_w, kvw=kvw),
        grid=(t // tm,),
        in_specs=[pl.BlockSpec((tm, pc), lambda i: (i, 0)),
                  pl.BlockSpec(gvec.shape, lambda i: (0, 0))],
        out_specs=[pl.BlockSpec((tm, nsa_w), lambda i: (i, 0)),
                   pl.BlockSpec((tm, 4 * kvw), lambda i: (i, 0)),
                   pl.BlockSpec((tm, 2 * kvw), lambda i: (i, 0)),
                   pl.BlockSpec((tm, gcols), lambda i: (i, 0))],
        out_shape=[jax.ShapeDtypeStruct((t, nsa_w), F32), jax.ShapeDtypeStruct((t, 4 * kvw), F32),
                   jax.ShapeDtypeStruct((t, 2 * kvw), F32), jax.ShapeDtypeStruct((t, gcols), F32)],
        compiler_params=_cparams(("arbitrary",)),
        name="nsa_project",
    )(p_n, gvec)


def _cmp_part_kernel(*refs, n_in, row_w, kvw, n_prefetch=0):
    refs = refs[n_prefetch:]
    x_refs, w_ref, o_ref = refs[:n_in], refs[n_in], refs[n_in + 1]
    for typ in range(2):
        acc = None
        for s in range(CMP_STRIDE):
            lo = s * row_w + typ * kvw
            xs = jnp.concatenate([x[0, :, lo:lo + kvw] for x in x_refs], axis=0) if n_in > 1 else x_refs[0][0, :, lo:lo + kvw]
            d = _dot(xs.astype(BF16), w_ref[typ, s])
            acc = d if acc is None else acc + d
        o_ref[0, :, typ * 2 * kvw:(typ + 1) * 2 * kvw] = acc


def _cmp_first_weights(cmp_w1, nkv):
    r2 = CMP_LEN // CMP_STRIDE
    e = cmp_w1.shape[-1]
    w1r = cmp_w1.reshape(2, r2, CMP_STRIDE, HEAD_DIM, e)
    eye = jnp.eye(nkv, dtype=F32)
    big = jnp.einsum('yhsde,gk->ysgdkhe', w1r, eye)
    return big.reshape(2, CMP_STRIDE, nkv * HEAD_DIM, nkv * r2 * e).astype(BF16)


def _cmp_parts_prompt(rows2d, w_big, b, t, kvw):
    row_w = rows2d.shape[2]
    nsub = t // CMP_STRIDE
    blk = min(nsub, LANES)
    return pl.pallas_call(
        functools.partial(_cmp_part_kernel, n_in=1, row_w=row_w, kvw=kvw),
        grid=(b, nsub // blk),
        in_specs=[pl.BlockSpec((1, blk, CMP_STRIDE * row_w), lambda i, j: (i, j, 0)),
                  pl.BlockSpec(w_big.shape, lambda i, j: (0, 0, 0, 0))],
        out_specs=pl.BlockSpec((1, blk, 4 * kvw), lambda i, j: (i, j, 0)),
        out_shape=jax.ShapeDtypeStruct((b, nsub, 4 * kvw), F32),
        compiler_params=_cparams(("arbitrary", "arbitrary")),
        name="cmp_parts",
    )(rows2d.reshape(b, nsub, CMP_STRIDE * row_w), w_big)


def _cmp_finish_kernel(*refs, n_parts, nc, nkv):
    p_refs = refs[:n_parts]
    b1_ref, w2_ref, b2_ref, g_ref, o_ref = refs[n_parts:]
    tg = pl.program_id(1)
    ns = o_ref.shape[2]
    e = p_refs[0].shape[2] // 2
    pieces = [p[0] for p in p_refs]
    have = sum(p.shape[0] for p in pieces)
    if have < ns:
        pieces.append(jnp.zeros((ns - have, 2 * e), F32))
    part = jnp.concatenate(pieces, axis=0) if len(pieces) > 1 else pieces[0]
    nxt = pltpu.roll(part[:, e:], ns - 1, axis=0)
    hid = part[:, :e] + nxt + b1_ref[0]
    out = _dot(jax.nn.gelu(hid).astype(BF16), w2_ref[0].astype(BF16)) + b2_ref[0]
    normed = out * lax.rsqrt(jnp.mean(out * out, axis=-1, keepdims=True) + RMS_EPS) * g_ref[...]
    out = jnp.where(tg < nkv, normed, out)
    row = lax.broadcasted_iota(jnp.int32, (ns, 1), 0)
    o_ref[0, 0] = jnp.where(row < nc, out, 0.0)


def _cmp_finish(parts_list, ns, cmp_b1, cmp_w2, cmp_b2, g1, nc, nkv):
    b = parts_list[0].shape[0]
    e = cmp_b1.shape[1]
    return pl.pallas_call(
        functools.partial(_cmp_finish_kernel, n_parts=len(parts_list), nc=nc, nkv=nkv),
        grid=(b, 2 * nkv),
        in_specs=[pl.BlockSpec((1, p.shape[1], 2 * e), lambda i, j: (i, 0, j)) for p in parts_list] + [
                  pl.BlockSpec((1, 1, e), lambda i, j: (j // nkv, 0, 0)),
                  pl.BlockSpec((1, e, HEAD_DIM), lambda i, j: (j // nkv, 0, 0)),
                  pl.BlockSpec((1, 1, HEAD_DIM), lambda i, j: (j // nkv, 0, 0)),
                  pl.BlockSpec((1, HEAD_DIM), lambda i, j: (0, 0))],
        out_specs=pl.BlockSpec((1, 1, ns, HEAD_DIM), lambda i, j: (i, j, 0, 0)),
        out_shape=jax.ShapeDtypeStruct((b, 2 * nkv, ns, HEAD_DIM), F32),
        compiler_params=_cparams(("arbitrary", "arbitrary")),
        name="cmp_finish",
    )(*parts_list, cmp_b1.reshape(2, 1, e), cmp_w2, cmp_b2.reshape(2, 1, HEAD_DIM), g1.reshape(1, HEAD_DIM))


QT = 128
SEL_CHAINS = 2
QK_SCALE = HEAD_DIM ** -0.5
assert math.log2(HEAD_DIM) % 2 == 0, "QK_SCALE must be a power of two to be folded into q exactly"


def _rel_table_np_dist(dist, table):
    onehot = (_rel_bucket(dist)[..., None] == jnp.arange(REL_BUCKETS)).astype(F32)
    return jnp.einsum('...b,bh->...h', onehot, table, precision=HI)


def _rel_table_per_lane(dist, table_l):
    bucket = _rel_bucket(dist)
    out = jnp.zeros(dist.shape, F32)
    for b in range(REL_BUCKETS):
        out = out + jnp.where(bucket == b, table_l[b][None, :], 0.0)
    return out


def _softmax_update(s, mask, m, l):
    m_new = jnp.maximum(m, jnp.max(jnp.where(mask, s, NEG), axis=0, keepdims=True))
    alpha = jnp.exp(m - m_new)
    p = jnp.where(mask, jnp.exp(s - m_new), 0.0)
    return p, m_new, alpha, alpha * l + jnp.sum(p, axis=0, keepdims=True)


def _rank_select(score, score_ref, cur, n_sel):
    nb = score.shape[0]
    score_ref[0:nb, :] = score
    jrow = lax.broadcasted_iota(jnp.int32, score.shape, 0)

    def body(j, rank):
        other = score_ref[pl.ds(j, 1), :]
        beats = (other > score) | ((other == score) & (jrow > j))
        return rank + jnp.where(beats, 1.0, 0.0)

    rank = lax.fori_loop(0, nb, body, jnp.zeros(score.shape, F32), unroll=8)
    return jnp.where((rank < n_sel) & (jrow <= cur), 1.0, 0.0)


def _nsa_prompt_kernel(qT_ref, gT_ref, kc_ref, vcT_ref, ks_ref, vsT_ref, kw_ref, vwT_ref, bc_ref, toep_ref,
                       o_ref, pg_ref, score_ref, sel_ref, *, nc, nb, n_sel, ngrp):
    qt = pl.program_id(2)
    lanes = ngrp * QT
    q = (qT_ref[0, 0, 0] * QK_SCALE).astype(BF16)
    iq = lax.broadcasted_iota(jnp.int32, (1, QT), 1)
    q_pos = qt * QT + iq
    tile4 = lambda x: jnp.concatenate([x] * ngrp, axis=1)

    ncp = kc_ref.shape[2]
    s = _dot(kc_ref[0, 0].astype(BF16), q)
    s = s + jnp.concatenate([bc_ref[0, r] for r in range(ngrp)], axis=1)
    crow = lax.broadcasted_iota(jnp.int32, (ncp, QT), 0)
    ok_c = tile4((crow * CMP_STRIDE + (CMP_LEN - 1) <= q_pos) & (crow < nc))
    p, _, _, l = _softmax_update(s, ok_c, jnp.full((1, lanes), NEG, F32), jnp.zeros((1, lanes), F32))
    p = p * jnp.where(l > 0.0, 1.0 / jnp.where(l > 0.0, l, 1.0), 0.0)
    o_c = _dot(vcT_ref[0, 0].astype(BF16), p.astype(BF16))

    p_grp = p[:, 0:QT]
    for r in range(1, ngrp):
        p_grp = p_grp + p[:, r * QT:(r + 1) * QT]
    pad = 8
    pg_ref[...] = jnp.zeros(pg_ref.shape, F32)
    pg_ref[pad:pad + ncp, :] = p_grp
    r1 = SEL_BLOCK // CMP_STRIDE
    offs, wts = _slc_offsets()
    p_slc = None
    for o, wt in zip(offs, wts):
        term = float(wt) * pg_ref[pl.ds(pad + int(o), nb, stride=r1), :]
        p_slc = term if p_slc is None else p_slc + term
    jrow = lax.broadcasted_iota(jnp.int32, (nb, QT), 0)
    cur = q_pos // SEL_BLOCK
    forced = (jrow == 0) | (jrow == cur) | (jrow == cur - 1)
    score = jnp.where(jrow > cur, -1.0, jnp.where(forced, 1e6, p_slc))
    sel_ref[0:nb, :] = _rank_select(score, score_ref, cur, n_sel)

    ik = lax.broadcasted_iota(jnp.int32, (QT, QT), 0)
    iqq = lax.broadcasted_iota(jnp.int32, (QT, QT), 1)
    blocks_per_tile = QT // SEL_BLOCK

    def attend(kp, carry, k_ref, vT_ref, mask_fn):
        m, l, acc = carry
        s_all = _dot(k_ref[0, 0, kp], q)
        deltas = [qt - (2 * kp + i) for i in range(2)]
        mask = jnp.concatenate([mask_fn(2 * kp + i, deltas[i]) for i in range(2)], axis=0)
        m_out, l_out, alphas, ps = [], [], [], []
        for r in range(ngrp):
            ls = slice(r * QT, (r + 1) * QT)
            bias = jnp.concatenate([toep_ref[0, r, jnp.maximum(d, 0)] for d in deltas], axis=0)
            s = jnp.where(mask, s_all[:, ls] + bias, NEG)
            m_new = jnp.maximum(m[:, ls], jnp.max(s, axis=0, keepdims=True))
            alpha = jnp.exp(m[:, ls] - m_new)
            p = jnp.exp(s - m_new)
            m_out.append(m_new)
            l_out.append(alpha * l[:, ls] + jnp.sum(p, axis=0, keepdims=True))
            alphas.append(alpha)
            ps.append(p.astype(BF16))
        cat = lambda xs: jnp.concatenate(xs, axis=1)
        acc = cat(alphas) * acc + _dot(vT_ref[0, 0, kp], cat(ps))
        return cat(m_out), cat(l_out), acc

    def sel_mask(kt, delta):
        rows = [jnp.broadcast_to(sel_ref[pl.ds(kt * blocks_per_tile + i, 1), :], (SEL_BLOCK, QT))
                for i in range(blocks_per_tile)]
        chosen = jnp.concatenate(rows, axis=0) > 0.5
        return chosen & (ik - iqq <= delta * QT)

    def win_mask(kt, delta):
        dist = delta * QT + iqq - ik
        return (dist >= 0) & (dist < WINDOW)

    init = (jnp.full((1, lanes), NEG, F32), jnp.zeros((1, lanes), F32), jnp.zeros((HEAD_DIM, lanes), F32))

    def merge(states):
        m = states[0][0]
        for st in states[1:]:
            m = jnp.maximum(m, st[0])
        l, acc = None, None
        for m_i, l_i, acc_i in states:
            w = jnp.exp(m_i - m)
            l = w * l_i if l is None else l + w * l_i
            acc = w * acc_i if acc is None else acc + w * acc_i
        return acc * jnp.where(l > 0.0, 1.0 / jnp.where(l > 0.0, l, 1.0), 0.0)

    diag = qt // 2
    def sel_body(i, carry):
        return tuple(attend(SEL_CHAINS * i + c, carry[c], ks_ref, vsT_ref, sel_mask) for c in range(SEL_CHAINS))

    o_s = merge(lax.fori_loop(0, (diag + SEL_CHAINS) // SEL_CHAINS, sel_body, (init,) * SEL_CHAINS))
    first = jnp.maximum(qt - WINDOW // QT, 0) // 2
    states = []
    for i in range((WINDOW // QT) // 2 + 1):
        kp = diag - i
        live = kp >= first
        states.append(attend(jnp.maximum(kp, 0), init, kw_ref, vwT_ref,
                             lambda kt, delta, live=live: win_mask(kt, delta) & live))
    o_w = merge(states)
    g = gT_ref[0, 0, 0]
    o_ref[0, 0, 0] = g[0:1, :] * o_c + g[1:2, :] * o_s + g[2:3, :] * o_w


def _nsa_prompt(p_n, W, b, t, nkv, ngrp, tm):
    nsa_w = nkv * ngrp * HEAD_DIM
    kvw = nkv * HEAD_DIM
    qn, rows2d, win2d, gates = _nsa_project_call(p_n, W['qk_norm_g'], nsa_w, kvw, tm)
    rows = rows2d.reshape(b, t, 4, nkv, HEAD_DIM)
    win = win2d.reshape(b, t, 2, nkv, HEAD_DIM)
    ns = t // CMP_STRIDE
    nc = ns - CMP_LEN // CMP_STRIDE + 1
    nb = t // SEL_BLOCK
    n_sel = min(SEL_TOPK, nb)
    nqt = t // QT
    parts = _cmp_parts_prompt(rows2d.reshape(b, t, 4 * kvw), _cmp_first_weights(W['cmp_w1'], nkv), b, t, kvw)
    kvc = _cmp_finish([parts], ns, W['cmp_b1'], W['cmp_w2'], W['cmp_b2'], W['qk_norm_g'][1], nc, nkv)
    kc = kvc[:, :nkv]
    vcT = kvc[:, nkv:].transpose(0, 1, 3, 2)
    qT = qn.reshape(b, nqt, QT, nkv, ngrp, HEAD_DIM).transpose(0, 3, 1, 5, 4, 2).reshape(b, nkv, nqt, HEAD_DIM, ngrp * QT)
    ng = 3 * nkv * ngrp
    gT = gates[:, :ng].reshape(b, nqt, QT, nkv, ngrp, 3).transpose(0, 3, 1, 5, 4, 2).reshape(b, nkv, nqt, 3, ngrp * QT)
    gT = jnp.pad(gT, ((0, 0), (0, 0), (0, 0), (0, 5), (0, 0)))
    assert nqt % (2 * SEL_CHAINS) == 0 and (WINDOW // QT) % 2 == 0
    npair = nqt // 2
    k_tiles = lambda x: x.transpose(0, 2, 1, 3).reshape(b, nkv, npair, 2 * QT, HEAD_DIM).astype(BF16)
    vT_tiles = lambda x: x.reshape(b, npair, 2 * QT, nkv, HEAD_DIM).transpose(0, 3, 1, 4, 2).astype(BF16)
    ks, vsT = k_tiles(rows[:, :, 2]), vT_tiles(rows[:, :, 3])
    kw, vwT = k_tiles(win[:, :, 0]), vT_tiles(win[:, :, 1])
    table = W['rel_bias'].astype(F32)
    c_end = jnp.arange(ns) * CMP_STRIDE + CMP_LEN - 1
    bias_c = _rel_table_np_dist(jnp.arange(t)[None, :] - c_end[:, None], table)
    bias_c = bias_c.transpose(2, 0, 1).reshape(nkv, ngrp, ns, t)
    dd = (jnp.arange(nqt)[:, None, None] * QT + jnp.arange(QT)[None, None, :] - jnp.arange(QT)[None, :, None])
    toep = _rel_table_np_dist(dd, table).transpose(3, 0, 1, 2).reshape(nkv, ngrp, nqt, QT, QT)
    lanes = ngrp * QT
    kv_spec = lambda shape: pl.BlockSpec((1, 1) + shape, lambda i, g, j: (i, g) + (0,) * len(shape))
    yT = pl.pallas_call(
        functools.partial(_nsa_prompt_kernel, nc=nc, nb=nb, n_sel=n_sel, ngrp=ngrp),
        grid=(b, nkv, nqt),
        in_specs=[pl.BlockSpec((1, 1, 1, HEAD_DIM, lanes), lambda i, g, j: (i, g, j, 0, 0)),
                  pl.BlockSpec((1, 1, 1, 8, lanes), lambda i, g, j: (i, g, j, 0, 0)),
                  kv_spec((ns, HEAD_DIM)), kv_spec((HEAD_DIM, ns)),
                  kv_spec((npair, 2 * QT, HEAD_DIM)), kv_spec((npair, HEAD_DIM, 2 * QT)),
                  kv_spec((npair, 2 * QT, HEAD_DIM)), kv_spec((npair, HEAD_DIM, 2 * QT)),
                  pl.BlockSpec((1, ngrp, ns, QT), lambda i, g, j: (g, 0, 0, j)),
                  pl.BlockSpec((1, ngrp, nqt, QT, QT), lambda i, g, j: (g, 0, 0, 0, 0))],
        out_specs=pl.BlockSpec((1, 1, 1, HEAD_DIM, lanes), lambda i, g, j: (i, g, j, 0, 0)),
        out_shape=jax.ShapeDtypeStruct((b, nkv, nqt, HEAD_DIM, lanes), F32),
        scratch_shapes=[pltpu.VMEM((ns + 16, QT), F32), pltpu.VMEM((_rup(nb, 8), QT), F32),
                        pltpu.VMEM((_rup(nb, 8), QT), F32)],
        compiler_params=_cparams(("arbitrary", "arbitrary", "arbitrary")),
        name="nsa_prompt_attn",
    )(qT, gT, kc, vcT, ks, vsT, kw, vwT, bias_c, toep)
    y = yT.reshape(b, nkv, nqt, HEAD_DIM, ngrp, QT).transpose(0, 2, 5, 1, 4, 3).reshape(b * t, nsa_w)
    return y, rows, win[:, t - min(WINDOW, t):]


def _cmp_part_paged_kernel(*refs, npg, kvw):
    x_refs, w_ref, o_ref = refs[1:1 + npg], refs[1 + npg], refs[2 + npg]
    rows = x_refs[0].shape[2]
    sub = rows // CMP_STRIDE
    ri = lax.broadcasted_iota(jnp.int32, (rows, rows), 0)
    ci = lax.broadcasted_iota(jnp.int32, (rows, rows), 1)
    perm = jnp.where(ci == (ri % sub) * CMP_STRIDE + ri // sub, 1.0, 0.0).astype(BF16)
    xp = [_dot_nt(perm, x[0].astype(BF16)) for x in x_refs]
    for typ in range(2):
        acc = None
        for s in range(CMP_STRIDE):
            xs = jnp.concatenate([p[s * sub:(s + 1) * sub, typ * kvw:(typ + 1) * kvw] for p in xp], axis=0)
            d = _dot(xs.astype(BF16), w_ref[typ, s])
            acc = d if acc is None else acc + d
        o_ref[0, :, typ * 2 * kvw:(typ + 1) * 2 * kvw] = acc


def _cmp_parts_sample(cache_pages, page_table, w_big, kvw, npg):
    b, n_pages = page_table.shape
    rows = cache_pages.shape[2]
    sub = rows // CMP_STRIDE
    in_specs = [pl.BlockSpec((1, 2 * kvw, rows), (lambda i, j, pt, k=k: (pt[i, j * npg + k], 0, 0)))
                for k in range(npg)]
    in_specs.append(pl.BlockSpec(w_big.shape, lambda i, j, pt: (0, 0, 0, 0)))
    return pl.pallas_call(
        functools.partial(_cmp_part_paged_kernel, npg=npg, kvw=kvw),
        grid_spec=pltpu.PrefetchScalarGridSpec(
            num_scalar_prefetch=1, grid=(b, n_pages // npg), in_specs=in_specs,
            out_specs=pl.BlockSpec((1, npg * sub, 4 * kvw), lambda i, j, pt: (i, j, 0))),
        out_shape=jax.ShapeDtypeStruct((b, n_pages * sub, 4 * kvw), F32),
        compiler_params=_cparams(("arbitrary", "arbitrary")),
        name="cmp_parts_paged",
    )(page_table, *([cache_pages] * npg), w_big)


def _inv_pos(l):
    return jnp.where(l > 0.0, 1.0 / jnp.where(l > 0.0, l, 1.0), 0.0)


def _nsa_sample_kernel(*refs, npg, n_steps, nc, nb, n_sel, past, ds, keep, ngrp, n_lanes):
    pt_ref = refs[0]
    q_ref, g_ref, kc_ref, vcT_ref, bc_ref, win_ref, wnew_ref, bw_ref, rnew_ref, bs_ref = refs[1:11]
    page_refs = refs[11:11 + npg]
    o_ref = refs[11 + npg]
    (m_ref, l_ref, acc_ref, base_ref, sel_ref, pg_ref, score_ref,
     kw_ref, kn_ref, vn_ref) = refs[12 + npg:]
    del pt_ref
    j = pl.program_id(1)
    scale = HEAD_DIM ** -0.5
    kvw = q_ref.shape[1]
    eye = jnp.where(lax.broadcasted_iota(jnp.int32, (LANES, LANES), 0)
                    == lax.broadcasted_iota(jnp.int32, (LANES, LANES), 1), 1.0, 0.0).astype(BF16)
    qbd = q_ref[0]
    lane = lax.broadcasted_iota(jnp.int32, (1, LANES), 1)
    qi = (lane // ngrp) % ds
    q_pos = past + qi
    n_pages = npg * n_steps

    @pl.when(j == 0)
    def _():
        nsp = kc_ref.shape[1]
        s = _dot(kc_ref[0].astype(BF16), qbd) * scale + bc_ref[...]
        crow = lax.broadcasted_iota(jnp.int32, (nsp, LANES), 0)
        ok = (crow * CMP_STRIDE + (CMP_LEN - 1) <= q_pos) & (crow < nc)
        p, _, _, l = _softmax_update(s, ok, jnp.full((1, LANES), NEG, F32), jnp.zeros((1, LANES), F32))
        p = p * _inv_pos(l)
        o_c = _dot(vcT_ref[0].astype(BF16), p.astype(BF16))
        li = lax.broadcasted_iota(jnp.int32, (LANES, LANES), 0)
        lj = lax.broadcasted_iota(jnp.int32, (LANES, LANES), 1)
        fold = jnp.where((li // ngrp == lj) & (li < n_lanes), 1.0, 0.0)
        p_grp = _dot(p, fold, HI)
        pad = 8
        pg_ref[...] = jnp.zeros(pg_ref.shape, F32)
        pg_ref[pad:pad + nsp, :] = p_grp
        nbp = sel_ref.shape[0]
        r1 = SEL_BLOCK // CMP_STRIDE
        offs, wts = _slc_offsets()
        p_slc = None
        for o, wt in zip(offs, wts):
            term = float(wt) * pg_ref[pl.ds(pad + int(o), nbp, stride=r1), :]
            p_slc = term if p_slc is None else p_slc + term
        cur = (past + lane % ds) // SEL_BLOCK
        jrow = lax.broadcasted_iota(jnp.int32, (nbp, LANES), 0)
        forced = (jrow == 0) | (jrow == cur) | (jrow == cur - 1)
        score = jnp.where((jrow > cur) | (jrow >= nb), -1.0, jnp.where(forced, 1e6, p_slc))
        selg = _rank_select(score, score_ref, cur, n_sel)
        unfold = jnp.where((li == lj // ngrp) & (lj < n_lanes), 1.0, 0.0)
        sel_ref[...] = _dot(selg, unfold)
        wk = kw_ref.shape[0]
        nn = wnew_ref.shape[1]
        kw_ref[...] = jnp.zeros(kw_ref.shape, F32)
        for c0 in range(0, keep, LANES):
            kw_ref[c0:c0 + LANES, :] = _dot_nt(eye, win_ref[0, 0:kvw, c0:c0 + LANES].astype(BF16))
        kw_ref[keep:keep + nn, :] = wnew_ref[0, :, 0:kvw]
        vn_ref[...] = jnp.zeros(vn_ref.shape, F32)
        vn_ref[0:nn, :] = wnew_ref[0, :, kvw:2 * kvw]
        s = _dot(kw_ref[...].astype(BF16), qbd) * scale + bw_ref[...]
        irow = lax.broadcasted_iota(jnp.int32, (wk, LANES), 0)
        dist = jnp.where(irow < keep, keep + qi - irow, qi - (irow - keep))
        ok = (dist >= 0) & (dist < WINDOW) & (irow < keep + ds)
        p, _, _, l = _softmax_update(s, ok, jnp.full((1, LANES), NEG, F32), jnp.zeros((1, LANES), F32))
        p = (p * _inv_pos(l)).astype(BF16)
        o_w = (_dot(win_ref[0, kvw:2 * kvw, :].astype(BF16), p[0:keep, :])
               + _dot(vn_ref[...].T.astype(BF16), p[keep:keep + PAGE_SIZE, :]))
        g = g_ref[0]
        base_ref[...] = g[0:1, :] * o_c + g[2:3, :] * o_w
        m_ref[...] = jnp.full(m_ref.shape, NEG, F32)
        l_ref[...] = jnp.zeros(l_ref.shape, F32)
        acc_ref[...] = jnp.zeros(acc_ref.shape, F32)
        kn_ref[...] = jnp.zeros(kn_ref.shape, F32)
        vn_ref[...] = jnp.zeros(vn_ref.shape, F32)
        kn_ref[0:nn, :] = rnew_ref[0, :, 2 * kvw:3 * kvw]
        vn_ref[0:nn, :] = rnew_ref[0, :, 3 * kvw:4 * kvw]

    ik = lax.broadcasted_iota(jnp.int32, (PAGE_SIZE, LANES), 0)
    blocks_per_page = PAGE_SIZE // SEL_BLOCK

    def pages_update(k, v_t, first_page, count):
        bias = jnp.concatenate([bs_ref[first_page + i] for i in range(count)], axis=0)
        s = _dot(k, qbd) * scale + bias
        rows = [jnp.broadcast_to(sel_ref[pl.ds(first_page * blocks_per_page + i, 1), :], (SEL_BLOCK, LANES))
                for i in range(count * blocks_per_page)]
        key_pos = first_page * PAGE_SIZE + lax.broadcasted_iota(jnp.int32, (count * PAGE_SIZE, LANES), 0)
        mask = (jnp.concatenate(rows, axis=0) > 0.5) & (key_pos <= q_pos)
        p, m_new, alpha, l_new = _softmax_update(s, mask, m_ref[...], l_ref[...])
        m_ref[...] = m_new
        l_ref[...] = l_new
        acc_ref[...] = alpha * acc_ref[...] + _dot(v_t, p.astype(BF16))

    k_rows = jnp.concatenate([_dot_nt(eye, blk[0, 0:kvw, :].astype(BF16)) for blk in page_refs], axis=0)
    v_cols = jnp.concatenate([blk[0, kvw:2 * kvw, :].astype(BF16) for blk in page_refs], axis=1)
    pages_update(k_rows.astype(BF16), v_cols, j * npg, npg)

    @pl.when(j == n_steps - 1)
    def _():
        pages_update(kn_ref[...].astype(BF16), vn_ref[...].T.astype(BF16), n_pages, 1)
        o_ref[0] = base_ref[...] + g_ref[0][1:2, :] * (acc_ref[...] * _inv_pos(l_ref[...]))


def _nsa_sample(p_n, cache_kv, cache_win, page_table, W, db, ds, nkv, ngrp):
    nsa_w = nkv * ngrp * HEAD_DIM
    kvw = nkv * HEAD_DIM
    row_w = 4 * kvw
    qn, rows2d, win2d, gates = _nsa_project_call(p_n, W['qk_norm_g'], nsa_w, kvw, db * ds)
    rows_new = rows2d.reshape(db, ds, 4, nkv, HEAD_DIM)
    win_new = win2d.reshape(db, ds, 2, nkv, HEAD_DIM)
    n_pool = cache_kv.shape[0]
    n_pages = page_table.shape[1]
    past = n_pages * PAGE_SIZE
    keep = cache_win.shape[1]
    tot = past + _rup(ds, SEL_BLOCK)
    ns = tot // CMP_STRIDE
    nc = ns - CMP_LEN // CMP_STRIDE + 1
    nb = tot // SEL_BLOCK
    n_sel = min(SEL_TOPK, nb)
    nsp = _rup(ns, LANES)
    nbp = _rup(nb, 8)
    n_lanes = nkv * ds * ngrp
    assert n_lanes <= LANES and ds <= 8
    w_big = _cmp_first_weights(W['cmp_w1'], nkv)
    cache_pages = cache_kv.transpose(0, 2, 3, 4, 1).reshape(n_pool, row_w, PAGE_SIZE)
    npg = min(16, n_pages)
    parts_past = _cmp_parts_sample(cache_pages, page_table, w_big, kvw, npg)
    rows_pad = jnp.pad(rows2d.reshape(db, ds, row_w), ((0, 0), (0, PAGE_SIZE - ds), (0, 0)))
    parts_new = _cmp_parts_prompt(rows_pad, w_big, db, PAGE_SIZE, kvw)
    kvc = _cmp_finish([parts_past, parts_new], nsp, W['cmp_b1'], W['cmp_w2'], W['cmp_b2'], W['qk_norm_g'][1], nc, nkv)
    kc_cat = kvc[:, :nkv].transpose(0, 2, 1, 3).reshape(db, nsp, kvw)
    vcT_cat = kvc[:, nkv:].transpose(0, 1, 3, 2).reshape(db, kvw, nsp)
    lane_pad = LANES - n_lanes
    q5 = qn.reshape(db, ds, nkv, ngrp, HEAD_DIM)
    qbd = jnp.einsum('bqgrd,gk->bgdkqr', q5, jnp.eye(nkv, dtype=F32)).reshape(db, kvw, n_lanes)
    qbd = jnp.pad(qbd, ((0, 0), (0, 0), (0, lane_pad))).astype(BF16)
    ng = 3 * nkv * ngrp
    gT = gates[:, :ng].reshape(db, ds, nkv, ngrp, 3).transpose(0, 4, 2, 1, 3).reshape(db, 3, n_lanes)
    gT = jnp.pad(gT, ((0, 0), (0, 5), (0, lane_pad)))
    lane = np.arange(LANES)
    live = lane < n_lanes
    head_of_lane = np.where(live, (lane // (ds * ngrp)) * ngrp + lane % ngrp, 0)
    qi = np.where(live, (lane // ngrp) % ds, 0)
    table_l = W['rel_bias'].astype(F32)[:, head_of_lane]
    bias_of = lambda dist: _rel_table_per_lane(dist, table_l)
    c_end = np.arange(nsp) * CMP_STRIDE + CMP_LEN - 1
    bias_c = bias_of(jnp.asarray(past + qi[None, :] - c_end[:, None], jnp.int32))
    pos = np.arange((n_pages + 1) * PAGE_SIZE)
    bias_s = bias_of(jnp.asarray(past + qi[None, :] - pos[:, None], jnp.int32)).reshape(n_pages + 1, PAGE_SIZE, LANES)
    wk = _rup(keep + 8, LANES)
    irow = np.arange(wk)[:, None]
    dist_w = np.where(irow < keep, keep + qi[None, :] - irow, qi[None, :] - (irow - keep))
    bias_w = bias_of(jnp.asarray(dist_w, jnp.int32))
    win_c = cache_win.transpose(0, 2, 3, 4, 1).reshape(db, 2 * kvw, keep)
    wnew8 = jnp.pad(win2d.reshape(db, ds, 2 * kvw), ((0, 0), (0, 8 - ds), (0, 0)))
    rnew8 = jnp.pad(rows2d.reshape(db, ds, row_w), ((0, 0), (0, 8 - ds), (0, 0)))
    n_steps = n_pages // npg
    per_b = lambda shape: pl.BlockSpec((1,) + shape, lambda i, j, pt: (i,) + (0,) * len(shape))
    const = lambda shape: pl.BlockSpec(shape, lambda i, j, pt: (0,) * len(shape))
    in_specs = [per_b((kvw, LANES)), per_b((8, LANES)), per_b((nsp, kvw)), per_b((kvw, nsp)), const((nsp, LANES)),
                per_b((2 * kvw, keep)), per_b((8, 2 * kvw)), const((wk, LANES)), per_b((8, row_w)),
                const((n_pages + 1, PAGE_SIZE, LANES))]
    in_specs += [pl.BlockSpec((1, 2 * kvw, PAGE_SIZE), (lambda i, j, pt, k=k: (pt[i, j * npg + k], 1, 0)))
                 for k in range(npg)]
    yT = pl.pallas_call(
        functools.partial(_nsa_sample_kernel, npg=npg, n_steps=n_steps, nc=nc, nb=nb, n_sel=n_sel, past=past,
                          ds=ds, keep=keep, ngrp=ngrp, n_lanes=n_lanes),
        grid_spec=pltpu.PrefetchScalarGridSpec(
            num_scalar_prefetch=1, grid=(db, n_steps), in_specs=in_specs,
            out_specs=pl.BlockSpec((1, kvw, LANES), lambda i, j, pt: (i, 0, 0)),
            scratch_shapes=[pltpu.VMEM((1, LANES), F32), pltpu.VMEM((1, LANES), F32), pltpu.VMEM((kvw, LANES), F32),
                            pltpu.VMEM((kvw, LANES), F32), pltpu.VMEM((nbp, LANES), F32),
                            pltpu.VMEM((nsp + 16, LANES), F32), pltpu.VMEM((nbp, LANES), F32),
                            pltpu.VMEM((wk, kvw), F32),
                            pltpu.VMEM((PAGE_SIZE, kvw), F32), pltpu.VMEM((PAGE_SIZE, kvw), F32)]),
        out_shape=jax.ShapeDtypeStruct((db, kvw, LANES), F32),
        compiler_params=_cparams(("arbitrary", "arbitrary")),
        name="nsa_sample_attn",
    )(page_table, qbd, gT, kc_cat, vcT_cat, bias_c, win_c, wnew8, bias_w, rnew8, bias_s, *([cache_pages] * npg))
    y6 = yT[:, :, :n_lanes].reshape(db, nkv, HEAD_DIM, nkv, ds, ngrp)
    y = jnp.einsum('bgdgqr->bqgrd', y6).reshape(db * ds, nsa_w)
    win_all = jnp.concatenate([cache_win, win_new.astype(cache_win.dtype)], axis=1)
    n_keep = min(WINDOW, past + ds)
    return y, rows_new, win_all[:, win_all.shape[1] - n_keep:]


def _rms(x, g):
    return x * lax.rsqrt(jnp.mean(x * x, axis=-1, keepdims=True) + RMS_EPS) * g


def _masked_softmax(logits, mask):
    z = jnp.where(mask, logits, NEG)
    return jax.nn.softmax(z, axis=-1) * jnp.any(mask, axis=-1, keepdims=True)


def _rel_bucket(dist):
    d = jnp.maximum(dist, 0)
    exact = REL_BUCKETS // 2
    ratio = jnp.maximum(d, exact).astype(F32) / exact
    large = exact + (jnp.log(ratio) / math.log(REL_MAX_DIST / exact) * (REL_BUCKETS - exact)).astype(jnp.int32)
    return jnp.where(d < exact, d, jnp.minimum(large, REL_BUCKETS - 1))


def _compress(k, w1, b1, w2, b2):
    b, l, g, dk = k.shape
    ns = l // CMP_STRIDE
    r2 = CMP_LEN // CMP_STRIDE
    nc = ns - r2 + 1
    kb = k[:, :ns * CMP_STRIDE].reshape(b, ns, CMP_STRIDE, g, dk)
    w1r = w1.reshape(r2, CMP_STRIDE, dk, -1)
    part = jnp.einsum('bnsgd,hsde->hbnge', kb, w1r)
    hid = part[0, :, :nc] + b1
    for h in range(1, r2):
        hid = hid + part[h, :, h:h + nc]
    return jnp.einsum('bnge,ed->bngd', jax.nn.gelu(hid), w2) + b2


def _to_blocks(k):
    b, l, g, dk = k.shape
    nb = l // SEL_BLOCK
    return k.reshape(b, nb, SEL_BLOCK, g, dk).transpose(0, 3, 1, 2, 4).reshape(b, g, nb, SEL_BLOCK * dk)


def _slc_offsets():
    r1 = SEL_BLOCK // CMP_STRIDE
    r2 = CMP_LEN // CMP_STRIDE
    offs = np.arange(-(r2 - 1), r1)
    wts = np.array([sum(1 for m in range(r1) for n in range(r2) if m - n == o) for o in offs], np.float32)
    return offs, wts


def _nsa_project(p_n, qk_norm_g, nkv, ngrp):
    b, t, _ = p_n.shape
    nsa_w = nkv * ngrp * HEAD_DIM
    kvw = nkv * HEAD_DIM
    q = _rms(p_n[..., :nsa_w].reshape(b, t, nkv, ngrp, HEAD_DIM), qk_norm_g[0])
    kv = p_n[..., nsa_w:nsa_w + 6 * kvw].reshape(b, t, 6, nkv, HEAD_DIM)
    gates = jax.nn.sigmoid(p_n[..., nsa_w + 6 * kvw:nsa_w + 6 * kvw + 3 * nkv * ngrp]).reshape(b, t, nkv, ngrp, 3)
    rows = jnp.stack([kv[:, :, 0], kv[:, :, 1], _rms(kv[:, :, 2], qk_norm_g[2]), kv[:, :, 3]], axis=2)
    win = jnp.stack([_rms(kv[:, :, 4], qk_norm_g[3]), kv[:, :, 5]], axis=2)
    return q, gates, rows, win


def _nsa_context(rows, W):
    kc = _rms(_compress(rows[:, :, 0], W['cmp_w1'][0], W['cmp_b1'][0], W['cmp_w2'][0], W['cmp_b2'][0]), W['qk_norm_g'][1])
    vc = _compress(rows[:, :, 1], W['cmp_w1'][1], W['cmp_b1'][1], W['cmp_w2'][1], W['cmp_b2'][1])
    c_end = jnp.arange(kc.shape[1]) * CMP_STRIDE + CMP_LEN - 1
    return kc, vc, c_end, _to_blocks(rows[:, :, 2]), _to_blocks(rows[:, :, 3])


def _nsa_core(q, gates, q_pos, kc, vc, c_end, ks_blk, vs_blk, kw, vw, kw_pos, rel_bias):
    b, tq, g, r, dk = q.shape
    nb = ks_blk.shape[2]
    nc = kc.shape[1]
    scale = HEAD_DIM ** -0.5
    table = rel_bias.astype(F32).reshape(REL_BUCKETS, g, r)
    dist_c = q_pos[:, None] - c_end[None, :]
    bias_c = table[_rel_bucket(dist_c)].transpose(2, 3, 0, 1)
    lg_c = jnp.einsum('bqgrd,bcgd->bgrqc', q, kc) * scale + bias_c
    p_c = _masked_softmax(lg_c, dist_c >= 0)
    o_c = jnp.einsum('bgrqc,bcgd->bqgrd', p_c, vc)
    offs, wts = _slc_offsets()
    jb = jnp.arange(nb)
    cidx = (SEL_BLOCK // CMP_STRIDE) * jb[:, None] + offs[None, :]
    cval = (cidx >= 0) & (cidx < nc)
    p_grp = jnp.sum(p_c, axis=2)
    p_slc = jnp.sum(p_grp[..., jnp.clip(cidx, 0, nc - 1)] * (wts * cval), axis=-1)
    cur = q_pos // SEL_BLOCK
    forced = (jb[None] == 0) | (jb[None] == cur[:, None]) | (jb[None] == cur[:, None] - 1)
    future = jb[None] > cur[:, None]
    score = jnp.where(future, -1.0, jnp.where(forced, 1e6, p_slc))
    n_sel = min(SEL_TOPK, nb)
    _, sel = lax.top_k(score, n_sel)
    sel_ok = sel <= cur[:, None]
    bi = jnp.arange(b)[:, None, None]
    gi = jnp.arange(g)[None, :, None]
    flat = sel.reshape(b, g, tq * n_sel)
    ksg = ks_blk[bi, gi, flat].reshape(b, g, tq, n_sel * SEL_BLOCK, dk)
    vsg = vs_blk[bi, gi, flat].reshape(b, g, tq, n_sel * SEL_BLOCK, dk)
    pos4 = sel[..., None] * SEL_BLOCK + jnp.arange(SEL_BLOCK)
    ok_s = (sel_ok[..., None] & (pos4 <= q_pos[:, None, None])).reshape(b, g, tq, n_sel * SEL_BLOCK)
    pos_s = pos4.reshape(b, g, tq, n_sel * SEL_BLOCK)
    tg = table.transpose(1, 0, 2)
    bias_s = tg[jnp.arange(g)[None, :, None, None], _rel_bucket(q_pos[:, None] - pos_s)].transpose(0, 1, 4, 2, 3)
    lg_s = jnp.einsum('bqgrd,bgqkd->bgrqk', q, ksg) * scale + bias_s
    p_s = _masked_softmax(lg_s, ok_s[:, :, None])
    o_s = jnp.einsum('bgrqk,bgqkd->bqgrd', p_s, vsg)
    dist_w = q_pos[:, None] - kw_pos[None, :]
    ok_w = (dist_w >= 0) & (dist_w < WINDOW) & (kw_pos[None, :] >= 0)
    bias_w = table[_rel_bucket(dist_w)].transpose(2, 3, 0, 1)
    lg_w = jnp.einsum('bqgrd,bkgd->bgrqk', q, kw) * scale + bias_w
    p_w = _masked_softmax(lg_w, ok_w)
    o_w = jnp.einsum('bgrqk,bkgd->bqgrd', p_w, vw)
    o = gates[..., 0:1] * o_c + gates[..., 1:2] * o_s + gates[..., 2:3] * o_w
    return o.reshape(b, tq, g * r * dk)


def _nsa_prompt_jnp(p_n, W, nkv, ngrp):
    q, gates, rows, win = _nsa_project(p_n, W['qk_norm_g'], nkv, ngrp)
    b, s = q.shape[:2]
    kc, vc, c_end, ks_blk, vs_blk = _nsa_context(rows, W)
    win_pad = jnp.pad(win, ((0, 0), (WINDOW, 0), (0, 0), (0, 0), (0, 0)))
    qb_sz = 128

    def block(i):
        start = i * qb_sz
        qb = lax.dynamic_slice_in_dim(q, start, qb_sz, axis=1)
        gb = lax.dynamic_slice_in_dim(gates, start, qb_sz, axis=1)
        wb = lax.dynamic_slice_in_dim(win_pad, start, WINDOW + qb_sz, axis=1)
        q_pos = start + jnp.arange(qb_sz)
        kw_pos = start - WINDOW + jnp.arange(WINDOW + qb_sz)
        return _nsa_core(qb, gb, q_pos, kc, vc, c_end, ks_blk, vs_blk, wb[:, :, 0], wb[:, :, 1], kw_pos, W['rel_bias'])

    o = lax.map(block, jnp.arange(s // qb_sz))
    o = o.transpose(1, 0, 2, 3).reshape(b, s, -1)
    return o, rows, win[:, s - min(WINDOW, s):]


def _nsa_sample_jnp(p_n, cache_kv, cache_win, page_table, W, nkv, ngrp, db):
    p_n = p_n.reshape(db, p_n.shape[0] // db, -1)
    q, gates, rows_new, win_new = _nsa_project(p_n, W['qk_norm_g'], nkv, ngrp)
    db, ds = q.shape[:2]
    past_len = page_table.shape[1] * PAGE_SIZE
    past_rows = cache_kv[page_table].reshape(db, past_len, 4, nkv, HEAD_DIM)
    pad = (-ds) % SEL_BLOCK
    rows = jnp.concatenate([past_rows, jnp.pad(rows_new, ((0, 0), (0, pad), (0, 0), (0, 0), (0, 0)))], axis=1)
    kc, vc, c_end, ks_blk, vs_blk = _nsa_context(rows, W)
    keep = cache_win.shape[1]
    win_all = jnp.concatenate([cache_win, win_new], axis=1)
    q_pos = past_len + jnp.arange(ds)
    kw_pos = past_len - keep + jnp.arange(keep + ds)
    o = _nsa_core(q, gates, q_pos, kc, vc, c_end, ks_blk, vs_blk, win_all[:, :, 0], win_all[:, :, 1], kw_pos, W['rel_bias'])
    n_keep = min(WINDOW, past_len + ds)
    return o.reshape(db * ds, -1), rows_new, win_all[:, win_all.shape[1] - n_keep:]


def _top_values(x, k):
    n = x.shape[0]
    row = lax.broadcasted_iota(jnp.int32, x.shape, 0)
    vals = []
    for _ in range(k):
        m = jnp.max(x, axis=0, keepdims=True)
        vals.append(m)
        first = jnp.min(jnp.where(x == m, row, n), axis=0, keepdims=True)
        x = jnp.where(row == first, -jnp.inf, x)
    return vals


def _peer_route_kernel(hT_ref, wq_ref, sk_ref, th_ref, c1_ref, s2_ref, e2_ref, s1_ref, *, nheads, topk):
    nk, half = sk_ref.shape[1], sk_ref.shape[2]
    cpt = s2_ref.shape[1]
    qT = _dot(wq_ref[...], hT_ref[...])
    for h in range(nheads):
        base = h * 2 * half
        s1 = _dot(sk_ref[0].astype(BF16), qT[base:base + half].astype(BF16))
        s2 = _dot(sk_ref[1].astype(BF16), qT[base + half:base + 2 * half].astype(BF16))
        for c in range(cpt):
            s1_ref[h, c] = s1[:, c * LANES:(c + 1) * LANES]
            s2_ref[h, c] = s2[:, c * LANES:(c + 1) * LANES]

    pairs = [(a, b) for a in range(topk) for b in range(topk) if (a + 1) * (b + 1) <= topk]

    def chunk(it, carry):
        h = it // cpt
        c = it % cpt
        s1 = s1_ref[h, c]
        s2 = s2_ref[h, c]
        v1 = _top_values(s1, topk)
        v2 = _top_values(s2, topk)
        sums = [v1[a] + v2[b] for a, b in pairs]
        cand = jnp.concatenate(sums, axis=0)
        tau = _top_values(cand, topk)[-1]
        z = jnp.sum(jnp.where(cand >= tau, jnp.exp(cand - (v1[0] + v2[0])), 0.0), axis=0, keepdims=True)
        theta = jnp.full(s1.shape, jnp.inf, F32)
        for a in range(topk):
            th_a = jnp.full(tau.shape, jnp.inf, F32)
            for (pa, pb), sm in zip(pairs, sums):
                if pa == a:
                    th_a = jnp.where(sm >= tau, v2[pb], th_a)
            theta = jnp.where(s1 == v1[a], th_a, theta)
        th_ref[h, c] = theta
        c1_ref[h, c] = jnp.exp(s1 - v1[0]) / z
        e2_ref[h, c] = jnp.exp(s2 - v2[0])
        return carry

    lax.fori_loop(0, nheads * cpt, chunk, 0)


def _peer_expert_kernel(hT_ref, x1_ref, ga_ref, th_ref, c1_ref, s2_ref, e2_ref, u_ref, v_ref,
                        o_ref, acc_ref, gate_ref, *, nheads, n_eblocks):
    eb = pl.program_id(1)
    n_chunks, nk = s2_ref.shape[1], s2_ref.shape[2]
    rows_per_block = u_ref.shape[0] // nk

    @pl.when(eb == 0)
    def _():
        acc_ref[...] = jnp.zeros(acc_ref.shape, F32)

    def gate_tile(i, c):
        i1 = eb * rows_per_block + i
        wd = None
        for h in range(nheads):
            chosen = s2_ref[h, c] >= th_ref[h, c, pl.ds(i1, 1), :]
            term = jnp.where(chosen, e2_ref[h, c], 0.0) * c1_ref[h, c, pl.ds(i1, 1), :]
            wd = term if wd is None else wd + term
        gate_ref[i, c] = wd

    def gate_step(it, carry):
        gate_tile(it // n_chunks, it % n_chunks)
        return carry

    lax.fori_loop(0, rows_per_block * n_chunks, gate_step, 0)
    act = jax.nn.gelu(_dot(u_ref[...], hT_ref[...]))
    gates = jnp.concatenate([jnp.concatenate([gate_ref[i, c] for c in range(n_chunks)], axis=1)
                             for i in range(rows_per_block)], axis=0)
    acc_ref[...] += _dot((gates * act).T.astype(BF16), v_ref[...])

    @pl.when(eb == n_eblocks - 1)
    def _():
        o_ref[...] = x1_ref[...] + ga_ref[0] * acc_ref[...]


def _peer(h2, x1, ga, W, tm):
    t, d = h2.shape
    sub_keys = W['peer_sub_keys']
    nk, half = sub_keys.shape[1], sub_keys.shape[2]
    qd = W['peer_w_query'].shape[1]
    nheads = qd // (2 * half)
    hT = h2.T
    wqT = W['peer_w_query'].T.astype(BF16)
    cpt = tm // LANES
    route_shape = jax.ShapeDtypeStruct((nheads, t // LANES, nk, LANES), F32)
    rspec = pl.BlockSpec((nheads, cpt, nk, LANES), lambda i: (0, i, 0, 0))
    th, c1, s2, e2 = pl.pallas_call(
        functools.partial(_peer_route_kernel, nheads=nheads, topk=PEER_TOPK),
        grid=(t // tm,),
        in_specs=[pl.BlockSpec((d, tm), lambda i: (0, i)),
                  pl.BlockSpec((qd, d), lambda i: (0, 0)),
                  pl.BlockSpec(sub_keys.shape, lambda i: (0, 0, 0))],
        out_specs=[rspec] * 4,
        out_shape=[route_shape] * 4,
        scratch_shapes=[pltpu.VMEM((nheads, cpt, nk, LANES), F32)],
        compiler_params=_cparams(("arbitrary",)),
        name="peer_route",
    )(hT, wqT, sub_keys)
    eblk = 2 * MXU
    n_eblocks = W['peer_u'].shape[0] // eblk
    nmod, rows, _ = ga.shape
    tiles_per_mod = (t // tm) // nmod
    rspec2 = pl.BlockSpec((nheads, cpt, nk, LANES), lambda i, e: (0, i, 0, 0))
    return pl.pallas_call(
        functools.partial(_peer_expert_kernel, nheads=nheads, n_eblocks=n_eblocks),
        grid=(t // tm, n_eblocks),
        in_specs=[pl.BlockSpec((d, tm), lambda i, e: (0, i)),
                  pl.BlockSpec((tm, d), lambda i, e: (i, 0)),
                  pl.BlockSpec((1, rows, d), lambda i, e: (i // tiles_per_mod, 0, 0)),
                  rspec2, rspec2, rspec2, rspec2,
                  pl.BlockSpec((eblk, d), lambda i, e: (e, 0)),
                  pl.BlockSpec((eblk, d), lambda i, e: (e, 0))],
        out_specs=pl.BlockSpec((tm, d), lambda i, e: (i, 0)),
        out_shape=jax.ShapeDtypeStruct((t, d), F32),
        scratch_shapes=[pltpu.VMEM((tm, d), F32), pltpu.VMEM((eblk // nk, cpt, nk, LANES), F32)],
        compiler_params=_cparams(("arbitrary", "arbitrary")),
        name="peer_experts",
    )(hT, x1, ga, th, c1, s2, e2, W['peer_u_bf16'], W['peer_v_bf16'])


def _peer_jnp(h, W):
    n, d = h.shape
    sub_keys, expert_u, expert_v = W['peer_sub_keys'], W['peer_u'], W['peer_v']
    nk, half = sub_keys.shape[1], sub_keys.shape[2]
    nheads = W['peer_w_query'].shape[1] // (2 * half)
    cs = min(128, n)
    hc = h.reshape(-1, cs, d)

    def chunk(x):
        qh = (x @ W['peer_w_query']).reshape(cs, nheads, 2, half)
        s1 = jnp.einsum('chd,kd->chk', qh[:, :, 0], sub_keys[0])
        s2 = jnp.einsum('chd,kd->chk', qh[:, :, 1], sub_keys[1])
        v1, i1 = lax.top_k(s1, PEER_TOPK)
        v2, i2 = lax.top_k(s2, PEER_TOPK)
        cand = (v1[..., :, None] + v2[..., None, :]).reshape(cs, nheads, -1)
        cidx = (i1[..., :, None] * nk + i2[..., None, :]).reshape(cs, nheads, -1)
        top, pos = lax.top_k(cand, PEER_TOPK)
        eidx = jnp.take_along_axis(cidx, pos, axis=-1)
        gw = jax.nn.softmax(top, axis=-1)
        act = jax.nn.gelu(jnp.einsum('chkd,cd->chk', expert_u[eidx], x))
        return jnp.einsum('chk,chkd->cd', gw * act, expert_v[eidx])

    return lax.map(chunk, hc).reshape(-1, d)


def _layer(x, mods, nsa_fn, shift_prev, wkv0, W, tm, rwkv_chunk):
    b, t, d = x.shape
    cw = W['rwkv_w0'].shape[0]
    nw, na, ng = W['rwkv_w_up'].shape[0], W['rwkv_a_up'].shape[0], W['rwkv_g_up'].shape[0]
    rwkv_proj = 3 * cw + nw + na + ng
    sh1, sc1, ga1, sh2, sc2, ga2 = mods
    xf = x.reshape(b * t, d)
    if (b * t) % tm == 0 and t % tm == 0:
        as_mod = lambda m: m.reshape(b, 1, d)
    else:
        tm = b * t
        as_mod = lambda m: jnp.repeat(m, t, axis=0).reshape(1, b * t, d)
    w_r = _rwkv_pad_cols(W['w_in'][:, :rwkv_proj], cw, nw, na, ng).astype(BF16)
    nsa_cols = W['w_in'].shape[1] - rwkv_proj
    w_n = jnp.pad(W['w_in'][:, rwkv_proj:], ((0, 0), (0, _rup(nsa_cols, LANES) - nsa_cols))).astype(BF16)
    p_r, _ = _norm_mod_matmul(xf, W['norm1_g'], as_mod(sc1), as_mod(sh1), w_r, tm, 512)
    p_n, _ = _norm_mod_matmul(xf, W['norm1_g'], as_mod(sc1), as_mod(sh1), w_n, tm, w_n.shape[1] // 3)
    pr = p_r.shape[1]
    p_r = p_r.reshape(b, t, pr)
    shift_new = _rwkv_unpad_cols(p_r[:, -1], cw, nw, na, ng)
    tpad = _rup(t, rwkv_chunk)
    p_r_pad = jnp.pad(p_r, ((0, 0), (0, tpad - t), (0, 0)))
    y_r, wkv_new = _rwkv_mix(p_r_pad, shift_prev, wkv0, W, rwkv_chunk, t)
    y_r = y_r[:, :t].reshape(b * t, cw)
    y_n, rows, win = nsa_fn(p_n)
    w_out = W['w_out'].astype(BF16)
    x1 = _out_proj(xf, y_r, y_n, as_mod(ga1), w_out[:cw], w_out[cw:], tm, 512)
    h2 = _norm_mod(x1, W['norm2_g'], as_mod(sc2), as_mod(sh2), tm)
    if (b * t) % LANES == 0:
        out = _peer(h2, x1, as_mod(ga2), W, tm)
    else:
        out = x1 + jnp.repeat(ga2, t, axis=0) * _peer_jnp(h2.astype(F32), W)
    return out.reshape(b, t, d), rows, win, wkv_new, shift_new


def kernel(x_prompt, x_sample, c_prompt, c_sample, cache_kv, cache_win, state_wkv, state_shift, page_table,
           norm1_g, norm2_g, w_ada, b_ada, w_in, w_out,
           rwkv_mu, rwkv_w0, rwkv_w_up, rwkv_a0, rwkv_a_up, rwkv_g_up, rwkv_k_k, rwkv_k_a, rwkv_r_k, lnx_w, lnx_b,
           qk_norm_g, cmp_w1, cmp_b1, cmp_w2, cmp_b2, rel_bias,
           peer_w_query, peer_sub_keys, peer_u, peer_v):
    W = dict(norm1_g=norm1_g, norm2_g=norm2_g, w_ada=w_ada, b_ada=b_ada, w_in=w_in, w_out=w_out,
             rwkv_mu=rwkv_mu, rwkv_w0=rwkv_w0, rwkv_w_up=rwkv_w_up, rwkv_a0=rwkv_a0, rwkv_a_up=rwkv_a_up,
             rwkv_g_up=rwkv_g_up, rwkv_k_k=rwkv_k_k, rwkv_k_a=rwkv_k_a, rwkv_r_k=rwkv_r_k, lnx_w=lnx_w, lnx_b=lnx_b,
             qk_norm_g=qk_norm_g, cmp_w1=cmp_w1, cmp_b1=cmp_b1, cmp_w2=cmp_w2, cmp_b2=cmp_b2, rel_bias=rel_bias,
             peer_w_query=peer_w_query, peer_sub_keys=peer_sub_keys, peer_u=peer_u, peer_v=peer_v)
    W['peer_u_bf16'] = peer_u.astype(BF16)
    W['peer_v_bf16'] = peer_v.astype(BF16)
    bp, seq, d = x_prompt.shape
    db = x_sample.shape[0]
    nkv = cache_kv.shape[3]
    nh_r = rwkv_w0.shape[0] // HEAD_DIM
    ngrp = (w_out.shape[0] - rwkv_w0.shape[0]) // HEAD_DIM // nkv

    mods = _ada_mods(jnp.concatenate([c_prompt, c_sample], axis=0), w_ada, b_ada)
    mods = mods.reshape(bp + db, N_MODS, d)
    mods_p = [mods[:bp, i] for i in range(N_MODS)]
    mods_s = [mods[bp:, i] for i in range(N_MODS)]

    shift0 = jnp.zeros((bp, state_shift.shape[1]), F32)
    wkv0 = jnp.zeros((bp, nh_r, HEAD_DIM, HEAD_DIM), F32)
    y_p, rows_p, win_p, wkv_p, shift_p = _layer(
        x_prompt, mods_p, lambda pn: _nsa_prompt(pn, W, bp, seq, nkv, ngrp, 512), shift0, wkv0, W, 512, 64)
    y_s, rows_s, win_s, wkv_s, shift_s = _layer(
        x_sample, mods_s,
        lambda pn: _nsa_sample(pn, cache_kv, cache_win, page_table, W, db, x_sample.shape[1], nkv, ngrp),
        state_shift, state_wkv, W, 512, 32)
    return (y_p, y_s, rows_p, win_p, wkv_p.astype(state_wkv.dtype), shift_p,
            rows_s, win_s, wkv_s.astype(state_wkv.dtype), shift_s)
```

```python
import functools
import math

import numpy as np
import jax
import jax.numpy as jnp
from jax import lax
from jax.experimental import pallas as pl
from jax.experimental.pallas import tpu as pltpu

F32 = jnp.float32
BF16 = jnp.bfloat16
HI = lax.Precision.HIGHEST

HEAD_DIM = 64
PAGE_SIZE = 128
CMP_LEN = 32
CMP_STRIDE = 16
SEL_BLOCK = 64
SEL_TOPK = 16
WINDOW = 512
REL_BUCKETS = 32
REL_MAX_DIST = 2048
PEER_TOPK = 16
N_MODS = 6
RMS_EPS = 1e-6
LNX_EPS = 64e-5
NEG = -1e30

LANES = 128
MXU = 256
HEADS_PER_GROUP = MXU // HEAD_DIM
VMEM_LIMIT = 56 * 1024 * 1024


def _cparams(sem):
    return pltpu.CompilerParams(dimension_semantics=sem, vmem_limit_bytes=VMEM_LIMIT)


def _dot(a, b, precision=None):
    return jnp.dot(a, b, preferred_element_type=F32, precision=precision)


def _dot_nt(a, b, precision=None):
    return lax.dot_general(a, b, (((1,), (1,)), ((), ())), preferred_element_type=F32, precision=precision)


def _split_bf16(x, parts):
    out = []
    for _ in range(parts):
        h = x.astype(BF16)
        out.append(h)
        x = x - h.astype(F32)
    return out


def _mm(a, b, mode, nt=False):
    f = _dot_nt if nt else _dot
    if mode == 6:
        return f(a, b, HI)
    if mode == 1:
        return f(a.astype(BF16), b.astype(BF16))
    if mode == 3:
        ah, al = _split_bf16(a, 2)
        bh, bl = _split_bf16(b, 2)
        return (f(al, bh) + f(ah, bl)) + f(ah, bh)
    if mode[0] == 'L':
        terms = [f(t, b.astype(BF16)) for t in _split_bf16(a, int(mode[1]))]
    else:
        terms = [f(a.astype(BF16), t) for t in _split_bf16(b, int(mode[1]))]
    out = terms[-1]
    for t in terms[-2::-1]:
        out = out + t
    return out


RWKV_MM = dict(lora=1, headsum='L2', cumsum='R3', gram=1, inverse=1, state=1)


def _ada_kernel(c_ref, w_ref, b_ref, o_ref):
    c = c_ref[...]
    s = c * jax.nn.sigmoid(c)
    o_ref[...] = _dot(s.astype(BF16), w_ref[...].astype(BF16)) + b_ref[...]


def _ada_mods(c, w_ada, b_ada):
    n, d = c.shape
    cols = w_ada.shape[1]
    tn = 1024
    return pl.pallas_call(
        _ada_kernel,
        grid=(cols // tn,),
        in_specs=[pl.BlockSpec((n, d), lambda j: (0, 0)),
                  pl.BlockSpec((d, tn), lambda j: (0, j)),
                  pl.BlockSpec((1, tn), lambda j: (0, j))],
        out_specs=pl.BlockSpec((n, tn), lambda j: (0, j)),
        out_shape=jax.ShapeDtypeStruct((n, cols), F32),
        compiler_params=_cparams(("arbitrary",)),
        name="ada_mods",
    )(c, w_ada, b_ada.reshape(1, cols))


def _nmm_kernel(x_ref, g_ref, sc_ref, sh_ref, w_ref, o_ref, h_ref):
    @pl.when(pl.program_id(1) == 0)
    def _():
        x = x_ref[...]
        ms = jnp.mean(x * x, axis=-1, keepdims=True)
        y = x * lax.rsqrt(ms + RMS_EPS) * g_ref[...]
        h_ref[...] = (y * (1.0 + sc_ref[0]) + sh_ref[0]).astype(h_ref.dtype)

    o_ref[...] = _dot(h_ref[...], w_ref[...])


def _norm_mod_matmul(x, g, sc, sh, w, tm, tn):
    t, d = x.shape
    n = w.shape[1]
    nmod, rows, _ = sc.shape
    tiles_per_mod = (t // tm) // nmod
    mod_spec = pl.BlockSpec((1, rows, d), lambda i, j: (i // tiles_per_mod, 0, 0))
    return pl.pallas_call(
        _nmm_kernel,
        grid=(t // tm, n // tn),
        in_specs=[pl.BlockSpec((tm, d), lambda i, j: (i, 0)),
                  pl.BlockSpec((1, d), lambda i, j: (0, 0)),
                  mod_spec, mod_spec,
                  pl.BlockSpec((d, tn), lambda i, j: (0, j))],
        out_specs=[pl.BlockSpec((tm, tn), lambda i, j: (i, j)),
                   pl.BlockSpec((tm, d), lambda i, j: (i, 0))],
        out_shape=[jax.ShapeDtypeStruct((t, n), F32), jax.ShapeDtypeStruct((t, d), BF16)],
        compiler_params=_cparams(("arbitrary", "arbitrary")),
        name="norm_mod_matmul",
    )(x, g.reshape(1, d), sc, sh, w)


def _nm_kernel(x_ref, g_ref, sc_ref, sh_ref, h_ref):
    x = x_ref[...]
    ms = jnp.mean(x * x, axis=-1, keepdims=True)
    y = x * lax.rsqrt(ms + RMS_EPS) * g_ref[...]
    h_ref[...] = (y * (1.0 + sc_ref[0]) + sh_ref[0]).astype(h_ref.dtype)


def _norm_mod(x, g, sc, sh, tm):
    t, d = x.shape
    nmod, rows, _ = sc.shape
    tiles_per_mod = (t // tm) // nmod
    mod_spec = pl.BlockSpec((1, rows, d), lambda i: (i // tiles_per_mod, 0, 0))
    return pl.pallas_call(
        _nm_kernel,
        grid=(t // tm,),
        in_specs=[pl.BlockSpec((tm, d), lambda i: (i, 0)), pl.BlockSpec((1, d), lambda i: (0, 0)), mod_spec, mod_spec],
        out_specs=pl.BlockSpec((tm, d), lambda i: (i, 0)),
        out_shape=jax.ShapeDtypeStruct((t, d), BF16),
        compiler_params=_cparams(("arbitrary",)),
        name="norm_mod",
    )(x, g.reshape(1, d), sc, sh)


def _outproj_kernel(x_ref, yr_ref, yn_ref, ga_ref, w1_ref, w2_ref, o_ref):
    acc = _dot(yr_ref[...].astype(BF16), w1_ref[...]) + _dot(yn_ref[...].astype(BF16), w2_ref[...])
    o_ref[...] = x_ref[...] + ga_ref[0] * acc


def _out_proj(x, y_r, y_n, ga, w1, w2, tm, tn):
    t, d = x.shape
    nmod, rows, _ = ga.shape
    tiles_per_mod = (t // tm) // nmod
    cr, cn = y_r.shape[1], y_n.shape[1]
    return pl.pallas_call(
        _outproj_kernel,
        grid=(t // tm, d // tn),
        in_specs=[pl.BlockSpec((tm, tn), lambda i, j: (i, j)),
                  pl.BlockSpec((tm, cr), lambda i, j: (i, 0)),
                  pl.BlockSpec((tm, cn), lambda i, j: (i, 0)),
                  pl.BlockSpec((1, rows, tn), lambda i, j: (i // tiles_per_mod, 0, j)),
                  pl.BlockSpec((cr, tn), lambda i, j: (0, j)),
                  pl.BlockSpec((cn, tn), lambda i, j: (0, j))],
        out_specs=pl.BlockSpec((tm, tn), lambda i, j: (i, j)),
        out_shape=jax.ShapeDtypeStruct((t, d), F32),
        compiler_params=_cparams(("arbitrary", "arbitrary")),
        name="out_proj",
    )(x, y_r, y_n, ga, w1, w2)


def _softplus(z):
    return jnp.maximum(z, 0.0) + jnp.log(1.0 + jnp.exp(-jnp.abs(z)))


def _rwkv_kernel(p_ref, shift_ref, s0_ref, mu_ref, vec_ref, wup_ref, aup_ref, gup_ref,
                 y_ref, sfin_ref, carry, state, *, t_valid, n_chunks):
    c = pl.program_id(1)
    chunk = p_ref.shape[1]
    cw = vec_ref.shape[1]
    n_groups = cw // MXU
    hg = HEADS_PER_GROUP
    rows_g = hg * chunk

    @pl.when(c == 0)
    def _():
        carry[...] = shift_ref[0]
        state[...] = s0_ref[0]

    p = p_ref[0]
    row = lax.broadcasted_iota(jnp.int32, (chunk, 1), 0)
    prev = jnp.where(row == 0, carry[...], pltpu.roll(p, 1, axis=0))
    carry[...] = p[chunk - 1:chunk, :]
    xs = p + (prev - p) * mu_ref[...]

    w0, a0, k_k, k_a, r_k, lnx_w, lnx_b = (vec_ref[i:i + 1, :] for i in range(7))
    r = xs[:, 0:cw]
    k = xs[:, cw:2 * cw]
    v = xs[:, 2 * cw:3 * cw]
    o = 3 * cw
    nw, na, ng = wup_ref.shape[0], aup_ref.shape[0], gup_ref.shape[0]
    xw = xs[:, o:o + nw]
    xa = xs[:, o + nw:o + nw + na]
    xg = xs[:, o + nw + na:o + nw + na + ng]
    pm = RWKV_MM
    w_log = -_softplus(-(w0 + _mm(jnp.tanh(xw), wup_ref[...], pm['lora']))) - 0.5
    a = jax.nn.sigmoid(a0 + _mm(xa, aup_ref[...], pm['lora']))
    gate = _mm(jax.nn.sigmoid(xg), gup_ref[...], pm['lora'])

    gi = lax.broadcasted_iota(jnp.int32, (MXU, MXU), 0) // HEAD_DIM
    gj = lax.broadcasted_iota(jnp.int32, (MXU, MXU), 1) // HEAD_DIM
    ones_bd = (gi == gj).astype(F32)

    def head_sum(x):
        return jnp.concatenate([_mm(x[:, g * MXU:(g + 1) * MXU], ones_bd, pm['headsum']) for g in range(n_groups)],
                               axis=1)

    kk = k * k_k
    kk = kk / jnp.maximum(jnp.sqrt(head_sum(kk * kk)), 1e-12)
    k2 = k * (1.0 + (a - 1.0) * k_a)
    logdec = -jnp.exp(w_log)
    if t_valid < chunk * n_chunks:
        valid = (row + c * chunk) < t_valid
        logdec = jnp.where(valid, logdec, 0.0)
        kk = jnp.where(valid, kk, 0.0)
        k2 = jnp.where(valid, k2, 0.0)
        v = jnp.where(valid, v, 0.0)

    ti = lax.broadcasted_iota(jnp.int32, (chunk, chunk), 0)
    tj = lax.broadcasted_iota(jnp.int32, (chunk, chunk), 1)
    cum = _mm((tj <= ti).astype(F32), logdec, pm['cumsum'])
    cum_end = cum[chunk - 1:chunk, :]
    e_neg = jnp.exp(-cum)
    e_rem = jnp.exp(cum_end - cum)
    r_t = r * jnp.exp(cum)
    a_t = -kk * jnp.exp(cum - logdec)
    b_vec = kk * a
    b_t = b_vec * e_neg
    k_t = k2 * e_neg
    b_rem = b_vec * e_rem
    k_rem = k2 * e_rem
    w_end = jnp.exp(cum_end)

    lane_head = lax.broadcasted_iota(jnp.int32, (chunk, MXU), 1) // HEAD_DIM
    ri = lax.broadcasted_iota(jnp.int32, (rows_g, rows_g), 0)
    rj = lax.broadcasted_iota(jnp.int32, (rows_g, rows_g), 1)
    strict = rj < ri
    incl = rj <= ri
    eye_r = (ri == rj).astype(F32)
    di = lax.broadcasted_iota(jnp.int32, (MXU, MXU), 0)
    dj = lax.broadcasted_iota(jnp.int32, (MXU, MXU), 1)
    n_double = max(int(math.ceil(math.log2(chunk))) - 1, 0)

    def bd(x):
        return jnp.concatenate([jnp.where(lane_head == h, x, 0.0) for h in range(hg)], axis=0)

    def stack(x):
        return jnp.concatenate([x[:, h * HEAD_DIM:(h + 1) * HEAD_DIM] for h in range(hg)], axis=0)

    def unstack(x):
        return jnp.concatenate([x[h * chunk:(h + 1) * chunk, :] for h in range(hg)], axis=1)

    ys = []
    for g in range(n_groups):
        sl = slice(g * MXU, (g + 1) * MXU)
        a_bd, r_bd = bd(a_t[:, sl]), bd(r_t[:, sl])
        b_bd, k_bd = bd(b_t[:, sl]), bd(k_t[:, sl])
        v_st = stack(v[:, sl])
        a_ab = jnp.where(strict, _mm(a_bd, b_bd, pm['gram'], nt=True), 0.0)
        a_ak = jnp.where(strict, _mm(a_bd, k_bd, pm['gram'], nt=True), 0.0)
        a_rb = jnp.where(incl, _mm(r_bd, b_bd, pm['gram'], nt=True), 0.0)
        a_rk = jnp.where(incl, _mm(r_bd, k_bd, pm['gram'], nt=True), 0.0)
        tinv = eye_r + a_ab
        pw = a_ab
        for _ in range(n_double):
            pw = _mm(pw, pw, pm['inverse'])
            tinv = tinv + _mm(pw, tinv, pm['inverse'])
        s0 = state[g]
        z = _mm(a_bd, s0, pm['state']) + _mm(a_ak, v_st, pm['state'])
        u = _mm(tinv, z, pm['state'])
        y_st = _mm(r_bd, s0, pm['state']) + _mm(a_rb, u, pm['state']) + _mm(a_rk, v_st, pm['state'])
        w_col = jnp.sum(jnp.where(di == dj, jnp.broadcast_to(w_end[:, sl], (MXU, MXU)), 0.0), axis=1, keepdims=True)
        state[g] = (w_col * s0 + _mm(bd(b_rem[:, sl]).T, u, pm['state'])
                    + _mm(bd(k_rem[:, sl]).T, v_st, pm['state']))
        ys.append(unstack(y_st))
    y = jnp.concatenate(ys, axis=1)

    inv_n = 1.0 / HEAD_DIM
    mean = head_sum(y) * inv_n
    d = y - mean
    var = head_sum(d * d) * inv_n
    yn = d * lax.rsqrt(var + LNX_EPS) * lnx_w + lnx_b
    bonus = head_sum(r * k2 * r_k) * v
    y_ref[0] = ((yn + bonus) * gate).astype(y_ref.dtype)

    @pl.when(c == n_chunks - 1)
    def _():
        sfin_ref[0] = state[...]


def _rwkv_pad_cols(x, cw, nw, na, ng):
    o = 3 * cw
    parts = [x[..., :o + nw], x[..., o + nw:o + nw + na], x[..., o + nw + na:]]
    widths = [o + _rup(nw, LANES), _rup(na, LANES), _rup(ng, LANES)]
    out = []
    for part, wd in zip(parts, widths):
        pad = [(0, 0)] * (x.ndim - 1) + [(0, wd - part.shape[-1])]
        out.append(jnp.pad(part, pad))
    return jnp.concatenate(out, axis=-1)


def _rwkv_unpad_cols(x, cw, nw, na, ng):
    o = 3 * cw
    o2 = o + _rup(nw, LANES)
    o3 = o2 + _rup(na, LANES)
    return jnp.concatenate([x[..., :o + nw], x[..., o2:o2 + na], x[..., o3:o3 + ng]], axis=-1)


def _rup(x, m):
    return (x + m - 1) // m * m


def _rwkv_mix(p_r, shift_prev, wkv0, W, chunk, t_valid):
    b, tpad, pr = p_r.shape
    cw = W['rwkv_w0'].shape[0]
    nh = cw // HEAD_DIM
    n_groups = cw // MXU
    nw, na, ng = W['rwkv_w_up'].shape[0], W['rwkv_a_up'].shape[0], W['rwkv_g_up'].shape[0]
    n_chunks = tpad // chunk
    mu = _rwkv_pad_cols(W['rwkv_mu'], cw, nw, na, ng).reshape(1, pr)
    vecs = jnp.stack([W['rwkv_w0'], W['rwkv_a0'], W['rwkv_k_k'], W['rwkv_k_a'], W['rwkv_r_k'].reshape(cw),
                      W['lnx_w'], W['lnx_b'], jnp.zeros((cw,), F32)])
    wup = jnp.pad(W['rwkv_w_up'], ((0, _rup(nw, LANES) - nw), (0, 0)))
    aup = jnp.pad(W['rwkv_a_up'], ((0, _rup(na, LANES) - na), (0, 0)))
    gup = jnp.pad(W['rwkv_g_up'], ((0, _rup(ng, LANES) - ng), (0, 0)))
    shift3 = _rwkv_pad_cols(shift_prev, cw, nw, na, ng).reshape(b, 1, pr)
    s0 = wkv0.astype(F32).transpose(0, 1, 3, 2).reshape(b, n_groups, MXU, HEAD_DIM)
    const = lambda shape: pl.BlockSpec(shape, lambda i, c: (0,) * len(shape))
    y, sfin = pl.pallas_call(
        functools.partial(_rwkv_kernel, t_valid=t_valid, n_chunks=n_chunks),
        grid=(b, n_chunks),
        in_specs=[pl.BlockSpec((1, chunk, pr), lambda i, c: (i, c, 0)),
                  pl.BlockSpec((1, 1, pr), lambda i, c: (i, 0, 0)),
                  pl.BlockSpec((1, n_groups, MXU, HEAD_DIM), lambda i, c: (i, 0, 0, 0)),
                  const((1, pr)), const((8, cw)), const(wup.shape), const(aup.shape), const(gup.shape)],
        out_specs=[pl.BlockSpec((1, chunk, cw), lambda i, c: (i, c, 0)),
                   pl.BlockSpec((1, n_groups, MXU, HEAD_DIM), lambda i, c: (i, 0, 0, 0))],
        out_shape=[jax.ShapeDtypeStruct((b, tpad, cw), F32),
                   jax.ShapeDtypeStruct((b, n_groups, MXU, HEAD_DIM), F32)],
        scratch_shapes=[pltpu.VMEM((1, pr), F32), pltpu.VMEM((n_groups, MXU, HEAD_DIM), F32)],
        compiler_params=_cparams(("arbitrary", "arbitrary")),
        name="rwkv_mix",
    )(p_r, shift3, s0, mu, vecs, wup, aup, gup)
    s_fin = sfin.reshape(b, nh, HEAD_DIM, HEAD_DIM).transpose(0, 1, 3, 2)
    return y, s_fin


def _head_ones():
    gi = lax.broadcasted_iota(jnp.int32, (MXU, MXU), 0) // HEAD_DIM
    gj = lax.broadcasted_iota(jnp.int32, (MXU, MXU), 1) // HEAD_DIM
    return (gi == gj).astype(F32)


def _nsa_proj_kernel(p_ref, g_ref, q_ref, rows_ref, win_ref, gate_ref, *, nsa_w, kvw):
    ones_bd = _head_ones()

    def hnorm(x, gvec):
        ms = _dot(x * x, ones_bd, HI) * (1.0 / HEAD_DIM)
        return x * lax.rsqrt(ms + RMS_EPS) * gvec

    for i in range(nsa_w // MXU):
        sl = slice(i * MXU, (i + 1) * MXU)
        q_ref[:, sl] = hnorm(p_ref[:, sl], g_ref[0:1, :])
    o = nsa_w
    rows_ref[:, 0:2 * kvw] = p_ref[:, o:o + 2 * kvw]
    rows_ref[:, 2 * kvw:3 * kvw] = hnorm(p_ref[:, o + 2 * kvw:o + 3 * kvw], g_ref[2:3, :])
    rows_ref[:, 3 * kvw:4 * kvw] = p_ref[:, o + 3 * kvw:o + 4 * kvw]
    win_ref[:, 0:kvw] = hnorm(p_ref[:, o + 4 * kvw:o + 5 * kvw], g_ref[3:4, :])
    win_ref[:, kvw:2 * kvw] = p_ref[:, o + 5 * kvw:o + 6 * kvw]
    gate_ref[...] = jax.nn.sigmoid(p_ref[:, o + 6 * kvw:])


def _nsa_project_call(p_n, qk_norm_g, nsa_w, kvw, tm):
    t, pc = p_n.shape
    assert kvw == MXU and nsa_w % MXU == 0
    gcols = pc - nsa_w - 6 * kvw
    gvec = jnp.tile(qk_norm_g, (1, MXU // HEAD_DIM))
    return pl.pallas_call(
        functools.partial(_nsa_proj_kernel, nsa_w=nsa_w, kvw=kvw),
        grid=(t // tm,),
        in_specs=[pl.BlockSpec((tm, pc), lambda i: (i, 0)),
                  pl.BlockSpec(gvec.shape, lambda i: (0, 0))],
        out_specs=[pl.BlockSpec((tm, nsa_w), lambda i: (i, 0)),
                   pl.BlockSpec((tm, 4 * kvw), lambda i: (i, 0)),
                   pl.BlockSpec((tm, 2 * kvw), lambda i: (i, 0)),
                   pl.BlockSpec((tm, gcols), lambda i: (i, 0))],
        out_shape=[jax.ShapeDtypeStruct((t, nsa_w), F32), jax.ShapeDtypeStruct((t, 4 * kvw), F32),
                   jax.ShapeDtypeStruct((t, 2 * kvw), F32), jax.ShapeDtypeStruct((t, gcols), F32)],
        compiler_params=_cparams(("arbitrary",)),
        name="nsa_project",
    )(p_n, gvec)


def _cmp_part_kernel(*refs, n_in, row_w, kvw, n_prefetch=0):
    refs = refs[n_prefetch:]
    x_refs, w_ref, o_ref = refs[:n_in], refs[n_in], refs[n_in + 1]
    for typ in range(2):
        acc = None
        for s in range(CMP_STRIDE):
            lo = s * row_w + typ * kvw
            xs = jnp.concatenate([x[0, :, lo:lo + kvw] for x in x_refs], axis=0) if n_in > 1 else x_refs[0][0, :, lo:lo + kvw]
            d = _dot(xs.astype(BF16), w_ref[typ, s])
            acc = d if acc is None else acc + d
        o_ref[0, :, typ * 2 * kvw:(typ + 1) * 2 * kvw] = acc


def _cmp_first_weights(cmp_w1, nkv):
    r2 = CMP_LEN // CMP_STRIDE
    e = cmp_w1.shape[-1]
    w1r = cmp_w1.reshape(2, r2, CMP_STRIDE, HEAD_DIM, e)
    eye = jnp.eye(nkv, dtype=F32)
    big = jnp.einsum('yhsde,gk->ysgdkhe', w1r, eye)
    return big.reshape(2, CMP_STRIDE, nkv * HEAD_DIM, nkv * r2 * e).astype(BF16)


def _cmp_parts_prompt(rows2d, w_big, b, t, kvw):
    row_w = rows2d.shape[2]
    nsub = t // CMP_STRIDE
    blk = min(nsub, LANES)
    return pl.pallas_call(
        functools.partial(_cmp_part_kernel, n_in=1, row_w=row_w, kvw=kvw),
        grid=(b, nsub // blk),
        in_specs=[pl.BlockSpec((1, blk, CMP_STRIDE * row_w), lambda i, j: (i, j, 0)),
                  pl.BlockSpec(w_big.shape, lambda i, j: (0, 0, 0, 0))],
        out_specs=pl.BlockSpec((1, blk, 4 * kvw), lambda i, j: (i, j, 0)),
        out_shape=jax.ShapeDtypeStruct((b, nsub, 4 * kvw), F32),
        compiler_params=_cparams(("arbitrary", "arbitrary")),
        name="cmp_parts",
    )(rows2d.reshape(b, nsub, CMP_STRIDE * row_w), w_big)


def _cmp_finish_kernel(*refs, n_parts, nc, nkv):
    p_refs = refs[:n_parts]
    b1_ref, w2_ref, b2_ref, g_ref, o_ref = refs[n_parts:]
    tg = pl.program_id(1)
    ns = o_ref.shape[2]
    e = p_refs[0].shape[2] // 2
    pieces = [p[0] for p in p_refs]
    have = sum(p.shape[0] for p in pieces)
    if have < ns:
        pieces.append(jnp.zeros((ns - have, 2 * e), F32))
    part = jnp.concatenate(pieces, axis=0) if len(pieces) > 1 else pieces[0]
    nxt = pltpu.roll(part[:, e:], ns - 1, axis=0)
    hid = part[:, :e] + nxt + b1_ref[0]
    out = _dot(jax.nn.gelu(hid).astype(BF16), w2_ref[0].astype(BF16)) + b2_ref[0]
    normed = out * lax.rsqrt(jnp.mean(out * out, axis=-1, keepdims=True) + RMS_EPS) * g_ref[...]
    out = jnp.where(tg < nkv, normed, out)
    row = lax.broadcasted_iota(jnp.int32, (ns, 1), 0)
    o_ref[0, 0] = jnp.where(row < nc, out, 0.0)


def _cmp_finish(parts_list, ns, cmp_b1, cmp_w2, cmp_b2, g1, nc, nkv):
    b = parts_list[0].shape[0]
    e = cmp_b1.shape[1]
    return pl.pallas_call(
        functools.partial(_cmp_finish_kernel, n_parts=len(parts_list), nc=nc, nkv=nkv),
        grid=(b, 2 * nkv),
        in_specs=[pl.BlockSpec((1, p.shape[1], 2 * e), lambda i, j: (i, 0, j)) for p in parts_list] + [
                  pl.BlockSpec((1, 1, e), lambda i, j: (j // nkv, 0, 0)),
                  pl.BlockSpec((1, e, HEAD_DIM), lambda i, j: (j // nkv, 0, 0)),
                  pl.BlockSpec((1, 1, HEAD_DIM), lambda i, j: (j // nkv, 0, 0)),
                  pl.BlockSpec((1, HEAD_DIM), lambda i, j: (0, 0))],
        out_specs=pl.BlockSpec((1, 1, ns, HEAD_DIM), lambda i, j: (i, j, 0, 0)),
        out_shape=jax.ShapeDtypeStruct((b, 2 * nkv, ns, HEAD_DIM), F32),
        compiler_params=_cparams(("arbitrary", "arbitrary")),
        name="cmp_finish",
    )(*parts_list, cmp_b1.reshape(2, 1, e), cmp_w2, cmp_b2.reshape(2, 1, HEAD_DIM), g1.reshape(1, HEAD_DIM))


QT = 128
SEL_CHAINS = 2
QK_SCALE = HEAD_DIM ** -0.5
assert math.log2(HEAD_DIM) % 2 == 0, "QK_SCALE must be a power of two to be folded into q exactly"


def _rel_table_np_dist(dist, table):
    onehot = (_rel_bucket(dist)[..., None] == jnp.arange(REL_BUCKETS)).astype(F32)
    return jnp.einsum('...b,bh->...h', onehot, table, precision=HI)


def _rel_table_per_lane(dist, table_l):
    bucket = _rel_bucket(dist)
    out = jnp.zeros(dist.shape, F32)
    for b in range(REL_BUCKETS):
        out = out + jnp.where(bucket == b, table_l[b][None, :], 0.0)
    return out


def _softmax_update(s, mask, m, l):
    m_new = jnp.maximum(m, jnp.max(jnp.where(mask, s, NEG), axis=0, keepdims=True))
    alpha = jnp.exp(m - m_new)
    p = jnp.where(mask, jnp.exp(s - m_new), 0.0)
    return p, m_new, alpha, alpha * l + jnp.sum(p, axis=0, keepdims=True)


def _rank_select(score, score_ref, cur, n_sel):
    nb = score.shape[0]
    score_ref[0:nb, :] = score
    jrow = lax.broadcasted_iota(jnp.int32, score.shape, 0)

    def body(j, rank):
        other = score_ref[pl.ds(j, 1), :]
        beats = (other > score) | ((other == score) & (jrow > j))
        return rank + jnp.where(beats, 1.0, 0.0)

    rank = lax.fori_loop(0, nb, body, jnp.zeros(score.shape, F32), unroll=8)
    return jnp.where((rank < n_sel) & (jrow <= cur), 1.0, 0.0)


def _extract_select(score, cur, n_sel):
    nb = score.shape[0]
    jrow = lax.broadcasted_iota(jnp.int32, score.shape, 0)
    sel = jnp.zeros(score.shape, F32)
    x = score
    for _ in range(n_sel):
        m = jnp.max(x, axis=0, keepdims=True)
        first = jnp.min(jnp.where(x == m, jrow, nb), axis=0, keepdims=True)
        hit = jrow == first
        sel = jnp.where(hit, 1.0, sel)
        x = jnp.where(hit, -jnp.inf, x)
    return jnp.where(jrow <= cur, sel, 0.0)


def _nsa_prompt_kernel(qT_ref, gT_ref, kc_ref, vcT_ref, ks_ref, vsT_ref, kw_ref, vwT_ref, bc_ref, toep_ref,
                       o_ref, pg_ref, score_ref, sel_ref, *, nc, nb, n_sel, ngrp):
    qt = pl.program_id(2)
    lanes = ngrp * QT
    q = (qT_ref[0, 0, 0] * QK_SCALE).astype(BF16)
    iq = lax.broadcasted_iota(jnp.int32, (1, QT), 1)
    q_pos = qt * QT + iq
    tile4 = lambda x: jnp.concatenate([x] * ngrp, axis=1)

    ncp = kc_ref.shape[2]
    s = _dot(kc_ref[0, 0].astype(BF16), q)
    s = s + jnp.concatenate([bc_ref[0, r] for r in range(ngrp)], axis=1)
    crow = lax.broadcasted_iota(jnp.int32, (ncp, QT), 0)
    ok_c = tile4((crow * CMP_STRIDE + (CMP_LEN - 1) <= q_pos) & (crow < nc))
    p, _, _, l = _softmax_update(s, ok_c, jnp.full((1, lanes), NEG, F32), jnp.zeros((1, lanes), F32))
    p = p * jnp.where(l > 0.0, 1.0 / jnp.where(l > 0.0, l, 1.0), 0.0)
    o_c = _dot(vcT_ref[0, 0].astype(BF16), p.astype(BF16))

    p_grp = p[:, 0:QT]
    for r in range(1, ngrp):
        p_grp = p_grp + p[:, r * QT:(r + 1) * QT]
    pad = 8
    pg_ref[...] = jnp.zeros(pg_ref.shape, F32)
    pg_ref[pad:pad + ncp, :] = p_grp
    r1 = SEL_BLOCK // CMP_STRIDE
    offs, wts = _slc_offsets()
    p_slc = None
    for o, wt in zip(offs, wts):
        term = float(wt) * pg_ref[pl.ds(pad + int(o), nb, stride=r1), :]
        p_slc = term if p_slc is None else p_slc + term
    jrow = lax.broadcasted_iota(jnp.int32, (nb, QT), 0)
    cur = q_pos // SEL_BLOCK
    forced = (jrow == 0) | (jrow == cur) | (jrow == cur - 1)
    score = jnp.where(jrow > cur, -1.0, jnp.where(forced, 1e6, p_slc))
    sel_ref[0:nb, :] = _rank_select(score, score_ref, cur, n_sel)

    ik = lax.broadcasted_iota(jnp.int32, (QT, QT), 0)
    iqq = lax.broadcasted_iota(jnp.int32, (QT, QT), 1)
    blocks_per_tile = QT // SEL_BLOCK

    def attend(kp, carry, k_ref, vT_ref, mask_fn):
        m, l, acc = carry
        s_all = _dot(k_ref[0, 0, kp], q)
        deltas = [qt - (2 * kp + i) for i in range(2)]
        mask = jnp.concatenate([mask_fn(2 * kp + i, deltas[i]) for i in range(2)], axis=0)
        m_out, l_out, alphas, ps = [], [], [], []
        for r in range(ngrp):
            ls = slice(r * QT, (r + 1) * QT)
            bias = jnp.concatenate([toep_ref[0, r, jnp.maximum(d, 0)] for d in deltas], axis=0)
            s = jnp.where(mask, s_all[:, ls] + bias, NEG)
            m_new = jnp.maximum(m[:, ls], jnp.max(s, axis=0, keepdims=True))
            alpha = jnp.exp(m[:, ls] - m_new)
            p = jnp.exp(s - m_new)
            m_out.append(m_new)
            l_out.append(alpha * l[:, ls] + jnp.sum(p, axis=0, keepdims=True))
            alphas.append(alpha)
            ps.append(p.astype(BF16))
        cat = lambda xs: jnp.concatenate(xs, axis=1)
        acc = cat(alphas) * acc + _dot(vT_ref[0, 0, kp], cat(ps))
        return cat(m_out), cat(l_out), acc

    def sel_mask(kt, delta):
        rows = [jnp.broadcast_to(sel_ref[pl.ds(kt * blocks_per_tile + i, 1), :], (SEL_BLOCK, QT))
                for i in range(blocks_per_tile)]
        chosen = jnp.concatenate(rows, axis=0) > 0.5
        return chosen & (ik - iqq <= delta * QT)

    def win_mask(kt, delta):
        dist = delta * QT + iqq - ik
        return (dist >= 0) & (dist < WINDOW)

    init = (jnp.full((1, lanes), NEG, F32), jnp.zeros((1, lanes), F32), jnp.zeros((HEAD_DIM, lanes), F32))

    def merge(states):
        m = states[0][0]
        for st in states[1:]:
            m = jnp.maximum(m, st[0])
        l, acc = None, None
        for m_i, l_i, acc_i in states:
            w = jnp.exp(m_i - m)
            l = w * l_i if l is None else l + w * l_i
            acc = w * acc_i if acc is None else acc + w * acc_i
        return acc * jnp.where(l > 0.0, 1.0 / jnp.where(l > 0.0, l, 1.0), 0.0)

    diag = qt // 2
    def sel_body(i, carry):
        return tuple(attend(SEL_CHAINS * i + c, carry[c], ks_ref, vsT_ref, sel_mask) for c in range(SEL_CHAINS))

    o_s = merge(lax.fori_loop(0, (diag + SEL_CHAINS) // SEL_CHAINS, sel_body, (init,) * SEL_CHAINS))
    first = jnp.maximum(qt - WINDOW // QT, 0) // 2
    states = []
    for i in range((WINDOW // QT) // 2 + 1):
        kp = diag - i
        live = kp >= first
        states.append(attend(jnp.maximum(kp, 0), init, kw_ref, vwT_ref,
                             lambda kt, delta, live=live: win_mask(kt, delta) & live))
    o_w = merge(states)
    g = gT_ref[0, 0, 0]
    o_ref[0, 0, 0] = g[0:1, :] * o_c + g[1:2, :] * o_s + g[2:3, :] * o_w


def _nsa_prompt(p_n, W, b, t, nkv, ngrp, tm):
    nsa_w = nkv * ngrp * HEAD_DIM
    kvw = nkv * HEAD_DIM
    qn, rows2d, win2d, gates = _nsa_project_call(p_n, W['qk_norm_g'], nsa_w, kvw, tm)
    rows = rows2d.reshape(b, t, 4, nkv, HEAD_DIM)
    win = win2d.reshape(b, t, 2, nkv, HEAD_DIM)
    ns = t // CMP_STRIDE
    nc = ns - CMP_LEN // CMP_STRIDE + 1
    nb = t // SEL_BLOCK
    n_sel = min(SEL_TOPK, nb)
    nqt = t // QT
    parts = _cmp_parts_prompt(rows2d.reshape(b, t, 4 * kvw), _cmp_first_weights(W['cmp_w1'], nkv), b, t, kvw)
    kvc = _cmp_finish([parts], ns, W['cmp_b1'], W['cmp_w2'], W['cmp_b2'], W['qk_norm_g'][1], nc, nkv)
    kc = kvc[:, :nkv]
    vcT = kvc[:, nkv:].transpose(0, 1, 3, 2)
    qT = qn.reshape(b, nqt, QT, nkv, ngrp, HEAD_DIM).transpose(0, 3, 1, 5, 4, 2).reshape(b, nkv, nqt, HEAD_DIM, ngrp * QT)
    ng = 3 * nkv * ngrp
    gT = gates[:, :ng].reshape(b, nqt, QT, nkv, ngrp, 3).transpose(0, 3, 1, 5, 4, 2).reshape(b, nkv, nqt, 3, ngrp * QT)
    gT = jnp.pad(gT, ((0, 0), (0, 0), (0, 0), (0, 5), (0, 0)))
    assert nqt % (2 * SEL_CHAINS) == 0 and (WINDOW // QT) % 2 == 0
    npair = nqt // 2
    k_tiles = lambda x: x.transpose(0, 2, 1, 3).reshape(b, nkv, npair, 2 * QT, HEAD_DIM).astype(BF16)
    vT_tiles = lambda x: x.reshape(b, npair, 2 * QT, nkv, HEAD_DIM).transpose(0, 3, 1, 4, 2).astype(BF16)
    ks, vsT = k_tiles(rows[:, :, 2]), vT_tiles(rows[:, :, 3])
    kw, vwT = k_tiles(win[:, :, 0]), vT_tiles(win[:, :, 1])
    table = W['rel_bias'].astype(F32)
    c_end = jnp.arange(ns) * CMP_STRIDE + CMP_LEN - 1
    bias_c = _rel_table_np_dist(jnp.arange(t)[None, :] - c_end[:, None], table)
    bias_c = bias_c.transpose(2, 0, 1).reshape(nkv, ngrp, ns, t)
    dd = (jnp.arange(nqt)[:, None, None] * QT + jnp.arange(QT)[None, None, :] - jnp.arange(QT)[None, :, None])
    toep = _rel_table_np_dist(dd, table).transpose(3, 0, 1, 2).reshape(nkv, ngrp, nqt, QT, QT)
    lanes = ngrp * QT
    kv_spec = lambda shape: pl.BlockSpec((1, 1) + shape, lambda i, g, j: (i, g) + (0,) * len(shape))
    yT = pl.pallas_call(
        functools.partial(_nsa_prompt_kernel, nc=nc, nb=nb, n_sel=n_sel, ngrp=ngrp),
        grid=(b, nkv, nqt),
        in_specs=[pl.BlockSpec((1, 1, 1, HEAD_DIM, lanes), lambda i, g, j: (i, g, j, 0, 0)),
                  pl.BlockSpec((1, 1, 1, 8, lanes), lambda i, g, j: (i, g, j, 0, 0)),
                  kv_spec((ns, HEAD_DIM)), kv_spec((HEAD_DIM, ns)),
                  kv_spec((npair, 2 * QT, HEAD_DIM)), kv_spec((npair, HEAD_DIM, 2 * QT)),
                  kv_spec((npair, 2 * QT, HEAD_DIM)), kv_spec((npair, HEAD_DIM, 2 * QT)),
                  pl.BlockSpec((1, ngrp, ns, QT), lambda i, g, j: (g, 0, 0, j)),
                  pl.BlockSpec((1, ngrp, nqt, QT, QT), lambda i, g, j: (g, 0, 0, 0, 0))],
        out_specs=pl.BlockSpec((1, 1, 1, HEAD_DIM, lanes), lambda i, g, j: (i, g, j, 0, 0)),
        out_shape=jax.ShapeDtypeStruct((b, nkv, nqt, HEAD_DIM, lanes), F32),
        scratch_shapes=[pltpu.VMEM((ns + 16, QT), F32), pltpu.VMEM((_rup(nb, 8), QT), F32),
                        pltpu.VMEM((_rup(nb, 8), QT), F32)],
        compiler_params=_cparams(("arbitrary", "arbitrary", "arbitrary")),
        name="nsa_prompt_attn",
    )(qT, gT, kc, vcT, ks, vsT, kw, vwT, bias_c, toep)
    y = yT.reshape(b, nkv, nqt, HEAD_DIM, ngrp, QT).transpose(0, 2, 5, 1, 4, 3).reshape(b * t, nsa_w)
    return y, rows, win[:, t - min(WINDOW, t):]


def _cmp_part_paged_kernel(*refs, npg, kvw):
    x_refs, w_ref, o_ref = refs[1:1 + npg], refs[1 + npg], refs[2 + npg]
    rows = x_refs[0].shape[2]
    sub = rows // CMP_STRIDE
    ri = lax.broadcasted_iota(jnp.int32, (rows, rows), 0)
    ci = lax.broadcasted_iota(jnp.int32, (rows, rows), 1)
    perm = jnp.where(ci == (ri % sub) * CMP_STRIDE + ri // sub, 1.0, 0.0).astype(BF16)
    xp = [_dot_nt(perm, x[0].astype(BF16)) for x in x_refs]
    for typ in range(2):
        acc = None
        for s in range(CMP_STRIDE):
            xs = jnp.concatenate([p[s * sub:(s + 1) * sub, typ * kvw:(typ + 1) * kvw] for p in xp], axis=0)
            d = _dot(xs.astype(BF16), w_ref[typ, s])
            acc = d if acc is None else acc + d
        o_ref[0, :, typ * 2 * kvw:(typ + 1) * 2 * kvw] = acc


def _cmp_parts_sample(cache_pages, page_table, w_big, kvw, npg):
    b, n_pages = page_table.shape
    rows = cache_pages.shape[2]
    sub = rows // CMP_STRIDE
    in_specs = [pl.BlockSpec((1, 2 * kvw, rows), (lambda i, j, pt, k=k: (pt[i, j * npg + k], 0, 0)))
                for k in range(npg)]
    in_specs.append(pl.BlockSpec(w_big.shape, lambda i, j, pt: (0, 0, 0, 0)))
    return pl.pallas_call(
        functools.partial(_cmp_part_paged_kernel, npg=npg, kvw=kvw),
        grid_spec=pltpu.PrefetchScalarGridSpec(
            num_scalar_prefetch=1, grid=(b, n_pages // npg), in_specs=in_specs,
            out_specs=pl.BlockSpec((1, npg * sub, 4 * kvw), lambda i, j, pt: (i, j, 0))),
        out_shape=jax.ShapeDtypeStruct((b, n_pages * sub, 4 * kvw), F32),
        compiler_params=_cparams(("arbitrary", "arbitrary")),
        name="cmp_parts_paged",
    )(page_table, *([cache_pages] * npg), w_big)


def _inv_pos(l):
    return jnp.where(l > 0.0, 1.0 / jnp.where(l > 0.0, l, 1.0), 0.0)


def _nsa_sample_kernel(*refs, npg, n_steps, nc, nb, n_sel, past, ds, keep, ngrp, n_lanes):
    pt_ref = refs[0]
    q_ref, g_ref, kc_ref, vcT_ref, bc_ref, win_ref, wnew_ref, bw_ref, rnew_ref, bs_ref = refs[1:11]
    page_refs = refs[11:11 + npg]
    o_ref = refs[11 + npg]
    (m_ref, l_ref, acc_ref, base_ref, sel_ref, pg_ref, score_ref,
     kw_ref, kn_ref, vn_ref) = refs[12 + npg:]
    del pt_ref
    j = pl.program_id(1)
    scale = HEAD_DIM ** -0.5
    kvw = q_ref.shape[1]
    eye = jnp.where(lax.broadcasted_iota(jnp.int32, (LANES, LANES), 0)
                    == lax.broadcasted_iota(jnp.int32, (LANES, LANES), 1), 1.0, 0.0).astype(BF16)
    qbd = q_ref[0]
    lane = lax.broadcasted_iota(jnp.int32, (1, LANES), 1)
    qi = (lane // ngrp) % ds
    q_pos = past + qi
    n_pages = npg * n_steps

    @pl.when(j == 0)
    def _():
        nsp = kc_ref.shape[1]
        s = _dot(kc_ref[0].astype(BF16), qbd) * scale + bc_ref[...]
        crow = lax.broadcasted_iota(jnp.int32, (nsp, LANES), 0)
        ok = (crow * CMP_STRIDE + (CMP_LEN - 1) <= q_pos) & (crow < nc)
        p, _, _, l = _softmax_update(s, ok, jnp.full((1, LANES), NEG, F32), jnp.zeros((1, LANES), F32))
        p = p * _inv_pos(l)
        o_c = _dot(vcT_ref[0].astype(BF16), p.astype(BF16))
        li = lax.broadcasted_iota(jnp.int32, (LANES, LANES), 0)
        lj = lax.broadcasted_iota(jnp.int32, (LANES, LANES), 1)
        fold = jnp.where((li // ngrp == lj) & (li < n_lanes), 1.0, 0.0)
        p_grp = _dot(p, fold, HI)
        pad = 8
        pg_ref[...] = jnp.zeros(pg_ref.shape, F32)
        pg_ref[pad:pad + nsp, :] = p_grp
        nbp = sel_ref.shape[0]
        r1 = SEL_BLOCK // CMP_STRIDE
        offs, wts = _slc_offsets()
        p_slc = None
        for o, wt in zip(offs, wts):
            term = float(wt) * pg_ref[pl.ds(pad + int(o), nbp, stride=r1), :]
            p_slc = term if p_slc is None else p_slc + term
        cur = (past + lane % ds) // SEL_BLOCK
        jrow = lax.broadcasted_iota(jnp.int32, (nbp, LANES), 0)
        forced = (jrow == 0) | (jrow == cur) | (jrow == cur - 1)
        score = jnp.where((jrow > cur) | (jrow >= nb), -1.0, jnp.where(forced, 1e6, p_slc))
        selg = _extract_select(score, cur, n_sel)
        unfold = jnp.where((li == lj // ngrp) & (lj < n_lanes), 1.0, 0.0)
        sel_ref[...] = _dot(selg, unfold)
        wk = kw_ref.shape[0]
        nn = wnew_ref.shape[1]
        kw_ref[...] = jnp.zeros(kw_ref.shape, F32)
        for c0 in range(0, keep, LANES):
            kw_ref[c0:c0 + LANES, :] = _dot_nt(eye, win_ref[0, 0:kvw, c0:c0 + LANES].astype(BF16))
        kw_ref[keep:keep + nn, :] = wnew_ref[0, :, 0:kvw]
        vn_ref[...] = jnp.zeros(vn_ref.shape, F32)
        vn_ref[0:nn, :] = wnew_ref[0, :, kvw:2 * kvw]
        s = _dot(kw_ref[...].astype(BF16), qbd) * scale + bw_ref[...]
        irow = lax.broadcasted_iota(jnp.int32, (wk, LANES), 0)
        dist = jnp.where(irow < keep, keep + qi - irow, qi - (irow - keep))
        ok = (dist >= 0) & (dist < WINDOW) & (irow < keep + ds)
        p, _, _, l = _softmax_update(s, ok, jnp.full((1, LANES), NEG, F32), jnp.zeros((1, LANES), F32))
        p = (p * _inv_pos(l)).astype(BF16)
        o_w = (_dot(win_ref[0, kvw:2 * kvw, :].astype(BF16), p[0:keep, :])
               + _dot(vn_ref[...].T.astype(BF16), p[keep:keep + PAGE_SIZE, :]))
        g = g_ref[0]
        base_ref[...] = g[0:1, :] * o_c + g[2:3, :] * o_w
        m_ref[...] = jnp.full(m_ref.shape, NEG, F32)
        l_ref[...] = jnp.zeros(l_ref.shape, F32)
        acc_ref[...] = jnp.zeros(acc_ref.shape, F32)
        kn_ref[...] = jnp.zeros(kn_ref.shape, F32)
        vn_ref[...] = jnp.zeros(vn_ref.shape, F32)
        kn_ref[0:nn, :] = rnew_ref[0, :, 2 * kvw:3 * kvw]
        vn_ref[0:nn, :] = rnew_ref[0, :, 3 * kvw:4 * kvw]

    ik = lax.broadcasted_iota(jnp.int32, (PAGE_SIZE, LANES), 0)
    blocks_per_page = PAGE_SIZE // SEL_BLOCK

    def pages_update(k, v_t, first_page, count):
        bias = jnp.concatenate([bs_ref[first_page + i] for i in range(count)], axis=0)
        s = _dot(k, qbd) * scale + bias
        rows = [jnp.broadcast_to(sel_ref[pl.ds(first_page * blocks_per_page + i, 1), :], (SEL_BLOCK, LANES))
                for i in range(count * blocks_per_page)]
        key_pos = first_page * PAGE_SIZE + lax.broadcasted_iota(jnp.int32, (count * PAGE_SIZE, LANES), 0)
        mask = (jnp.concatenate(rows, axis=0) > 0.5) & (key_pos <= q_pos)
        p, m_new, alpha, l_new = _softmax_update(s, mask, m_ref[...], l_ref[...])
        m_ref[...] = m_new
        l_ref[...] = l_new
        acc_ref[...] = alpha * acc_ref[...] + _dot(v_t, p.astype(BF16))

    k_rows = jnp.concatenate([_dot_nt(eye, blk[0, 0:kvw, :].astype(BF16)) for blk in page_refs], axis=0)
    v_cols = jnp.concatenate([blk[0, kvw:2 * kvw, :].astype(BF16) for blk in page_refs], axis=1)
    pages_update(k_rows.astype(BF16), v_cols, j * npg, npg)

    @pl.when(j == n_steps - 1)
    def _():
        pages_update(kn_ref[...].astype(BF16), vn_ref[...].T.astype(BF16), n_pages, 1)
        o_ref[0] = base_ref[...] + g_ref[0][1:2, :] * (acc_ref[...] * _inv_pos(l_ref[...]))


def _nsa_sample(p_n, cache_kv, cache_win, page_table, W, db, ds, nkv, ngrp):
    nsa_w = nkv * ngrp * HEAD_DIM
    kvw = nkv * HEAD_DIM
    row_w = 4 * kvw
    qn, rows2d, win2d, gates = _nsa_project_call(p_n, W['qk_norm_g'], nsa_w, kvw, db * ds)
    rows_new = rows2d.reshape(db, ds, 4, nkv, HEAD_DIM)
    win_new = win2d.reshape(db, ds, 2, nkv, HEAD_DIM)
    n_pool = cache_kv.shape[0]
    n_pages = page_table.shape[1]
    past = n_pages * PAGE_SIZE
    keep = cache_win.shape[1]
    tot = past + _rup(ds, SEL_BLOCK)
    ns = tot // CMP_STRIDE
    nc = ns - CMP_LEN // CMP_STRIDE + 1
    nb = tot // SEL_BLOCK
    n_sel = min(SEL_TOPK, nb)
    nsp = _rup(ns, LANES)
    nbp = _rup(nb, 8)
    n_lanes = nkv * ds * ngrp
    assert n_lanes <= LANES and ds <= 8
    w_big = _cmp_first_weights(W['cmp_w1'], nkv)
    cache_pages = cache_kv.transpose(0, 2, 3, 4, 1).reshape(n_pool, row_w, PAGE_SIZE)
    parts_past = _cmp_parts_sample(cache_pages, page_table, w_big, kvw, min(MXU // (PAGE_SIZE // CMP_STRIDE), n_pages))
    npg = min(16, n_pages)
    rows_pad = jnp.pad(rows2d.reshape(db, ds, row_w), ((0, 0), (0, PAGE_SIZE - ds), (0, 0)))
    parts_new = _cmp_parts_prompt(rows_pad, w_big, db, PAGE_SIZE, kvw)
    kvc = _cmp_finish([parts_past, parts_new], nsp, W['cmp_b1'], W['cmp_w2'], W['cmp_b2'], W['qk_norm_g'][1], nc, nkv)
    kc_cat = kvc[:, :nkv].transpose(0, 2, 1, 3).reshape(db, nsp, kvw)
    vcT_cat = kvc[:, nkv:].transpose(0, 1, 3, 2).reshape(db, kvw, nsp)
    lane_pad = LANES - n_lanes
    q5 = qn.reshape(db, ds, nkv, ngrp, HEAD_DIM)
    qbd = jnp.einsum('bqgrd,gk->bgdkqr', q5, jnp.eye(nkv, dtype=F32)).reshape(db, kvw, n_lanes)
    qbd = jnp.pad(qbd, ((0, 0), (0, 0), (0, lane_pad))).astype(BF16)
    ng = 3 * nkv * ngrp
    gT = gates[:, :ng].reshape(db, ds, nkv, ngrp, 3).transpose(0, 4, 2, 1, 3).reshape(db, 3, n_lanes)
    gT = jnp.pad(gT, ((0, 0), (0, 5), (0, lane_pad)))
    lane = np.arange(LANES)
    live = lane < n_lanes
    head_of_lane = np.where(live, (lane // (ds * ngrp)) * ngrp + lane % ngrp, 0)
    qi = np.where(live, (lane // ngrp) % ds, 0)
    table_l = W['rel_bias'].astype(F32)[:, head_of_lane]
    bias_of = lambda dist: _rel_table_per_lane(dist, table_l)
    c_end = np.arange(nsp) * CMP_STRIDE + CMP_LEN - 1
    bias_c = bias_of(jnp.asarray(past + qi[None, :] - c_end[:, None], jnp.int32))
    pos = np.arange((n_pages + 1) * PAGE_SIZE)
    bias_s = bias_of(jnp.asarray(past + qi[None, :] - pos[:, None], jnp.int32)).reshape(n_pages + 1, PAGE_SIZE, LANES)
    wk = _rup(keep + 8, LANES)
    irow = np.arange(wk)[:, None]
    dist_w = np.where(irow < keep, keep + qi[None, :] - irow, qi[None, :] - (irow - keep))
    bias_w = bias_of(jnp.asarray(dist_w, jnp.int32))
    win_c = cache_win.transpose(0, 2, 3, 4, 1).reshape(db, 2 * kvw, keep)
    wnew8 = jnp.pad(win2d.reshape(db, ds, 2 * kvw), ((0, 0), (0, 8 - ds), (0, 0)))
    rnew8 = jnp.pad(rows2d.reshape(db, ds, row_w), ((0, 0), (0, 8 - ds), (0, 0)))
    n_steps = n_pages // npg
    per_b = lambda shape: pl.BlockSpec((1,) + shape, lambda i, j, pt: (i,) + (0,) * len(shape))
    const = lambda shape: pl.BlockSpec(shape, lambda i, j, pt: (0,) * len(shape))
    in_specs = [per_b((kvw, LANES)), per_b((8, LANES)), per_b((nsp, kvw)), per_b((kvw, nsp)), const((nsp, LANES)),
                per_b((2 * kvw, keep)), per_b((8, 2 * kvw)), const((wk, LANES)), per_b((8, row_w)),
                const((n_pages + 1, PAGE_SIZE, LANES))]
    in_specs += [pl.BlockSpec((1, 2 * kvw, PAGE_SIZE), (lambda i, j, pt, k=k: (pt[i, j * npg + k], 1, 0)))
                 for k in range(npg)]
    yT = pl.pallas_call(
        functools.partial(_nsa_sample_kernel, npg=npg, n_steps=n_steps, nc=nc, nb=nb, n_sel=n_sel, past=past,
                          ds=ds, keep=keep, ngrp=ngrp, n_lanes=n_lanes),
        grid_spec=pltpu.PrefetchScalarGridSpec(
            num_scalar_prefetch=1, grid=(db, n_steps), in_specs=in_specs,
            out_specs=pl.BlockSpec((1, kvw, LANES), lambda i, j, pt: (i, 0, 0)),
            scratch_shapes=[pltpu.VMEM((1, LANES), F32), pltpu.VMEM((1, LANES), F32), pltpu.VMEM((kvw, LANES), F32),
                            pltpu.VMEM((kvw, LANES), F32), pltpu.VMEM((nbp, LANES), F32),
                            pltpu.VMEM((nsp + 16, LANES), F32), pltpu.VMEM((nbp, LANES), F32),
                            pltpu.VMEM((wk, kvw), F32),
                            pltpu.VMEM((PAGE_SIZE, kvw), F32), pltpu.VMEM((PAGE_SIZE, kvw), F32)]),
        out_shape=jax.ShapeDtypeStruct((db, kvw, LANES), F32),
        compiler_params=_cparams(("arbitrary", "arbitrary")),
        name="nsa_sample_attn",
    )(page_table, qbd, gT, kc_cat, vcT_cat, bias_c, win_c, wnew8, bias_w, rnew8, bias_s, *([cache_pages] * npg))
    y6 = yT[:, :, :n_lanes].reshape(db, nkv, HEAD_DIM, nkv, ds, ngrp)
    y = jnp.einsum('bgdgqr->bqgrd', y6).reshape(db * ds, nsa_w)
    win_all = jnp.concatenate([cache_win, win_new.astype(cache_win.dtype)], axis=1)
    n_keep = min(WINDOW, past + ds)
    return y, rows_new, win_all[:, win_all.shape[1] - n_keep:]


def _rel_bucket(dist):
    d = jnp.maximum(dist, 0)
    exact = REL_BUCKETS // 2
    ratio = jnp.maximum(d, exact).astype(F32) / exact
    large = exact + (jnp.log(ratio) / math.log(REL_MAX_DIST / exact) * (REL_BUCKETS - exact)).astype(jnp.int32)
    return jnp.where(d < exact, d, jnp.minimum(large, REL_BUCKETS - 1))


def _slc_offsets():
    r1 = SEL_BLOCK // CMP_STRIDE
    r2 = CMP_LEN // CMP_STRIDE
    offs = np.arange(-(r2 - 1), r1)
    wts = np.array([sum(1 for m in range(r1) for n in range(r2) if m - n == o) for o in offs], np.float32)
    return offs, wts


def _top_values(x, k):
    n = x.shape[0]
    row = lax.broadcasted_iota(jnp.int32, x.shape, 0)
    vals = []
    for _ in range(k):
        m = jnp.max(x, axis=0, keepdims=True)
        vals.append(m)
        first = jnp.min(jnp.where(x == m, row, n), axis=0, keepdims=True)
        x = jnp.where(row == first, -jnp.inf, x)
    return vals


def _peer_route_kernel(hT_ref, wq_ref, sk_ref, th_ref, c1_ref, s2_ref, e2_ref, *, nheads, topk):
    nk, half = sk_ref.shape[1], sk_ref.shape[2]
    cpt = s2_ref.shape[1]
    qT = _dot(wq_ref[...], hT_ref[...])
    pairs = [(a, b) for a in range(topk) for b in range(topk) if (a + 1) * (b + 1) <= topk]
    for h in range(nheads):
        base = h * 2 * half
        s1 = _dot(sk_ref[0].astype(BF16), qT[base:base + half].astype(BF16))
        s2 = _dot(sk_ref[1].astype(BF16), qT[base + half:base + 2 * half].astype(BF16))
        v1 = _top_values(s1, topk)
        v2 = _top_values(s2, topk)
        sums = [v1[a] + v2[b] for a, b in pairs]
        cand = jnp.concatenate(sums, axis=0)
        tau = _top_values(cand, topk)[-1]
        z = jnp.sum(jnp.where(cand >= tau, jnp.exp(cand - (v1[0] + v2[0])), 0.0), axis=0, keepdims=True)
        theta = jnp.full(s1.shape, jnp.inf, F32)
        for a in range(topk):
            th_a = jnp.full(tau.shape, jnp.inf, F32)
            for (pa, pb), sm in zip(pairs, sums):
                if pa == a:
                    th_a = jnp.where(sm >= tau, v2[pb], th_a)
            theta = jnp.where(s1 == v1[a], th_a, theta)
        c1 = jnp.exp(s1 - v1[0]) / z
        e2 = jnp.exp(s2 - v2[0])
        for c in range(cpt):
            cs = slice(c * LANES, (c + 1) * LANES)
            th_ref[h, c] = theta[:, cs]
            c1_ref[h, c] = c1[:, cs]
            s2_ref[h, c] = s2[:, cs]
            e2_ref[h, c] = e2[:, cs]


def _peer_expert_kernel(hT_ref, x1_ref, ga_ref, th_ref, c1_ref, s2_ref, e2_ref, u_ref, v_ref,
                        o_ref, acc_ref, gate_ref, *, nheads, n_eblocks):
    eb = pl.program_id(1)
    n_chunks, nk = s2_ref.shape[1], s2_ref.shape[2]
    rows_per_block = u_ref.shape[0] // nk

    @pl.when(eb == 0)
    def _():
        acc_ref[...] = jnp.zeros(acc_ref.shape, F32)

    def gate_tile(i, c):
        i1 = eb * rows_per_block + i
        wd = None
        for h in range(nheads):
            chosen = s2_ref[h, c] >= th_ref[h, c, pl.ds(i1, 1), :]
            term = jnp.where(chosen, e2_ref[h, c], 0.0) * c1_ref[h, c, pl.ds(i1, 1), :]
            wd = term if wd is None else wd + term
        gate_ref[i, c] = wd

    def gate_step(it, carry):
        gate_tile(it // n_chunks, it % n_chunks)
        return carry

    lax.fori_loop(0, rows_per_block * n_chunks, gate_step, 0)
    act = jax.nn.gelu(_dot(u_ref[...], hT_ref[...]))
    gates = jnp.concatenate([jnp.concatenate([gate_ref[i, c] for c in range(n_chunks)], axis=1)
                             for i in range(rows_per_block)], axis=0)
    acc_ref[...] += _dot((gates * act).T.astype(BF16), v_ref[...])

    @pl.when(eb == n_eblocks - 1)
    def _():
        o_ref[...] = x1_ref[...] + ga_ref[0] * acc_ref[...]


def _peer(h2, x1, ga, W, tm):
    t, d = h2.shape
    sub_keys = W['peer_sub_keys']
    nk, half = sub_keys.shape[1], sub_keys.shape[2]
    qd = W['peer_w_query'].shape[1]
    nheads = qd // (2 * half)
    hT = h2.T
    wqT = W['peer_w_query'].T.astype(BF16)
    cpt = tm // LANES
    route_shape = jax.ShapeDtypeStruct((nheads, t // LANES, nk, LANES), F32)
    rspec = pl.BlockSpec((nheads, cpt, nk, LANES), lambda i: (0, i, 0, 0))
    th, c1, s2, e2 = pl.pallas_call(
        functools.partial(_peer_route_kernel, nheads=nheads, topk=PEER_TOPK),
        grid=(t // tm,),
        in_specs=[pl.BlockSpec((d, tm), lambda i: (0, i)),
                  pl.BlockSpec((qd, d), lambda i: (0, 0)),
                  pl.BlockSpec(sub_keys.shape, lambda i: (0, 0, 0))],
        out_specs=[rspec] * 4,
        out_shape=[route_shape] * 4,
        compiler_params=_cparams(("arbitrary",)),
        name="peer_route",
    )(hT, wqT, sub_keys)
    eblk = 2 * MXU
    n_eblocks = W['peer_u'].shape[0] // eblk
    nmod, rows, _ = ga.shape
    tiles_per_mod = (t // tm) // nmod
    rspec2 = pl.BlockSpec((nheads, cpt, nk, LANES), lambda i, e: (0, i, 0, 0))
    return pl.pallas_call(
        functools.partial(_peer_expert_kernel, nheads=nheads, n_eblocks=n_eblocks),
        grid=(t // tm, n_eblocks),
        in_specs=[pl.BlockSpec((d, tm), lambda i, e: (0, i)),
                  pl.BlockSpec((tm, d), lambda i, e: (i, 0)),
                  pl.BlockSpec((1, rows, d), lambda i, e: (i // tiles_per_mod, 0, 0)),
                  rspec2, rspec2, rspec2, rspec2,
                  pl.BlockSpec((eblk, d), lambda i, e: (e, 0)),
                  pl.BlockSpec((eblk, d), lambda i, e: (e, 0))],
        out_specs=pl.BlockSpec((tm, d), lambda i, e: (i, 0)),
        out_shape=jax.ShapeDtypeStruct((t, d), F32),
        scratch_shapes=[pltpu.VMEM((tm, d), F32), pltpu.VMEM((eblk // nk, cpt, nk, LANES), F32)],
        compiler_params=_cparams(("arbitrary", "arbitrary")),
        name="peer_experts",
    )(hT, x1, ga, th, c1, s2, e2, W['peer_u_bf16'], W['peer_v_bf16'])


def _layer(x, mods, nsa_fn, shift_prev, wkv0, W, tm, rwkv_chunk):
    b, t, d = x.shape
    cw = W['rwkv_w0'].shape[0]
    nw, na, ng = W['rwkv_w_up'].shape[0], W['rwkv_a_up'].shape[0], W['rwkv_g_up'].shape[0]
    rwkv_proj = 3 * cw + nw + na + ng
    sh1, sc1, ga1, sh2, sc2, ga2 = mods
    xf = x.reshape(b * t, d)
    if (b * t) % tm == 0 and t % tm == 0:
        as_mod = lambda m: m.reshape(b, 1, d)
    else:
        tm = b * t
        as_mod = lambda m: jnp.repeat(m, t, axis=0).reshape(1, b * t, d)
    w_r = _rwkv_pad_cols(W['w_in'][:, :rwkv_proj], cw, nw, na, ng).astype(BF16)
    nsa_cols = W['w_in'].shape[1] - rwkv_proj
    w_n = jnp.pad(W['w_in'][:, rwkv_proj:], ((0, 0), (0, _rup(nsa_cols, LANES) - nsa_cols))).astype(BF16)
    p_r, _ = _norm_mod_matmul(xf, W['norm1_g'], as_mod(sc1), as_mod(sh1), w_r, tm, 512)
    p_n, _ = _norm_mod_matmul(xf, W['norm1_g'], as_mod(sc1), as_mod(sh1), w_n, tm, w_n.shape[1] // 3)
    pr = p_r.shape[1]
    p_r = p_r.reshape(b, t, pr)
    shift_new = _rwkv_unpad_cols(p_r[:, -1], cw, nw, na, ng)
    tpad = _rup(t, rwkv_chunk)
    p_r_pad = jnp.pad(p_r, ((0, 0), (0, tpad - t), (0, 0)))
    y_r, wkv_new = _rwkv_mix(p_r_pad, shift_prev, wkv0, W, rwkv_chunk, t)
    y_r = y_r[:, :t].reshape(b * t, cw)
    y_n, rows, win = nsa_fn(p_n)
    w_out = W['w_out'].astype(BF16)
    x1 = _out_proj(xf, y_r, y_n, as_mod(ga1), w_out[:cw], w_out[cw:], tm, 512)
    h2 = _norm_mod(x1, W['norm2_g'], as_mod(sc2), as_mod(sh2), tm)
    assert (b * t) % LANES == 0, "PEER kernels keep tokens on lanes"
    out = _peer(h2, x1, as_mod(ga2), W, tm)
    return out.reshape(b, t, d), rows, win, wkv_new, shift_new


def kernel(x_prompt, x_sample, c_prompt, c_sample, cache_kv, cache_win, state_wkv, state_shift, page_table,
           norm1_g, norm2_g, w_ada, b_ada, w_in, w_out,
           rwkv_mu, rwkv_w0, rwkv_w_up, rwkv_a0, rwkv_a_up, rwkv_g_up, rwkv_k_k, rwkv_k_a, rwkv_r_k, lnx_w, lnx_b,
           qk_norm_g, cmp_w1, cmp_b1, cmp_w2, cmp_b2, rel_bias,
           peer_w_query, peer_sub_keys, peer_u, peer_v):
    W = dict(norm1_g=norm1_g, norm2_g=norm2_g, w_ada=w_ada, b_ada=b_ada, w_in=w_in, w_out=w_out,
             rwkv_mu=rwkv_mu, rwkv_w0=rwkv_w0, rwkv_w_up=rwkv_w_up, rwkv_a0=rwkv_a0, rwkv_a_up=rwkv_a_up,
             rwkv_g_up=rwkv_g_up, rwkv_k_k=rwkv_k_k, rwkv_k_a=rwkv_k_a, rwkv_r_k=rwkv_r_k, lnx_w=lnx_w, lnx_b=lnx_b,
             qk_norm_g=qk_norm_g, cmp_w1=cmp_w1, cmp_b1=cmp_b1, cmp_w2=cmp_w2, cmp_b2=cmp_b2, rel_bias=rel_bias,
             peer_w_query=peer_w_query, peer_sub_keys=peer_sub_keys, peer_u=peer_u, peer_v=peer_v)
    W['peer_u_bf16'] = peer_u.astype(BF16)
    W['peer_v_bf16'] = peer_v.astype(BF16)
    bp, seq, d = x_prompt.shape
    db = x_sample.shape[0]
    nkv = cache_kv.shape[3]
    nh_r = rwkv_w0.shape[0] // HEAD_DIM
    ngrp = (w_out.shape[0] - rwkv_w0.shape[0]) // HEAD_DIM // nkv

    mods = _ada_mods(jnp.concatenate([c_prompt, c_sample], axis=0), w_ada, b_ada)
    mods = mods.reshape(bp + db, N_MODS, d)
    mods_p = [mods[:bp, i] for i in range(N_MODS)]
    mods_s = [mods[bp:, i] for i in range(N_MODS)]

    shift0 = jnp.zeros((bp, state_shift.shape[1]), F32)
    wkv0 = jnp.zeros((bp, nh_r, HEAD_DIM, HEAD_DIM), F32)
    y_p, rows_p, win_p, wkv_p, shift_p = _layer(
        x_prompt, mods_p, lambda pn: _nsa_prompt(pn, W, bp, seq, nkv, ngrp, 512), shift0, wkv0, W, 512, 64)
    y_s, rows_s, win_s, wkv_s, shift_s = _layer(
        x_sample, mods_s,
        lambda pn: _nsa_sample(pn, cache_kv, cache_win, page_table, W, db, x_sample.shape[1], nkv, ngrp),
        state_shift, state_wkv, W, 512, 32)
    return (y_p, y_s, rows_p, win_p, wkv_p.astype(state_wkv.dtype), shift_p,
            rows_s, win_s, wkv_s.astype(state_wkv.dtype), shift_s)
```

```python
import functools
import math

import numpy as np
import jax
import jax.numpy as jnp
from jax import lax
from jax.experimental import pallas as pl
from jax.experimental.pallas import tpu as pltpu

F32 = jnp.float32
BF16 = jnp.bfloat16
HI = lax.Precision.HIGHEST

HEAD_DIM = 64
PAGE_SIZE = 128
CMP_LEN = 32
CMP_STRIDE = 16
SEL_BLOCK = 64
SEL_TOPK = 16
WINDOW = 512
REL_BUCKETS = 32
REL_MAX_DIST = 2048
PEER_TOPK = 16
N_MODS = 6
RMS_EPS = 1e-6
LNX_EPS = 64e-5
NEG = -1e30

LANES = 128
MXU = 256
HEADS_PER_GROUP = MXU // HEAD_DIM
VMEM_LIMIT = 56 * 1024 * 1024


def _cparams(sem):
    return pltpu.CompilerParams(dimension_semantics=sem, vmem_limit_bytes=VMEM_LIMIT)


def _dot(a, b, precision=None):
    return jnp.dot(a, b, preferred_element_type=F32, precision=precision)


def _dot_nt(a, b, precision=None):
    return lax.dot_general(a, b, (((1,), (1,)), ((), ())), preferred_element_type=F32, precision=precision)


def _split_bf16(x, parts):
    out = []
    for _ in range(parts):
        h = x.astype(BF16)
        out.append(h)
        x = x - h.astype(F32)
    return out


def _mm(a, b, mode, nt=False):
    f = _dot_nt if nt else _dot
    if mode == 6:
        return f(a, b, HI)
    if mode == 1:
        return f(a.astype(BF16), b.astype(BF16))
    if mode == 3:
        ah, al = _split_bf16(a, 2)
        bh, bl = _split_bf16(b, 2)
        return (f(al, bh) + f(ah, bl)) + f(ah, bh)
    if mode[0] == 'L':
        terms = [f(t, b.astype(BF16)) for t in _split_bf16(a, int(mode[1]))]
    else:
        terms = [f(a.astype(BF16), t) for t in _split_bf16(b, int(mode[1]))]
    out = terms[-1]
    for t in terms[-2::-1]:
        out = out + t
    return out


RWKV_MM = dict(lora=1, headsum='L2', cumsum='R3', gram=1, inverse=1, state=1)


def _ada_kernel(c_ref, w_ref, b_ref, o_ref):
    c = c_ref[...]
    s = c * jax.nn.sigmoid(c)
    o_ref[...] = _dot(s.astype(BF16), w_ref[...].astype(BF16)) + b_ref[...]


def _ada_mods(c, w_ada, b_ada):
    n, d = c.shape
    cols = w_ada.shape[1]
    tn = 1024
    return pl.pallas_call(
        _ada_kernel,
        grid=(cols // tn,),
        in_specs=[pl.BlockSpec((n, d), lambda j: (0, 0)),
                  pl.BlockSpec((d, tn), lambda j: (0, j)),
                  pl.BlockSpec((1, tn), lambda j: (0, j))],
        out_specs=pl.BlockSpec((n, tn), lambda j: (0, j)),
        out_shape=jax.ShapeDtypeStruct((n, cols), F32),
        compiler_params=_cparams(("arbitrary",)),
        name="ada_mods",
    )(c, w_ada, b_ada.reshape(1, cols))


def _nmm_kernel(x_ref, g_ref, sc_ref, sh_ref, w_ref, o_ref, h_ref):
    @pl.when(pl.program_id(1) == 0)
    def _():
        x = x_ref[...]
        ms = jnp.mean(x * x, axis=-1, keepdims=True)
        y = x * lax.rsqrt(ms + RMS_EPS) * g_ref[...]
        h_ref[...] = (y * (1.0 + sc_ref[0]) + sh_ref[0]).astype(h_ref.dtype)

    o_ref[...] = _dot(h_ref[...], w_ref[...])


def _norm_mod_matmul(x, g, sc, sh, w, tm, tn):
    t, d = x.shape
    n = w.shape[1]
    nmod, rows, _ = sc.shape
    tiles_per_mod = (t // tm) // nmod
    mod_spec = pl.BlockSpec((1, rows, d), lambda i, j: (i // tiles_per_mod, 0, 0))
    return pl.pallas_call(
        _nmm_kernel,
        grid=(t // tm, n // tn),
        in_specs=[pl.BlockSpec((tm, d), lambda i, j: (i, 0)),
                  pl.BlockSpec((1, d), lambda i, j: (0, 0)),
                  mod_spec, mod_spec,
                  pl.BlockSpec((d, tn), lambda i, j: (0, j))],
        out_specs=[pl.BlockSpec((tm, tn), lambda i, j: (i, j)),
                   pl.BlockSpec((tm, d), lambda i, j: (i, 0))],
        out_shape=[jax.ShapeDtypeStruct((t, n), F32), jax.ShapeDtypeStruct((t, d), BF16)],
        compiler_params=_cparams(("arbitrary", "arbitrary")),
        name="norm_mod_matmul",
    )(x, g.reshape(1, d), sc, sh, w)


def _nm_kernel(x_ref, g_ref, sc_ref, sh_ref, h_ref):
    x = x_ref[...]
    ms = jnp.mean(x * x, axis=-1, keepdims=True)
    y = x * lax.rsqrt(ms + RMS_EPS) * g_ref[...]
    h_ref[...] = (y * (1.0 + sc_ref[0]) + sh_ref[0]).astype(h_ref.dtype)


def _norm_mod(x, g, sc, sh, tm):
    t, d = x.shape
    nmod, rows, _ = sc.shape
    tiles_per_mod = (t // tm) // nmod
    mod_spec = pl.BlockSpec((1, rows, d), lambda i: (i // tiles_per_mod, 0, 0))
    return pl.pallas_call(
        _nm_kernel,
        grid=(t // tm,),
        in_specs=[pl.BlockSpec((tm, d), lambda i: (i, 0)), pl.BlockSpec((1, d), lambda i: (0, 0)), mod_spec, mod_spec],
        out_specs=pl.BlockSpec((tm, d), lambda i: (i, 0)),
        out_shape=jax.ShapeDtypeStruct((t, d), BF16),
        compiler_params=_cparams(("arbitrary",)),
        name="norm_mod",
    )(x, g.reshape(1, d), sc, sh)


def _outproj_kernel(x_ref, yr_ref, yn_ref, ga_ref, w1_ref, w2_ref, o_ref):
    acc = _dot(yr_ref[...].astype(BF16), w1_ref[...]) + _dot(yn_ref[...].astype(BF16), w2_ref[...])
    o_ref[...] = x_ref[...] + ga_ref[0] * acc


def _out_proj(x, y_r, y_n, ga, w1, w2, tm, tn):
    t, d = x.shape
    nmod, rows, _ = ga.shape
    tiles_per_mod = (t // tm) // nmod
    cr, cn = y_r.shape[1], y_n.shape[1]
    return pl.pallas_call(
        _outproj_kernel,
        grid=(t // tm, d // tn),
        in_specs=[pl.BlockSpec((tm, tn), lambda i, j: (i, j)),
                  pl.BlockSpec((tm, cr), lambda i, j: (i, 0)),
                  pl.BlockSpec((tm, cn), lambda i, j: (i, 0)),
                  pl.BlockSpec((1, rows, tn), lambda i, j: (i // tiles_per_mod, 0, j)),
                  pl.BlockSpec((cr, tn), lambda i, j: (0, j)),
                  pl.BlockSpec((cn, tn), lambda i, j: (0, j))],
        out_specs=pl.BlockSpec((tm, tn), lambda i, j: (i, j)),
        out_shape=jax.ShapeDtypeStruct((t, d), F32),
        compiler_params=_cparams(("arbitrary", "arbitrary")),
        name="out_proj",
    )(x, y_r, y_n, ga, w1, w2)


def _softplus(z):
    return jnp.maximum(z, 0.0) + jnp.log(1.0 + jnp.exp(-jnp.abs(z)))


def _rwkv_kernel(p_ref, shift_ref, s0_ref, mu_ref, vec_ref, wup_ref, aup_ref, gup_ref,
                 y_ref, sfin_ref, carry, state, *, t_valid, n_chunks):
    c = pl.program_id(1)
    chunk = p_ref.shape[1]
    cw = vec_ref.shape[1]
    n_groups = cw // MXU
    hg = HEADS_PER_GROUP
    rows_g = hg * chunk

    @pl.when(c == 0)
    def _():
        carry[...] = shift_ref[0]
        state[...] = s0_ref[0]

    p = p_ref[0]
    row = lax.broadcasted_iota(jnp.int32, (chunk, 1), 0)
    prev = jnp.where(row == 0, carry[...], pltpu.roll(p, 1, axis=0))
    carry[...] = p[chunk - 1:chunk, :]
    xs = p + (prev - p) * mu_ref[...]

    w0, a0, k_k, k_a, r_k, lnx_w, lnx_b = (vec_ref[i:i + 1, :] for i in range(7))
    r = xs[:, 0:cw]
    k = xs[:, cw:2 * cw]
    v = xs[:, 2 * cw:3 * cw]
    o = 3 * cw
    nw, na, ng = wup_ref.shape[0], aup_ref.shape[0], gup_ref.shape[0]
    xw = xs[:, o:o + nw]
    xa = xs[:, o + nw:o + nw + na]
    xg = xs[:, o + nw + na:o + nw + na + ng]
    pm = RWKV_MM
    w_log = -_softplus(-(w0 + _mm(jnp.tanh(xw), wup_ref[...], pm['lora']))) - 0.5
    a = jax.nn.sigmoid(a0 + _mm(xa, aup_ref[...], pm['lora']))
    gate = _mm(jax.nn.sigmoid(xg), gup_ref[...], pm['lora'])

    gi = lax.broadcasted_iota(jnp.int32, (MXU, MXU), 0) // HEAD_DIM
    gj = lax.broadcasted_iota(jnp.int32, (MXU, MXU), 1) // HEAD_DIM
    ones_bd = (gi == gj).astype(F32)

    def head_sum(x):
        return jnp.concatenate([_mm(x[:, g * MXU:(g + 1) * MXU], ones_bd, pm['headsum']) for g in range(n_groups)],
                               axis=1)

    kk = k * k_k
    kk = kk / jnp.maximum(jnp.sqrt(head_sum(kk * kk)), 1e-12)
    k2 = k * (1.0 + (a - 1.0) * k_a)
    logdec = -jnp.exp(w_log)
    if t_valid < chunk * n_chunks:
        valid = (row + c * chunk) < t_valid
        logdec = jnp.where(valid, logdec, 0.0)
        kk = jnp.where(valid, kk, 0.0)
        k2 = jnp.where(valid, k2, 0.0)
        v = jnp.where(valid, v, 0.0)

    ti = lax.broadcasted_iota(jnp.int32, (chunk, chunk), 0)
    tj = lax.broadcasted_iota(jnp.int32, (chunk, chunk), 1)
    cum = _mm((tj <= ti).astype(F32), logdec, pm['cumsum'])
    cum_end = cum[chunk - 1:chunk, :]
    e_neg = jnp.exp(-cum)
    e_rem = jnp.exp(cum_end - cum)
    r_t = r * jnp.exp(cum)
    a_t = -kk * jnp.exp(cum - logdec)
    b_vec = kk * a
    b_t = b_vec * e_neg
    k_t = k2 * e_neg
    b_rem = b_vec * e_rem
    k_rem = k2 * e_rem
    w_end = jnp.exp(cum_end)

    lane_head = lax.broadcasted_iota(jnp.int32, (chunk, MXU), 1) // HEAD_DIM
    ri = lax.broadcasted_iota(jnp.int32, (rows_g, rows_g), 0)
    rj = lax.broadcasted_iota(jnp.int32, (rows_g, rows_g), 1)
    strict = rj < ri
    incl = rj <= ri
    eye_r = (ri == rj).astype(F32)
    di = lax.broadcasted_iota(jnp.int32, (MXU, MXU), 0)
    dj = lax.broadcasted_iota(jnp.int32, (MXU, MXU), 1)
    n_double = max(int(math.ceil(math.log2(chunk))) - 1, 0)

    def bd(x):
        return jnp.concatenate([jnp.where(lane_head == h, x, 0.0) for h in range(hg)], axis=0)

    def stack(x):
        return jnp.concatenate([x[:, h * HEAD_DIM:(h + 1) * HEAD_DIM] for h in range(hg)], axis=0)

    def unstack(x):
        return jnp.concatenate([x[h * chunk:(h + 1) * chunk, :] for h in range(hg)], axis=1)

    ys = []
    for g in range(n_groups):
        sl = slice(g * MXU, (g + 1) * MXU)
        a_bd, r_bd = bd(a_t[:, sl]), bd(r_t[:, sl])
        b_bd, k_bd = bd(b_t[:, sl]), bd(k_t[:, sl])
        v_st = stack(v[:, sl])
        a_ab = jnp.where(strict, _mm(a_bd, b_bd, pm['gram'], nt=True), 0.0)
        a_ak = jnp.where(strict, _mm(a_bd, k_bd, pm['gram'], nt=True), 0.0)
        a_rb = jnp.where(incl, _mm(r_bd, b_bd, pm['gram'], nt=True), 0.0)
        a_rk = jnp.where(incl, _mm(r_bd, k_bd, pm['gram'], nt=True), 0.0)
        tinv = eye_r + a_ab
        pw = a_ab
        for _ in range(n_double):
            pw = _mm(pw, pw, pm['inverse'])
            tinv = tinv + _mm(pw, tinv, pm['inverse'])
        s0 = state[g]
        z = _mm(a_bd, s0, pm['state']) + _mm(a_ak, v_st, pm['state'])
        u = _mm(tinv, z, pm['state'])
        y_st = _mm(r_bd, s0, pm['state']) + _mm(a_rb, u, pm['state']) + _mm(a_rk, v_st, pm['state'])
        w_col = jnp.sum(jnp.where(di == dj, jnp.broadcast_to(w_end[:, sl], (MXU, MXU)), 0.0), axis=1, keepdims=True)
        state[g] = (w_col * s0 + _mm(bd(b_rem[:, sl]).T, u, pm['state'])
                    + _mm(bd(k_rem[:, sl]).T, v_st, pm['state']))
        ys.append(unstack(y_st))
    y = jnp.concatenate(ys, axis=1)

    inv_n = 1.0 / HEAD_DIM
    mean = head_sum(y) * inv_n
    d = y - mean
    var = head_sum(d * d) * inv_n
    yn = d * lax.rsqrt(var + LNX_EPS) * lnx_w + lnx_b
    bonus = head_sum(r * k2 * r_k) * v
    y_ref[0] = ((yn + bonus) * gate).astype(y_ref.dtype)

    @pl.when(c == n_chunks - 1)
    def _():
        sfin_ref[0] = state[...]


def _rwkv_pad_cols(x, cw, nw, na, ng):
    o = 3 * cw
    parts = [x[..., :o + nw], x[..., o + nw:o + nw + na], x[..., o + nw + na:]]
    widths = [o + _rup(nw, LANES), _rup(na, LANES), _rup(ng, LANES)]
    out = []
    for part, wd in zip(parts, widths):
        pad = [(0, 0)] * (x.ndim - 1) + [(0, wd - part.shape[-1])]
        out.append(jnp.pad(part, pad))
    return jnp.concatenate(out, axis=-1)


def _rwkv_unpad_cols(x, cw, nw, na, ng):
    o = 3 * cw
    o2 = o + _rup(nw, LANES)
    o3 = o2 + _rup(na, LANES)
    return jnp.concatenate([x[..., :o + nw], x[..., o2:o2 + na], x[..., o3:o3 + ng]], axis=-1)


def _rup(x, m):
    return (x + m - 1) // m * m


def _rwkv_mix(p_r, shift_prev, wkv0, W, chunk, t_valid):
    b, tpad, pr = p_r.shape
    cw = W['rwkv_w0'].shape[0]
    nh = cw // HEAD_DIM
    n_groups = cw // MXU
    nw, na, ng = W['rwkv_w_up'].shape[0], W['rwkv_a_up'].shape[0], W['rwkv_g_up'].shape[0]
    n_chunks = tpad // chunk
    mu = _rwkv_pad_cols(W['rwkv_mu'], cw, nw, na, ng).reshape(1, pr)
    vecs = jnp.stack([W['rwkv_w0'], W['rwkv_a0'], W['rwkv_k_k'], W['rwkv_k_a'], W['rwkv_r_k'].reshape(cw),
                      W['lnx_w'], W['lnx_b'], jnp.zeros((cw,), F32)])
    wup = jnp.pad(W['rwkv_w_up'], ((0, _rup(nw, LANES) - nw), (0, 0)))
    aup = jnp.pad(W['rwkv_a_up'], ((0, _rup(na, LANES) - na), (0, 0)))
    gup = jnp.pad(W['rwkv_g_up'], ((0, _rup(ng, LANES) - ng), (0, 0)))
    shift3 = _rwkv_pad_cols(shift_prev, cw, nw, na, ng).reshape(b, 1, pr)
    s0 = wkv0.astype(F32).transpose(0, 1, 3, 2).reshape(b, n_groups, MXU, HEAD_DIM)
    const = lambda shape: pl.BlockSpec(shape, lambda i, c: (0,) * len(shape))
    y, sfin = pl.pallas_call(
        functools.partial(_rwkv_kernel, t_valid=t_valid, n_chunks=n_chunks),
        grid=(b, n_chunks),
        in_specs=[pl.BlockSpec((1, chunk, pr), lambda i, c: (i, c, 0)),
                  pl.BlockSpec((1, 1, pr), lambda i, c: (i, 0, 0)),
                  pl.BlockSpec((1, n_groups, MXU, HEAD_DIM), lambda i, c: (i, 0, 0, 0)),
                  const((1, pr)), const((8, cw)), const(wup.shape), const(aup.shape), const(gup.shape)],
        out_specs=[pl.BlockSpec((1, chunk, cw), lambda i, c: (i, c, 0)),
                   pl.BlockSpec((1, n_groups, MXU, HEAD_DIM), lambda i, c: (i, 0, 0, 0))],
        out_shape=[jax.ShapeDtypeStruct((b, tpad, cw), F32),
                   jax.ShapeDtypeStruct((b, n_groups, MXU, HEAD_DIM), F32)],
        scratch_shapes=[pltpu.VMEM((1, pr), F32), pltpu.VMEM((n_groups, MXU, HEAD_DIM), F32)],
        compiler_params=_cparams(("arbitrary", "arbitrary")),
        name="rwkv_mix",
    )(p_r, shift3, s0, mu, vecs, wup, aup, gup)
    s_fin = sfin.reshape(b, nh, HEAD_DIM, HEAD_DIM).transpose(0, 1, 3, 2)
    return y, s_fin


def _head_ones():
    gi = lax.broadcasted_iota(jnp.int32, (MXU, MXU), 0) // HEAD_DIM
    gj = lax.broadcasted_iota(jnp.int32, (MXU, MXU), 1) // HEAD_DIM
    return (gi == gj).astype(F32)


def _nsa_proj_kernel(p_ref, g_ref, q_ref, rows_ref, win_ref, gate_ref, *, nsa_w, kvw):
    ones_bd = _head_ones()

    def hnorm(x, gvec):
        ms = _dot(x * x, ones_bd, HI) * (1.0 / HEAD_DIM)
        return x * lax.rsqrt(ms + RMS_EPS) * gvec

    for i in range(nsa_w // MXU):
        sl = slice(i * MXU, (i + 1) * MXU)
        q_ref[:, sl] = hnorm(p_ref[:, sl], g_ref[0:1, :])
    o = nsa_w
    rows_ref[:, 0:2 * kvw] = p_ref[:, o:o + 2 * kvw]
    rows_ref[:, 2 * kvw:3 * kvw] = hnorm(p_ref[:, o + 2 * kvw:o + 3 * kvw], g_ref[2:3, :])
    rows_ref[:, 3 * kvw:4 * kvw] = p_ref[:, o + 3 * kvw:o + 4 * kvw]
    win_ref[:, 0:kvw] = hnorm(p_ref[:, o + 4 * kvw:o + 5 * kvw], g_ref[3:4, :])
    win_ref[:, kvw:2 * kvw] = p_ref[:, o + 5 * kvw:o + 6 * kvw]
    gate_ref[...] = jax.nn.sigmoid(p_ref[:, o + 6 * kvw:])


def _nsa_project_call(p_n, qk_norm_g, nsa_w, kvw, tm):
    t, pc = p_n.shape
    assert kvw == MXU and nsa_w % MXU == 0
    gcols = pc - nsa_w - 6 * kvw
    gvec = jnp.tile(qk_norm_g, (1, MXU // HEAD_DIM))
    return pl.pallas_call(
        functools.partial(_nsa_proj_kernel, nsa_w=nsa_w, kvw=kvw),
        grid=(t // tm,),
        in_specs=[pl.BlockSpec((tm, pc), lambda i: (i, 0)),
                  pl.BlockSpec(gvec.shape, lambda i: (0, 0))],
        out_specs=[pl.BlockSpec((tm, nsa_w), lambda i: (i, 0)),
                   pl.BlockSpec((tm, 4 * kvw), lambda i: (i, 0)),
                   pl.BlockSpec((tm, 2 * kvw), lambda i: (i, 0)),
                   pl.BlockSpec((tm, gcols), lambda i: (i, 0))],
        out_shape=[jax.ShapeDtypeStruct((t, nsa_w), F32), jax.ShapeDtypeStruct((t, 4 * kvw), F32),
                   jax.ShapeDtypeStruct((t, 2 * kvw), F32), jax.ShapeDtypeStruct((t, gcols), F32)],
        compiler_params=_cparams(("arbitrary",)),
        name="nsa_project",
    )(p_n, gvec)


def _cmp_part_kernel(*refs, n_in, row_w, kvw, n_prefetch=0):
    refs = refs[n_prefetch:]
    x_refs, w_ref, o_ref = refs[:n_in], refs[n_in], refs[n_in + 1]
    for typ in range(2):
        acc = None
        for s in range(CMP_STRIDE):
            lo = s * row_w + typ * kvw
            xs = jnp.concatenate([x[0, :, lo:lo + kvw] for x in x_refs], axis=0) if n_in > 1 else x_refs[0][0, :, lo:lo + kvw]
            d = _dot(xs.astype(BF16), w_ref[typ, s])
            acc = d if acc is None else acc + d
        o_ref[0, :, typ * 2 * kvw:(typ + 1) * 2 * kvw] = acc


def _cmp_first_weights(cmp_w1, nkv):
    r2 = CMP_LEN // CMP_STRIDE
    e = cmp_w1.shape[-1]
    w1r = cmp_w1.reshape(2, r2, CMP_STRIDE, HEAD_DIM, e)
    eye = jnp.eye(nkv, dtype=F32)
    big = jnp.einsum('yhsde,gk->ysgdkhe', w1r, eye)
    return big.reshape(2, CMP_STRIDE, nkv * HEAD_DIM, nkv * r2 * e).astype(BF16)


def _cmp_parts_prompt(rows2d, w_big, b, t, kvw):
    row_w = rows2d.shape[2]
    nsub = t // CMP_STRIDE
    blk = min(nsub, LANES)
    return pl.pallas_call(
        functools.partial(_cmp_part_kernel, n_in=1, row_w=row_w, kvw=kvw),
        grid=(b, nsub // blk),
        in_specs=[pl.BlockSpec((1, blk, CMP_STRIDE * row_w), lambda i, j: (i, j, 0)),
                  pl.BlockSpec(w_big.shape, lambda i, j: (0, 0, 0, 0))],
        out_specs=pl.BlockSpec((1, blk, 4 * kvw), lambda i, j: (i, j, 0)),
        out_shape=jax.ShapeDtypeStruct((b, nsub, 4 * kvw), F32),
        compiler_params=_cparams(("arbitrary", "arbitrary")),
        name="cmp_parts",
    )(rows2d.reshape(b, nsub, CMP_STRIDE * row_w), w_big)


def _cmp_finish_kernel(*refs, n_parts, nc, nkv):
    p_refs = refs[:n_parts]
    b1_ref, w2_ref, b2_ref, g_ref, o_ref = refs[n_parts:]
    ns = o_ref.shape[2]
    e = b1_ref.shape[2]
    have = sum(p.shape[1] for p in p_refs)
    row = lax.broadcasted_iota(jnp.int32, (ns, 1), 0)
    for tg in range(2 * nkv):
        typ = tg // nkv
        cols = slice(tg * 2 * e, (tg + 1) * 2 * e)
        pieces = [p[0, :, cols] for p in p_refs]
        if have < ns:
            pieces.append(jnp.zeros((ns - have, 2 * e), F32))
        part = jnp.concatenate(pieces, axis=0) if len(pieces) > 1 else pieces[0]
        nxt = pltpu.roll(part[:, e:], ns - 1, axis=0)
        hid = part[:, :e] + nxt + b1_ref[typ]
        out = _dot(jax.nn.gelu(hid).astype(BF16), w2_ref[typ].astype(BF16)) + b2_ref[typ]
        if typ == 0:
            out = out * lax.rsqrt(jnp.mean(out * out, axis=-1, keepdims=True) + RMS_EPS) * g_ref[...]
        o_ref[0, tg] = jnp.where(row < nc, out, 0.0)


def _cmp_finish(parts_list, ns, cmp_b1, cmp_w2, cmp_b2, g1, nc, nkv):
    b = parts_list[0].shape[0]
    e = cmp_b1.shape[1]
    return pl.pallas_call(
        functools.partial(_cmp_finish_kernel, n_parts=len(parts_list), nc=nc, nkv=nkv),
        grid=(b,),
        in_specs=[pl.BlockSpec((1,) + p.shape[1:], lambda i: (i, 0, 0)) for p in parts_list] + [
                  pl.BlockSpec((2, 1, e), lambda i: (0, 0, 0)),
                  pl.BlockSpec((2, e, HEAD_DIM), lambda i: (0, 0, 0)),
                  pl.BlockSpec((2, 1, HEAD_DIM), lambda i: (0, 0, 0)),
                  pl.BlockSpec((1, HEAD_DIM), lambda i: (0, 0))],
        out_specs=pl.BlockSpec((1, 2 * nkv, ns, HEAD_DIM), lambda i: (i, 0, 0, 0)),
        out_shape=jax.ShapeDtypeStruct((b, 2 * nkv, ns, HEAD_DIM), F32),
        compiler_params=_cparams(("arbitrary",)),
        name="cmp_finish",
    )(*parts_list, cmp_b1.reshape(2, 1, e), cmp_w2, cmp_b2.reshape(2, 1, HEAD_DIM), g1.reshape(1, HEAD_DIM))


QT = 128
SEL_CHAINS = 2
QK_SCALE = HEAD_DIM ** -0.5
assert math.log2(HEAD_DIM) % 2 == 0, "QK_SCALE must be a power of two to be folded into q exactly"


def _rel_table_np_dist(dist, table):
    onehot = (_rel_bucket(dist)[..., None] == jnp.arange(REL_BUCKETS)).astype(F32)
    return jnp.einsum('...b,bh->...h', onehot, table, precision=HI)


def _rel_table_per_lane(dist, table_l):
    bucket = _rel_bucket(dist)
    out = jnp.zeros(dist.shape, F32)
    for b in range(REL_BUCKETS):
        out = out + jnp.where(bucket == b, table_l[b][None, :], 0.0)
    return out


def _softmax_update(s, mask, m, l):
    m_new = jnp.maximum(m, jnp.max(jnp.where(mask, s, NEG), axis=0, keepdims=True))
    alpha = jnp.exp(m - m_new)
    p = jnp.where(mask, jnp.exp(s - m_new), 0.0)
    return p, m_new, alpha, alpha * l + jnp.sum(p, axis=0, keepdims=True)


def _rank_select(score, score_ref, cur, n_sel):
    nb = score.shape[0]
    score_ref[0:nb, :] = score
    jrow = lax.broadcasted_iota(jnp.int32, score.shape, 0)

    def body(j, rank):
        other = score_ref[pl.ds(j, 1), :]
        beats = (other > score) | ((other == score) & (jrow > j))
        return rank + jnp.where(beats, 1.0, 0.0)

    rank = lax.fori_loop(0, nb, body, jnp.zeros(score.shape, F32), unroll=8)
    return jnp.where((rank < n_sel) & (jrow <= cur), 1.0, 0.0)


def _extract_select(score, cur, n_sel):
    nb = score.shape[0]
    jrow = lax.broadcasted_iota(jnp.int32, score.shape, 0)
    sel = jnp.zeros(score.shape, F32)
    x = score
    for _ in range(n_sel):
        m = jnp.max(x, axis=0, keepdims=True)
        first = jnp.min(jnp.where(x == m, jrow, nb), axis=0, keepdims=True)
        hit = jrow == first
        sel = jnp.where(hit, 1.0, sel)
        x = jnp.where(hit, -jnp.inf, x)
    return jnp.where(jrow <= cur, sel, 0.0)


def _nsa_prompt_kernel(qT_ref, gT_ref, kc_ref, vcT_ref, ks_ref, vsT_ref, kw_ref, vwT_ref, bc_ref, toep_ref,
                       o_ref, pg_ref, score_ref, sel_ref, *, nc, nb, n_sel, ngrp):
    qt = pl.program_id(2)
    lanes = ngrp * QT
    q = (qT_ref[0, 0, 0] * QK_SCALE).astype(BF16)
    iq = lax.broadcasted_iota(jnp.int32, (1, QT), 1)
    q_pos = qt * QT + iq
    tile4 = lambda x: jnp.concatenate([x] * ngrp, axis=1)

    ncp = kc_ref.shape[2]
    s = _dot(kc_ref[0, 0].astype(BF16), q)
    s = s + jnp.concatenate([bc_ref[0, r] for r in range(ngrp)], axis=1)
    crow = lax.broadcasted_iota(jnp.int32, (ncp, QT), 0)
    ok_c = tile4((crow * CMP_STRIDE + (CMP_LEN - 1) <= q_pos) & (crow < nc))
    p, _, _, l = _softmax_update(s, ok_c, jnp.full((1, lanes), NEG, F32), jnp.zeros((1, lanes), F32))
    p = p * jnp.where(l > 0.0, 1.0 / jnp.where(l > 0.0, l, 1.0), 0.0)
    o_c = _dot(vcT_ref[0, 0].astype(BF16), p.astype(BF16))

    p_grp = p[:, 0:QT]
    for r in range(1, ngrp):
        p_grp = p_grp + p[:, r * QT:(r + 1) * QT]
    pad = 8
    pg_ref[...] = jnp.zeros(pg_ref.shape, F32)
    pg_ref[pad:pad + ncp, :] = p_grp
    r1 = SEL_BLOCK // CMP_STRIDE
    offs, wts = _slc_offsets()
    p_slc = None
    for o, wt in zip(offs, wts):
        term = float(wt) * pg_ref[pl.ds(pad + int(o), nb, stride=r1), :]
        p_slc = term if p_slc is None else p_slc + term
    jrow = lax.broadcasted_iota(jnp.int32, (nb, QT), 0)
    cur = q_pos // SEL_BLOCK
    forced = (jrow == 0) | (jrow == cur) | (jrow == cur - 1)
    score = jnp.where(jrow > cur, -1.0, jnp.where(forced, 1e6, p_slc))
    sel_ref[0:nb, :] = _rank_select(score, score_ref, cur, n_sel)

    ik = lax.broadcasted_iota(jnp.int32, (QT, QT), 0)
    iqq = lax.broadcasted_iota(jnp.int32, (QT, QT), 1)
    blocks_per_tile = QT // SEL_BLOCK

    def attend(kp, carry, k_ref, vT_ref, mask_fn):
        m, l, acc = carry
        s_all = _dot(k_ref[0, 0, kp], q)
        deltas = [qt - (2 * kp + i) for i in range(2)]
        mask = jnp.concatenate([mask_fn(2 * kp + i, deltas[i]) for i in range(2)], axis=0)
        m_out, l_out, alphas, ps = [], [], [], []
        for r in range(ngrp):
            ls = slice(r * QT, (r + 1) * QT)
            bias = jnp.concatenate([toep_ref[0, r, jnp.maximum(d, 0)] for d in deltas], axis=0)
            s = jnp.where(mask, s_all[:, ls] + bias, NEG)
            m_new = jnp.maximum(m[:, ls], jnp.max(s, axis=0, keepdims=True))
            alpha = jnp.exp(m[:, ls] - m_new)
            p = jnp.exp(s - m_new)
            m_out.append(m_new)
            l_out.append(alpha * l[:, ls] + jnp.sum(p, axis=0, keepdims=True))
            alphas.append(alpha)
            ps.append(p.astype(BF16))
        cat = lambda xs: jnp.concatenate(xs, axis=1)
        acc = cat(alphas) * acc + _dot(vT_ref[0, 0, kp], cat(ps))
        return cat(m_out), cat(l_out), acc

    def sel_mask(kt, delta):
        rows = [jnp.broadcast_to(sel_ref[pl.ds(kt * blocks_per_tile + i, 1), :], (SEL_BLOCK, QT))
                for i in range(blocks_per_tile)]
        chosen = jnp.concatenate(rows, axis=0) > 0.5
        return chosen & (ik - iqq <= delta * QT)

    def win_mask(kt, delta):
        dist = delta * QT + iqq - ik
        return (dist >= 0) & (dist < WINDOW)

    init = (jnp.full((1, lanes), NEG, F32), jnp.zeros((1, lanes), F32), jnp.zeros((HEAD_DIM, lanes), F32))

    def merge(states):
        m = states[0][0]
        for st in states[1:]:
            m = jnp.maximum(m, st[0])
        l, acc = None, None
        for m_i, l_i, acc_i in states:
            w = jnp.exp(m_i - m)
            l = w * l_i if l is None else l + w * l_i
            acc = w * acc_i if acc is None else acc + w * acc_i
        return acc * jnp.where(l > 0.0, 1.0 / jnp.where(l > 0.0, l, 1.0), 0.0)

    diag = qt // 2
    def sel_body(i, carry):
        return tuple(attend(SEL_CHAINS * i + c, carry[c], ks_ref, vsT_ref, sel_mask) for c in range(SEL_CHAINS))

    o_s = merge(lax.fori_loop(0, (diag + SEL_CHAINS) // SEL_CHAINS, sel_body, (init,) * SEL_CHAINS))
    first = jnp.maximum(qt - WINDOW // QT, 0) // 2
    states = []
    for i in range((WINDOW // QT) // 2 + 1):
        kp = diag - i
        live = kp >= first
        states.append(attend(jnp.maximum(kp, 0), init, kw_ref, vwT_ref,
                             lambda kt, delta, live=live: win_mask(kt, delta) & live))
    o_w = merge(states)
    g = gT_ref[0, 0, 0]
    o_ref[0, 0, 0] = g[0:1, :] * o_c + g[1:2, :] * o_s + g[2:3, :] * o_w


def _nsa_prompt(p_n, W, b, t, nkv, ngrp, tm):
    nsa_w = nkv * ngrp * HEAD_DIM
    kvw = nkv * HEAD_DIM
    qn, rows2d, win2d, gates = _nsa_project_call(p_n, W['qk_norm_g'], nsa_w, kvw, tm)
    rows = rows2d.reshape(b, t, 4, nkv, HEAD_DIM)
    win = win2d.reshape(b, t, 2, nkv, HEAD_DIM)
    ns = t // CMP_STRIDE
    nc = ns - CMP_LEN // CMP_STRIDE + 1
    nb = t // SEL_BLOCK
    n_sel = min(SEL_TOPK, nb)
    nqt = t // QT
    parts = _cmp_parts_prompt(rows2d.reshape(b, t, 4 * kvw), _cmp_first_weights(W['cmp_w1'], nkv), b, t, kvw)
    kvc = _cmp_finish([parts], ns, W['cmp_b1'], W['cmp_w2'], W['cmp_b2'], W['qk_norm_g'][1], nc, nkv)
    kc = kvc[:, :nkv]
    vcT = kvc[:, nkv:].transpose(0, 1, 3, 2)
    qT = qn.reshape(b, nqt, QT, nkv, ngrp, HEAD_DIM).transpose(0, 3, 1, 5, 4, 2).reshape(b, nkv, nqt, HEAD_DIM, ngrp * QT)
    ng = 3 * nkv * ngrp
    gT = gates[:, :ng].reshape(b, nqt, QT, nkv, ngrp, 3).transpose(0, 3, 1, 5, 4, 2).reshape(b, nkv, nqt, 3, ngrp * QT)
    gT = jnp.pad(gT, ((0, 0), (0, 0), (0, 0), (0, 5), (0, 0)))
    assert nqt % (2 * SEL_CHAINS) == 0 and (WINDOW // QT) % 2 == 0
    npair = nqt // 2
    k_tiles = lambda x: x.transpose(0, 2, 1, 3).reshape(b, nkv, npair, 2 * QT, HEAD_DIM).astype(BF16)
    vT_tiles = lambda x: x.reshape(b, npair, 2 * QT, nkv, HEAD_DIM).transpose(0, 3, 1, 4, 2).astype(BF16)
    ks, vsT = k_tiles(rows[:, :, 2]), vT_tiles(rows[:, :, 3])
    kw, vwT = k_tiles(win[:, :, 0]), vT_tiles(win[:, :, 1])
    table = W['rel_bias'].astype(F32)
    c_end = jnp.arange(ns) * CMP_STRIDE + CMP_LEN - 1
    bias_c = _rel_table_np_dist(jnp.arange(t)[None, :] - c_end[:, None], table)
    bias_c = bias_c.transpose(2, 0, 1).reshape(nkv, ngrp, ns, t)
    dd = (jnp.arange(nqt)[:, None, None] * QT + jnp.arange(QT)[None, None, :] - jnp.arange(QT)[None, :, None])
    toep = _rel_table_np_dist(dd, table).transpose(3, 0, 1, 2).reshape(nkv, ngrp, nqt, QT, QT)
    lanes = ngrp * QT
    kv_spec = lambda shape: pl.BlockSpec((1, 1) + shape, lambda i, g, j: (i, g) + (0,) * len(shape))
    yT = pl.pallas_call(
        functools.partial(_nsa_prompt_kernel, nc=nc, nb=nb, n_sel=n_sel, ngrp=ngrp),
        grid=(b, nkv, nqt),
        in_specs=[pl.BlockSpec((1, 1, 1, HEAD_DIM, lanes), lambda i, g, j: (i, g, j, 0, 0)),
                  pl.BlockSpec((1, 1, 1, 8, lanes), lambda i, g, j: (i, g, j, 0, 0)),
                  kv_spec((ns, HEAD_DIM)), kv_spec((HEAD_DIM, ns)),
                  kv_spec((npair, 2 * QT, HEAD_DIM)), kv_spec((npair, HEAD_DIM, 2 * QT)),
                  kv_spec((npair, 2 * QT, HEAD_DIM)), kv_spec((npair, HEAD_DIM, 2 * QT)),
                  pl.BlockSpec((1, ngrp, ns, QT), lambda i, g, j: (g, 0, 0, j)),
                  pl.BlockSpec((1, ngrp, nqt, QT, QT), lambda i, g, j: (g, 0, 0, 0, 0))],
        out_specs=pl.BlockSpec((1, 1, 1, HEAD_DIM, lanes), lambda i, g, j: (i, g, j, 0, 0)),
        out_shape=jax.ShapeDtypeStruct((b, nkv, nqt, HEAD_DIM, lanes), F32),
        scratch_shapes=[pltpu.VMEM((ns + 16, QT), F32), pltpu.VMEM((_rup(nb, 8), QT), F32),
                        pltpu.VMEM((_rup(nb, 8), QT), F32)],
        compiler_params=_cparams(("arbitrary", "arbitrary", "arbitrary")),
        name="nsa_prompt_attn",
    )(qT, gT, kc, vcT, ks, vsT, kw, vwT, bias_c, toep)
    y = yT.reshape(b, nkv, nqt, HEAD_DIM, ngrp, QT).transpose(0, 2, 5, 1, 4, 3).reshape(b * t, nsa_w)
    return y, rows, win[:, t - min(WINDOW, t):]


def _cmp_part_paged_kernel(*refs, npg, kvw):
    x_refs, w_ref, o_ref = refs[1:1 + npg], refs[1 + npg], refs[2 + npg]
    rows = x_refs[0].shape[2]
    sub = rows // CMP_STRIDE
    ri = lax.broadcasted_iota(jnp.int32, (rows, rows), 0)
    ci = lax.broadcasted_iota(jnp.int32, (rows, rows), 1)
    perm = jnp.where(ci == (ri % sub) * CMP_STRIDE + ri // sub, 1.0, 0.0).astype(BF16)
    xp = [_dot_nt(perm, x[0].astype(BF16)) for x in x_refs]
    for typ in range(2):
        acc = None
        for s in range(CMP_STRIDE):
            xs = jnp.concatenate([p[s * sub:(s + 1) * sub, typ * kvw:(typ + 1) * kvw] for p in xp], axis=0)
            d = _dot(xs.astype(BF16), w_ref[typ, s])
            acc = d if acc is None else acc + d
        o_ref[0, :, typ * 2 * kvw:(typ + 1) * 2 * kvw] = acc


def _cmp_parts_sample(cache_pages, page_table, w_big, kvw, npg):
    b, n_pages = page_table.shape
    rows = cache_pages.shape[2]
    sub = rows // CMP_STRIDE
    in_specs = [pl.BlockSpec((1, 2 * kvw, rows), (lambda i, j, pt, k=k: (pt[i, j * npg + k], 0, 0)))
                for k in range(npg)]
    in_specs.append(pl.BlockSpec(w_big.shape, lambda i, j, pt: (0, 0, 0, 0)))
    return pl.pallas_call(
        functools.partial(_cmp_part_paged_kernel, npg=npg, kvw=kvw),
        grid_spec=pltpu.PrefetchScalarGridSpec(
            num_scalar_prefetch=1, grid=(b, n_pages // npg), in_specs=in_specs,
            out_specs=pl.BlockSpec((1, npg * sub, 4 * kvw), lambda i, j, pt: (i, j, 0))),
        out_shape=jax.ShapeDtypeStruct((b, n_pages * sub, 4 * kvw), F32),
        compiler_params=_cparams(("arbitrary", "arbitrary")),
        name="cmp_parts_paged",
    )(page_table, *([cache_pages] * npg), w_big)


def _inv_pos(l):
    return jnp.where(l > 0.0, 1.0 / jnp.where(l > 0.0, l, 1.0), 0.0)


def _nsa_sample_kernel(*refs, npg, n_steps, nc, nb, n_sel, past, ds, keep, ngrp, n_lanes):
    pt_ref = refs[0]
    q_ref, g_ref, kc_ref, vcT_ref, bc_ref, win_ref, wnew_ref, bw_ref, rnew_ref, bs_ref = refs[1:11]
    page_refs = refs[11:11 + npg]
    o_ref = refs[11 + npg]
    (m_ref, l_ref, acc_ref, base_ref, sel_ref, pg_ref, score_ref,
     kw_ref, kn_ref, vn_ref) = refs[12 + npg:]
    del pt_ref
    j = pl.program_id(1)
    scale = HEAD_DIM ** -0.5
    kvw = q_ref.shape[1]
    eye = jnp.where(lax.broadcasted_iota(jnp.int32, (LANES, LANES), 0)
                    == lax.broadcasted_iota(jnp.int32, (LANES, LANES), 1), 1.0, 0.0).astype(BF16)
    qbd = q_ref[0]
    lane = lax.broadcasted_iota(jnp.int32, (1, LANES), 1)
    qi = (lane // ngrp) % ds
    q_pos = past + qi
    n_pages = npg * n_steps

    @pl.when(j == 0)
    def _():
        nsp = kc_ref.shape[1]
        s = _dot(kc_ref[0].astype(BF16), qbd) * scale + bc_ref[...]
        crow = lax.broadcasted_iota(jnp.int32, (nsp, LANES), 0)
        ok = (crow * CMP_STRIDE + (CMP_LEN - 1) <= q_pos) & (crow < nc)
        p, _, _, l = _softmax_update(s, ok, jnp.full((1, LANES), NEG, F32), jnp.zeros((1, LANES), F32))
        p = p * _inv_pos(l)
        o_c = _dot(vcT_ref[0].astype(BF16), p.astype(BF16))
        li = lax.broadcasted_iota(jnp.int32, (LANES, LANES), 0)
        lj = lax.broadcasted_iota(jnp.int32, (LANES, LANES), 1)
        fold = jnp.where((li // ngrp == lj) & (li < n_lanes), 1.0, 0.0)
        p_grp = _dot(p, fold, HI)
        pad = 8
        pg_ref[...] = jnp.zeros(pg_ref.shape, F32)
        pg_ref[pad:pad + nsp, :] = p_grp
        nbp = sel_ref.shape[0]
        r1 = SEL_BLOCK // CMP_STRIDE
        offs, wts = _slc_offsets()
        p_slc = None
        for o, wt in zip(offs, wts):
            term = float(wt) * pg_ref[pl.ds(pad + int(o), nbp, stride=r1), :]
            p_slc = term if p_slc is None else p_slc + term
        cur = (past + lane % ds) // SEL_BLOCK
        jrow = lax.broadcasted_iota(jnp.int32, (nbp, LANES), 0)
        forced = (jrow == 0) | (jrow == cur) | (jrow == cur - 1)
        score = jnp.where((jrow > cur) | (jrow >= nb), -1.0, jnp.where(forced, 1e6, p_slc))
        selg = _extract_select(score, cur, n_sel)
        unfold = jnp.where((li == lj // ngrp) & (lj < n_lanes), 1.0, 0.0)
        sel_ref[...] = _dot(selg, unfold)
        wk = kw_ref.shape[0]
        nn = wnew_ref.shape[1]
        kw_ref[...] = jnp.zeros(kw_ref.shape, F32)
        for c0 in range(0, keep, LANES):
            kw_ref[c0:c0 + LANES, :] = _dot_nt(eye, win_ref[0, 0:kvw, c0:c0 + LANES].astype(BF16))
        kw_ref[keep:keep + nn, :] = wnew_ref[0, :, 0:kvw]
        vn_ref[...] = jnp.zeros(vn_ref.shape, F32)
        vn_ref[0:nn, :] = wnew_ref[0, :, kvw:2 * kvw]
        s = _dot(kw_ref[...].astype(BF16), qbd) * scale + bw_ref[...]
        irow = lax.broadcasted_iota(jnp.int32, (wk, LANES), 0)
        dist = jnp.where(irow < keep, keep + qi - irow, qi - (irow - keep))
        ok = (dist >= 0) & (dist < WINDOW) & (irow < keep + ds)
        p, _, _, l = _softmax_update(s, ok, jnp.full((1, LANES), NEG, F32), jnp.zeros((1, LANES), F32))
        p = (p * _inv_pos(l)).astype(BF16)
        o_w = (_dot(win_ref[0, kvw:2 * kvw, :].astype(BF16), p[0:keep, :])
               + _dot(vn_ref[...].T.astype(BF16), p[keep:keep + PAGE_SIZE, :]))
        g = g_ref[0]
        base_ref[...] = g[0:1, :] * o_c + g[2:3, :] * o_w
        m_ref[...] = jnp.full(m_ref.shape, NEG, F32)
        l_ref[...] = jnp.zeros(l_ref.shape, F32)
        acc_ref[...] = jnp.zeros(acc_ref.shape, F32)
        kn_ref[...] = jnp.zeros(kn_ref.shape, F32)
        vn_ref[...] = jnp.zeros(vn_ref.shape, F32)
        kn_ref[0:nn, :] = rnew_ref[0, :, 2 * kvw:3 * kvw]
        vn_ref[0:nn, :] = rnew_ref[0, :, 3 * kvw:4 * kvw]

    ik = lax.broadcasted_iota(jnp.int32, (PAGE_SIZE, LANES), 0)
    blocks_per_page = PAGE_SIZE // SEL_BLOCK

    def pages_update(k, v_t, first_page, count):
        bias = jnp.concatenate([bs_ref[first_page + i] for i in range(count)], axis=0)
        s = _dot(k, qbd) * scale + bias
        rows = [jnp.broadcast_to(sel_ref[pl.ds(first_page * blocks_per_page + i, 1), :], (SEL_BLOCK, LANES))
                for i in range(count * blocks_per_page)]
        key_pos = first_page * PAGE_SIZE + lax.broadcasted_iota(jnp.int32, (count * PAGE_SIZE, LANES), 0)
        mask = (jnp.concatenate(rows, axis=0) > 0.5) & (key_pos <= q_pos)
        p, m_new, alpha, l_new = _softmax_update(s, mask, m_ref[...], l_ref[...])
        m_ref[...] = m_new
        l_ref[...] = l_new
        acc_ref[...] = alpha * acc_ref[...] + _dot(v_t, p.astype(BF16))

    k_rows = jnp.concatenate([_dot_nt(eye, blk[0, 0:kvw, :].astype(BF16)) for blk in page_refs], axis=0)
    v_cols = jnp.concatenate([blk[0, kvw:2 * kvw, :].astype(BF16) for blk in page_refs], axis=1)
    pages_update(k_rows.astype(BF16), v_cols, j * npg, npg)

    @pl.when(j == n_steps - 1)
    def _():
        pages_update(kn_ref[...].astype(BF16), vn_ref[...].T.astype(BF16), n_pages, 1)
        o_ref[0] = base_ref[...] + g_ref[0][1:2, :] * (acc_ref[...] * _inv_pos(l_ref[...]))


def _nsa_sample(p_n, cache_kv, cache_win, page_table, W, db, ds, nkv, ngrp):
    nsa_w = nkv * ngrp * HEAD_DIM
    kvw = nkv * HEAD_DIM
    row_w = 4 * kvw
    qn, rows2d, win2d, gates = _nsa_project_call(p_n, W['qk_norm_g'], nsa_w, kvw, db * ds)
    rows_new = rows2d.reshape(db, ds, 4, nkv, HEAD_DIM)
    win_new = win2d.reshape(db, ds, 2, nkv, HEAD_DIM)
    n_pool = cache_kv.shape[0]
    n_pages = page_table.shape[1]
    past = n_pages * PAGE_SIZE
    keep = cache_win.shape[1]
    tot = past + _rup(ds, SEL_BLOCK)
    ns = tot // CMP_STRIDE
    nc = ns - CMP_LEN // CMP_STRIDE + 1
    nb = tot // SEL_BLOCK
    n_sel = min(SEL_TOPK, nb)
    nsp = _rup(ns, LANES)
    nbp = _rup(nb, 8)
    n_lanes = nkv * ds * ngrp
    assert n_lanes <= LANES and ds <= 8
    w_big = _cmp_first_weights(W['cmp_w1'], nkv)
    cache_pages = cache_kv.transpose(0, 2, 3, 4, 1).reshape(n_pool, row_w, PAGE_SIZE)
    parts_past = _cmp_parts_sample(cache_pages, page_table, w_big, kvw, min(MXU // (PAGE_SIZE // CMP_STRIDE), n_pages))
    npg = min(16, n_pages)
    rows_pad = jnp.pad(rows2d.reshape(db, ds, row_w), ((0, 0), (0, PAGE_SIZE - ds), (0, 0)))
    parts_new = _cmp_parts_prompt(rows_pad, w_big, db, PAGE_SIZE, kvw)
    kvc = _cmp_finish([parts_past, parts_new], nsp, W['cmp_b1'], W['cmp_w2'], W['cmp_b2'], W['qk_norm_g'][1], nc, nkv)
    kc_cat = kvc[:, :nkv].transpose(0, 2, 1, 3).reshape(db, nsp, kvw)
    vcT_cat = kvc[:, nkv:].transpose(0, 1, 3, 2).reshape(db, kvw, nsp)
    lane_pad = LANES - n_lanes
    q5 = qn.reshape(db, ds, nkv, ngrp, HEAD_DIM)
    qbd = jnp.einsum('bqgrd,gk->bgdkqr', q5, jnp.eye(nkv, dtype=F32)).reshape(db, kvw, n_lanes)
    qbd = jnp.pad(qbd, ((0, 0), (0, 0), (0, lane_pad))).astype(BF16)
    ng = 3 * nkv * ngrp
    gT = gates[:, :ng].reshape(db, ds, nkv, ngrp, 3).transpose(0, 4, 2, 1, 3).reshape(db, 3, n_lanes)
    gT = jnp.pad(gT, ((0, 0), (0, 5), (0, lane_pad)))
    lane = np.arange(LANES)
    live = lane < n_lanes
    head_of_lane = np.where(live, (lane // (ds * ngrp)) * ngrp + lane % ngrp, 0)
    qi = np.where(live, (lane // ngrp) % ds, 0)
    table_l = W['rel_bias'].astype(F32)[:, head_of_lane]
    bias_of = lambda dist: _rel_table_per_lane(dist, table_l)
    c_end = np.arange(nsp) * CMP_STRIDE + CMP_LEN - 1
    bias_c = bias_of(jnp.asarray(past + qi[None, :] - c_end[:, None], jnp.int32))
    pos = np.arange((n_pages + 1) * PAGE_SIZE)
    bias_s = bias_of(jnp.asarray(past + qi[None, :] - pos[:, None], jnp.int32)).reshape(n_pages + 1, PAGE_SIZE, LANES)
    wk = _rup(keep + 8, LANES)
    irow = np.arange(wk)[:, None]
    dist_w = np.where(irow < keep, keep + qi[None, :] - irow, qi[None, :] - (irow - keep))
    bias_w = bias_of(jnp.asarray(dist_w, jnp.int32))
    win_c = cache_win.transpose(0, 2, 3, 4, 1).reshape(db, 2 * kvw, keep)
    wnew8 = jnp.pad(win2d.reshape(db, ds, 2 * kvw), ((0, 0), (0, 8 - ds), (0, 0)))
    rnew8 = jnp.pad(rows2d.reshape(db, ds, row_w), ((0, 0), (0, 8 - ds), (0, 0)))
    n_steps = n_pages // npg
    per_b = lambda shape: pl.BlockSpec((1,) + shape, lambda i, j, pt: (i,) + (0,) * len(shape))
    const = lambda shape: pl.BlockSpec(shape, lambda i, j, pt: (0,) * len(shape))
    in_specs = [per_b((kvw, LANES)), per_b((8, LANES)), per_b((nsp, kvw)), per_b((kvw, nsp)), const((nsp, LANES)),
                per_b((2 * kvw, keep)), per_b((8, 2 * kvw)), const((wk, LANES)), per_b((8, row_w)),
                const((n_pages + 1, PAGE_SIZE, LANES))]
    in_specs += [pl.BlockSpec((1, 2 * kvw, PAGE_SIZE), (lambda i, j, pt, k=k: (pt[i, j * npg + k], 1, 0)))
                 for k in range(npg)]
    yT = pl.pallas_call(
        functools.partial(_nsa_sample_kernel, npg=npg, n_steps=n_steps, nc=nc, nb=nb, n_sel=n_sel, past=past,
                          ds=ds, keep=keep, ngrp=ngrp, n_lanes=n_lanes),
        grid_spec=pltpu.PrefetchScalarGridSpec(
            num_scalar_prefetch=1, grid=(db, n_steps), in_specs=in_specs,
            out_specs=pl.BlockSpec((1, kvw, LANES), lambda i, j, pt: (i, 0, 0)),
            scratch_shapes=[pltpu.VMEM((1, LANES), F32), pltpu.VMEM((1, LANES), F32), pltpu.VMEM((kvw, LANES), F32),
                            pltpu.VMEM((kvw, LANES), F32), pltpu.VMEM((nbp, LANES), F32),
                            pltpu.VMEM((nsp + 16, LANES), F32), pltpu.VMEM((nbp, LANES), F32),
                            pltpu.VMEM((wk, kvw), F32),
                            pltpu.VMEM((PAGE_SIZE, kvw), F32), pltpu.VMEM((PAGE_SIZE, kvw), F32)]),
        out_shape=jax.ShapeDtypeStruct((db, kvw, LANES), F32),
        compiler_params=_cparams(("arbitrary", "arbitrary")),
        name="nsa_sample_attn",
    )(page_table, qbd, gT, kc_cat, vcT_cat, bias_c, win_c, wnew8, bias_w, rnew8, bias_s, *([cache_pages] * npg))
    y6 = yT[:, :, :n_lanes].reshape(db, nkv, HEAD_DIM, nkv, ds, ngrp)
    y = jnp.einsum('bgdgqr->bqgrd', y6).reshape(db * ds, nsa_w)
    win_all = jnp.concatenate([cache_win, win_new.astype(cache_win.dtype)], axis=1)
    n_keep = min(WINDOW, past + ds)
    return y, rows_new, win_all[:, win_all.shape[1] - n_keep:]


def _rel_bucket(dist):
    d = jnp.maximum(dist, 0)
    exact = REL_BUCKETS // 2
    ratio = jnp.maximum(d, exact).astype(F32) / exact
    large = exact + (jnp.log(ratio) / math.log(REL_MAX_DIST / exact) * (REL_BUCKETS - exact)).astype(jnp.int32)
    return jnp.where(d < exact, d, jnp.minimum(large, REL_BUCKETS - 1))


def _slc_offsets():
    r1 = SEL_BLOCK // CMP_STRIDE
    r2 = CMP_LEN // CMP_STRIDE
    offs = np.arange(-(r2 - 1), r1)
    wts = np.array([sum(1 for m in range(r1) for n in range(r2) if m - n == o) for o in offs], np.float32)
    return offs, wts


def _top_values(x, k):
    n = x.shape[0]
    row = lax.broadcasted_iota(jnp.int32, x.shape, 0)
    vals = []
    for _ in range(k):
        m = jnp.max(x, axis=0, keepdims=True)
        vals.append(m)
        first = jnp.min(jnp.where(x == m, row, n), axis=0, keepdims=True)
        x = jnp.where(row == first, -jnp.inf, x)
    return vals


def _peer_route_kernel(hT_ref, wq_ref, sk_ref, th_ref, c1_ref, s2_ref, e2_ref, *, nheads, topk):
    nk, half = sk_ref.shape[1], sk_ref.shape[2]
    cpt = s2_ref.shape[1]
    qT = _dot(wq_ref[...], hT_ref[...])
    pairs = [(a, b) for a in range(topk) for b in range(topk) if (a + 1) * (b + 1) <= topk]
    for h in range(nheads):
        base = h * 2 * half
        s1 = _dot(sk_ref[0].astype(BF16), qT[base:base + half].astype(BF16))
        s2 = _dot(sk_ref[1].astype(BF16), qT[base + half:base + 2 * half].astype(BF16))
        v1 = _top_values(s1, topk)
        v2 = _top_values(s2, topk)
        sums = [v1[a] + v2[b] for a, b in pairs]
        cand = jnp.concatenate(sums, axis=0)
        tau = _top_values(cand, topk)[-1]
        z = jnp.sum(jnp.where(cand >= tau, jnp.exp(cand - (v1[0] + v2[0])), 0.0), axis=0, keepdims=True)
        theta = jnp.full(s1.shape, jnp.inf, F32)
        for a in range(topk):
            th_a = jnp.full(tau.shape, jnp.inf, F32)
            for (pa, pb), sm in zip(pairs, sums):
                if pa == a:
                    th_a = jnp.where(sm >= tau, v2[pb], th_a)
            theta = jnp.where(s1 == v1[a], th_a, theta)
        c1 = jnp.exp(s1 - v1[0]) / z
        e2 = jnp.exp(s2 - v2[0])
        for c in range(cpt):
            cs = slice(c * LANES, (c + 1) * LANES)
            th_ref[h, c] = theta[:, cs]
            c1_ref[h, c] = c1[:, cs]
            s2_ref[h, c] = s2[:, cs]
            e2_ref[h, c] = e2[:, cs]


def _peer_expert_kernel(hT_ref, x1_ref, ga_ref, th_ref, c1_ref, s2_ref, e2_ref, u_ref, v_ref,
                        o_ref, acc_ref, gate_ref, *, nheads, n_eblocks):
    eb = pl.program_id(1)
    n_chunks, nk = s2_ref.shape[1], s2_ref.shape[2]
    rows_per_block = u_ref.shape[0] // nk

    @pl.when(eb == 0)
    def _():
        acc_ref[...] = jnp.zeros(acc_ref.shape, F32)

    def gate_tile(i, c):
        i1 = eb * rows_per_block + i
        wd = None
        for h in range(nheads):
            chosen = s2_ref[h, c] >= th_ref[h, c, pl.ds(i1, 1), :]
            term = jnp.where(chosen, e2_ref[h, c], 0.0) * c1_ref[h, c, pl.ds(i1, 1), :]
            wd = term if wd is None else wd + term
        gate_ref[i, c] = wd

    def gate_step(it, carry):
        gate_tile(it // n_chunks, it % n_chunks)
        return carry

    lax.fori_loop(0, rows_per_block * n_chunks, gate_step, 0)
    act = jax.nn.gelu(_dot(u_ref[...], hT_ref[...]))
    gates = jnp.concatenate([jnp.concatenate([gate_ref[i, c] for c in range(n_chunks)], axis=1)
                             for i in range(rows_per_block)], axis=0)
    acc_ref[...] += _dot((gates * act).T.astype(BF16), v_ref[...])

    @pl.when(eb == n_eblocks - 1)
    def _():
        o_ref[...] = x1_ref[...] + ga_ref[0] * acc_ref[...]


def _peer(h2, x1, ga, W, tm):
    t, d = h2.shape
    sub_keys = W['peer_sub_keys']
    nk, half = sub_keys.shape[1], sub_keys.shape[2]
    qd = W['peer_w_query'].shape[1]
    nheads = qd // (2 * half)
    hT = h2.T
    wqT = W['peer_w_query'].T.astype(BF16)
    cpt = tm // LANES
    route_shape = jax.ShapeDtypeStruct((nheads, t // LANES, nk, LANES), F32)
    rspec = pl.BlockSpec((nheads, cpt, nk, LANES), lambda i: (0, i, 0, 0))
    th, c1, s2, e2 = pl.pallas_call(
        functools.partial(_peer_route_kernel, nheads=nheads, topk=PEER_TOPK),
        grid=(t // tm,),
        in_specs=[pl.BlockSpec((d, tm), lambda i: (0, i)),
                  pl.BlockSpec((qd, d), lambda i: (0, 0)),
                  pl.BlockSpec(sub_keys.shape, lambda i: (0, 0, 0))],
        out_specs=[rspec] * 4,
        out_shape=[route_shape] * 4,
        compiler_params=_cparams(("arbitrary",)),
        name="peer_route",
    )(hT, wqT, sub_keys)
    eblk = 2 * MXU
    n_eblocks = W['peer_u'].shape[0] // eblk
    nmod, rows, _ = ga.shape
    tiles_per_mod = (t // tm) // nmod
    rspec2 = pl.BlockSpec((nheads, cpt, nk, LANES), lambda i, e: (0, i, 0, 0))
    return pl.pallas_call(
        functools.partial(_peer_expert_kernel, nheads=nheads, n_eblocks=n_eblocks),
        grid=(t // tm, n_eblocks),
        in_specs=[pl.BlockSpec((d, tm), lambda i, e: (0, i)),
                  pl.BlockSpec((tm, d), lambda i, e: (i, 0)),
                  pl.BlockSpec((1, rows, d), lambda i, e: (i // tiles_per_mod, 0, 0)),
                  rspec2, rspec2, rspec2, rspec2,
                  pl.BlockSpec((eblk, d), lambda i, e: (e, 0)),
                  pl.BlockSpec((eblk, d), lambda i, e: (e, 0))],
        out_specs=pl.BlockSpec((tm, d), lambda i, e: (i, 0)),
        out_shape=jax.ShapeDtypeStruct((t, d), F32),
        scratch_shapes=[pltpu.VMEM((tm, d), F32), pltpu.VMEM((eblk // nk, cpt, nk, LANES), F32)],
        compiler_params=_cparams(("arbitrary", "arbitrary")),
        name="peer_experts",
    )(hT, x1, ga, th, c1, s2, e2, W['peer_u_bf16'], W['peer_v_bf16'])


def _layer(x, mods, nsa_fn, shift_prev, wkv0, W, tm, rwkv_chunk):
    b, t, d = x.shape
    cw = W['rwkv_w0'].shape[0]
    nw, na, ng = W['rwkv_w_up'].shape[0], W['rwkv_a_up'].shape[0], W['rwkv_g_up'].shape[0]
    rwkv_proj = 3 * cw + nw + na + ng
    sh1, sc1, ga1, sh2, sc2, ga2 = mods
    xf = x.reshape(b * t, d)
    if (b * t) % tm == 0 and t % tm == 0:
        as_mod = lambda m: m.reshape(b, 1, d)
    else:
        tm = b * t
        as_mod = lambda m: jnp.repeat(m, t, axis=0).reshape(1, b * t, d)
    w_r = _rwkv_pad_cols(W['w_in'][:, :rwkv_proj], cw, nw, na, ng).astype(BF16)
    nsa_cols = W['w_in'].shape[1] - rwkv_proj
    w_n = jnp.pad(W['w_in'][:, rwkv_proj:], ((0, 0), (0, _rup(nsa_cols, LANES) - nsa_cols))).astype(BF16)
    p_r, _ = _norm_mod_matmul(xf, W['norm1_g'], as_mod(sc1), as_mod(sh1), w_r, tm, w_r.shape[1] // 2)
    p_n, _ = _norm_mod_matmul(xf, W['norm1_g'], as_mod(sc1), as_mod(sh1), w_n, tm, w_n.shape[1])
    pr = p_r.shape[1]
    p_r = p_r.reshape(b, t, pr)
    shift_new = _rwkv_unpad_cols(p_r[:, -1], cw, nw, na, ng)
    tpad = _rup(t, rwkv_chunk)
    p_r_pad = jnp.pad(p_r, ((0, 0), (0, tpad - t), (0, 0)))
    y_r, wkv_new = _rwkv_mix(p_r_pad, shift_prev, wkv0, W, rwkv_chunk, t)
    y_r = y_r[:, :t].reshape(b * t, cw)
    y_n, rows, win = nsa_fn(p_n)
    w_out = W['w_out'].astype(BF16)
    x1 = _out_proj(xf, y_r, y_n, as_mod(ga1), w_out[:cw], w_out[cw:], tm, 1024)
    h2 = _norm_mod(x1, W['norm2_g'], as_mod(sc2), as_mod(sh2), tm)
    assert (b * t) % LANES == 0, "PEER kernels keep tokens on lanes"
    out = _peer(h2, x1, as_mod(ga2), W, tm)
    return out.reshape(b, t, d), rows, win, wkv_new, shift_new


def kernel(x_prompt, x_sample, c_prompt, c_sample, cache_kv, cache_win, state_wkv, state_shift, page_table,
           norm1_g, norm2_g, w_ada, b_ada, w_in, w_out,
           rwkv_mu, rwkv_w0, rwkv_w_up, rwkv_a0, rwkv_a_up, rwkv_g_up, rwkv_k_k, rwkv_k_a, rwkv_r_k, lnx_w, lnx_b,
           qk_norm_g, cmp_w1, cmp_b1, cmp_w2, cmp_b2, rel_bias,
           peer_w_query, peer_sub_keys, peer_u, peer_v):
    W = dict(norm1_g=norm1_g, norm2_g=norm2_g, w_ada=w_ada, b_ada=b_ada, w_in=w_in, w_out=w_out,
             rwkv_mu=rwkv_mu, rwkv_w0=rwkv_w0, rwkv_w_up=rwkv_w_up, rwkv_a0=rwkv_a0, rwkv_a_up=rwkv_a_up,
             rwkv_g_up=rwkv_g_up, rwkv_k_k=rwkv_k_k, rwkv_k_a=rwkv_k_a, rwkv_r_k=rwkv_r_k, lnx_w=lnx_w, lnx_b=lnx_b,
             qk_norm_g=qk_norm_g, cmp_w1=cmp_w1, cmp_b1=cmp_b1, cmp_w2=cmp_w2, cmp_b2=cmp_b2, rel_bias=rel_bias,
             peer_w_query=peer_w_query, peer_sub_keys=peer_sub_keys, peer_u=peer_u, peer_v=peer_v)
    W['peer_u_bf16'] = peer_u.astype(BF16)
    W['peer_v_bf16'] = peer_v.astype(BF16)
    bp, seq, d = x_prompt.shape
    db = x_sample.shape[0]
    nkv = cache_kv.shape[3]
    nh_r = rwkv_w0.shape[0] // HEAD_DIM
    ngrp = (w_out.shape[0] - rwkv_w0.shape[0]) // HEAD_DIM // nkv

    mods = _ada_mods(jnp.concatenate([c_prompt, c_sample], axis=0), w_ada, b_ada)
    mods = mods.reshape(bp + db, N_MODS, d)
    mods_p = [mods[:bp, i] for i in range(N_MODS)]
    mods_s = [mods[bp:, i] for i in range(N_MODS)]

    shift0 = jnp.zeros((bp, state_shift.shape[1]), F32)
    wkv0 = jnp.zeros((bp, nh_r, HEAD_DIM, HEAD_DIM), F32)
    y_p, rows_p, win_p, wkv_p, shift_p = _layer(
        x_prompt, mods_p, lambda pn: _nsa_prompt(pn, W, bp, seq, nkv, ngrp, 512), shift0, wkv0, W, 512, 64)
    y_s, rows_s, win_s, wkv_s, shift_s = _layer(
        x_sample, mods_s,
        lambda pn: _nsa_sample(pn, cache_kv, cache_win, page_table, W, db, x_sample.shape[1], nkv, ngrp),
        state_shift, state_wkv, W, 512, 32)
    return (y_p, y_s, rows_p, win_p, wkv_p.astype(state_wkv.dtype), shift_p,
            rows_s, win_s, wkv_s.astype(state_wkv.dtype), shift_s)
```

```python
import functools
import math

import numpy as np
import jax
import jax.numpy as jnp
from jax import lax
from jax.experimental import pallas as pl
from jax.experimental.pallas import tpu as pltpu

F32 = jnp.float32
BF16 = jnp.bfloat16
HI = lax.Precision.HIGHEST

HEAD_DIM = 64
PAGE_SIZE = 128
CMP_LEN = 32
CMP_STRIDE = 16
SEL_BLOCK = 64
SEL_TOPK = 16
WINDOW = 512
REL_BUCKETS = 32
REL_MAX_DIST = 2048
PEER_TOPK = 16
N_MODS = 6
RMS_EPS = 1e-6
LNX_EPS = 64e-5
NEG = -1e30

LANES = 128
MXU = 256
HEADS_PER_GROUP = MXU // HEAD_DIM
VMEM_LIMIT = 56 * 1024 * 1024


def _cparams(sem):
    return pltpu.CompilerParams(dimension_semantics=sem, vmem_limit_bytes=VMEM_LIMIT)


def _dot(a, b, precision=None):
    return jnp.dot(a, b, preferred_element_type=F32, precision=precision)


def _dot_nt(a, b, precision=None):
    return lax.dot_general(a, b, (((1,), (1,)), ((), ())), preferred_element_type=F32, precision=precision)


def _split_bf16(x, parts):
    out = []
    for _ in range(parts):
        h = x.astype(BF16)
        out.append(h)
        x = x - h.astype(F32)
    return out


def _mm(a, b, mode, nt=False):
    f = _dot_nt if nt else _dot
    if mode == 6:
        return f(a, b, HI)
    if mode == 1:
        return f(a.astype(BF16), b.astype(BF16))
    if mode == 3:
        ah, al = _split_bf16(a, 2)
        bh, bl = _split_bf16(b, 2)
        return (f(al, bh) + f(ah, bl)) + f(ah, bh)
    if mode[0] == 'L':
        terms = [f(t, b.astype(BF16)) for t in _split_bf16(a, int(mode[1]))]
    else:
        terms = [f(a.astype(BF16), t) for t in _split_bf16(b, int(mode[1]))]
    out = terms[-1]
    for t in terms[-2::-1]:
        out = out + t
    return out


RWKV_MM = dict(lora=1, headsum='L2', cumsum='R3', gram=1, inverse=1, state=1)


def _ada_kernel(c_ref, w_ref, b_ref, o_ref):
    c = c_ref[...]
    s = c * jax.nn.sigmoid(c)
    o_ref[...] = _dot(s.astype(BF16), w_ref[...].astype(BF16)) + b_ref[...]


def _ada_mods(c, w_ada, b_ada):
    n, d = c.shape
    cols = w_ada.shape[1]
    tn = 1024
    return pl.pallas_call(
        _ada_kernel,
        grid=(cols // tn,),
        in_specs=[pl.BlockSpec((n, d), lambda j: (0, 0)),
                  pl.BlockSpec((d, tn), lambda j: (0, j)),
                  pl.BlockSpec((1, tn), lambda j: (0, j))],
        out_specs=pl.BlockSpec((n, tn), lambda j: (0, j)),
        out_shape=jax.ShapeDtypeStruct((n, cols), F32),
        compiler_params=_cparams(("arbitrary",)),
        name="ada_mods",
    )(c, w_ada, b_ada.reshape(1, cols))


def _nmm_kernel(x_ref, g_ref, sc_ref, sh_ref, w_ref, o_ref, h_ref):
    @pl.when(pl.program_id(1) == 0)
    def _():
        x = x_ref[...]
        ms = jnp.mean(x * x, axis=-1, keepdims=True)
        y = x * lax.rsqrt(ms + RMS_EPS) * g_ref[...]
        h_ref[...] = (y * (1.0 + sc_ref[0]) + sh_ref[0]).astype(h_ref.dtype)

    o_ref[...] = _dot(h_ref[...], w_ref[...])


def _norm_mod_matmul(x, g, sc, sh, w, tm, tn):
    t, d = x.shape
    n = w.shape[1]
    nmod, rows, _ = sc.shape
    tiles_per_mod = (t // tm) // nmod
    mod_spec = pl.BlockSpec((1, rows, d), lambda i, j: (i // tiles_per_mod, 0, 0))
    return pl.pallas_call(
        _nmm_kernel,
        grid=(t // tm, n // tn),
        in_specs=[pl.BlockSpec((tm, d), lambda i, j: (i, 0)),
                  pl.BlockSpec((1, d), lambda i, j: (0, 0)),
                  mod_spec, mod_spec,
                  pl.BlockSpec((d, tn), lambda i, j: (0, j))],
        out_specs=[pl.BlockSpec((tm, tn), lambda i, j: (i, j)),
                   pl.BlockSpec((tm, d), lambda i, j: (i, 0))],
        out_shape=[jax.ShapeDtypeStruct((t, n), F32), jax.ShapeDtypeStruct((t, d), BF16)],
        compiler_params=_cparams(("arbitrary", "arbitrary")),
        name="norm_mod_matmul",
    )(x, g.reshape(1, d), sc, sh, w)


def _nm_kernel(x_ref, g_ref, sc_ref, sh_ref, h_ref):
    x = x_ref[...]
    ms = jnp.mean(x * x, axis=-1, keepdims=True)
    y = x * lax.rsqrt(ms + RMS_EPS) * g_ref[...]
    h_ref[...] = (y * (1.0 + sc_ref[0]) + sh_ref[0]).astype(h_ref.dtype)


def _norm_mod(x, g, sc, sh, tm):
    t, d = x.shape
    nmod, rows, _ = sc.shape
    tiles_per_mod = (t // tm) // nmod
    mod_spec = pl.BlockSpec((1, rows, d), lambda i: (i // tiles_per_mod, 0, 0))
    return pl.pallas_call(
        _nm_kernel,
        grid=(t // tm,),
        in_specs=[pl.BlockSpec((tm, d), lambda i: (i, 0)), pl.BlockSpec((1, d), lambda i: (0, 0)), mod_spec, mod_spec],
        out_specs=pl.BlockSpec((tm, d), lambda i: (i, 0)),
        out_shape=jax.ShapeDtypeStruct((t, d), BF16),
        compiler_params=_cparams(("arbitrary",)),
        name="norm_mod",
    )(x, g.reshape(1, d), sc, sh)


def _outproj_kernel(x_ref, yr_ref, yn_ref, ga_ref, w1_ref, w2_ref, o_ref):
    acc = _dot(yr_ref[...].astype(BF16), w1_ref[...]) + _dot(yn_ref[...].astype(BF16), w2_ref[...])
    o_ref[...] = x_ref[...] + ga_ref[0] * acc


def _out_proj(x, y_r, y_n, ga, w1, w2, tm, tn):
    t, d = x.shape
    nmod, rows, _ = ga.shape
    tiles_per_mod = (t // tm) // nmod
    cr, cn = y_r.shape[1], y_n.shape[1]
    return pl.pallas_call(
        _outproj_kernel,
        grid=(t // tm, d // tn),
        in_specs=[pl.BlockSpec((tm, tn), lambda i, j: (i, j)),
                  pl.BlockSpec((tm, cr), lambda i, j: (i, 0)),
                  pl.BlockSpec((tm, cn), lambda i, j: (i, 0)),
                  pl.BlockSpec((1, rows, tn), lambda i, j: (i // tiles_per_mod, 0, j)),
                  pl.BlockSpec((cr, tn), lambda i, j: (0, j)),
                  pl.BlockSpec((cn, tn), lambda i, j: (0, j))],
        out_specs=pl.BlockSpec((tm, tn), lambda i, j: (i, j)),
        out_shape=jax.ShapeDtypeStruct((t, d), F32),
        compiler_params=_cparams(("arbitrary", "arbitrary")),
        name="out_proj",
    )(x, y_r, y_n, ga, w1, w2)


def _softplus(z):
    return jnp.maximum(z, 0.0) + jnp.log(1.0 + jnp.exp(-jnp.abs(z)))


def _rwkv_kernel(p_ref, shift_ref, s0_ref, mu_ref, vec_ref, wup_ref, aup_ref, gup_ref,
                 y_ref, sfin_ref, carry, state, *, t_valid, n_chunks):
    c = pl.program_id(1)
    chunk = p_ref.shape[1]
    cw = vec_ref.shape[1]
    n_groups = cw // MXU
    hg = HEADS_PER_GROUP
    rows_g = hg * chunk

    @pl.when(c == 0)
    def _():
        carry[...] = shift_ref[0]
        state[...] = s0_ref[0]

    p = p_ref[0]
    row = lax.broadcasted_iota(jnp.int32, (chunk, 1), 0)
    prev = jnp.where(row == 0, carry[...], pltpu.roll(p, 1, axis=0))
    carry[...] = p[chunk - 1:chunk, :]
    xs = p + (prev - p) * mu_ref[...]

    w0, a0, k_k, k_a, r_k, lnx_w, lnx_b = (vec_ref[i:i + 1, :] for i in range(7))
    r = xs[:, 0:cw]
    k = xs[:, cw:2 * cw]
    v = xs[:, 2 * cw:3 * cw]
    o = 3 * cw
    nw, na, ng = wup_ref.shape[0], aup_ref.shape[0], gup_ref.shape[0]
    xw = xs[:, o:o + nw]
    xa = xs[:, o + nw:o + nw + na]
    xg = xs[:, o + nw + na:o + nw + na + ng]
    pm = RWKV_MM
    w_log = -_softplus(-(w0 + _mm(jnp.tanh(xw), wup_ref[...], pm['lora']))) - 0.5
    a = jax.nn.sigmoid(a0 + _mm(xa, aup_ref[...], pm['lora']))
    gate = _mm(jax.nn.sigmoid(xg), gup_ref[...], pm['lora'])

    gi = lax.broadcasted_iota(jnp.int32, (MXU, MXU), 0) // HEAD_DIM
    gj = lax.broadcasted_iota(jnp.int32, (MXU, MXU), 1) // HEAD_DIM
    ones_bd = (gi == gj).astype(F32)

    def head_sum(x):
        return jnp.concatenate([_mm(x[:, g * MXU:(g + 1) * MXU], ones_bd, pm['headsum']) for g in range(n_groups)],
                               axis=1)

    kk = k * k_k
    kk = kk / jnp.maximum(jnp.sqrt(head_sum(kk * kk)), 1e-12)
    k2 = k * (1.0 + (a - 1.0) * k_a)
    logdec = -jnp.exp(w_log)
    if t_valid < chunk * n_chunks:
        valid = (row + c * chunk) < t_valid
        logdec = jnp.where(valid, logdec, 0.0)
        kk = jnp.where(valid, kk, 0.0)
        k2 = jnp.where(valid, k2, 0.0)
        v = jnp.where(valid, v, 0.0)

    ti = lax.broadcasted_iota(jnp.int32, (chunk, chunk), 0)
    tj = lax.broadcasted_iota(jnp.int32, (chunk, chunk), 1)
    cum = _mm((tj <= ti).astype(F32), logdec, pm['cumsum'])
    cum_end = cum[chunk - 1:chunk, :]
    e_neg = jnp.exp(-cum)
    e_rem = jnp.exp(cum_end - cum)
    r_t = r * jnp.exp(cum)
    a_t = -kk * jnp.exp(cum - logdec)
    b_vec = kk * a
    b_t = b_vec * e_neg
    k_t = k2 * e_neg
    b_rem = b_vec * e_rem
    k_rem = k2 * e_rem
    w_end = jnp.exp(cum_end)

    lane_head = lax.broadcasted_iota(jnp.int32, (chunk, MXU), 1) // HEAD_DIM
    ri = lax.broadcasted_iota(jnp.int32, (rows_g, rows_g), 0)
    rj = lax.broadcasted_iota(jnp.int32, (rows_g, rows_g), 1)
    strict = rj < ri
    incl = rj <= ri
    eye_r = (ri == rj).astype(F32)
    di = lax.broadcasted_iota(jnp.int32, (MXU, MXU), 0)
    dj = lax.broadcasted_iota(jnp.int32, (MXU, MXU), 1)
    n_double = max(int(math.ceil(math.log2(chunk))) - 1, 0)

    def bd(x):
        return jnp.concatenate([jnp.where(lane_head == h, x, 0.0) for h in range(hg)], axis=0)

    def stack(x):
        return jnp.concatenate([x[:, h * HEAD_DIM:(h + 1) * HEAD_DIM] for h in range(hg)], axis=0)

    def unstack(x):
        return jnp.concatenate([x[h * chunk:(h + 1) * chunk, :] for h in range(hg)], axis=1)

    ys = []
    for g in range(n_groups):
        sl = slice(g * MXU, (g + 1) * MXU)
        a_bd, r_bd = bd(a_t[:, sl]), bd(r_t[:, sl])
        b_bd, k_bd = bd(b_t[:, sl]), bd(k_t[:, sl])
        v_st = stack(v[:, sl])
        a_ab = jnp.where(strict, _mm(a_bd, b_bd, pm['gram'], nt=True), 0.0)
        a_ak = jnp.where(strict, _mm(a_bd, k_bd, pm['gram'], nt=True), 0.0)
        a_rb = jnp.where(incl, _mm(r_bd, b_bd, pm['gram'], nt=True), 0.0)
        a_rk = jnp.where(incl, _mm(r_bd, k_bd, pm['gram'], nt=True), 0.0)
        tinv = eye_r + a_ab
        pw = a_ab
        for _ in range(n_double):
            pw = _mm(pw, pw, pm['inverse'])
            tinv = tinv + _mm(pw, tinv, pm['inverse'])
        s0 = state[g]
        z = _mm(a_bd, s0, pm['state']) + _mm(a_ak, v_st, pm['state'])
        u = _mm(tinv, z, pm['state'])
        y_st = _mm(r_bd, s0, pm['state']) + _mm(a_rb, u, pm['state']) + _mm(a_rk, v_st, pm['state'])
        w_col = jnp.sum(jnp.where(di == dj, jnp.broadcast_to(w_end[:, sl], (MXU, MXU)), 0.0), axis=1, keepdims=True)
        state[g] = (w_col * s0 + _mm(bd(b_rem[:, sl]).T, u, pm['state'])
                    + _mm(bd(k_rem[:, sl]).T, v_st, pm['state']))
        ys.append(unstack(y_st))
    y = jnp.concatenate(ys, axis=1)

    inv_n = 1.0 / HEAD_DIM
    mean = head_sum(y) * inv_n
    d = y - mean
    var = head_sum(d * d) * inv_n
    yn = d * lax.rsqrt(var + LNX_EPS) * lnx_w + lnx_b
    bonus = head_sum(r * k2 * r_k) * v
    y_ref[0] = ((yn + bonus) * gate).astype(y_ref.dtype)

    @pl.when(c == n_chunks - 1)
    def _():
        sfin_ref[0] = state[...]


def _rwkv_pad_cols(x, cw, nw, na, ng):
    o = 3 * cw
    parts = [x[..., :o + nw], x[..., o + nw:o + nw + na], x[..., o + nw + na:]]
    widths = [o + _rup(nw, LANES), _rup(na, LANES), _rup(ng, LANES)]
    out = []
    for part, wd in zip(parts, widths):
        pad = [(0, 0)] * (x.ndim - 1) + [(0, wd - part.shape[-1])]
        out.append(jnp.pad(part, pad))
    return jnp.concatenate(out, axis=-1)


def _rwkv_unpad_cols(x, cw, nw, na, ng):
    o = 3 * cw
    o2 = o + _rup(nw, LANES)
    o3 = o2 + _rup(na, LANES)
    return jnp.concatenate([x[..., :o + nw], x[..., o2:o2 + na], x[..., o3:o3 + ng]], axis=-1)


def _rup(x, m):
    return (x + m - 1) // m * m


def _rwkv_mix(p_r, shift_prev, wkv0, W, chunk, t_valid):
    b, tpad, pr = p_r.shape
    cw = W['rwkv_w0'].shape[0]
    nh = cw // HEAD_DIM
    n_groups = cw // MXU
    nw, na, ng = W['rwkv_w_up'].shape[0], W['rwkv_a_up'].shape[0], W['rwkv_g_up'].shape[0]
    n_chunks = tpad // chunk
    mu = _rwkv_pad_cols(W['rwkv_mu'], cw, nw, na, ng).reshape(1, pr)
    vecs = jnp.stack([W['rwkv_w0'], W['rwkv_a0'], W['rwkv_k_k'], W['rwkv_k_a'], W['rwkv_r_k'].reshape(cw),
                      W['lnx_w'], W['lnx_b'], jnp.zeros((cw,), F32)])
    wup = jnp.pad(W['rwkv_w_up'], ((0, _rup(nw, LANES) - nw), (0, 0)))
    aup = jnp.pad(W['rwkv_a_up'], ((0, _rup(na, LANES) - na), (0, 0)))
    gup = jnp.pad(W['rwkv_g_up'], ((0, _rup(ng, LANES) - ng), (0, 0)))
    shift3 = _rwkv_pad_cols(shift_prev, cw, nw, na, ng).reshape(b, 1, pr)
    s0 = wkv0.astype(F32).transpose(0, 1, 3, 2).reshape(b, n_groups, MXU, HEAD_DIM)
    const = lambda shape: pl.BlockSpec(shape, lambda i, c: (0,) * len(shape))
    y, sfin = pl.pallas_call(
        functools.partial(_rwkv_kernel, t_valid=t_valid, n_chunks=n_chunks),
        grid=(b, n_chunks),
        in_specs=[pl.BlockSpec((1, chunk, pr), lambda i, c: (i, c, 0)),
                  pl.BlockSpec((1, 1, pr), lambda i, c: (i, 0, 0)),
                  pl.BlockSpec((1, n_groups, MXU, HEAD_DIM), lambda i, c: (i, 0, 0, 0)),
                  const((1, pr)), const((8, cw)), const(wup.shape), const(aup.shape), const(gup.shape)],
        out_specs=[pl.BlockSpec((1, chunk, cw), lambda i, c: (i, c, 0)),
                   pl.BlockSpec((1, n_groups, MXU, HEAD_DIM), lambda i, c: (i, 0, 0, 0))],
        out_shape=[jax.ShapeDtypeStruct((b, tpad, cw), F32),
                   jax.ShapeDtypeStruct((b, n_groups, MXU, HEAD_DIM), F32)],
        scratch_shapes=[pltpu.VMEM((1, pr), F32), pltpu.VMEM((n_groups, MXU, HEAD_DIM), F32)],
        compiler_params=_cparams(("arbitrary", "arbitrary")),
        name="rwkv_mix",
    )(p_r, shift3, s0, mu, vecs, wup, aup, gup)
    s_fin = sfin.reshape(b, nh, HEAD_DIM, HEAD_DIM).transpose(0, 1, 3, 2)
    return y, s_fin


def _head_ones():
    gi = lax.broadcasted_iota(jnp.int32, (MXU, MXU), 0) // HEAD_DIM
    gj = lax.broadcasted_iota(jnp.int32, (MXU, MXU), 1) // HEAD_DIM
    return (gi == gj).astype(F32)


def _nsa_proj_kernel(p_ref, g_ref, q_ref, rows_ref, win_ref, gate_ref, *, nsa_w, kvw):
    ones_bd = _head_ones()

    def hnorm(x, gvec):
        ms = _dot(x * x, ones_bd, HI) * (1.0 / HEAD_DIM)
        return x * lax.rsqrt(ms + RMS_EPS) * gvec

    for i in range(nsa_w // MXU):
        sl = slice(i * MXU, (i + 1) * MXU)
        q_ref[:, sl] = hnorm(p_ref[:, sl], g_ref[0:1, :])
    o = nsa_w
    rows_ref[:, 0:2 * kvw] = p_ref[:, o:o + 2 * kvw]
    rows_ref[:, 2 * kvw:3 * kvw] = hnorm(p_ref[:, o + 2 * kvw:o + 3 * kvw], g_ref[2:3, :])
    rows_ref[:, 3 * kvw:4 * kvw] = p_ref[:, o + 3 * kvw:o + 4 * kvw]
    win_ref[:, 0:kvw] = hnorm(p_ref[:, o + 4 * kvw:o + 5 * kvw], g_ref[3:4, :])
    win_ref[:, kvw:2 * kvw] = p_ref[:, o + 5 * kvw:o + 6 * kvw]
    gate_ref[...] = jax.nn.sigmoid(p_ref[:, o + 6 * kvw:])


def _nsa_project_call(p_n, qk_norm_g, nsa_w, kvw, tm):
    t, pc = p_n.shape
    assert kvw == MXU and nsa_w % MXU == 0
    gcols = pc - nsa_w - 6 * kvw
    gvec = jnp.tile(qk_norm_g, (1, MXU // HEAD_DIM))
    return pl.pallas_call(
        functools.partial(_nsa_proj_kernel, nsa_w=nsa_w, kvw=kvw),
        grid=(t // tm,),
        in_specs=[pl.BlockSpec((tm, pc), lambda i: (i, 0)),
                  pl.BlockSpec(gvec.shape, lambda i: (0, 0))],
        out_specs=[pl.BlockSpec((tm, nsa_w), lambda i: (i, 0)),
                   pl.BlockSpec((tm, 4 * kvw), lambda i: (i, 0)),
                   pl.BlockSpec((tm, 2 * kvw), lambda i: (i, 0)),
                   pl.BlockSpec((tm, gcols), lambda i: (i, 0))],
        out_shape=[jax.ShapeDtypeStruct((t, nsa_w), F32), jax.ShapeDtypeStruct((t, 4 * kvw), F32),
                   jax.ShapeDtypeStruct((t, 2 * kvw), F32), jax.ShapeDtypeStruct((t, gcols), F32)],
        compiler_params=_cparams(("arbitrary",)),
        name="nsa_project",
    )(p_n, gvec)


def _cmp_part_kernel(*refs, n_in, row_w, kvw, n_prefetch=0):
    refs = refs[n_prefetch:]
    x_refs, w_ref, o_ref = refs[:n_in], refs[n_in], refs[n_in + 1]
    for typ in range(2):
        acc = None
        for s in range(CMP_STRIDE):
            lo = s * row_w + typ * kvw
            xs = jnp.concatenate([x[0, :, lo:lo + kvw] for x in x_refs], axis=0) if n_in > 1 else x_refs[0][0, :, lo:lo + kvw]
            d = _dot(xs.astype(BF16), w_ref[typ, s])
            acc = d if acc is None else acc + d
        o_ref[0, :, typ * 2 * kvw:(typ + 1) * 2 * kvw] = acc


def _cmp_first_weights(cmp_w1, nkv):
    r2 = CMP_LEN // CMP_STRIDE
    e = cmp_w1.shape[-1]
    w1r = cmp_w1.reshape(2, r2, CMP_STRIDE, HEAD_DIM, e)
    eye = jnp.eye(nkv, dtype=F32)
    big = jnp.einsum('yhsde,gk->ysgdkhe', w1r, eye)
    return big.reshape(2, CMP_STRIDE, nkv * HEAD_DIM, nkv * r2 * e).astype(BF16)


def _cmp_parts_prompt(rows2d, w_big, b, t, kvw):
    row_w = rows2d.shape[2]
    nsub = t // CMP_STRIDE
    blk = min(nsub, LANES)
    return pl.pallas_call(
        functools.partial(_cmp_part_kernel, n_in=1, row_w=row_w, kvw=kvw),
        grid=(b, nsub // blk),
        in_specs=[pl.BlockSpec((1, blk, CMP_STRIDE * row_w), lambda i, j: (i, j, 0)),
                  pl.BlockSpec(w_big.shape, lambda i, j: (0, 0, 0, 0))],
        out_specs=pl.BlockSpec((1, blk, 4 * kvw), lambda i, j: (i, j, 0)),
        out_shape=jax.ShapeDtypeStruct((b, nsub, 4 * kvw), F32),
        compiler_params=_cparams(("arbitrary", "arbitrary")),
        name="cmp_parts",
    )(rows2d.reshape(b, nsub, CMP_STRIDE * row_w), w_big)


def _cmp_finish_kernel(*refs, n_parts, nc, nkv):
    p_refs = refs[:n_parts]
    b1_ref, w2_ref, b2_ref, g_ref, o_ref = refs[n_parts:]
    ns = o_ref.shape[2]
    e = b1_ref.shape[2]
    have = sum(p.shape[1] for p in p_refs)
    row = lax.broadcasted_iota(jnp.int32, (ns, 1), 0)
    for tg in range(2 * nkv):
        typ = tg // nkv
        cols = slice(tg * 2 * e, (tg + 1) * 2 * e)
        pieces = [p[0, :, cols] for p in p_refs]
        if have < ns:
            pieces.append(jnp.zeros((ns - have, 2 * e), F32))
        part = jnp.concatenate(pieces, axis=0) if len(pieces) > 1 else pieces[0]
        nxt = pltpu.roll(part[:, e:], ns - 1, axis=0)
        hid = part[:, :e] + nxt + b1_ref[typ]
        out = _dot(jax.nn.gelu(hid).astype(BF16), w2_ref[typ].astype(BF16)) + b2_ref[typ]
        if typ == 0:
            out = out * lax.rsqrt(jnp.mean(out * out, axis=-1, keepdims=True) + RMS_EPS) * g_ref[...]
        o_ref[0, tg] = jnp.where(row < nc, out, 0.0)


def _cmp_finish(parts_list, ns, cmp_b1, cmp_w2, cmp_b2, g1, nc, nkv):
    b = parts_list[0].shape[0]
    e = cmp_b1.shape[1]
    return pl.pallas_call(
        functools.partial(_cmp_finish_kernel, n_parts=len(parts_list), nc=nc, nkv=nkv),
        grid=(b,),
        in_specs=[pl.BlockSpec((1,) + p.shape[1:], lambda i: (i, 0, 0)) for p in parts_list] + [
                  pl.BlockSpec((2, 1, e), lambda i: (0, 0, 0)),
                  pl.BlockSpec((2, e, HEAD_DIM), lambda i: (0, 0, 0)),
                  pl.BlockSpec((2, 1, HEAD_DIM), lambda i: (0, 0, 0)),
                  pl.BlockSpec((1, HEAD_DIM), lambda i: (0, 0))],
        out_specs=pl.BlockSpec((1, 2 * nkv, ns, HEAD_DIM), lambda i: (i, 0, 0, 0)),
        out_shape=jax.ShapeDtypeStruct((b, 2 * nkv, ns, HEAD_DIM), F32),
        compiler_params=_cparams(("arbitrary",)),
        name="cmp_finish",
    )(*parts_list, cmp_b1.reshape(2, 1, e), cmp_w2, cmp_b2.reshape(2, 1, HEAD_DIM), g1.reshape(1, HEAD_DIM))


QT = 128
SEL_CHAINS = 2
QK_SCALE = HEAD_DIM ** -0.5
assert math.log2(HEAD_DIM) % 2 == 0, "QK_SCALE must be a power of two to be folded into q exactly"


def _rel_table_np_dist(dist, table):
    onehot = (_rel_bucket(dist)[..., None] == jnp.arange(REL_BUCKETS)).astype(F32)
    return jnp.einsum('...b,bh->...h', onehot, table, precision=HI)


def _rel_table_per_lane(dist, table_l):
    bucket = _rel_bucket(dist)
    out = jnp.zeros(dist.shape, F32)
    for b in range(REL_BUCKETS):
        out = out + jnp.where(bucket == b, table_l[b][None, :], 0.0)
    return out


def _softmax_update(s, mask, m, l):
    m_new = jnp.maximum(m, jnp.max(jnp.where(mask, s, NEG), axis=0, keepdims=True))
    alpha = jnp.exp(m - m_new)
    p = jnp.where(mask, jnp.exp(s - m_new), 0.0)
    return p, m_new, alpha, alpha * l + jnp.sum(p, axis=0, keepdims=True)


def _rank_select(score, score_ref, cur, n_sel):
    nb = score.shape[0]
    score_ref[0:nb, :] = score
    jrow = lax.broadcasted_iota(jnp.int32, score.shape, 0)

    def body(j, rank):
        other = score_ref[pl.ds(j, 1), :]
        beats = (other > score) | ((other == score) & (jrow > j))
        return rank + jnp.where(beats, 1.0, 0.0)

    rank = lax.fori_loop(0, nb, body, jnp.zeros(score.shape, F32), unroll=8)
    return jnp.where((rank < n_sel) & (jrow <= cur), 1.0, 0.0)


def _extract_select(score, cur, n_sel):
    nb = score.shape[0]
    jrow = lax.broadcasted_iota(jnp.int32, score.shape, 0)
    sel = jnp.zeros(score.shape, F32)
    x = score
    for _ in range(n_sel):
        m = jnp.max(x, axis=0, keepdims=True)
        first = jnp.min(jnp.where(x == m, jrow, nb), axis=0, keepdims=True)
        hit = jrow == first
        sel = jnp.where(hit, 1.0, sel)
        x = jnp.where(hit, -jnp.inf, x)
    return jnp.where(jrow <= cur, sel, 0.0)


def _nsa_prompt_kernel(qT_ref, gT_ref, kc_ref, vcT_ref, ks_ref, vsT_ref, kw_ref, vwT_ref, bc_ref, toep_ref,
                       o_ref, pg_ref, score_ref, sel_ref, *, nc, nb, n_sel, ngrp):
    qt = pl.program_id(2)
    lanes = ngrp * QT
    q = (qT_ref[0, 0, 0] * QK_SCALE).astype(BF16)
    iq = lax.broadcasted_iota(jnp.int32, (1, QT), 1)
    q_pos = qt * QT + iq
    tile4 = lambda x: jnp.concatenate([x] * ngrp, axis=1)

    ncp = kc_ref.shape[2]
    s = _dot(kc_ref[0, 0].astype(BF16), q)
    s = s + jnp.concatenate([bc_ref[0, r] for r in range(ngrp)], axis=1)
    crow = lax.broadcasted_iota(jnp.int32, (ncp, QT), 0)
    ok_c = tile4((crow * CMP_STRIDE + (CMP_LEN - 1) <= q_pos) & (crow < nc))
    p, _, _, l = _softmax_update(s, ok_c, jnp.full((1, lanes), NEG, F32), jnp.zeros((1, lanes), F32))
    p = p * jnp.where(l > 0.0, 1.0 / jnp.where(l > 0.0, l, 1.0), 0.0)
    o_c = _dot(vcT_ref[0, 0].astype(BF16), p.astype(BF16))

    p_grp = p[:, 0:QT]
    for r in range(1, ngrp):
        p_grp = p_grp + p[:, r * QT:(r + 1) * QT]
    pad = 8
    pg_ref[...] = jnp.zeros(pg_ref.shape, F32)
    pg_ref[pad:pad + ncp, :] = p_grp
    r1 = SEL_BLOCK // CMP_STRIDE
    offs, wts = _slc_offsets()
    p_slc = None
    for o, wt in zip(offs, wts):
        term = float(wt) * pg_ref[pl.ds(pad + int(o), nb, stride=r1), :]
        p_slc = term if p_slc is None else p_slc + term
    jrow = lax.broadcasted_iota(jnp.int32, (nb, QT), 0)
    cur = q_pos // SEL_BLOCK
    forced = (jrow == 0) | (jrow == cur) | (jrow == cur - 1)
    score = jnp.where(jrow > cur, -1.0, jnp.where(forced, 1e6, p_slc))
    sel_ref[0:nb, :] = _rank_select(score, score_ref, cur, n_sel)

    ik = lax.broadcasted_iota(jnp.int32, (QT, QT), 0)
    iqq = lax.broadcasted_iota(jnp.int32, (QT, QT), 1)
    blocks_per_tile = QT // SEL_BLOCK

    def attend(kp, carry, k_ref, vT_ref, mask_fn):
        m, l, acc = carry
        s_all = _dot(k_ref[0, 0, kp], q)
        deltas = [qt - (2 * kp + i) for i in range(2)]
        mask = jnp.concatenate([mask_fn(2 * kp + i, deltas[i]) for i in range(2)], axis=0)
        m_out, l_out, alphas, ps = [], [], [], []
        for r in range(ngrp):
            ls = slice(r * QT, (r + 1) * QT)
            bias = jnp.concatenate([toep_ref[0, r, jnp.maximum(d, 0)] for d in deltas], axis=0)
            s = jnp.where(mask, s_all[:, ls] + bias, NEG)
            m_new = jnp.maximum(m[:, ls], jnp.max(s, axis=0, keepdims=True))
            alpha = jnp.exp(m[:, ls] - m_new)
            p = jnp.exp(s - m_new)
            m_out.append(m_new)
            l_out.append(alpha * l[:, ls] + jnp.sum(p, axis=0, keepdims=True))
            alphas.append(alpha)
            ps.append(p.astype(BF16))
        cat = lambda xs: jnp.concatenate(xs, axis=1)
        acc = cat(alphas) * acc + _dot(vT_ref[0, 0, kp], cat(ps))
        return cat(m_out), cat(l_out), acc

    def sel_mask(kt, delta):
        rows = [jnp.broadcast_to(sel_ref[pl.ds(kt * blocks_per_tile + i, 1), :], (SEL_BLOCK, QT))
                for i in range(blocks_per_tile)]
        chosen = jnp.concatenate(rows, axis=0) > 0.5
        return chosen & (ik - iqq <= delta * QT)

    def win_mask(kt, delta):
        dist = delta * QT + iqq - ik
        return (dist >= 0) & (dist < WINDOW)

    init = (jnp.full((1, lanes), NEG, F32), jnp.zeros((1, lanes), F32), jnp.zeros((HEAD_DIM, lanes), F32))

    def merge(states):
        m = states[0][0]
        for st in states[1:]:
            m = jnp.maximum(m, st[0])
        l, acc = None, None
        for m_i, l_i, acc_i in states:
            w = jnp.exp(m_i - m)
            l = w * l_i if l is None else l + w * l_i
            acc = w * acc_i if acc is None else acc + w * acc_i
        return acc * jnp.where(l > 0.0, 1.0 / jnp.where(l > 0.0, l, 1.0), 0.0)

    diag = qt // 2
    def sel_body(i, carry):
        return tuple(attend(SEL_CHAINS * i + c, carry[c], ks_ref, vsT_ref, sel_mask) for c in range(SEL_CHAINS))

    o_s = merge(lax.fori_loop(0, (diag + SEL_CHAINS) // SEL_CHAINS, sel_body, (init,) * SEL_CHAINS))
    first = jnp.maximum(qt - WINDOW // QT, 0) // 2
    states = []
    for i in range((WINDOW // QT) // 2 + 1):
        kp = diag - i
        live = kp >= first
        states.append(attend(jnp.maximum(kp, 0), init, kw_ref, vwT_ref,
                             lambda kt, delta, live=live: win_mask(kt, delta) & live))
    o_w = merge(states)
    g = gT_ref[0, 0, 0]
    o_ref[0, 0, 0] = g[0:1, :] * o_c + g[1:2, :] * o_s + g[2:3, :] * o_w


def _nsa_prompt(p_n, W, b, t, nkv, ngrp, tm):
    nsa_w = nkv * ngrp * HEAD_DIM
    kvw = nkv * HEAD_DIM
    qn, rows2d, win2d, gates = _nsa_project_call(p_n, W['qk_norm_g'], nsa_w, kvw, tm)
    rows = rows2d.reshape(b, t, 4, nkv, HEAD_DIM)
    win = win2d.reshape(b, t, 2, nkv, HEAD_DIM)
    ns = t // CMP_STRIDE
    nc = ns - CMP_LEN // CMP_STRIDE + 1
    nb = t // SEL_BLOCK
    n_sel = min(SEL_TOPK, nb)
    nqt = t // QT
    parts = _cmp_parts_prompt(rows2d.reshape(b, t, 4 * kvw), _cmp_first_weights(W['cmp_w1'], nkv), b, t, kvw)
    kvc = _cmp_finish([parts], ns, W['cmp_b1'], W['cmp_w2'], W['cmp_b2'], W['qk_norm_g'][1], nc, nkv)
    kc = kvc[:, :nkv]
    vcT = kvc[:, nkv:].transpose(0, 1, 3, 2)
    qT = qn.reshape(b, nqt, QT, nkv, ngrp, HEAD_DIM).transpose(0, 3, 1, 5, 4, 2).reshape(b, nkv, nqt, HEAD_DIM, ngrp * QT)
    ng = 3 * nkv * ngrp
    gT = gates[:, :ng].reshape(b, nqt, QT, nkv, ngrp, 3).transpose(0, 3, 1, 5, 4, 2).reshape(b, nkv, nqt, 3, ngrp * QT)
    gT = jnp.pad(gT, ((0, 0), (0, 0), (0, 0), (0, 5), (0, 0)))
    assert nqt % (2 * SEL_CHAINS) == 0 and (WINDOW // QT) % 2 == 0
    npair = nqt // 2
    k_tiles = lambda x: x.transpose(0, 2, 1, 3).reshape(b, nkv, npair, 2 * QT, HEAD_DIM).astype(BF16)
    vT_tiles = lambda x: x.reshape(b, npair, 2 * QT, nkv, HEAD_DIM).transpose(0, 3, 1, 4, 2).astype(BF16)
    ks, vsT = k_tiles(rows[:, :, 2]), vT_tiles(rows[:, :, 3])
    kw, vwT = k_tiles(win[:, :, 0]), vT_tiles(win[:, :, 1])
    table = W['rel_bias'].astype(F32)
    c_end = jnp.arange(ns) * CMP_STRIDE + CMP_LEN - 1
    bias_c = _rel_table_np_dist(jnp.arange(t)[None, :] - c_end[:, None], table)
    bias_c = bias_c.transpose(2, 0, 1).reshape(nkv, ngrp, ns, t)
    dd = (jnp.arange(nqt)[:, None, None] * QT + jnp.arange(QT)[None, None, :] - jnp.arange(QT)[None, :, None])
    toep = _rel_table_np_dist(dd, table).transpose(3, 0, 1, 2).reshape(nkv, ngrp, nqt, QT, QT)
    lanes = ngrp * QT
    kv_spec = lambda shape: pl.BlockSpec((1, 1) + shape, lambda i, g, j: (i, g) + (0,) * len(shape))
    yT = pl.pallas_call(
        functools.partial(_nsa_prompt_kernel, nc=nc, nb=nb, n_sel=n_sel, ngrp=ngrp),
        grid=(b, nkv, nqt),
        in_specs=[pl.BlockSpec((1, 1, 1, HEAD_DIM, lanes), lambda i, g, j: (i, g, j, 0, 0)),
                  pl.BlockSpec((1, 1, 1, 8, lanes), lambda i, g, j: (i, g, j, 0, 0)),
                  kv_spec((ns, HEAD_DIM)), kv_spec((HEAD_DIM, ns)),
                  kv_spec((npair, 2 * QT, HEAD_DIM)), kv_spec((npair, HEAD_DIM, 2 * QT)),
                  kv_spec((npair, 2 * QT, HEAD_DIM)), kv_spec((npair, HEAD_DIM, 2 * QT)),
                  pl.BlockSpec((1, ngrp, ns, QT), lambda i, g, j: (g, 0, 0, j)),
                  pl.BlockSpec((1, ngrp, nqt, QT, QT), lambda i, g, j: (g, 0, 0, 0, 0))],
        out_specs=pl.BlockSpec((1, 1, 1, HEAD_DIM, lanes), lambda i, g, j: (i, g, j, 0, 0)),
        out_shape=jax.ShapeDtypeStruct((b, nkv, nqt, HEAD_DIM, lanes), F32),
        scratch_shapes=[pltpu.VMEM((ns + 16, QT), F32), pltpu.VMEM((_rup(nb, 8), QT), F32),
                        pltpu.VMEM((_rup(nb, 8), QT), F32)],
        compiler_params=_cparams(("arbitrary", "arbitrary", "arbitrary")),
        name="nsa_prompt_attn",
    )(qT, gT, kc, vcT, ks, vsT, kw, vwT, bias_c, toep)
    y = yT.reshape(b, nkv, nqt, HEAD_DIM, ngrp, QT).transpose(0, 2, 5, 1, 4, 3).reshape(b * t, nsa_w)
    return y, rows, win[:, t - min(WINDOW, t):]


def _cmp_part_paged_kernel(*refs, npg, kvw):
    x_refs, w_ref, o_ref = refs[1:1 + npg], refs[1 + npg], refs[2 + npg]
    rows = x_refs[0].shape[2]
    sub = rows // CMP_STRIDE
    ri = lax.broadcasted_iota(jnp.int32, (rows, rows), 0)
    ci = lax.broadcasted_iota(jnp.int32, (rows, rows), 1)
    perm = jnp.where(ci == (ri % sub) * CMP_STRIDE + ri // sub, 1.0, 0.0).astype(BF16)
    xp = [_dot_nt(perm, x[0].astype(BF16)) for x in x_refs]
    for typ in range(2):
        acc = None
        for s in range(CMP_STRIDE):
            xs = jnp.concatenate([p[s * sub:(s + 1) * sub, typ * kvw:(typ + 1) * kvw] for p in xp], axis=0)
            d = _dot(xs.astype(BF16), w_ref[typ, s])
            acc = d if acc is None else acc + d
        o_ref[0, :, typ * 2 * kvw:(typ + 1) * 2 * kvw] = acc


def _cmp_parts_sample(cache_pages, page_table, w_big, kvw, npg):
    b, n_pages = page_table.shape
    rows = cache_pages.shape[2]
    sub = rows // CMP_STRIDE
    in_specs = [pl.BlockSpec((1, 2 * kvw, rows), (lambda i, j, pt, k=k: (pt[i, j * npg + k], 0, 0)))
                for k in range(npg)]
    in_specs.append(pl.BlockSpec(w_big.shape, lambda i, j, pt: (0, 0, 0, 0)))
    return pl.pallas_call(
        functools.partial(_cmp_part_paged_kernel, npg=npg, kvw=kvw),
        grid_spec=pltpu.PrefetchScalarGridSpec(
            num_scalar_prefetch=1, grid=(b, n_pages // npg), in_specs=in_specs,
            out_specs=pl.BlockSpec((1, npg * sub, 4 * kvw), lambda i, j, pt: (i, j, 0))),
        out_shape=jax.ShapeDtypeStruct((b, n_pages * sub, 4 * kvw), F32),
        compiler_params=_cparams(("arbitrary", "arbitrary")),
        name="cmp_parts_paged",
    )(page_table, *([cache_pages] * npg), w_big)


def _inv_pos(l):
    return jnp.where(l > 0.0, 1.0 / jnp.where(l > 0.0, l, 1.0), 0.0)


def _nsa_sample_kernel(*refs, npg, n_steps, nc, nb, n_sel, past, ds, keep, ngrp, n_lanes):
    pt_ref = refs[0]
    q_ref, g_ref, kc_ref, vcT_ref, bc_ref, win_ref, wnew_ref, bw_ref, rnew_ref, bs_ref = refs[1:11]
    page_refs = refs[11:11 + npg]
    o_ref = refs[11 + npg]
    (m_ref, l_ref, acc_ref, base_ref, sel_ref, pg_ref, score_ref,
     kw_ref, kn_ref, vn_ref) = refs[12 + npg:]
    del pt_ref
    j = pl.program_id(1)
    scale = HEAD_DIM ** -0.5
    kvw = q_ref.shape[1]
    eye = jnp.where(lax.broadcasted_iota(jnp.int32, (LANES, LANES), 0)
                    == lax.broadcasted_iota(jnp.int32, (LANES, LANES), 1), 1.0, 0.0).astype(BF16)
    qbd = q_ref[0]
    lane = lax.broadcasted_iota(jnp.int32, (1, LANES), 1)
    qi = (lane // ngrp) % ds
    q_pos = past + qi
    n_pages = npg * n_steps

    @pl.when(j == 0)
    def _():
        nsp = kc_ref.shape[1]
        s = _dot(kc_ref[0].astype(BF16), qbd) * scale + bc_ref[...]
        crow = lax.broadcasted_iota(jnp.int32, (nsp, LANES), 0)
        ok = (crow * CMP_STRIDE + (CMP_LEN - 1) <= q_pos) & (crow < nc)
        p, _, _, l = _softmax_update(s, ok, jnp.full((1, LANES), NEG, F32), jnp.zeros((1, LANES), F32))
        p = p * _inv_pos(l)
        o_c = _dot(vcT_ref[0].astype(BF16), p.astype(BF16))
        li = lax.broadcasted_iota(jnp.int32, (LANES, LANES), 0)
        lj = lax.broadcasted_iota(jnp.int32, (LANES, LANES), 1)
        fold = jnp.where((li // ngrp == lj) & (li < n_lanes), 1.0, 0.0)
        p_grp = _dot(p, fold, HI)
        pad = 8
        pg_ref[...] = jnp.zeros(pg_ref.shape, F32)
        pg_ref[pad:pad + nsp, :] = p_grp
        nbp = sel_ref.shape[0]
        r1 = SEL_BLOCK // CMP_STRIDE
        offs, wts = _slc_offsets()
        p_slc = None
        for o, wt in zip(offs, wts):
            term = float(wt) * pg_ref[pl.ds(pad + int(o), nbp, stride=r1), :]
            p_slc = term if p_slc is None else p_slc + term
        cur = (past + lane % ds) // SEL_BLOCK
        jrow = lax.broadcasted_iota(jnp.int32, (nbp, LANES), 0)
        forced = (jrow == 0) | (jrow == cur) | (jrow == cur - 1)
        score = jnp.where((jrow > cur) | (jrow >= nb), -1.0, jnp.where(forced, 1e6, p_slc))
        selg = _extract_select(score, cur, n_sel)
        unfold = jnp.where((li == lj // ngrp) & (lj < n_lanes), 1.0, 0.0)
        sel_ref[...] = _dot(selg, unfold)
        wk = kw_ref.shape[0]
        nn = wnew_ref.shape[1]
        kw_ref[...] = jnp.zeros(kw_ref.shape, F32)
        for c0 in range(0, keep, LANES):
            kw_ref[c0:c0 + LANES, :] = _dot_nt(eye, win_ref[0, 0:kvw, c0:c0 + LANES].astype(BF16))
        kw_ref[keep:keep + nn, :] = wnew_ref[0, :, 0:kvw]
        vn_ref[...] = jnp.zeros(vn_ref.shape, F32)
        vn_ref[0:nn, :] = wnew_ref[0, :, kvw:2 * kvw]
        s = _dot(kw_ref[...].astype(BF16), qbd) * scale + bw_ref[...]
        irow = lax.broadcasted_iota(jnp.int32, (wk, LANES), 0)
        dist = jnp.where(irow < keep, keep + qi - irow, qi - (irow - keep))
        ok = (dist >= 0) & (dist < WINDOW) & (irow < keep + ds)
        p, _, _, l = _softmax_update(s, ok, jnp.full((1, LANES), NEG, F32), jnp.zeros((1, LANES), F32))
        p = (p * _inv_pos(l)).astype(BF16)
        o_w = (_dot(win_ref[0, kvw:2 * kvw, :].astype(BF16), p[0:keep, :])
               + _dot(vn_ref[...].T.astype(BF16), p[keep:keep + PAGE_SIZE, :]))
        g = g_ref[0]
        base_ref[...] = g[0:1, :] * o_c + g[2:3, :] * o_w
        m_ref[...] = jnp.full(m_ref.shape, NEG, F32)
        l_ref[...] = jnp.zeros(l_ref.shape, F32)
        acc_ref[...] = jnp.zeros(acc_ref.shape, F32)
        kn_ref[...] = jnp.zeros(kn_ref.shape, F32)
        vn_ref[...] = jnp.zeros(vn_ref.shape, F32)
        kn_ref[0:nn, :] = rnew_ref[0, :, 2 * kvw:3 * kvw]
        vn_ref[0:nn, :] = rnew_ref[0, :, 3 * kvw:4 * kvw]

    ik = lax.broadcasted_iota(jnp.int32, (PAGE_SIZE, LANES), 0)
    blocks_per_page = PAGE_SIZE // SEL_BLOCK

    def pages_update(k, v_t, first_page, count):
        bias = jnp.concatenate([bs_ref[first_page + i] for i in range(count)], axis=0)
        s = _dot(k, qbd) * scale + bias
        rows = [jnp.broadcast_to(sel_ref[pl.ds(first_page * blocks_per_page + i, 1), :], (SEL_BLOCK, LANES))
                for i in range(count * blocks_per_page)]
        key_pos = first_page * PAGE_SIZE + lax.broadcasted_iota(jnp.int32, (count * PAGE_SIZE, LANES), 0)
        mask = (jnp.concatenate(rows, axis=0) > 0.5) & (key_pos <= q_pos)
        p, m_new, alpha, l_new = _softmax_update(s, mask, m_ref[...], l_ref[...])
        m_ref[...] = m_new
        l_ref[...] = l_new
        acc_ref[...] = alpha * acc_ref[...] + _dot(v_t, p.astype(BF16))

    k_rows = jnp.concatenate([_dot_nt(eye, blk[0, 0:kvw, :].astype(BF16)) for blk in page_refs], axis=0)
    v_cols = jnp.concatenate([blk[0, kvw:2 * kvw, :].astype(BF16) for blk in page_refs], axis=1)
    pages_update(k_rows.astype(BF16), v_cols, j * npg, npg)

    @pl.when(j == n_steps - 1)
    def _():
        pages_update(kn_ref[...].astype(BF16), vn_ref[...].T.astype(BF16), n_pages, 1)
        o_ref[0] = base_ref[...] + g_ref[0][1:2, :] * (acc_ref[...] * _inv_pos(l_ref[...]))


def _nsa_sample(p_n, cache_kv, cache_win, page_table, W, db, ds, nkv, ngrp):
    nsa_w = nkv * ngrp * HEAD_DIM
    kvw = nkv * HEAD_DIM
    row_w = 4 * kvw
    qn, rows2d, win2d, gates = _nsa_project_call(p_n, W['qk_norm_g'], nsa_w, kvw, db * ds)
    rows_new = rows2d.reshape(db, ds, 4, nkv, HEAD_DIM)
    win_new = win2d.reshape(db, ds, 2, nkv, HEAD_DIM)
    n_pool = cache_kv.shape[0]
    n_pages = page_table.shape[1]
    past = n_pages * PAGE_SIZE
    keep = cache_win.shape[1]
    tot = past + _rup(ds, SEL_BLOCK)
    ns = tot // CMP_STRIDE
    nc = ns - CMP_LEN // CMP_STRIDE + 1
    nb = tot // SEL_BLOCK
    n_sel = min(SEL_TOPK, nb)
    nsp = _rup(ns, LANES)
    nbp = _rup(nb, 8)
    n_lanes = nkv * ds * ngrp
    assert n_lanes <= LANES and ds <= 8
    w_big = _cmp_first_weights(W['cmp_w1'], nkv)
    cache_pages = cache_kv.transpose(0, 2, 3, 4, 1).reshape(n_pool, row_w, PAGE_SIZE)
    parts_past = _cmp_parts_sample(cache_pages, page_table, w_big, kvw, min(MXU // (PAGE_SIZE // CMP_STRIDE), n_pages))
    npg = min(16, n_pages)
    rows_pad = jnp.pad(rows2d.reshape(db, ds, row_w), ((0, 0), (0, PAGE_SIZE - ds), (0, 0)))
    parts_new = _cmp_parts_prompt(rows_pad, w_big, db, PAGE_SIZE, kvw)
    kvc = _cmp_finish([parts_past, parts_new], nsp, W['cmp_b1'], W['cmp_w2'], W['cmp_b2'], W['qk_norm_g'][1], nc, nkv)
    kc_cat = kvc[:, :nkv].transpose(0, 2, 1, 3).reshape(db, nsp, kvw)
    vcT_cat = kvc[:, nkv:].transpose(0, 1, 3, 2).reshape(db, kvw, nsp)
    lane_pad = LANES - n_lanes
    q5 = qn.reshape(db, ds, nkv, ngrp, HEAD_DIM)
    qbd = jnp.einsum('bqgrd,gk->bgdkqr', q5, jnp.eye(nkv, dtype=F32)).reshape(db, kvw, n_lanes)
    qbd = jnp.pad(qbd, ((0, 0), (0, 0), (0, lane_pad))).astype(BF16)
    ng = 3 * nkv * ngrp
    gT = gates[:, :ng].reshape(db, ds, nkv, ngrp, 3).transpose(0, 4, 2, 1, 3).reshape(db, 3, n_lanes)
    gT = jnp.pad(gT, ((0, 0), (0, 5), (0, lane_pad)))
    lane = np.arange(LANES)
    live = lane < n_lanes
    head_of_lane = np.where(live, (lane // (ds * ngrp)) * ngrp + lane % ngrp, 0)
    qi = np.where(live, (lane // ngrp) % ds, 0)
    table_l = W['rel_bias'].astype(F32)[:, head_of_lane]
    bias_of = lambda dist: _rel_table_per_lane(dist, table_l)
    c_end = np.arange(nsp) * CMP_STRIDE + CMP_LEN - 1
    bias_c = bias_of(jnp.asarray(past + qi[None, :] - c_end[:, None], jnp.int32))
    pos = np.arange((n_pages + 1) * PAGE_SIZE)
    bias_s = bias_of(jnp.asarray(past + qi[None, :] - pos[:, None], jnp.int32)).reshape(n_pages + 1, PAGE_SIZE, LANES)
    wk = _rup(keep + 8, LANES)
    irow = np.arange(wk)[:, None]
    dist_w = np.where(irow < keep, keep + qi[None, :] - irow, qi[None, :] - (irow - keep))
    bias_w = bias_of(jnp.asarray(dist_w, jnp.int32))
    win_c = cache_win.transpose(0, 2, 3, 4, 1).reshape(db, 2 * kvw, keep)
    wnew8 = jnp.pad(win2d.reshape(db, ds, 2 * kvw), ((0, 0), (0, 8 - ds), (0, 0)))
    rnew8 = jnp.pad(rows2d.reshape(db, ds, row_w), ((0, 0), (0, 8 - ds), (0, 0)))
    n_steps = n_pages // npg
    per_b = lambda shape: pl.BlockSpec((1,) + shape, lambda i, j, pt: (i,) + (0,) * len(shape))
    const = lambda shape: pl.BlockSpec(shape, lambda i, j, pt: (0,) * len(shape))
    in_specs = [per_b((kvw, LANES)), per_b((8, LANES)), per_b((nsp, kvw)), per_b((kvw, nsp)), const((nsp, LANES)),
                per_b((2 * kvw, keep)), per_b((8, 2 * kvw)), const((wk, LANES)), per_b((8, row_w)),
                const((n_pages + 1, PAGE_SIZE, LANES))]
    in_specs += [pl.BlockSpec((1, 2 * kvw, PAGE_SIZE), (lambda i, j, pt, k=k: (pt[i, j * npg + k], 1, 0)))
                 for k in range(npg)]
    yT = pl.pallas_call(
        functools.partial(_nsa_sample_kernel, npg=npg, n_steps=n_steps, nc=nc, nb=nb, n_sel=n_sel, past=past,
                          ds=ds, keep=keep, ngrp=ngrp, n_lanes=n_lanes),
        grid_spec=pltpu.PrefetchScalarGridSpec(
            num_scalar_prefetch=1, grid=(db, n_steps), in_specs=in_specs,
            out_specs=pl.BlockSpec((1, kvw, LANES), lambda i, j, pt: (i, 0, 0)),
            scratch_shapes=[pltpu.VMEM((1, LANES), F32), pltpu.VMEM((1, LANES), F32), pltpu.VMEM((kvw, LANES), F32),
                            pltpu.VMEM((kvw, LANES), F32), pltpu.VMEM((nbp, LANES), F32),
                            pltpu.VMEM((nsp + 16, LANES), F32), pltpu.VMEM((nbp, LANES), F32),
                            pltpu.VMEM((wk, kvw), F32),
                            pltpu.VMEM((PAGE_SIZE, kvw), F32), pltpu.VMEM((PAGE_SIZE, kvw), F32)]),
        out_shape=jax.ShapeDtypeStruct((db, kvw, LANES), F32),
        compiler_params=_cparams(("arbitrary", "arbitrary")),
        name="nsa_sample_attn",
    )(page_table, qbd, gT, kc_cat, vcT_cat, bias_c, win_c, wnew8, bias_w, rnew8, bias_s, *([cache_pages] * npg))
    y6 = yT[:, :, :n_lanes].reshape(db, nkv, HEAD_DIM, nkv, ds, ngrp)
    y = jnp.einsum('bgdgqr->bqgrd', y6).reshape(db * ds, nsa_w)
    win_all = jnp.concatenate([cache_win, win_new.astype(cache_win.dtype)], axis=1)
    n_keep = min(WINDOW, past + ds)
    return y, rows_new, win_all[:, win_all.shape[1] - n_keep:]


def _rel_bucket(dist):
    d = jnp.maximum(dist, 0)
    exact = REL_BUCKETS // 2
    ratio = jnp.maximum(d, exact).astype(F32) / exact
    large = exact + (jnp.log(ratio) / math.log(REL_MAX_DIST / exact) * (REL_BUCKETS - exact)).astype(jnp.int32)
    return jnp.where(d < exact, d, jnp.minimum(large, REL_BUCKETS - 1))


def _slc_offsets():
    r1 = SEL_BLOCK // CMP_STRIDE
    r2 = CMP_LEN // CMP_STRIDE
    offs = np.arange(-(r2 - 1), r1)
    wts = np.array([sum(1 for m in range(r1) for n in range(r2) if m - n == o) for o in offs], np.float32)
    return offs, wts


def _top_values(x, k):
    n = x.shape[0]
    row = lax.broadcasted_iota(jnp.int32, x.shape, 0)
    vals = []
    for _ in range(k):
        m = jnp.max(x, axis=0, keepdims=True)
        vals.append(m)
        first = jnp.min(jnp.where(x == m, row, n), axis=0, keepdims=True)
        x = jnp.where(row == first, -jnp.inf, x)
    return vals


def _peer_route_kernel(hT_ref, wq_ref, sk_ref, th_ref, c1_ref, s2_ref, e2_ref, *, nheads, topk):
    nk, half = sk_ref.shape[1], sk_ref.shape[2]
    cpt = s2_ref.shape[1]
    qT = _dot(wq_ref[...], hT_ref[...])
    pairs = [(a, b) for a in range(topk) for b in range(topk) if (a + 1) * (b + 1) <= topk]
    for h in range(nheads):
        base = h * 2 * half
        s1 = _dot(sk_ref[0].astype(BF16), qT[base:base + half].astype(BF16))
        s2 = _dot(sk_ref[1].astype(BF16), qT[base + half:base + 2 * half].astype(BF16))
        v1 = _top_values(s1, topk)
        v2 = _top_values(s2, topk)
        sums = [v1[a] + v2[b] for a, b in pairs]
        cand = jnp.concatenate(sums, axis=0)
        tau = _top_values(cand, topk)[-1]
        z = jnp.sum(jnp.where(cand >= tau, jnp.exp(cand - (v1[0] + v2[0])), 0.0), axis=0, keepdims=True)
        theta = jnp.full(s1.shape, jnp.inf, F32)
        for a in range(topk):
            th_a = jnp.full(tau.shape, jnp.inf, F32)
            for (pa, pb), sm in zip(pairs, sums):
                if pa == a:
                    th_a = jnp.where(sm >= tau, v2[pb], th_a)
            theta = jnp.where(s1 == v1[a], th_a, theta)
        c1 = jnp.exp(s1 - v1[0]) / z
        e2 = jnp.exp(s2 - v2[0])
        for c in range(cpt):
            cs = slice(c * LANES, (c + 1) * LANES)
            th_ref[h, c] = theta[:, cs]
            c1_ref[h, c] = c1[:, cs]
            s2_ref[h, c] = s2[:, cs]
            e2_ref[h, c] = e2[:, cs]


def _peer_expert_kernel(hT_ref, x1_ref, ga_ref, th_ref, c1_ref, s2_ref, e2_ref, u_ref, v_ref,
                        o_ref, acc_ref, gate_ref, *, nheads, n_eblocks):
    eb = pl.program_id(1)
    n_chunks, nk = s2_ref.shape[1], s2_ref.shape[2]
    rows_per_block = u_ref.shape[0] // nk

    @pl.when(eb == 0)
    def _():
        acc_ref[...] = jnp.zeros(acc_ref.shape, F32)

    def gate_tile(i, c):
        i1 = eb * rows_per_block + i
        wd = None
        for h in range(nheads):
            chosen = s2_ref[h, c] >= th_ref[h, c, pl.ds(i1, 1), :]
            term = jnp.where(chosen, e2_ref[h, c], 0.0) * c1_ref[h, c, pl.ds(i1, 1), :]
            wd = term if wd is None else wd + term
        gate_ref[i, c] = wd

    def gate_step(it, carry):
        gate_tile(it // n_chunks, it % n_chunks)
        return carry

    lax.fori_loop(0, rows_per_block * n_chunks, gate_step, 0)
    act = jax.nn.gelu(_dot(u_ref[...], hT_ref[...]))
    gates = jnp.concatenate([jnp.concatenate([gate_ref[i, c] for c in range(n_chunks)], axis=1)
                             for i in range(rows_per_block)], axis=0)
    acc_ref[...] += _dot((gates * act).T.astype(BF16), v_ref[...])

    @pl.when(eb == n_eblocks - 1)
    def _():
        o_ref[...] = x1_ref[...] + ga_ref[0] * acc_ref[...]


def _peer(h2, x1, ga, W, tm):
    t, d = h2.shape
    sub_keys = W['peer_sub_keys']
    nk, half = sub_keys.shape[1], sub_keys.shape[2]
    qd = W['peer_w_query'].shape[1]
    nheads = qd // (2 * half)
    hT = h2.T
    wqT = W['peer_w_query'].T.astype(BF16)
    cpt = tm // LANES
    route_shape = jax.ShapeDtypeStruct((nheads, t // LANES, nk, LANES), F32)
    rspec = pl.BlockSpec((nheads, cpt, nk, LANES), lambda i: (0, i, 0, 0))
    th, c1, s2, e2 = pl.pallas_call(
        functools.partial(_peer_route_kernel, nheads=nheads, topk=PEER_TOPK),
        grid=(t // tm,),
        in_specs=[pl.BlockSpec((d, tm), lambda i: (0, i)),
                  pl.BlockSpec((qd, d), lambda i: (0, 0)),
                  pl.BlockSpec(sub_keys.shape, lambda i: (0, 0, 0))],
        out_specs=[rspec] * 4,
        out_shape=[route_shape] * 4,
        compiler_params=_cparams(("arbitrary",)),
        name="peer_route",
    )(hT, wqT, sub_keys)
    eblk = 2 * MXU
    n_eblocks = W['peer_u'].shape[0] // eblk
    nmod, rows, _ = ga.shape
    tiles_per_mod = (t // tm) // nmod
    rspec2 = pl.BlockSpec((nheads, cpt, nk, LANES), lambda i, e: (0, i, 0, 0))
    return pl.pallas_call(
        functools.partial(_peer_expert_kernel, nheads=nheads, n_eblocks=n_eblocks),
        grid=(t // tm, n_eblocks),
        in_specs=[pl.BlockSpec((d, tm), lambda i, e: (0, i)),
                  pl.BlockSpec((tm, d), lambda i, e: (i, 0)),
                  pl.BlockSpec((1, rows, d), lambda i, e: (i // tiles_per_mod, 0, 0)),
                  rspec2, rspec2, rspec2, rspec2,
                  pl.BlockSpec((eblk, d), lambda i, e: (e, 0)),
                  pl.BlockSpec((eblk, d), lambda i, e: (e, 0))],
        out_specs=pl.BlockSpec((tm, d), lambda i, e: (i, 0)),
        out_shape=jax.ShapeDtypeStruct((t, d), F32),
        scratch_shapes=[pltpu.VMEM((tm, d), F32), pltpu.VMEM((eblk // nk, cpt, nk, LANES), F32)],
        compiler_params=_cparams(("arbitrary", "arbitrary")),
        name="peer_experts",
    )(hT, x1, ga, th, c1, s2, e2, W['peer_u_bf16'], W['peer_v_bf16'])


def _layer(x, mods, nsa_fn, shift_prev, wkv0, W, tm, rwkv_chunk):
    b, t, d = x.shape
    cw = W['rwkv_w0'].shape[0]
    nw, na, ng = W['rwkv_w_up'].shape[0], W['rwkv_a_up'].shape[0], W['rwkv_g_up'].shape[0]
    rwkv_proj = 3 * cw + nw + na + ng
    sh1, sc1, ga1, sh2, sc2, ga2 = mods
    xf = x.reshape(b * t, d)
    if (b * t) % tm == 0 and t % tm == 0:
        as_mod = lambda m: m.reshape(b, 1, d)
        tm_wide = tm // 2
    else:
        tm = tm_wide = b * t
        as_mod = lambda m: jnp.repeat(m, t, axis=0).reshape(1, b * t, d)
    w_r = _rwkv_pad_cols(W['w_in'][:, :rwkv_proj], cw, nw, na, ng).astype(BF16)
    nsa_cols = W['w_in'].shape[1] - rwkv_proj
    w_n = jnp.pad(W['w_in'][:, rwkv_proj:], ((0, 0), (0, _rup(nsa_cols, LANES) - nsa_cols))).astype(BF16)
    p_r, _ = _norm_mod_matmul(xf, W['norm1_g'], as_mod(sc1), as_mod(sh1), w_r, tm_wide, w_r.shape[1])
    p_n, _ = _norm_mod_matmul(xf, W['norm1_g'], as_mod(sc1), as_mod(sh1), w_n, tm, w_n.shape[1])
    pr = p_r.shape[1]
    p_r = p_r.reshape(b, t, pr)
    shift_new = _rwkv_unpad_cols(p_r[:, -1], cw, nw, na, ng)
    tpad = _rup(t, rwkv_chunk)
    p_r_pad = jnp.pad(p_r, ((0, 0), (0, tpad - t), (0, 0)))
    y_r, wkv_new = _rwkv_mix(p_r_pad, shift_prev, wkv0, W, rwkv_chunk, t)
    y_r = y_r[:, :t].reshape(b * t, cw)
    y_n, rows, win = nsa_fn(p_n)
    w_out = W['w_out'].astype(BF16)
    x1 = _out_proj(xf, y_r, y_n, as_mod(ga1), w_out[:cw], w_out[cw:], tm, d)
    h2 = _norm_mod(x1, W['norm2_g'], as_mod(sc2), as_mod(sh2), tm)
    assert (b * t) % LANES == 0, "PEER kernels keep tokens on lanes"
    out = _peer(h2, x1, as_mod(ga2), W, tm)
    return out.reshape(b, t, d), rows, win, wkv_new, shift_new


def kernel(x_prompt, x_sample, c_prompt, c_sample, cache_kv, cache_win, state_wkv, state_shift, page_table,
           norm1_g, norm2_g, w_ada, b_ada, w_in, w_out,
           rwkv_mu, rwkv_w0, rwkv_w_up, rwkv_a0, rwkv_a_up, rwkv_g_up, rwkv_k_k, rwkv_k_a, rwkv_r_k, lnx_w, lnx_b,
           qk_norm_g, cmp_w1, cmp_b1, cmp_w2, cmp_b2, rel_bias,
           peer_w_query, peer_sub_keys, peer_u, peer_v):
    W = dict(norm1_g=norm1_g, norm2_g=norm2_g, w_ada=w_ada, b_ada=b_ada, w_in=w_in, w_out=w_out,
             rwkv_mu=rwkv_mu, rwkv_w0=rwkv_w0, rwkv_w_up=rwkv_w_up, rwkv_a0=rwkv_a0, rwkv_a_up=rwkv_a_up,
             rwkv_g_up=rwkv_g_up, rwkv_k_k=rwkv_k_k, rwkv_k_a=rwkv_k_a, rwkv_r_k=rwkv_r_k, lnx_w=lnx_w, lnx_b=lnx_b,
             qk_norm_g=qk_norm_g, cmp_w1=cmp_w1, cmp_b1=cmp_b1, cmp_w2=cmp_w2, cmp_b2=cmp_b2, rel_bias=rel_bias,
             peer_w_query=peer_w_query, peer_sub_keys=peer_sub_keys, peer_u=peer_u, peer_v=peer_v)
    W['peer_u_bf16'] = peer_u.astype(BF16)
    W['peer_v_bf16'] = peer_v.astype(BF16)
    bp, seq, d = x_prompt.shape
    db = x_sample.shape[0]
    nkv = cache_kv.shape[3]
    nh_r = rwkv_w0.shape[0] // HEAD_DIM
    ngrp = (w_out.shape[0] - rwkv_w0.shape[0]) // HEAD_DIM // nkv

    mods = _ada_mods(jnp.concatenate([c_prompt, c_sample], axis=0), w_ada, b_ada)
    mods = mods.reshape(bp + db, N_MODS, d)
    mods_p = [mods[:bp, i] for i in range(N_MODS)]
    mods_s = [mods[bp:, i] for i in range(N_MODS)]

    shift0 = jnp.zeros((bp, state_shift.shape[1]), F32)
    wkv0 = jnp.zeros((bp, nh_r, HEAD_DIM, HEAD_DIM), F32)
    y_p, rows_p, win_p, wkv_p, shift_p = _layer(
        x_prompt, mods_p, lambda pn: _nsa_prompt(pn, W, bp, seq, nkv, ngrp, 512), shift0, wkv0, W, 512, 64)
    y_s, rows_s, win_s, wkv_s, shift_s = _layer(
        x_sample, mods_s,
        lambda pn: _nsa_sample(pn, cache_kv, cache_win, page_table, W, db, x_sample.shape[1], nkv, ngrp),
        state_shift, state_wkv, W, 512, 32)
    return (y_p, y_s, rows_p, win_p, wkv_p.astype(state_wkv.dtype), shift_p,
            rows_s, win_s, wkv_s.astype(state_wkv.dtype), shift_s)
```

```python
import functools
import math

import numpy as np
import jax
import jax.numpy as jnp
from jax import lax
from jax.experimental import pallas as pl
from jax.experimental.pallas import tpu as pltpu

F32 = jnp.float32
BF16 = jnp.bfloat16
HI = lax.Precision.HIGHEST

HEAD_DIM = 64
PAGE_SIZE = 128
CMP_LEN = 32
CMP_STRIDE = 16
SEL_BLOCK = 64
SEL_TOPK = 16
WINDOW = 512
REL_BUCKETS = 32
REL_MAX_DIST = 2048
PEER_TOPK = 16
N_MODS = 6
RMS_EPS = 1e-6
LNX_EPS = 64e-5
NEG = -1e30

LANES = 128
MXU = 256
HEADS_PER_GROUP = MXU // HEAD_DIM
VMEM_LIMIT = 56 * 1024 * 1024


def _cparams(sem):
    return pltpu.CompilerParams(dimension_semantics=sem, vmem_limit_bytes=VMEM_LIMIT)


def _dot(a, b, precision=None):
    return jnp.dot(a, b, preferred_element_type=F32, precision=precision)


def _dot_nt(a, b, precision=None):
    return lax.dot_general(a, b, (((1,), (1,)), ((), ())), preferred_element_type=F32, precision=precision)


def _split_bf16(x, parts):
    out = []
    for _ in range(parts):
        h = x.astype(BF16)
        out.append(h)
        x = x - h.astype(F32)
    return out


def _mm(a, b, mode, nt=False):
    f = _dot_nt if nt else _dot
    if mode == 6:
        return f(a, b, HI)
    if mode == 1:
        return f(a.astype(BF16), b.astype(BF16))
    if mode == 3:
        ah, al = _split_bf16(a, 2)
        bh, bl = _split_bf16(b, 2)
        return (f(al, bh) + f(ah, bl)) + f(ah, bh)
    if mode[0] == 'L':
        terms = [f(t, b.astype(BF16)) for t in _split_bf16(a, int(mode[1]))]
    else:
        terms = [f(a.astype(BF16), t) for t in _split_bf16(b, int(mode[1]))]
    out = terms[-1]
    for t in terms[-2::-1]:
        out = out + t
    return out


RWKV_MM = dict(lora=1, headsum='L2', cumsum='R3', gram=1, inverse=1, state=1)


def _ada_kernel(c_ref, w_ref, b_ref, o_ref):
    c = c_ref[...]
    s = c * jax.nn.sigmoid(c)
    o_ref[...] = _dot(s.astype(BF16), w_ref[...].astype(BF16)) + b_ref[...]


def _ada_mods(c, w_ada, b_ada):
    n, d = c.shape
    cols = w_ada.shape[1]
    tn = 1024
    return pl.pallas_call(
        _ada_kernel,
        grid=(cols // tn,),
        in_specs=[pl.BlockSpec((n, d), lambda j: (0, 0)),
                  pl.BlockSpec((d, tn), lambda j: (0, j)),
                  pl.BlockSpec((1, tn), lambda j: (0, j))],
        out_specs=pl.BlockSpec((n, tn), lambda j: (0, j)),
        out_shape=jax.ShapeDtypeStruct((n, cols), F32),
        compiler_params=_cparams(("arbitrary",)),
        name="ada_mods",
    )(c, w_ada, b_ada.reshape(1, cols))


def _nmm_kernel(x_ref, g_ref, sc_ref, sh_ref, w_ref, o_ref, h_ref):
    @pl.when(pl.program_id(1) == 0)
    def _():
        x = x_ref[...]
        ms = jnp.mean(x * x, axis=-1, keepdims=True)
        y = x * lax.rsqrt(ms + RMS_EPS) * g_ref[...]
        h_ref[...] = (y * (1.0 + sc_ref[0]) + sh_ref[0]).astype(h_ref.dtype)

    o_ref[...] = _dot(h_ref[...], w_ref[...])


def _norm_mod_matmul(x, g, sc, sh, w, tm, tn):
    t, d = x.shape
    n = w.shape[1]
    nmod, rows, _ = sc.shape
    tiles_per_mod = (t // tm) // nmod
    mod_spec = pl.BlockSpec((1, rows, d), lambda i, j: (i // tiles_per_mod, 0, 0))
    return pl.pallas_call(
        _nmm_kernel,
        grid=(t // tm, n // tn),
        in_specs=[pl.BlockSpec((tm, d), lambda i, j: (i, 0)),
                  pl.BlockSpec((1, d), lambda i, j: (0, 0)),
                  mod_spec, mod_spec,
                  pl.BlockSpec((d, tn), lambda i, j: (0, j))],
        out_specs=[pl.BlockSpec((tm, tn), lambda i, j: (i, j)),
                   pl.BlockSpec((tm, d), lambda i, j: (i, 0))],
        out_shape=[jax.ShapeDtypeStruct((t, n), F32), jax.ShapeDtypeStruct((t, d), BF16)],
        compiler_params=_cparams(("arbitrary", "arbitrary")),
        name="norm_mod_matmul",
    )(x, g.reshape(1, d), sc, sh, w)


def _nm_kernel(x_ref, g_ref, sc_ref, sh_ref, h_ref):
    x = x_ref[...]
    ms = jnp.mean(x * x, axis=-1, keepdims=True)
    y = x * lax.rsqrt(ms + RMS_EPS) * g_ref[...]
    h_ref[...] = (y * (1.0 + sc_ref[0]) + sh_ref[0]).astype(h_ref.dtype)


def _norm_mod(x, g, sc, sh, tm):
    t, d = x.shape
    nmod, rows, _ = sc.shape
    tiles_per_mod = (t // tm) // nmod
    mod_spec = pl.BlockSpec((1, rows, d), lambda i: (i // tiles_per_mod, 0, 0))
    return pl.pallas_call(
        _nm_kernel,
        grid=(t // tm,),
        in_specs=[pl.BlockSpec((tm, d), lambda i: (i, 0)), pl.BlockSpec((1, d), lambda i: (0, 0)), mod_spec, mod_spec],
        out_specs=pl.BlockSpec((tm, d), lambda i: (i, 0)),
        out_shape=jax.ShapeDtypeStruct((t, d), BF16),
        compiler_params=_cparams(("arbitrary",)),
        name="norm_mod",
    )(x, g.reshape(1, d), sc, sh)


def _outproj_kernel(x_ref, yr_ref, yn_ref, ga_ref, w1_ref, w2_ref, o_ref):
    acc = _dot(yr_ref[...].astype(BF16), w1_ref[...]) + _dot(yn_ref[...].astype(BF16), w2_ref[...])
    o_ref[...] = x_ref[...] + ga_ref[0] * acc


def _out_proj(x, y_r, y_n, ga, w1, w2, tm, tn):
    t, d = x.shape
    nmod, rows, _ = ga.shape
    tiles_per_mod = (t // tm) // nmod
    cr, cn = y_r.shape[1], y_n.shape[1]
    return pl.pallas_call(
        _outproj_kernel,
        grid=(t // tm, d // tn),
        in_specs=[pl.BlockSpec((tm, tn), lambda i, j: (i, j)),
                  pl.BlockSpec((tm, cr), lambda i, j: (i, 0)),
                  pl.BlockSpec((tm, cn), lambda i, j: (i, 0)),
                  pl.BlockSpec((1, rows, tn), lambda i, j: (i // tiles_per_mod, 0, j)),
                  pl.BlockSpec((cr, tn), lambda i, j: (0, j)),
                  pl.BlockSpec((cn, tn), lambda i, j: (0, j))],
        out_specs=pl.BlockSpec((tm, tn), lambda i, j: (i, j)),
        out_shape=jax.ShapeDtypeStruct((t, d), F32),
        compiler_params=_cparams(("arbitrary", "arbitrary")),
        name="out_proj",
    )(x, y_r, y_n, ga, w1, w2)


def _softplus(z):
    return jnp.maximum(z, 0.0) + jnp.log(1.0 + jnp.exp(-jnp.abs(z)))


def _rwkv_kernel(p_ref, shift_ref, s0_ref, mu_ref, vec_ref, wup_ref, aup_ref, gup_ref,
                 y_ref, sfin_ref, carry, state, *, t_valid, n_chunks):
    c = pl.program_id(1)
    chunk = p_ref.shape[1]
    cw = vec_ref.shape[1]
    n_groups = cw // MXU
    hg = HEADS_PER_GROUP
    rows_g = hg * chunk

    @pl.when(c == 0)
    def _():
        carry[...] = shift_ref[0]
        state[...] = s0_ref[0]

    p = p_ref[0]
    row = lax.broadcasted_iota(jnp.int32, (chunk, 1), 0)
    prev = jnp.where(row == 0, carry[...], pltpu.roll(p, 1, axis=0))
    carry[...] = p[chunk - 1:chunk, :]
    xs = p + (prev - p) * mu_ref[...]

    w0, a0, k_k, k_a, r_k, lnx_w, lnx_b = (vec_ref[i:i + 1, :] for i in range(7))
    r = xs[:, 0:cw]
    k = xs[:, cw:2 * cw]
    v = xs[:, 2 * cw:3 * cw]
    o = 3 * cw
    nw, na, ng = wup_ref.shape[0], aup_ref.shape[0], gup_ref.shape[0]
    xw = xs[:, o:o + nw]
    xa = xs[:, o + nw:o + nw + na]
    xg = xs[:, o + nw + na:o + nw + na + ng]
    pm = RWKV_MM
    w_log = -_softplus(-(w0 + _mm(jnp.tanh(xw), wup_ref[...], pm['lora']))) - 0.5
    a = jax.nn.sigmoid(a0 + _mm(xa, aup_ref[...], pm['lora']))
    gate = _mm(jax.nn.sigmoid(xg), gup_ref[...], pm['lora'])

    gi = lax.broadcasted_iota(jnp.int32, (MXU, MXU), 0) // HEAD_DIM
    gj = lax.broadcasted_iota(jnp.int32, (MXU, MXU), 1) // HEAD_DIM
    ones_bd = (gi == gj).astype(F32)

    def head_sum(x):
        return jnp.concatenate([_mm(x[:, g * MXU:(g + 1) * MXU], ones_bd, pm['headsum']) for g in range(n_groups)],
                               axis=1)

    kk = k * k_k
    kk = kk / jnp.maximum(jnp.sqrt(head_sum(kk * kk)), 1e-12)
    k2 = k * (1.0 + (a - 1.0) * k_a)
    logdec = -jnp.exp(w_log)
    if t_valid < chunk * n_chunks:
        valid = (row + c * chunk) < t_valid
        logdec = jnp.where(valid, logdec, 0.0)
        kk = jnp.where(valid, kk, 0.0)
        k2 = jnp.where(valid, k2, 0.0)
        v = jnp.where(valid, v, 0.0)

    ti = lax.broadcasted_iota(jnp.int32, (chunk, chunk), 0)
    tj = lax.broadcasted_iota(jnp.int32, (chunk, chunk), 1)
    cum = _mm((tj <= ti).astype(F32), logdec, pm['cumsum'])
    cum_end = cum[chunk - 1:chunk, :]
    e_neg = jnp.exp(-cum)
    e_rem = jnp.exp(cum_end - cum)
    r_t = r * jnp.exp(cum)
    a_t = -kk * jnp.exp(cum - logdec)
    b_vec = kk * a
    b_t = b_vec * e_neg
    k_t = k2 * e_neg
    b_rem = b_vec * e_rem
    k_rem = k2 * e_rem
    w_end = jnp.exp(cum_end)

    lane_head = lax.broadcasted_iota(jnp.int32, (chunk, MXU), 1) // HEAD_DIM
    ri = lax.broadcasted_iota(jnp.int32, (rows_g, rows_g), 0)
    rj = lax.broadcasted_iota(jnp.int32, (rows_g, rows_g), 1)
    strict = rj < ri
    incl = rj <= ri
    eye_r = (ri == rj).astype(F32)
    di = lax.broadcasted_iota(jnp.int32, (MXU, MXU), 0)
    dj = lax.broadcasted_iota(jnp.int32, (MXU, MXU), 1)
    n_double = max(int(math.ceil(math.log2(chunk))) - 1, 0)

    def bd(x):
        return jnp.concatenate([jnp.where(lane_head == h, x, 0.0) for h in range(hg)], axis=0)

    def stack(x):
        return jnp.concatenate([x[:, h * HEAD_DIM:(h + 1) * HEAD_DIM] for h in range(hg)], axis=0)

    def unstack(x):
        return jnp.concatenate([x[h * chunk:(h + 1) * chunk, :] for h in range(hg)], axis=1)

    ys = []
    for g in range(n_groups):
        sl = slice(g * MXU, (g + 1) * MXU)
        a_bd, r_bd = bd(a_t[:, sl]), bd(r_t[:, sl])
        b_bd, k_bd = bd(b_t[:, sl]), bd(k_t[:, sl])
        v_st = stack(v[:, sl])
        a_ab = jnp.where(strict, _mm(a_bd, b_bd, pm['gram'], nt=True), 0.0)
        a_ak = jnp.where(strict, _mm(a_bd, k_bd, pm['gram'], nt=True), 0.0)
        a_rb = jnp.where(incl, _mm(r_bd, b_bd, pm['gram'], nt=True), 0.0)
        a_rk = jnp.where(incl, _mm(r_bd, k_bd, pm['gram'], nt=True), 0.0)
        tinv = eye_r + a_ab
        pw = a_ab
        for _ in range(n_double):
            pw = _mm(pw, pw, pm['inverse'])
            tinv = tinv + _mm(pw, tinv, pm['inverse'])
        s0 = state[g]
        z = _mm(a_bd, s0, pm['state']) + _mm(a_ak, v_st, pm['state'])
        u = _mm(tinv, z, pm['state'])
        y_st = _mm(r_bd, s0, pm['state']) + _mm(a_rb, u, pm['state']) + _mm(a_rk, v_st, pm['state'])
        w_col = jnp.sum(jnp.where(di == dj, jnp.broadcast_to(w_end[:, sl], (MXU, MXU)), 0.0), axis=1, keepdims=True)
        state[g] = (w_col * s0 + _mm(bd(b_rem[:, sl]).T, u, pm['state'])
                    + _mm(bd(k_rem[:, sl]).T, v_st, pm['state']))
        ys.append(unstack(y_st))
    y = jnp.concatenate(ys, axis=1)

    inv_n = 1.0 / HEAD_DIM
    mean = head_sum(y) * inv_n
    d = y - mean
    var = head_sum(d * d) * inv_n
    yn = d * lax.rsqrt(var + LNX_EPS) * lnx_w + lnx_b
    bonus = head_sum(r * k2 * r_k) * v
    y_ref[0] = ((yn + bonus) * gate).astype(y_ref.dtype)

    @pl.when(c == n_chunks - 1)
    def _():
        sfin_ref[0] = state[...]


def _rwkv_pad_cols(x, cw, nw, na, ng):
    o = 3 * cw
    parts = [x[..., :o + nw], x[..., o + nw:o + nw + na], x[..., o + nw + na:]]
    widths = [o + _rup(nw, LANES), _rup(na, LANES), _rup(ng, LANES)]
    out = []
    for part, wd in zip(parts, widths):
        pad = [(0, 0)] * (x.ndim - 1) + [(0, wd - part.shape[-1])]
        out.append(jnp.pad(part, pad))
    return jnp.concatenate(out, axis=-1)


def _rwkv_unpad_cols(x, cw, nw, na, ng):
    o = 3 * cw
    o2 = o + _rup(nw, LANES)
    o3 = o2 + _rup(na, LANES)
    return jnp.concatenate([x[..., :o + nw], x[..., o2:o2 + na], x[..., o3:o3 + ng]], axis=-1)


def _rup(x, m):
    return (x + m - 1) // m * m


def _rwkv_mix(p_r, shift_prev, wkv0, W, chunk, t_valid):
    b, tpad, pr = p_r.shape
    cw = W['rwkv_w0'].shape[0]
    nh = cw // HEAD_DIM
    n_groups = cw // MXU
    nw, na, ng = W['rwkv_w_up'].shape[0], W['rwkv_a_up'].shape[0], W['rwkv_g_up'].shape[0]
    n_chunks = tpad // chunk
    mu = _rwkv_pad_cols(W['rwkv_mu'], cw, nw, na, ng).reshape(1, pr)
    vecs = jnp.stack([W['rwkv_w0'], W['rwkv_a0'], W['rwkv_k_k'], W['rwkv_k_a'], W['rwkv_r_k'].reshape(cw),
                      W['lnx_w'], W['lnx_b'], jnp.zeros((cw,), F32)])
    wup = jnp.pad(W['rwkv_w_up'], ((0, _rup(nw, LANES) - nw), (0, 0)))
    aup = jnp.pad(W['rwkv_a_up'], ((0, _rup(na, LANES) - na), (0, 0)))
    gup = jnp.pad(W['rwkv_g_up'], ((0, _rup(ng, LANES) - ng), (0, 0)))
    shift3 = _rwkv_pad_cols(shift_prev, cw, nw, na, ng).reshape(b, 1, pr)
    s0 = wkv0.astype(F32).transpose(0, 1, 3, 2).reshape(b, n_groups, MXU, HEAD_DIM)
    const = lambda shape: pl.BlockSpec(shape, lambda i, c: (0,) * len(shape))
    y, sfin = pl.pallas_call(
        functools.partial(_rwkv_kernel, t_valid=t_valid, n_chunks=n_chunks),
        grid=(b, n_chunks),
        in_specs=[pl.BlockSpec((1, chunk, pr), lambda i, c: (i, c, 0)),
                  pl.BlockSpec((1, 1, pr), lambda i, c: (i, 0, 0)),
                  pl.BlockSpec((1, n_groups, MXU, HEAD_DIM), lambda i, c: (i, 0, 0, 0)),
                  const((1, pr)), const((8, cw)), const(wup.shape), const(aup.shape), const(gup.shape)],
        out_specs=[pl.BlockSpec((1, chunk, cw), lambda i, c: (i, c, 0)),
                   pl.BlockSpec((1, n_groups, MXU, HEAD_DIM), lambda i, c: (i, 0, 0, 0))],
        out_shape=[jax.ShapeDtypeStruct((b, tpad, cw), F32),
                   jax.ShapeDtypeStruct((b, n_groups, MXU, HEAD_DIM), F32)],
        scratch_shapes=[pltpu.VMEM((1, pr), F32), pltpu.VMEM((n_groups, MXU, HEAD_DIM), F32)],
        compiler_params=_cparams(("arbitrary", "arbitrary")),
        name="rwkv_mix",
    )(p_r, shift3, s0, mu, vecs, wup, aup, gup)
    s_fin = sfin.reshape(b, nh, HEAD_DIM, HEAD_DIM).transpose(0, 1, 3, 2)
    return y, s_fin


def _head_ones():
    gi = lax.broadcasted_iota(jnp.int32, (MXU, MXU), 0) // HEAD_DIM
    gj = lax.broadcasted_iota(jnp.int32, (MXU, MXU), 1) // HEAD_DIM
    return (gi == gj).astype(F32)


def _nsa_proj_kernel(p_ref, g_ref, q_ref, rows_ref, win_ref, gate_ref, *, nsa_w, kvw):
    ones_bd = _head_ones()

    def hnorm(x, gvec):
        ms = _dot(x * x, ones_bd, HI) * (1.0 / HEAD_DIM)
        return x * lax.rsqrt(ms + RMS_EPS) * gvec

    for i in range(nsa_w // MXU):
        sl = slice(i * MXU, (i + 1) * MXU)
        q_ref[:, sl] = hnorm(p_ref[:, sl], g_ref[0:1, :])
    o = nsa_w
    rows_ref[:, 0:2 * kvw] = p_ref[:, o:o + 2 * kvw]
    rows_ref[:, 2 * kvw:3 * kvw] = hnorm(p_ref[:, o + 2 * kvw:o + 3 * kvw], g_ref[2:3, :])
    rows_ref[:, 3 * kvw:4 * kvw] = p_ref[:, o + 3 * kvw:o + 4 * kvw]
    win_ref[:, 0:kvw] = hnorm(p_ref[:, o + 4 * kvw:o + 5 * kvw], g_ref[3:4, :])
    win_ref[:, kvw:2 * kvw] = p_ref[:, o + 5 * kvw:o + 6 * kvw]
    gate_ref[...] = jax.nn.sigmoid(p_ref[:, o + 6 * kvw:])


def _nsa_project_call(p_n, qk_norm_g, nsa_w, kvw, tm):
    t, pc = p_n.shape
    assert kvw == MXU and nsa_w % MXU == 0
    gcols = pc - nsa_w - 6 * kvw
    gvec = jnp.tile(qk_norm_g, (1, MXU // HEAD_DIM))
    return pl.pallas_call(
        functools.partial(_nsa_proj_kernel, nsa_w=nsa_w, kvw=kvw),
        grid=(t // tm,),
        in_specs=[pl.BlockSpec((tm, pc), lambda i: (i, 0)),
                  pl.BlockSpec(gvec.shape, lambda i: (0, 0))],
        out_specs=[pl.BlockSpec((tm, nsa_w), lambda i: (i, 0)),
                   pl.BlockSpec((tm, 4 * kvw), lambda i: (i, 0)),
                   pl.BlockSpec((tm, 2 * kvw), lambda i: (i, 0)),
                   pl.BlockSpec((tm, gcols), lambda i: (i, 0))],
        out_shape=[jax.ShapeDtypeStruct((t, nsa_w), F32), jax.ShapeDtypeStruct((t, 4 * kvw), F32),
                   jax.ShapeDtypeStruct((t, 2 * kvw), F32), jax.ShapeDtypeStruct((t, gcols), F32)],
        compiler_params=_cparams(("arbitrary",)),
        name="nsa_project",
    )(p_n, gvec)


def _cmp_part_kernel(*refs, n_in, row_w, kvw, n_prefetch=0):
    refs = refs[n_prefetch:]
    x_refs, w_ref, o_ref = refs[:n_in], refs[n_in], refs[n_in + 1]
    for typ in range(2):
        acc = None
        for s in range(CMP_STRIDE):
            lo = s * row_w + typ * kvw
            xs = jnp.concatenate([x[0, :, lo:lo + kvw] for x in x_refs], axis=0) if n_in > 1 else x_refs[0][0, :, lo:lo + kvw]
            d = _dot(xs.astype(BF16), w_ref[typ, s])
            acc = d if acc is None else acc + d
        o_ref[0, :, typ * 2 * kvw:(typ + 1) * 2 * kvw] = acc


def _cmp_first_weights(cmp_w1, nkv):
    r2 = CMP_LEN // CMP_STRIDE
    e = cmp_w1.shape[-1]
    w1r = cmp_w1.reshape(2, r2, CMP_STRIDE, HEAD_DIM, e)
    eye = jnp.eye(nkv, dtype=F32)
    big = jnp.einsum('yhsde,gk->ysgdkhe', w1r, eye)
    return big.reshape(2, CMP_STRIDE, nkv * HEAD_DIM, nkv * r2 * e).astype(BF16)


def _cmp_parts_prompt(rows2d, w_big, b, t, kvw):
    row_w = rows2d.shape[2]
    nsub = t // CMP_STRIDE
    blk = min(nsub, LANES)
    return pl.pallas_call(
        functools.partial(_cmp_part_kernel, n_in=1, row_w=row_w, kvw=kvw),
        grid=(b, nsub // blk),
        in_specs=[pl.BlockSpec((1, blk, CMP_STRIDE * row_w), lambda i, j: (i, j, 0)),
                  pl.BlockSpec(w_big.shape, lambda i, j: (0, 0, 0, 0))],
        out_specs=pl.BlockSpec((1, blk, 4 * kvw), lambda i, j: (i, j, 0)),
        out_shape=jax.ShapeDtypeStruct((b, nsub, 4 * kvw), F32),
        compiler_params=_cparams(("arbitrary", "arbitrary")),
        name="cmp_parts",
    )(rows2d.reshape(b, nsub, CMP_STRIDE * row_w), w_big)


def _cmp_finish_kernel(*refs, n_parts, nc, nkv):
    p_refs = refs[:n_parts]
    b1_ref, w2_ref, b2_ref, g_ref, kc_ref, vT_ref = refs[n_parts:]
    ns = kc_ref.shape[1]
    e = b1_ref.shape[2]
    have = sum(p.shape[1] for p in p_refs)
    row = lax.broadcasted_iota(jnp.int32, (ns, 1), 0)
    eye = jnp.where(lax.broadcasted_iota(jnp.int32, (HEAD_DIM, HEAD_DIM), 0)
                    == lax.broadcasted_iota(jnp.int32, (HEAD_DIM, HEAD_DIM), 1), 1.0, 0.0)
    for tg in range(2 * nkv):
        typ = tg // nkv
        cols = slice(tg * 2 * e, (tg + 1) * 2 * e)
        pieces = [p[0, :, cols] for p in p_refs]
        if have < ns:
            pieces.append(jnp.zeros((ns - have, 2 * e), F32))
        part = jnp.concatenate(pieces, axis=0) if len(pieces) > 1 else pieces[0]
        nxt = pltpu.roll(part[:, e:], ns - 1, axis=0)
        hid = part[:, :e] + nxt + b1_ref[typ]
        out = _dot(jax.nn.gelu(hid).astype(BF16), w2_ref[typ].astype(BF16)) + b2_ref[typ]
        g = tg % nkv
        hs = slice(g * HEAD_DIM, (g + 1) * HEAD_DIM)
        if typ == 0:
            out = out * lax.rsqrt(jnp.mean(out * out, axis=-1, keepdims=True) + RMS_EPS) * g_ref[...]
            kc_ref[0, :, hs] = jnp.where(row < nc, out, 0.0)
        else:
            vT_ref[0, hs, :] = _dot_nt(eye, jnp.where(row < nc, out, 0.0), HI)


def _cmp_finish(parts_list, ns, cmp_b1, cmp_w2, cmp_b2, g1, nc, nkv):
    b = parts_list[0].shape[0]
    e = cmp_b1.shape[1]
    kvw = nkv * HEAD_DIM
    return pl.pallas_call(
        functools.partial(_cmp_finish_kernel, n_parts=len(parts_list), nc=nc, nkv=nkv),
        grid=(b,),
        in_specs=[pl.BlockSpec((1,) + p.shape[1:], lambda i: (i, 0, 0)) for p in parts_list] + [
                  pl.BlockSpec((2, 1, e), lambda i: (0, 0, 0)),
                  pl.BlockSpec((2, e, HEAD_DIM), lambda i: (0, 0, 0)),
                  pl.BlockSpec((2, 1, HEAD_DIM), lambda i: (0, 0, 0)),
                  pl.BlockSpec((1, HEAD_DIM), lambda i: (0, 0))],
        out_specs=[pl.BlockSpec((1, ns, kvw), lambda i: (i, 0, 0)), pl.BlockSpec((1, kvw, ns), lambda i: (i, 0, 0))],
        out_shape=[jax.ShapeDtypeStruct((b, ns, kvw), F32), jax.ShapeDtypeStruct((b, kvw, ns), F32)],
        compiler_params=_cparams(("arbitrary",)),
        name="cmp_finish",
    )(*parts_list, cmp_b1.reshape(2, 1, e), cmp_w2, cmp_b2.reshape(2, 1, HEAD_DIM), g1.reshape(1, HEAD_DIM))


QT = 128
SEL_CHAINS = 2
QK_SCALE = HEAD_DIM ** -0.5
assert math.log2(HEAD_DIM) % 2 == 0, "QK_SCALE must be a power of two to be folded into q exactly"


def _rel_table_np_dist(dist, table):
    onehot = (_rel_bucket(dist)[..., None] == jnp.arange(REL_BUCKETS)).astype(F32)
    return jnp.einsum('...b,bh->...h', onehot, table, precision=HI)


def _rel_table_per_lane(dist, table_l):
    bucket = _rel_bucket(dist)
    out = jnp.zeros(dist.shape, F32)
    for b in range(REL_BUCKETS):
        out = out + jnp.where(bucket == b, table_l[b][None, :], 0.0)
    return out


def _softmax_update(s, mask, m, l):
    m_new = jnp.maximum(m, jnp.max(jnp.where(mask, s, NEG), axis=0, keepdims=True))
    alpha = jnp.exp(m - m_new)
    p = jnp.where(mask, jnp.exp(s - m_new), 0.0)
    return p, m_new, alpha, alpha * l + jnp.sum(p, axis=0, keepdims=True)


def _rank_select(score, score_ref, cur, n_sel):
    nb = score.shape[0]
    score_ref[0:nb, :] = score
    jrow = lax.broadcasted_iota(jnp.int32, score.shape, 0)

    def body(j, rank):
        other = score_ref[pl.ds(j, 1), :]
        beats = (other > score) | ((other == score) & (jrow > j))
        return rank + jnp.where(beats, 1.0, 0.0)

    rank = lax.fori_loop(0, nb, body, jnp.zeros(score.shape, F32), unroll=8)
    return jnp.where((rank < n_sel) & (jrow <= cur), 1.0, 0.0)


def _extract_select(score, cur, n_sel):
    nb = score.shape[0]
    jrow = lax.broadcasted_iota(jnp.int32, score.shape, 0)
    sel = jnp.zeros(score.shape, F32)
    x = score
    for _ in range(n_sel):
        m = jnp.max(x, axis=0, keepdims=True)
        first = jnp.min(jnp.where(x == m, jrow, nb), axis=0, keepdims=True)
        hit = jrow == first
        sel = jnp.where(hit, 1.0, sel)
        x = jnp.where(hit, -jnp.inf, x)
    return jnp.where(jrow <= cur, sel, 0.0)


def _nsa_prompt_kernel(qT_ref, gT_ref, kc_ref, vcT_ref, ks_ref, vsT_ref, kw_ref, vwT_ref, bc_ref, toep_ref,
                       o_ref, pg_ref, score_ref, sel_ref, *, nc, nb, n_sel, ngrp):
    qt = pl.program_id(2)
    lanes = ngrp * QT
    q = (qT_ref[0, 0, 0] * QK_SCALE).astype(BF16)
    iq = lax.broadcasted_iota(jnp.int32, (1, QT), 1)
    q_pos = qt * QT + iq
    tile4 = lambda x: jnp.concatenate([x] * ngrp, axis=1)

    ncp = kc_ref.shape[2]
    s = _dot(kc_ref[0, 0].astype(BF16), q)
    s = s + jnp.concatenate([bc_ref[0, r] for r in range(ngrp)], axis=1)
    crow = lax.broadcasted_iota(jnp.int32, (ncp, QT), 0)
    ok_c = tile4((crow * CMP_STRIDE + (CMP_LEN - 1) <= q_pos) & (crow < nc))
    p, _, _, l = _softmax_update(s, ok_c, jnp.full((1, lanes), NEG, F32), jnp.zeros((1, lanes), F32))
    p = p * jnp.where(l > 0.0, 1.0 / jnp.where(l > 0.0, l, 1.0), 0.0)
    o_c = _dot(vcT_ref[0, 0].astype(BF16), p.astype(BF16))

    p_grp = p[:, 0:QT]
    for r in range(1, ngrp):
        p_grp = p_grp + p[:, r * QT:(r + 1) * QT]
    pad = 8
    pg_ref[...] = jnp.zeros(pg_ref.shape, F32)
    pg_ref[pad:pad + ncp, :] = p_grp
    r1 = SEL_BLOCK // CMP_STRIDE
    offs, wts = _slc_offsets()
    p_slc = None
    for o, wt in zip(offs, wts):
        term = float(wt) * pg_ref[pl.ds(pad + int(o), nb, stride=r1), :]
        p_slc = term if p_slc is None else p_slc + term
    jrow = lax.broadcasted_iota(jnp.int32, (nb, QT), 0)
    cur = q_pos // SEL_BLOCK
    forced = (jrow == 0) | (jrow == cur) | (jrow == cur - 1)
    score = jnp.where(jrow > cur, -1.0, jnp.where(forced, 1e6, p_slc))
    sel_ref[0:nb, :] = _rank_select(score, score_ref, cur, n_sel)

    ik = lax.broadcasted_iota(jnp.int32, (QT, QT), 0)
    iqq = lax.broadcasted_iota(jnp.int32, (QT, QT), 1)
    blocks_per_tile = QT // SEL_BLOCK

    def attend(kp, carry, k_ref, vT_ref, mask_fn):
        m, l, acc = carry
        s_all = _dot(k_ref[0, 0, kp], q)
        deltas = [qt - (2 * kp + i) for i in range(2)]
        mask = jnp.concatenate([mask_fn(2 * kp + i, deltas[i]) for i in range(2)], axis=0)
        m_out, l_out, alphas, ps = [], [], [], []
        for r in range(ngrp):
            ls = slice(r * QT, (r + 1) * QT)
            bias = jnp.concatenate([toep_ref[0, r, jnp.maximum(d, 0)] for d in deltas], axis=0)
            s = jnp.where(mask, s_all[:, ls] + bias, NEG)
            m_new = jnp.maximum(m[:, ls], jnp.max(s, axis=0, keepdims=True))
            alpha = jnp.exp(m[:, ls] - m_new)
            p = jnp.exp(s - m_new)
            m_out.append(m_new)
            l_out.append(alpha * l[:, ls] + jnp.sum(p, axis=0, keepdims=True))
            alphas.append(alpha)
            ps.append(p.astype(BF16))
        cat = lambda xs: jnp.concatenate(xs, axis=1)
        acc = cat(alphas) * acc + _dot(vT_ref[0, 0, kp], cat(ps))
        return cat(m_out), cat(l_out), acc

    def sel_mask(kt, delta):
        rows = [jnp.broadcast_to(sel_ref[pl.ds(kt * blocks_per_tile + i, 1), :], (SEL_BLOCK, QT))
                for i in range(blocks_per_tile)]
        chosen = jnp.concatenate(rows, axis=0) > 0.5
        return chosen & (ik - iqq <= delta * QT)

    def win_mask(kt, delta):
        dist = delta * QT + iqq - ik
        return (dist >= 0) & (dist < WINDOW)

    init = (jnp.full((1, lanes), NEG, F32), jnp.zeros((1, lanes), F32), jnp.zeros((HEAD_DIM, lanes), F32))

    def merge(states):
        m = states[0][0]
        for st in states[1:]:
            m = jnp.maximum(m, st[0])
        l, acc = None, None
        for m_i, l_i, acc_i in states:
            w = jnp.exp(m_i - m)
            l = w * l_i if l is None else l + w * l_i
            acc = w * acc_i if acc is None else acc + w * acc_i
        return acc * jnp.where(l > 0.0, 1.0 / jnp.where(l > 0.0, l, 1.0), 0.0)

    diag = qt // 2
    def sel_body(i, carry):
        return tuple(attend(SEL_CHAINS * i + c, carry[c], ks_ref, vsT_ref, sel_mask) for c in range(SEL_CHAINS))

    o_s = merge(lax.fori_loop(0, (diag + SEL_CHAINS) // SEL_CHAINS, sel_body, (init,) * SEL_CHAINS))
    first = jnp.maximum(qt - WINDOW // QT, 0) // 2
    states = []
    for i in range((WINDOW // QT) // 2 + 1):
        kp = diag - i
        live = kp >= first
        states.append(attend(jnp.maximum(kp, 0), init, kw_ref, vwT_ref,
                             lambda kt, delta, live=live: win_mask(kt, delta) & live))
    o_w = merge(states)
    g = gT_ref[0, 0, 0]
    o_ref[0, 0, 0] = g[0:1, :] * o_c + g[1:2, :] * o_s + g[2:3, :] * o_w


def _nsa_prompt(p_n, W, b, t, nkv, ngrp, tm):
    nsa_w = nkv * ngrp * HEAD_DIM
    kvw = nkv * HEAD_DIM
    qn, rows2d, win2d, gates = _nsa_project_call(p_n, W['qk_norm_g'], nsa_w, kvw, tm)
    rows = rows2d.reshape(b, t, 4, nkv, HEAD_DIM)
    win = win2d.reshape(b, t, 2, nkv, HEAD_DIM)
    ns = t // CMP_STRIDE
    nc = ns - CMP_LEN // CMP_STRIDE + 1
    nb = t // SEL_BLOCK
    n_sel = min(SEL_TOPK, nb)
    nqt = t // QT
    parts = _cmp_parts_prompt(rows2d.reshape(b, t, 4 * kvw), _cmp_first_weights(W['cmp_w1'], nkv), b, t, kvw)
    kc_cat, vcT_cat = _cmp_finish([parts], ns, W['cmp_b1'], W['cmp_w2'], W['cmp_b2'], W['qk_norm_g'][1], nc, nkv)
    kc = kc_cat.reshape(b, ns, nkv, HEAD_DIM).transpose(0, 2, 1, 3)
    vcT = vcT_cat.reshape(b, nkv, HEAD_DIM, ns)
    qT = qn.reshape(b, nqt, QT, nkv, ngrp, HEAD_DIM).transpose(0, 3, 1, 5, 4, 2).reshape(b, nkv, nqt, HEAD_DIM, ngrp * QT)
    ng = 3 * nkv * ngrp
    gT = gates[:, :ng].reshape(b, nqt, QT, nkv, ngrp, 3).transpose(0, 3, 1, 5, 4, 2).reshape(b, nkv, nqt, 3, ngrp * QT)
    gT = jnp.pad(gT, ((0, 0), (0, 0), (0, 0), (0, 5), (0, 0)))
    assert nqt % (2 * SEL_CHAINS) == 0 and (WINDOW // QT) % 2 == 0
    npair = nqt // 2
    k_tiles = lambda x: x.transpose(0, 2, 1, 3).reshape(b, nkv, npair, 2 * QT, HEAD_DIM).astype(BF16)
    vT_tiles = lambda x: x.reshape(b, npair, 2 * QT, nkv, HEAD_DIM).transpose(0, 3, 1, 4, 2).astype(BF16)
    ks, vsT = k_tiles(rows[:, :, 2]), vT_tiles(rows[:, :, 3])
    kw, vwT = k_tiles(win[:, :, 0]), vT_tiles(win[:, :, 1])
    table = W['rel_bias'].astype(F32)
    c_end = jnp.arange(ns) * CMP_STRIDE + CMP_LEN - 1
    bias_c = _rel_table_np_dist(jnp.arange(t)[None, :] - c_end[:, None], table)
    bias_c = bias_c.transpose(2, 0, 1).reshape(nkv, ngrp, ns, t)
    dd = (jnp.arange(nqt)[:, None, None] * QT + jnp.arange(QT)[None, None, :] - jnp.arange(QT)[None, :, None])
    toep = _rel_table_np_dist(dd, table).transpose(3, 0, 1, 2).reshape(nkv, ngrp, nqt, QT, QT)
    lanes = ngrp * QT
    kv_spec = lambda shape: pl.BlockSpec((1, 1) + shape, lambda i, g, j: (i, g) + (0,) * len(shape))
    yT = pl.pallas_call(
        functools.partial(_nsa_prompt_kernel, nc=nc, nb=nb, n_sel=n_sel, ngrp=ngrp),
        grid=(b, nkv, nqt),
        in_specs=[pl.BlockSpec((1, 1, 1, HEAD_DIM, lanes), lambda i, g, j: (i, g, j, 0, 0)),
                  pl.BlockSpec((1, 1, 1, 8, lanes), lambda i, g, j: (i, g, j, 0, 0)),
                  kv_spec((ns, HEAD_DIM)), kv_spec((HEAD_DIM, ns)),
                  kv_spec((npair, 2 * QT, HEAD_DIM)), kv_spec((npair, HEAD_DIM, 2 * QT)),
                  kv_spec((npair, 2 * QT, HEAD_DIM)), kv_spec((npair, HEAD_DIM, 2 * QT)),
                  pl.BlockSpec((1, ngrp, ns, QT), lambda i, g, j: (g, 0, 0, j)),
                  pl.BlockSpec((1, ngrp, nqt, QT, QT), lambda i, g, j: (g, 0, 0, 0, 0))],
        out_specs=pl.BlockSpec((1, 1, 1, HEAD_DIM, lanes), lambda i, g, j: (i, g, j, 0, 0)),
        out_shape=jax.ShapeDtypeStruct((b, nkv, nqt, HEAD_DIM, lanes), F32),
        scratch_shapes=[pltpu.VMEM((ns + 16, QT), F32), pltpu.VMEM((_rup(nb, 8), QT), F32),
                        pltpu.VMEM((_rup(nb, 8), QT), F32)],
        compiler_params=_cparams(("arbitrary", "arbitrary", "arbitrary")),
        name="nsa_prompt_attn",
    )(qT, gT, kc, vcT, ks, vsT, kw, vwT, bias_c, toep)
    y = yT.reshape(b, nkv, nqt, HEAD_DIM, ngrp, QT).transpose(0, 2, 5, 1, 4, 3).reshape(b * t, nsa_w)
    return y, rows, win[:, t - min(WINDOW, t):]


def _cmp_part_paged_kernel(*refs, npg, kvw):
    x_refs, w_ref, o_ref = refs[1:1 + npg], refs[1 + npg], refs[2 + npg]
    rows = x_refs[0].shape[2]
    sub = rows // CMP_STRIDE
    ri = lax.broadcasted_iota(jnp.int32, (rows, rows), 0)
    ci = lax.broadcasted_iota(jnp.int32, (rows, rows), 1)
    perm = jnp.where(ci == (ri % sub) * CMP_STRIDE + ri // sub, 1.0, 0.0).astype(BF16)
    xp = [_dot_nt(perm, x[0].astype(BF16)) for x in x_refs]
    for typ in range(2):
        acc = None
        for s in range(CMP_STRIDE):
            xs = jnp.concatenate([p[s * sub:(s + 1) * sub, typ * kvw:(typ + 1) * kvw] for p in xp], axis=0)
            d = _dot(xs.astype(BF16), w_ref[typ, s])
            acc = d if acc is None else acc + d
        o_ref[0, :, typ * 2 * kvw:(typ + 1) * 2 * kvw] = acc


def _cmp_parts_sample(cache_pages, page_table, w_big, kvw, npg):
    b, n_pages = page_table.shape
    rows = cache_pages.shape[2]
    sub = rows // CMP_STRIDE
    in_specs = [pl.BlockSpec((1, 2 * kvw, rows), (lambda i, j, pt, k=k: (pt[i, j * npg + k], 0, 0)))
                for k in range(npg)]
    in_specs.append(pl.BlockSpec(w_big.shape, lambda i, j, pt: (0, 0, 0, 0)))
    return pl.pallas_call(
        functools.partial(_cmp_part_paged_kernel, npg=npg, kvw=kvw),
        grid_spec=pltpu.PrefetchScalarGridSpec(
            num_scalar_prefetch=1, grid=(b, n_pages // npg), in_specs=in_specs,
            out_specs=pl.BlockSpec((1, npg * sub, 4 * kvw), lambda i, j, pt: (i, j, 0))),
        out_shape=jax.ShapeDtypeStruct((b, n_pages * sub, 4 * kvw), F32),
        compiler_params=_cparams(("arbitrary", "arbitrary")),
        name="cmp_parts_paged",
    )(page_table, *([cache_pages] * npg), w_big)


def _inv_pos(l):
    return jnp.where(l > 0.0, 1.0 / jnp.where(l > 0.0, l, 1.0), 0.0)


def _nsa_sample_kernel(*refs, npg, n_steps, nc, nb, n_sel, past, ds, keep, ngrp, n_lanes):
    pt_ref = refs[0]
    q_ref, g_ref, kc_ref, vcT_ref, bc_ref, win_ref, wnew_ref, bw_ref, rnew_ref, bs_ref = refs[1:11]
    page_refs = refs[11:11 + npg]
    o_ref = refs[11 + npg]
    (m_ref, l_ref, acc_ref, base_ref, sel_ref, pg_ref, score_ref,
     kw_ref, kn_ref, vn_ref) = refs[12 + npg:]
    del pt_ref
    j = pl.program_id(1)
    scale = HEAD_DIM ** -0.5
    kvw = q_ref.shape[1]
    eye = jnp.where(lax.broadcasted_iota(jnp.int32, (LANES, LANES), 0)
                    == lax.broadcasted_iota(jnp.int32, (LANES, LANES), 1), 1.0, 0.0).astype(BF16)
    qbd = q_ref[0]
    lane = lax.broadcasted_iota(jnp.int32, (1, LANES), 1)
    qi = (lane // ngrp) % ds
    q_pos = past + qi
    n_pages = npg * n_steps

    @pl.when(j == 0)
    def _():
        nsp = kc_ref.shape[1]
        s = _dot(kc_ref[0].astype(BF16), qbd) * scale + bc_ref[...]
        crow = lax.broadcasted_iota(jnp.int32, (nsp, LANES), 0)
        ok = (crow * CMP_STRIDE + (CMP_LEN - 1) <= q_pos) & (crow < nc)
        p, _, _, l = _softmax_update(s, ok, jnp.full((1, LANES), NEG, F32), jnp.zeros((1, LANES), F32))
        p = p * _inv_pos(l)
        o_c = _dot(vcT_ref[0].astype(BF16), p.astype(BF16))
        li = lax.broadcasted_iota(jnp.int32, (LANES, LANES), 0)
        lj = lax.broadcasted_iota(jnp.int32, (LANES, LANES), 1)
        fold = jnp.where((li // ngrp == lj) & (li < n_lanes), 1.0, 0.0)
        p_grp = _dot(p, fold, HI)
        pad = 8
        pg_ref[...] = jnp.zeros(pg_ref.shape, F32)
        pg_ref[pad:pad + nsp, :] = p_grp
        nbp = sel_ref.shape[0]
        r1 = SEL_BLOCK // CMP_STRIDE
        offs, wts = _slc_offsets()
        p_slc = None
        for o, wt in zip(offs, wts):
            term = float(wt) * pg_ref[pl.ds(pad + int(o), nbp, stride=r1), :]
            p_slc = term if p_slc is None else p_slc + term
        cur = (past + lane % ds) // SEL_BLOCK
        jrow = lax.broadcasted_iota(jnp.int32, (nbp, LANES), 0)
        forced = (jrow == 0) | (jrow == cur) | (jrow == cur - 1)
        score = jnp.where((jrow > cur) | (jrow >= nb), -1.0, jnp.where(forced, 1e6, p_slc))
        selg = _extract_select(score, cur, n_sel)
        unfold = jnp.where((li == lj // ngrp) & (lj < n_lanes), 1.0, 0.0)
        sel_ref[...] = _dot(selg, unfold)
        wk = kw_ref.shape[0]
        nn = wnew_ref.shape[1]
        kw_ref[...] = jnp.zeros(kw_ref.shape, F32)
        for c0 in range(0, keep, LANES):
            kw_ref[c0:c0 + LANES, :] = _dot_nt(eye, win_ref[0, 0:kvw, c0:c0 + LANES].astype(BF16))
        kw_ref[keep:keep + nn, :] = wnew_ref[0, :, 0:kvw]
        vn_ref[...] = jnp.zeros(vn_ref.shape, F32)
        vn_ref[0:nn, :] = wnew_ref[0, :, kvw:2 * kvw]
        s = _dot(kw_ref[...].astype(BF16), qbd) * scale + bw_ref[...]
        irow = lax.broadcasted_iota(jnp.int32, (wk, LANES), 0)
        dist = jnp.where(irow < keep, keep + qi - irow, qi - (irow - keep))
        ok = (dist >= 0) & (dist < WINDOW) & (irow < keep + ds)
        p, _, _, l = _softmax_update(s, ok, jnp.full((1, LANES), NEG, F32), jnp.zeros((1, LANES), F32))
        p = (p * _inv_pos(l)).astype(BF16)
        o_w = (_dot(win_ref[0, kvw:2 * kvw, :].astype(BF16), p[0:keep, :])
               + _dot(vn_ref[...].T.astype(BF16), p[keep:keep + PAGE_SIZE, :]))
        g = g_ref[0]
        base_ref[...] = g[0:1, :] * o_c + g[2:3, :] * o_w
        m_ref[...] = jnp.full(m_ref.shape, NEG, F32)
        l_ref[...] = jnp.zeros(l_ref.shape, F32)
        acc_ref[...] = jnp.zeros(acc_ref.shape, F32)
        kn_ref[...] = jnp.zeros(kn_ref.shape, F32)
        vn_ref[...] = jnp.zeros(vn_ref.shape, F32)
        kn_ref[0:nn, :] = rnew_ref[0, :, 2 * kvw:3 * kvw]
        vn_ref[0:nn, :] = rnew_ref[0, :, 3 * kvw:4 * kvw]

    ik = lax.broadcasted_iota(jnp.int32, (PAGE_SIZE, LANES), 0)
    blocks_per_page = PAGE_SIZE // SEL_BLOCK

    def pages_update(k, v_t, first_page, count):
        bias = jnp.concatenate([bs_ref[first_page + i] for i in range(count)], axis=0)
        s = _dot(k, qbd) * scale + bias
        rows = [jnp.broadcast_to(sel_ref[pl.ds(first_page * blocks_per_page + i, 1), :], (SEL_BLOCK, LANES))
                for i in range(count * blocks_per_page)]
        key_pos = first_page * PAGE_SIZE + lax.broadcasted_iota(jnp.int32, (count * PAGE_SIZE, LANES), 0)
        mask = (jnp.concatenate(rows, axis=0) > 0.5) & (key_pos <= q_pos)
        p, m_new, alpha, l_new = _softmax_update(s, mask, m_ref[...], l_ref[...])
        m_ref[...] = m_new
        l_ref[...] = l_new
        acc_ref[...] = alpha * acc_ref[...] + _dot(v_t, p.astype(BF16))

    k_rows = jnp.concatenate([_dot_nt(eye, blk[0, 0:kvw, :].astype(BF16)) for blk in page_refs], axis=0)
    v_cols = jnp.concatenate([blk[0, kvw:2 * kvw, :].astype(BF16) for blk in page_refs], axis=1)
    pages_update(k_rows.astype(BF16), v_cols, j * npg, npg)

    @pl.when(j == n_steps - 1)
    def _():
        pages_update(kn_ref[...].astype(BF16), vn_ref[...].T.astype(BF16), n_pages, 1)
        o_ref[0] = base_ref[...] + g_ref[0][1:2, :] * (acc_ref[...] * _inv_pos(l_ref[...]))


def _nsa_sample(p_n, cache_kv, cache_win, page_table, W, db, ds, nkv, ngrp):
    nsa_w = nkv * ngrp * HEAD_DIM
    kvw = nkv * HEAD_DIM
    row_w = 4 * kvw
    qn, rows2d, win2d, gates = _nsa_project_call(p_n, W['qk_norm_g'], nsa_w, kvw, db * ds)
    rows_new = rows2d.reshape(db, ds, 4, nkv, HEAD_DIM)
    win_new = win2d.reshape(db, ds, 2, nkv, HEAD_DIM)
    n_pool = cache_kv.shape[0]
    n_pages = page_table.shape[1]
    past = n_pages * PAGE_SIZE
    keep = cache_win.shape[1]
    tot = past + _rup(ds, SEL_BLOCK)
    ns = tot // CMP_STRIDE
    nc = ns - CMP_LEN // CMP_STRIDE + 1
    nb = tot // SEL_BLOCK
    n_sel = min(SEL_TOPK, nb)
    nsp = _rup(ns, LANES)
    nbp = _rup(nb, 8)
    n_lanes = nkv * ds * ngrp
    assert n_lanes <= LANES and ds <= 8
    w_big = _cmp_first_weights(W['cmp_w1'], nkv)
    cache_pages = cache_kv.transpose(0, 2, 3, 4, 1).reshape(n_pool, row_w, PAGE_SIZE)
    parts_past = _cmp_parts_sample(cache_pages, page_table, w_big, kvw, min(MXU // (PAGE_SIZE // CMP_STRIDE), n_pages))
    npg = min(16, n_pages)
    rows_pad = jnp.pad(rows2d.reshape(db, ds, row_w), ((0, 0), (0, PAGE_SIZE - ds), (0, 0)))
    parts_new = _cmp_parts_prompt(rows_pad, w_big, db, PAGE_SIZE, kvw)
    kc_cat, vcT_cat = _cmp_finish([parts_past, parts_new], nsp, W['cmp_b1'], W['cmp_w2'], W['cmp_b2'],
                                  W['qk_norm_g'][1], nc, nkv)
    lane_pad = LANES - n_lanes
    q5 = qn.reshape(db, ds, nkv, ngrp, HEAD_DIM)
    qbd = jnp.einsum('bqgrd,gk->bgdkqr', q5, jnp.eye(nkv, dtype=F32)).reshape(db, kvw, n_lanes)
    qbd = jnp.pad(qbd, ((0, 0), (0, 0), (0, lane_pad))).astype(BF16)
    ng = 3 * nkv * ngrp
    gT = gates[:, :ng].reshape(db, ds, nkv, ngrp, 3).transpose(0, 4, 2, 1, 3).reshape(db, 3, n_lanes)
    gT = jnp.pad(gT, ((0, 0), (0, 5), (0, lane_pad)))
    lane = np.arange(LANES)
    live = lane < n_lanes
    head_of_lane = np.where(live, (lane // (ds * ngrp)) * ngrp + lane % ngrp, 0)
    qi = np.where(live, (lane // ngrp) % ds, 0)
    table_l = W['rel_bias'].astype(F32)[:, head_of_lane]
    bias_of = lambda dist: _rel_table_per_lane(dist, table_l)
    c_end = np.arange(nsp) * CMP_STRIDE + CMP_LEN - 1
    bias_c = bias_of(jnp.asarray(past + qi[None, :] - c_end[:, None], jnp.int32))
    pos = np.arange((n_pages + 1) * PAGE_SIZE)
    bias_s = bias_of(jnp.asarray(past + qi[None, :] - pos[:, None], jnp.int32)).reshape(n_pages + 1, PAGE_SIZE, LANES)
    wk = _rup(keep + 8, LANES)
    irow = np.arange(wk)[:, None]
    dist_w = np.where(irow < keep, keep + qi[None, :] - irow, qi[None, :] - (irow - keep))
    bias_w = bias_of(jnp.asarray(dist_w, jnp.int32))
    win_c = cache_win.transpose(0, 2, 3, 4, 1).reshape(db, 2 * kvw, keep)
    wnew8 = jnp.pad(win2d.reshape(db, ds, 2 * kvw), ((0, 0), (0, 8 - ds), (0, 0)))
    rnew8 = jnp.pad(rows2d.reshape(db, ds, row_w), ((0, 0), (0, 8 - ds), (0, 0)))
    n_steps = n_pages // npg
    per_b = lambda shape: pl.BlockSpec((1,) + shape, lambda i, j, pt: (i,) + (0,) * len(shape))
    const = lambda shape: pl.BlockSpec(shape, lambda i, j, pt: (0,) * len(shape))
    in_specs = [per_b((kvw, LANES)), per_b((8, LANES)), per_b((nsp, kvw)), per_b((kvw, nsp)), const((nsp, LANES)),
                per_b((2 * kvw, keep)), per_b((8, 2 * kvw)), const((wk, LANES)), per_b((8, row_w)),
                const((n_pages + 1, PAGE_SIZE, LANES))]
    in_specs += [pl.BlockSpec((1, 2 * kvw, PAGE_SIZE), (lambda i, j, pt, k=k: (pt[i, j * npg + k], 1, 0)))
                 for k in range(npg)]
    yT = pl.pallas_call(
        functools.partial(_nsa_sample_kernel, npg=npg, n_steps=n_steps, nc=nc, nb=nb, n_sel=n_sel, past=past,
                          ds=ds, keep=keep, ngrp=ngrp, n_lanes=n_lanes),
        grid_spec=pltpu.PrefetchScalarGridSpec(
            num_scalar_prefetch=1, grid=(db, n_steps), in_specs=in_specs,
            out_specs=pl.BlockSpec((1, kvw, LANES), lambda i, j, pt: (i, 0, 0)),
            scratch_shapes=[pltpu.VMEM((1, LANES), F32), pltpu.VMEM((1, LANES), F32), pltpu.VMEM((kvw, LANES), F32),
                            pltpu.VMEM((kvw, LANES), F32), pltpu.VMEM((nbp, LANES), F32),
                            pltpu.VMEM((nsp + 16, LANES), F32), pltpu.VMEM((nbp, LANES), F32),
                            pltpu.VMEM((wk, kvw), F32),
                            pltpu.VMEM((PAGE_SIZE, kvw), F32), pltpu.VMEM((PAGE_SIZE, kvw), F32)]),
        out_shape=jax.ShapeDtypeStruct((db, kvw, LANES), F32),
        compiler_params=_cparams(("arbitrary", "arbitrary")),
        name="nsa_sample_attn",
    )(page_table, qbd, gT, kc_cat, vcT_cat, bias_c, win_c, wnew8, bias_w, rnew8, bias_s, *([cache_pages] * npg))
    y6 = yT[:, :, :n_lanes].reshape(db, nkv, HEAD_DIM, nkv, ds, ngrp)
    y = jnp.einsum('bgdgqr->bqgrd', y6).reshape(db * ds, nsa_w)
    win_all = jnp.concatenate([cache_win, win_new.astype(cache_win.dtype)], axis=1)
    n_keep = min(WINDOW, past + ds)
    return y, rows_new, win_all[:, win_all.shape[1] - n_keep:]


def _rel_bucket(dist):
    d = jnp.maximum(dist, 0)
    exact = REL_BUCKETS // 2
    ratio = jnp.maximum(d, exact).astype(F32) / exact
    large = exact + (jnp.log(ratio) / math.log(REL_MAX_DIST / exact) * (REL_BUCKETS - exact)).astype(jnp.int32)
    return jnp.where(d < exact, d, jnp.minimum(large, REL_BUCKETS - 1))


def _slc_offsets():
    r1 = SEL_BLOCK // CMP_STRIDE
    r2 = CMP_LEN // CMP_STRIDE
    offs = np.arange(-(r2 - 1), r1)
    wts = np.array([sum(1 for m in range(r1) for n in range(r2) if m - n == o) for o in offs], np.float32)
    return offs, wts


def _top_values(x, k):
    n = x.shape[0]
    row = lax.broadcasted_iota(jnp.int32, x.shape, 0)
    vals = []
    for _ in range(k):
        m = jnp.max(x, axis=0, keepdims=True)
        vals.append(m)
        first = jnp.min(jnp.where(x == m, row, n), axis=0, keepdims=True)
        x = jnp.where(row == first, -jnp.inf, x)
    return vals


def _peer_route_kernel(hT_ref, wq_ref, sk_ref, th_ref, c1_ref, s2_ref, e2_ref, *, nheads, topk):
    nk, half = sk_ref.shape[1], sk_ref.shape[2]
    cpt = s2_ref.shape[1]
    qT = _dot(wq_ref[...], hT_ref[...])
    pairs = [(a, b) for a in range(topk) for b in range(topk) if (a + 1) * (b + 1) <= topk]
    for h in range(nheads):
        base = h * 2 * half
        s1 = _dot(sk_ref[0].astype(BF16), qT[base:base + half].astype(BF16))
        s2 = _dot(sk_ref[1].astype(BF16), qT[base + half:base + 2 * half].astype(BF16))
        v1 = _top_values(s1, topk)
        v2 = _top_values(s2, topk)
        sums = [v1[a] + v2[b] for a, b in pairs]
        cand = jnp.concatenate(sums, axis=0)
        tau = _top_values(cand, topk)[-1]
        z = jnp.sum(jnp.where(cand >= tau, jnp.exp(cand - (v1[0] + v2[0])), 0.0), axis=0, keepdims=True)
        theta = jnp.full(s1.shape, jnp.inf, F32)
        for a in range(topk):
            th_a = jnp.full(tau.shape, jnp.inf, F32)
            for (pa, pb), sm in zip(pairs, sums):
                if pa == a:
                    th_a = jnp.where(sm >= tau, v2[pb], th_a)
            theta = jnp.where(s1 == v1[a], th_a, theta)
        c1 = jnp.exp(s1 - v1[0]) / z
        e2 = jnp.exp(s2 - v2[0])
        for c in range(cpt):
            cs = slice(c * LANES, (c + 1) * LANES)
            th_ref[h, c] = theta[:, cs]
            c1_ref[h, c] = c1[:, cs]
            s2_ref[h, c] = s2[:, cs]
            e2_ref[h, c] = e2[:, cs]


def _peer_expert_kernel(hT_ref, x1_ref, ga_ref, th_ref, c1_ref, s2_ref, e2_ref, u_ref, v_ref,
                        o_ref, acc_ref, gate_ref, *, nheads, n_eblocks):
    eb = pl.program_id(1)
    n_chunks, nk = s2_ref.shape[1], s2_ref.shape[2]
    rows_per_block = u_ref.shape[0] // nk

    @pl.when(eb == 0)
    def _():
        acc_ref[...] = jnp.zeros(acc_ref.shape, F32)

    def gate_tile(i, c):
        i1 = eb * rows_per_block + i
        wd = None
        for h in range(nheads):
            chosen = s2_ref[h, c] >= th_ref[h, c, pl.ds(i1, 1), :]
            term = jnp.where(chosen, e2_ref[h, c], 0.0) * c1_ref[h, c, pl.ds(i1, 1), :]
            wd = term if wd is None else wd + term
        gate_ref[i, c] = wd

    def gate_step(it, carry):
        gate_tile(it // n_chunks, it % n_chunks)
        return carry

    lax.fori_loop(0, rows_per_block * n_chunks, gate_step, 0)
    act = jax.nn.gelu(_dot(u_ref[...], hT_ref[...]))
    gates = jnp.concatenate([jnp.concatenate([gate_ref[i, c] for c in range(n_chunks)], axis=1)
                             for i in range(rows_per_block)], axis=0)
    acc_ref[...] += _dot((gates * act).T.astype(BF16), v_ref[...])

    @pl.when(eb == n_eblocks - 1)
    def _():
        o_ref[...] = x1_ref[...] + ga_ref[0] * acc_ref[...]


def _peer(h2, x1, ga, W, tm):
    t, d = h2.shape
    sub_keys = W['peer_sub_keys']
    nk, half = sub_keys.shape[1], sub_keys.shape[2]
    qd = W['peer_w_query'].shape[1]
    nheads = qd // (2 * half)
    hT = h2.T
    wqT = W['peer_w_query'].T.astype(BF16)
    cpt = tm // LANES
    route_shape = jax.ShapeDtypeStruct((nheads, t // LANES, nk, LANES), F32)
    rspec = pl.BlockSpec((nheads, cpt, nk, LANES), lambda i: (0, i, 0, 0))
    th, c1, s2, e2 = pl.pallas_call(
        functools.partial(_peer_route_kernel, nheads=nheads, topk=PEER_TOPK),
        grid=(t // tm,),
        in_specs=[pl.BlockSpec((d, tm), lambda i: (0, i)),
                  pl.BlockSpec((qd, d), lambda i: (0, 0)),
                  pl.BlockSpec(sub_keys.shape, lambda i: (0, 0, 0))],
        out_specs=[rspec] * 4,
        out_shape=[route_shape] * 4,
        compiler_params=_cparams(("arbitrary",)),
        name="peer_route",
    )(hT, wqT, sub_keys)
    eblk = 2 * MXU
    n_eblocks = W['peer_u'].shape[0] // eblk
    nmod, rows, _ = ga.shape
    tiles_per_mod = (t // tm) // nmod
    rspec2 = pl.BlockSpec((nheads, cpt, nk, LANES), lambda i, e: (0, i, 0, 0))
    return pl.pallas_call(
        functools.partial(_peer_expert_kernel, nheads=nheads, n_eblocks=n_eblocks),
        grid=(t // tm, n_eblocks),
        in_specs=[pl.BlockSpec((d, tm), lambda i, e: (0, i)),
                  pl.BlockSpec((tm, d), lambda i, e: (i, 0)),
                  pl.BlockSpec((1, rows, d), lambda i, e: (i // tiles_per_mod, 0, 0)),
                  rspec2, rspec2, rspec2, rspec2,
                  pl.BlockSpec((eblk, d), lambda i, e: (e, 0)),
                  pl.BlockSpec((eblk, d), lambda i, e: (e, 0))],
        out_specs=pl.BlockSpec((tm, d), lambda i, e: (i, 0)),
        out_shape=jax.ShapeDtypeStruct((t, d), F32),
        scratch_shapes=[pltpu.VMEM((tm, d), F32), pltpu.VMEM((eblk // nk, cpt, nk, LANES), F32)],
        compiler_params=_cparams(("arbitrary", "arbitrary")),
        name="peer_experts",
    )(hT, x1, ga, th, c1, s2, e2, W['peer_u_bf16'], W['peer_v_bf16'])


def _layer(x, mods, nsa_fn, shift_prev, wkv0, W, tm, rwkv_chunk):
    b, t, d = x.shape
    cw = W['rwkv_w0'].shape[0]
    nw, na, ng = W['rwkv_w_up'].shape[0], W['rwkv_a_up'].shape[0], W['rwkv_g_up'].shape[0]
    rwkv_proj = 3 * cw + nw + na + ng
    sh1, sc1, ga1, sh2, sc2, ga2 = mods
    xf = x.reshape(b * t, d)
    if (b * t) % tm == 0 and t % tm == 0:
        as_mod = lambda m: m.reshape(b, 1, d)
        tm_wide = tm // 2
    else:
        tm = tm_wide = b * t
        as_mod = lambda m: jnp.repeat(m, t, axis=0).reshape(1, b * t, d)
    w_r = _rwkv_pad_cols(W['w_in'][:, :rwkv_proj], cw, nw, na, ng).astype(BF16)
    nsa_cols = W['w_in'].shape[1] - rwkv_proj
    w_n = jnp.pad(W['w_in'][:, rwkv_proj:], ((0, 0), (0, _rup(nsa_cols, LANES) - nsa_cols))).astype(BF16)
    p_r, _ = _norm_mod_matmul(xf, W['norm1_g'], as_mod(sc1), as_mod(sh1), w_r, tm_wide, w_r.shape[1])
    p_n, _ = _norm_mod_matmul(xf, W['norm1_g'], as_mod(sc1), as_mod(sh1), w_n, tm, w_n.shape[1])
    pr = p_r.shape[1]
    p_r = p_r.reshape(b, t, pr)
    shift_new = _rwkv_unpad_cols(p_r[:, -1], cw, nw, na, ng)
    tpad = _rup(t, rwkv_chunk)
    p_r_pad = jnp.pad(p_r, ((0, 0), (0, tpad - t), (0, 0)))
    y_r, wkv_new = _rwkv_mix(p_r_pad, shift_prev, wkv0, W, rwkv_chunk, t)
    y_r = y_r[:, :t].reshape(b * t, cw)
    y_n, rows, win = nsa_fn(p_n)
    w_out = W['w_out'].astype(BF16)
    x1 = _out_proj(xf, y_r, y_n, as_mod(ga1), w_out[:cw], w_out[cw:], tm, d)
    h2 = _norm_mod(x1, W['norm2_g'], as_mod(sc2), as_mod(sh2), tm)
    assert (b * t) % LANES == 0, "PEER kernels keep tokens on lanes"
    out = _peer(h2, x1, as_mod(ga2), W, tm)
    return out.reshape(b, t, d), rows, win, wkv_new, shift_new


def kernel(x_prompt, x_sample, c_prompt, c_sample, cache_kv, cache_win, state_wkv, state_shift, page_table,
           norm1_g, norm2_g, w_ada, b_ada, w_in, w_out,
           rwkv_mu, rwkv_w0, rwkv_w_up, rwkv_a0, rwkv_a_up, rwkv_g_up, rwkv_k_k, rwkv_k_a, rwkv_r_k, lnx_w, lnx_b,
           qk_norm_g, cmp_w1, cmp_b1, cmp_w2, cmp_b2, rel_bias,
           peer_w_query, peer_sub_keys, peer_u, peer_v):
    W = dict(norm1_g=norm1_g, norm2_g=norm2_g, w_ada=w_ada, b_ada=b_ada, w_in=w_in, w_out=w_out,
             rwkv_mu=rwkv_mu, rwkv_w0=rwkv_w0, rwkv_w_up=rwkv_w_up, rwkv_a0=rwkv_a0, rwkv_a_up=rwkv_a_up,
             rwkv_g_up=rwkv_g_up, rwkv_k_k=rwkv_k_k, rwkv_k_a=rwkv_k_a, rwkv_r_k=rwkv_r_k, lnx_w=lnx_w, lnx_b=lnx_b,
             qk_norm_g=qk_norm_g, cmp_w1=cmp_w1, cmp_b1=cmp_b1, cmp_w2=cmp_w2, cmp_b2=cmp_b2, rel_bias=rel_bias,
             peer_w_query=peer_w_query, peer_sub_keys=peer_sub_keys, peer_u=peer_u, peer_v=peer_v)
    W['peer_u_bf16'] = peer_u.astype(BF16)
    W['peer_v_bf16'] = peer_v.astype(BF16)
    bp, seq, d = x_prompt.shape
    db = x_sample.shape[0]
    nkv = cache_kv.shape[3]
    nh_r = rwkv_w0.shape[0] // HEAD_DIM
    ngrp = (w_out.shape[0] - rwkv_w0.shape[0]) // HEAD_DIM // nkv

    mods = _ada_mods(jnp.concatenate([c_prompt, c_sample], axis=0), w_ada, b_ada)
    mods = mods.reshape(bp + db, N_MODS, d)
    mods_p = [mods[:bp, i] for i in range(N_MODS)]
    mods_s = [mods[bp:, i] for i in range(N_MODS)]

    shift0 = jnp.zeros((bp, state_shift.shape[1]), F32)
    wkv0 = jnp.zeros((bp, nh_r, HEAD_DIM, HEAD_DIM), F32)
    y_p, rows_p, win_p, wkv_p, shift_p = _layer(
        x_prompt, mods_p, lambda pn: _nsa_prompt(pn, W, bp, seq, nkv, ngrp, 512), shift0, wkv0, W, 512, 64)
    y_s, rows_s, win_s, wkv_s, shift_s = _layer(
        x_sample, mods_s,
        lambda pn: _nsa_sample(pn, cache_kv, cache_win, page_table, W, db, x_sample.shape[1], nkv, ngrp),
        state_shift, state_wkv, W, 512, 32)
    return (y_p, y_s, rows_p, win_p, wkv_p.astype(state_wkv.dtype), shift_p,
            rows_s, win_s, wkv_s.astype(state_wkv.dtype), shift_s)
```
